```python
import math
import jax, jax.numpy as jnp
from jax import lax
import numpy as np

D_MODEL = 2048
BATCH = 1
SEQ = 8192
DEPTH = 1
DEC_BATCH = 32
DEC_SEQ = 16
PAST_LEN = 1024

CHUNK = 64
ATT_HEADS = 8
QK_DIM = 64
V_DIM = 2 * QK_DIM
D_ATT = ATT_HEADS * V_DIM
QK_W = ATT_HEADS * 2 * QK_DIM
SSM_HEADS = 16
SSM_HEAD_DIM = 64
D_SSM = SSM_HEADS * SSM_HEAD_DIM
SSM_GROUPS = 2
SSM_STATE = 128
SSM_CONV = 4
SSD_CHUNK = CHUNK
CONV_DIM = D_SSM + 2 * SSM_GROUPS * SSM_STATE
D_MIX = D_ATT + D_SSM
IN_SPLITS = (QK_W, 2 * QK_W, 2 * QK_W + D_ATT, 2 * QK_W + D_ATT + D_SSM, 2 * QK_W + D_ATT + D_SSM + CONV_DIM)
D_IN_PROJ = 2 * QK_W + D_ATT + D_SSM + CONV_DIM + SSM_HEADS
D_FF = 5632
FFN_CONV = 3
Q_BLOCK = 128
DENSE_SWEEP_MIN_KEYS = 2048
EPS = 1e-6

kernel_name = 'hymba_diffattn_mamba2_convffn_stream_step'


def alibi_slopes():
    return jnp.asarray(2.0 ** (-8.0 * np.arange(1, ATT_HEADS + 1) / ATT_HEADS), dtype=jnp.float32)


def rmsnorm(x, g):
    xf = x.astype(jnp.float32)
    xf = xf * lax.rsqrt(jnp.mean(xf * xf, axis=-1, keepdims=True) + EPS)
    return (xf * g.astype(jnp.float32)).astype(x.dtype)


def causal_dwconv(x, prev, w, b):
    width = w.shape[0]
    length = x.shape[1]
    xp = jnp.concatenate([prev.astype(x.dtype), x], axis=1)
    y = b
    for tap in range(width):
        y = y + xp[:, tap:tap + length] * w[tap]
    return y, xp[:, xp.shape[1] - (width - 1):]


def diff_attend(q, k, v, q_pos, k_pos, lam):
    s = jnp.einsum('bqhmd,bkhmd->bhmqk', q, k).astype(jnp.float32) * (QK_DIM ** -0.5)
    dist = jnp.abs(q_pos[:, None] - k_pos[None, :]).astype(jnp.float32)
    s = s - (alibi_slopes()[:, None, None] * dist)[None, :, None]
    visible = (k_pos // CHUNK)[None, :] <= (q_pos // CHUNK)[:, None]
    s = jnp.where(visible, s, -jnp.inf)
    p = jax.nn.softmax(s, axis=-1)
    a = p[:, :, 0] - lam * p[:, :, 1]
    return jnp.einsum('bhqk,bkhv->bqhv', a.astype(v.dtype), v)


def diff_attention(q, k, v, q_pos, k_pos, lam):
    bsz, lq = q.shape[:2]
    if lq % Q_BLOCK == 0 and k.shape[1] >= DENSE_SWEEP_MIN_KEYS:
        nb = lq // Q_BLOCK
        qb = jnp.swapaxes(q.reshape(bsz, nb, Q_BLOCK, ATT_HEADS, 2, QK_DIM), 0, 1)
        pb = q_pos.reshape(nb, Q_BLOCK)
        ob = lax.map(lambda qp: diff_attend(qp[0], k, v, qp[1], k_pos, lam), (qb, pb))
        return jnp.swapaxes(ob, 0, 1).reshape(bsz, lq, ATT_HEADS, V_DIM)
    return diff_attend(q, k, v, q_pos, k_pos, lam)


def ssd_scan(x, dt, a, bm, cm, h0, chunk):
    bsz, length, nh, hd = x.shape
    nc = length // chunk
    f32 = jnp.float32
    la = (dt.astype(f32) * a.astype(f32)).reshape(bsz, nc, chunk, nh)
    acum = jnp.cumsum(la, axis=2)
    xdt = (x.astype(f32) * dt.astype(f32)[..., None]).reshape(bsz, nc, chunk, nh, hd)
    bc = bm.astype(f32).reshape(bsz, nc, chunk, nh, -1)
    cc = cm.astype(f32).reshape(bsz, nc, chunk, nh, -1)
    tri = jnp.tril(jnp.ones((chunk, chunk), dtype=bool))
    seg = acum[:, :, :, None, :] - acum[:, :, None, :, :]
    decay_ts = jnp.exp(jnp.where(tri[None, None, :, :, None], seg, -jnp.inf))
    cb = jnp.einsum('bcthn,bcshn->bctsh', cc, bc)
    y_intra = jnp.einsum('bctsh,bcshp->bcthp', cb * decay_ts, xdt)
    decay_end = jnp.exp(acum[:, :, -1:, :] - acum)
    s_chunk = jnp.einsum('bcshn,bcsh,bcshp->bchpn', bc, decay_end, xdt)
    chunk_decay = jnp.exp(acum[:, :, -1, :])

    def step(h, inp):
        dec, s_c = inp
        return dec[:, :, None, None] * h + s_c, h

    h_last, h_prev = lax.scan(step, h0.astype(f32),
                              (jnp.swapaxes(chunk_decay, 0, 1), jnp.swapaxes(s_chunk, 0, 1)))
    h_prev = jnp.swapaxes(h_prev, 0, 1)
    y_inter = jnp.einsum('bcthn,bchpn,bcth->bcthp', cc, h_prev, jnp.exp(acum))
    y = (y_intra + y_inter).reshape(bsz, length, nh, hd)
    return y.astype(x.dtype), h_last.astype(h0.dtype)


def hybrid_layer(x, k_past, v_past, conv_prev, ssm_prev, ffn_prev,
                 norm_mix_g, w_in, lambda_q1, lambda_k1, lambda_q2, lambda_k2, attn_subln_g,
                 conv_w, conv_b, dt_bias, a_log, d_skip, ssm_norm_g, w_out,
                 norm_ffn_g, w_gate, w_up, ffn_conv_w, ffn_conv_b, w_down, lam_init):
    f32 = jnp.float32
    bsz, length, _ = x.shape
    past = k_past.shape[1]
    h = rmsnorm(x, norm_mix_g)
    proj = h @ w_in
    q, k_new, v_new, z, xbc, dt_raw = jnp.split(proj, IN_SPLITS, axis=-1)

    q = q.reshape(bsz, length, ATT_HEADS, 2, QK_DIM)
    k_new = k_new.reshape(bsz, length, ATT_HEADS, 2 * QK_DIM)
    v_new = v_new.reshape(bsz, length, ATT_HEADS, V_DIM)
    k_all = jnp.concatenate([k_past.astype(x.dtype), k_new], axis=1).reshape(bsz, past + length, ATT_HEADS, 2, QK_DIM)
    v_all = jnp.concatenate([v_past.astype(x.dtype), v_new], axis=1)
    q_pos = past + jnp.arange(length, dtype=jnp.int32)
    k_pos = jnp.arange(past + length, dtype=jnp.int32)
    lam = (jnp.exp(jnp.sum(lambda_q1.astype(f32) * lambda_k1.astype(f32)))
           - jnp.exp(jnp.sum(lambda_q2.astype(f32) * lambda_k2.astype(f32))) + lam_init)
    o = diff_attention(q, k_all, v_all, q_pos, k_pos, lam)
    o = (rmsnorm(o, attn_subln_g) * (1.0 - lam_init)).reshape(bsz, length, D_ATT)

    xbc, conv_new = causal_dwconv(xbc, conv_prev, conv_w, conv_b)
    xbc = jax.nn.silu(xbc)
    xs, bm, cm = jnp.split(xbc, (D_SSM, D_SSM + SSM_GROUPS * SSM_STATE), axis=-1)
    xs = xs.reshape(bsz, length, SSM_HEADS, SSM_HEAD_DIM)
    heads_per_group = SSM_HEADS // SSM_GROUPS
    bm = jnp.repeat(bm.reshape(bsz, length, SSM_GROUPS, SSM_STATE), heads_per_group, axis=2)
    cm = jnp.repeat(cm.reshape(bsz, length, SSM_GROUPS, SSM_STATE), heads_per_group, axis=2)
    dt = jax.nn.softplus((dt_raw + dt_bias).astype(f32))
    a = -jnp.exp(a_log.astype(f32))
    chunk = SSD_CHUNK if length % SSD_CHUNK == 0 else length
    y, ssm_new = ssd_scan(xs, dt, a, bm, cm, ssm_prev, chunk)
    y = (y + d_skip[:, None] * xs).reshape(bsz, length, D_SSM) * jax.nn.silu(z)
    y = rmsnorm(y.reshape(bsz, length, SSM_GROUPS, D_SSM // SSM_GROUPS),
                ssm_norm_g.reshape(SSM_GROUPS, D_SSM // SSM_GROUPS)).reshape(bsz, length, D_SSM)

    x = x + jnp.concatenate([o, y], axis=-1) @ w_out

    hf = rmsnorm(x, norm_ffn_g)
    gate, ffn_new = causal_dwconv(hf @ w_gate, ffn_prev, ffn_conv_w, ffn_conv_b)
    x = x + (jax.nn.silu(gate) * (hf @ w_up)) @ w_down
    return x, k_new, v_new, conv_new, ssm_new, ffn_new


def setup_inputs(seed: int = 0) -> dict:
    key = jax.random.key(seed)
    ks = jax.random.split(key, 32)
    f32 = jnp.float32

    def nrm(k, shape, scale):
        return scale * jax.random.normal(k, shape, f32)

    dt0 = jnp.exp(jax.random.uniform(ks[20], (DEPTH, SSM_HEADS), f32, math.log(1e-3), math.log(1e-1)))
    return {
        'x_prompt': nrm(ks[0], (BATCH, SEQ, D_MODEL), 1.0),
        'x_sample': nrm(ks[1], (DEC_BATCH, DEC_SEQ, D_MODEL), 1.0),
        'cache_k': nrm(ks[2], (DEPTH, DEC_BATCH, PAST_LEN, ATT_HEADS, 2 * QK_DIM), 1.0),
        'cache_v': nrm(ks[3], (DEPTH, DEC_BATCH, PAST_LEN, ATT_HEADS, V_DIM), 1.0),
        'state_ssm_conv': nrm(ks[4], (DEPTH, DEC_BATCH, SSM_CONV - 1, CONV_DIM), 1.0),
        'state_ssm': nrm(ks[5], (DEPTH, DEC_BATCH, SSM_HEADS, SSM_HEAD_DIM, SSM_STATE), 0.1),
        'state_ffn_conv': nrm(ks[6], (DEPTH, DEC_BATCH, FFN_CONV - 1, D_FF), 1.0),
        'norm_mix_g': 1.0 + nrm(ks[7], (DEPTH, D_MODEL), 0.02),
        'w_in': nrm(ks[8], (DEPTH, D_MODEL, D_IN_PROJ), D_MODEL ** -0.5),
        'lambda_q1': nrm(ks[9], (DEPTH, QK_DIM), 0.1),
        'lambda_k1': nrm(ks[10], (DEPTH, QK_DIM), 0.1),
        'lambda_q2': nrm(ks[11], (DEPTH, QK_DIM), 0.1),
        'lambda_k2': nrm(ks[12], (DEPTH, QK_DIM), 0.1),
        'attn_subln_g': 1.0 + nrm(ks[13], (DEPTH, V_DIM), 0.02),
        'conv_w': nrm(ks[14], (DEPTH, SSM_CONV, CONV_DIM), SSM_CONV ** -0.5),
        'conv_b': nrm(ks[15], (DEPTH, CONV_DIM), 0.01),
        'dt_bias': dt0 + jnp.log(-jnp.expm1(-dt0)),
        'a_log': jnp.log(jax.random.uniform(ks[16], (DEPTH, SSM_HEADS), f32, 1.0, 16.0)),
        'd_skip': 1.0 + nrm(ks[17], (DEPTH, SSM_HEADS), 0.02),
        'ssm_norm_g': 1.0 + nrm(ks[18], (DEPTH, D_SSM), 0.02),
        'w_out': nrm(ks[19], (DEPTH, D_MIX, D_MODEL), D_MIX ** -0.5),
        'norm_ffn_g': 1.0 + nrm(ks[21], (DEPTH, D_MODEL), 0.02),
        'w_gate': nrm(ks[22], (DEPTH, D_MODEL, D_FF), D_MODEL ** -0.5),
        'w_up': nrm(ks[23], (DEPTH, D_MODEL, D_FF), D_MODEL ** -0.5),
        'ffn_conv_w': nrm(ks[24], (DEPTH, FFN_CONV, D_FF), FFN_CONV ** -0.5),
        'ffn_conv_b': nrm(ks[25], (DEPTH, D_FF), 0.01),
        'w_down': nrm(ks[26], (DEPTH, D_FF, D_MODEL), D_FF ** -0.5),
        'norm_final_g': 1.0 + nrm(ks[27], (D_MODEL,), 0.02),
    }


def reference(x_prompt, x_sample, cache_k, cache_v, state_ssm_conv, state_ssm, state_ffn_conv,
              norm_mix_g, w_in, lambda_q1, lambda_k1, lambda_q2, lambda_k2, attn_subln_g,
              conv_w, conv_b, dt_bias, a_log, d_skip, ssm_norm_g, w_out,
              norm_ffn_g, w_gate, w_up, ffn_conv_w, ffn_conv_b, w_down, norm_final_g):
    bsz = x_prompt.shape[0]
    xp, xs = x_prompt, x_sample
    kp_l, vp_l, cp_l, sp_l, fp_l = [], [], [], [], []
    ks_l, vs_l, cs_l, ss_l, fs_l = [], [], [], [], []
    for layer in range(DEPTH):
        lam_init = 0.8 - 0.6 * math.exp(-0.3 * layer)
        weights = (norm_mix_g[layer], w_in[layer], lambda_q1[layer], lambda_k1[layer],
                   lambda_q2[layer], lambda_k2[layer], attn_subln_g[layer],
                   conv_w[layer], conv_b[layer], dt_bias[layer], a_log[layer], d_skip[layer],
                   ssm_norm_g[layer], w_out[layer], norm_ffn_g[layer], w_gate[layer], w_up[layer],
                   ffn_conv_w[layer], ffn_conv_b[layer], w_down[layer])
        xp, kp, vp, cp, sp, fp = hybrid_layer(
            xp,
            jnp.zeros((bsz, 0, ATT_HEADS, 2 * QK_DIM), xp.dtype),
            jnp.zeros((bsz, 0, ATT_HEADS, V_DIM), xp.dtype),
            jnp.zeros((bsz, SSM_CONV - 1, CONV_DIM), xp.dtype),
            jnp.zeros((bsz, SSM_HEADS, SSM_HEAD_DIM, SSM_STATE), state_ssm.dtype),
            jnp.zeros((bsz, FFN_CONV - 1, D_FF), xp.dtype),
            *weights, lam_init)
        xs, k_s, v_s, c_s, s_s, f_s = hybrid_layer(
            xs, cache_k[layer], cache_v[layer], state_ssm_conv[layer], state_ssm[layer],
            state_ffn_conv[layer], *weights, lam_init)
        kp_l.append(kp); vp_l.append(vp); cp_l.append(cp); sp_l.append(sp); fp_l.append(fp)
        ks_l.append(k_s); vs_l.append(v_s); cs_l.append(c_s); ss_l.append(s_s); fs_l.append(f_s)
    y_prompt = rmsnorm(xp, norm_final_g)
    y_sample = rmsnorm(xs, norm_final_g)
    return (y_prompt, y_sample,
            jnp.stack(kp_l), jnp.stack(vp_l), jnp.stack(cp_l), jnp.stack(sp_l), jnp.stack(fp_l),
            jnp.stack(ks_l), jnp.stack(vs_l), jnp.stack(cs_l), jnp.stack(ss_l), jnp.stack(fs_l))
```

```python
import functools
import math

import jax
import jax.numpy as jnp
import numpy as np
from jax import lax
from jax.experimental import pallas as pl
from jax.experimental.pallas import tpu as pltpu

F32 = jnp.float32
BF16 = jnp.bfloat16

D_MODEL = 2048
CHUNK = 64
ATT_HEADS = 8
QK_DIM = 64
V_DIM = 128
D_ATT = ATT_HEADS * V_DIM
SSM_HEADS = 16
SSM_HEAD_DIM = 64
D_SSM = SSM_HEADS * SSM_HEAD_DIM
SSM_GROUPS = 2
SSM_STATE = 128
SSM_CONV = 4
BC_DIM = 2 * SSM_GROUPS * SSM_STATE
CONV_DIM = D_SSM + BC_DIM
D_FF = 5632
FFN_CONV = 3
EPS = 1e-6
LANES = 128
SUBLANES = 8
VMEM_LIMIT = 52 * 1024 * 1024

_NT = (((1,), (1,)), ((), ()))
_TN = (((0,), (0,)), ((), ()))


def _cparams(sem):
    return pltpu.CompilerParams(dimension_semantics=sem, vmem_limit_bytes=VMEM_LIMIT)


def _silu(x):
    return x * jax.nn.sigmoid(x)


def _rms(x, g):
    return x * lax.rsqrt(jnp.mean(x * x, axis=-1, keepdims=True) + EPS) * g


_PROJ_TN = 512
_PROJ_SEGS = ((0, 2), (2, 4), (4, 6), (6, 8), (8, 10), (10, 11))


def _in_proj_kernel(x_ref, g_ref, w_ref, wdt_ref,
                    q_ref, k_ref, v_ref, z_ref, xs_ref, bc_ref, dt_ref, h_scr):
    j = pl.program_id(1)

    @pl.when(j == 0)
    def _():
        hb = _rms(x_ref[...], g_ref[...]).astype(BF16)
        h_scr[...] = hb
        dt_ref[...] = jnp.dot(hb, wdt_ref[...], preferred_element_type=F32)

    res = jnp.dot(h_scr[...], w_ref[...], preferred_element_type=F32)
    outs = (q_ref, k_ref, v_ref, z_ref, xs_ref, bc_ref)
    for (lo, hi), ref in zip(_PROJ_SEGS, outs):
        @pl.when((j >= lo) & (j < hi))
        def _(ref=ref):
            ref[...] = res.astype(ref.dtype)


def _in_proj(x, g, w_main, w_dt, tm):
    m = x.shape[0]
    tn = _PROJ_TN
    nj = w_main.shape[1] // tn

    def seg_spec(lo, hi):
        return pl.BlockSpec((tm, tn), lambda i, j: (i, jnp.clip(j - lo, 0, hi - lo - 1)))

    out_shape = (
        jax.ShapeDtypeStruct((m, D_ATT), BF16),
        jax.ShapeDtypeStruct((m, D_ATT), F32),
        jax.ShapeDtypeStruct((m, D_ATT), F32),
        jax.ShapeDtypeStruct((m, D_SSM), F32),
        jax.ShapeDtypeStruct((m, D_SSM), F32),
        jax.ShapeDtypeStruct((m, BC_DIM), F32),
        jax.ShapeDtypeStruct((m, LANES), F32),
    )
    out_specs = tuple(seg_spec(lo, hi) for lo, hi in _PROJ_SEGS) + (
        pl.BlockSpec((tm, LANES), lambda i, j: (i, 0)),)
    return pl.pallas_call(
        _in_proj_kernel,
        grid=(m // tm, nj),
        in_specs=[
            pl.BlockSpec((tm, D_MODEL), lambda i, j: (i, 0)),
            pl.BlockSpec((1, D_MODEL), lambda i, j: (0, 0)),
            pl.BlockSpec((D_MODEL, tn), lambda i, j: (0, j)),
            pl.BlockSpec((D_MODEL, LANES), lambda i, j: (0, 0)),
        ],
        out_specs=out_specs,
        out_shape=out_shape,
        scratch_shapes=[pltpu.VMEM((tm, D_MODEL), BF16)],
        compiler_params=_cparams(("arbitrary", "arbitrary")),
        name="in_proj",
    )(x, g, w_main, w_dt)


def _lambda_value(lamv_ref, lam_init):
    lv = lamv_ref[...]
    s1 = jnp.sum(lv[0:1] * lv[1:2], axis=-1, keepdims=True)
    s2 = jnp.sum(lv[2:3] * lv[3:4], axis=-1, keepdims=True)
    return jnp.exp(s1) - jnp.exp(s2) + lam_init


def _split_maps(q):
    lane = lax.broadcasted_iota(jnp.int32, q.shape, 1)
    qs = q * jnp.asarray(QK_DIM ** -0.5, q.dtype)
    zero = jnp.zeros_like(qs)
    return jnp.where(lane < QK_DIM, qs, zero), jnp.where(lane >= QK_DIM, qs, zero)


def _attn_prompt_kernel(q_ref, k_ref, v_ref, slope_ref, lamv_ref, g_ref, o_ref, *, tile, lam_init):
    qi = pl.program_id(1)
    slope = slope_ref[0][:, :1]
    lam = _lambda_value(lamv_ref, lam_init)
    q1, q2 = _split_maps(q_ref[...])
    row = lax.broadcasted_iota(jnp.int32, (tile, tile), 0)
    col = lax.broadcasted_iota(jnp.int32, (tile, tile), 1)
    rc = (row - col).astype(F32)

    def load_kv(j):
        start = pl.multiple_of(j * tile, tile)
        kj = k_ref[pl.ds(start, tile), :].astype(BF16)
        vj = v_ref[pl.ds(start, tile), :].astype(BF16)
        return kj, vj

    kj, vj = load_kv(qi)
    visible = (col // CHUNK) <= (row // CHUNK)
    bias = slope * jnp.abs(rc)
    carry = []
    for qm in (q1, q2):
        s = lax.dot_general(qm, kj, _NT, preferred_element_type=F32) - bias
        s = jnp.where(visible, s, -jnp.inf)
        m = jnp.max(s, axis=-1, keepdims=True)
        p = jnp.exp(s - m)
        l = jnp.sum(p, axis=-1, keepdims=True)
        a = jnp.dot(p.astype(BF16), vj, preferred_element_type=F32)
        carry += [m, l, a]

    def body(j, carry):
        kj, vj = load_kv(j)
        bias = slope * (rc + ((qi - j) * tile).astype(F32))
        out = []
        for idx, qm in enumerate((q1, q2)):
            m, l, a = carry[3 * idx:3 * idx + 3]
            s = lax.dot_general(qm, kj, _NT, preferred_element_type=F32) - bias
            m_new = jnp.maximum(m, jnp.max(s, axis=-1, keepdims=True))
            alpha = jnp.exp(m - m_new)
            p = jnp.exp(s - m_new)
            l = alpha * l + jnp.sum(p, axis=-1, keepdims=True)
            a = alpha * a + jnp.dot(p.astype(BF16), vj, preferred_element_type=F32)
            out += [m_new, l, a]
        return tuple(out)

    m1, l1, a1, m2, l2, a2 = lax.fori_loop(0, qi, body, tuple(carry))
    o = a1 / l1 - lam * (a2 / l2)
    o_ref[...] = (_rms(o, g_ref[...]) * (1.0 - lam_init)).astype(o_ref.dtype)


def _attn_prompt(q, k, v, slopes, lamv, g, lam_init, tile):
    length = q.shape[0]
    return pl.pallas_call(
        functools.partial(_attn_prompt_kernel, tile=tile, lam_init=lam_init),
        grid=(ATT_HEADS, length // tile),
        in_specs=[
            pl.BlockSpec((tile, V_DIM), lambda h, i: (i, h)),
            pl.BlockSpec((length, V_DIM), lambda h, i: (0, h)),
            pl.BlockSpec((length, V_DIM), lambda h, i: (0, h)),
            pl.BlockSpec((1, 1, LANES), lambda h, i: (h, 0, 0)),
            pl.BlockSpec((4, QK_DIM), lambda h, i: (0, 0)),
            pl.BlockSpec((1, V_DIM), lambda h, i: (0, 0)),
        ],
        out_specs=pl.BlockSpec((tile, V_DIM), lambda h, i: (i, h)),
        out_shape=jax.ShapeDtypeStruct((length, D_ATT), BF16),
        compiler_params=_cparams(("arbitrary", "arbitrary")),
        name="attn_prompt",
    )(q, k, v, slopes, lamv, g)


def _attn_sample_kernel(q_ref, kn_ref, vn_ref, ck_ref, cv_ref, lamv_ref, g_ref, o_ref,
                        *, seq, past, lam_init):
    lam = _lambda_value(lamv_ref, lam_init)
    r_c = (lax.broadcasted_iota(jnp.int32, (2 * seq, past), 0) % seq
           - lax.broadcasted_iota(jnp.int32, (2 * seq, past), 1) + past).astype(F32)
    r_n = jnp.abs(lax.broadcasted_iota(jnp.int32, (2 * seq, seq), 0) % seq
                  - lax.broadcasted_iota(jnp.int32, (2 * seq, seq), 1)).astype(F32)
    for h in range(ATT_HEADS):
        slope = 2.0 ** (-8.0 * (h + 1) / ATT_HEADS)
        sl = slice(h * V_DIM, (h + 1) * V_DIM)
        q1, q2 = _split_maps(q_ref[:, sl])
        qq = jnp.concatenate([q1, q2], axis=0)
        kc = ck_ref[0, :, sl].astype(BF16)
        vc = cv_ref[0, :, sl].astype(BF16)
        kn = kn_ref[:, sl].astype(BF16)
        vn = vn_ref[:, sl].astype(BF16)
        sc = lax.dot_general(qq, kc, _NT, preferred_element_type=F32) - slope * r_c
        sn = lax.dot_general(qq, kn, _NT, preferred_element_type=F32) - slope * r_n
        m = jnp.maximum(jnp.max(sc, axis=-1, keepdims=True), jnp.max(sn, axis=-1, keepdims=True))
        pc = jnp.exp(sc - m)
        pn = jnp.exp(sn - m)
        l = jnp.sum(pc, axis=-1, keepdims=True) + jnp.sum(pn, axis=-1, keepdims=True)
        acc = (jnp.dot(pc.astype(BF16), vc, preferred_element_type=F32)
               + jnp.dot(pn.astype(BF16), vn, preferred_element_type=F32))
        on = acc / l
        o = on[:seq] - lam * on[seq:]
        o_ref[:, sl] = (_rms(o, g_ref[...]) * (1.0 - lam_init)).astype(o_ref.dtype)


def _attn_sample(q, k_new, v_new, cache_k, cache_v, lamv, g, lam_init, seq):
    nb, past = cache_k.shape[0], cache_k.shape[1]
    row_spec = pl.BlockSpec((seq, D_ATT), lambda b: (b, 0))
    cache_spec = pl.BlockSpec((1, past, D_ATT), lambda b: (b, 0, 0))
    return pl.pallas_call(
        functools.partial(_attn_sample_kernel, seq=seq, past=past, lam_init=lam_init),
        grid=(nb,),
        in_specs=[row_spec, row_spec, row_spec, cache_spec, cache_spec,
                  pl.BlockSpec((4, QK_DIM), lambda b: (0, 0)),
                  pl.BlockSpec((1, V_DIM), lambda b: (0, 0))],
        out_specs=row_spec,
        out_shape=jax.ShapeDtypeStruct((nb * seq, D_ATT), BF16),
        compiler_params=_cparams(("arbitrary",)),
        name="attn_sample",
    )(q, k_new, v_new, cache_k, cache_v, lamv, g)


def _split3(x):
    hi = x.astype(BF16)
    r = x - hi.astype(F32)
    mid = r.astype(BF16)
    lo = (r - mid.astype(F32)).astype(BF16)
    return hi, mid, lo


def _ssd_chunk(xs, bm, cm, dt, z, s_ref, a_row, a_seg, dskip, gn, e3, es3, t3, ones3, lc):
    seg_w = SSM_HEADS * lc
    half = D_SSM // SSM_GROUPS
    d3 = jnp.concatenate(_split3(dt), axis=1)
    dt_x = jnp.dot(d3, e3, preferred_element_type=F32)
    la = dt_x * a_row
    if seg_w == D_SSM:
        la_s = la
    else:
        la_s = jnp.dot(d3, es3, preferred_element_type=F32) * a_seg
    la3 = jnp.concatenate(_split3(la), axis=0)
    acol = jnp.dot(t3, la3, preferred_element_type=F32)
    t_idx = lax.broadcasted_iota(jnp.int32, (lc, seg_w), 0)
    s_idx = lax.broadcasted_iota(jnp.int32, (lc, seg_w), 1) % lc
    w3 = jnp.concatenate(_split3(jnp.where(t_idx > s_idx, la_s, 0.0)), axis=0)
    seg = jnp.dot(t3, w3, preferred_element_type=F32)
    decay = jnp.where(t_idx >= s_idx, jnp.exp(seg), 0.0)

    cb16 = cm.astype(BF16)
    bb16 = bm.astype(BF16)
    hpg = SSM_HEADS // SSM_GROUPS
    cbs = []
    for g in range(SSM_GROUPS):
        gs = slice(g * SSM_STATE, (g + 1) * SSM_STATE)
        b_rep = jnp.concatenate([bb16[:, gs]] * hpg, axis=0)
        cbs.append(lax.dot_general(cb16[:, gs], b_rep, _NT, preferred_element_type=F32))
    mmat = (jnp.concatenate(cbs, axis=1) * decay).astype(BF16)

    xdt = xs * dt_x
    xdt16 = xdt.astype(BF16)
    hk = (2 * LANES) // lc
    wd = hk * SSM_HEAD_DIM
    blk = (lax.broadcasted_iota(jnp.int32, (hk * lc, wd), 0) // lc
           == lax.broadcasted_iota(jnp.int32, (hk * lc, wd), 1) // SSM_HEAD_DIM)
    parts = []
    for i in range(SSM_HEADS // hk):
        xd = xdt16[:, i * wd:(i + 1) * wd]
        bd = jnp.where(blk, jnp.concatenate([xd] * hk, axis=0), jnp.zeros((), BF16))
        parts.append(jnp.dot(mmat[:, i * hk * lc:(i + 1) * hk * lc], bd, preferred_element_type=F32))
    y_intra = jnp.concatenate(parts, axis=1) if len(parts) > 1 else parts[0]

    yi = []
    for g in range(SSM_GROUPS):
        sg = s_ref[g * half:(g + 1) * half, :].astype(BF16)
        yi.append(lax.dot_general(cb16[:, g * SSM_STATE:(g + 1) * SSM_STATE], sg, _NT,
                                  preferred_element_type=F32))
    y_inter = jnp.exp(acol) * jnp.concatenate(yi, axis=1)

    dec_end = jnp.exp(acol[lc - 1:lc, :] - acol)
    xd_end = (xdt * dec_end).astype(BF16)
    acl = lax.dot_general(la3, ones3, _TN, preferred_element_type=F32)
    for g in range(SSM_GROUPS):
        rows = slice(g * half, (g + 1) * half)
        upd = lax.dot_general(xd_end[:, rows], bb16[:, g * SSM_STATE:(g + 1) * SSM_STATE], _TN,
                              preferred_element_type=F32)
        s_ref[rows, :] = jnp.exp(acl[rows, :]) * s_ref[rows, :] + upd

    y = (y_intra + y_inter + dskip * xs) * _silu(z)
    outs = []
    for g in range(SSM_GROUPS):
        cs = slice(g * half, (g + 1) * half)
        outs.append(_rms(y[:, cs], gn[:, cs]))
    return jnp.concatenate(outs, axis=1)


def _ssd_kernel(*refs, lc, nchunk, has_state):
    if has_state:
        (xs_ref, bc_ref, dt_ref, z_ref, prev_ref, h0_ref, cw_ref, cbias_ref, dtb_ref, alog_ref,
         alogs_ref, dskip_ref, gn_ref, e3_ref, es3_ref, t3_ref, ones3_ref,
         y_ref, s_ref, scr_x, scr_bc) = refs
    else:
        (xs_ref, bc_ref, dt_ref, z_ref, cw_ref, cbias_ref, dtb_ref, alog_ref,
         alogs_ref, dskip_ref, gn_ref, e3_ref, es3_ref, t3_ref, ones3_ref,
         y_ref, s_ref, scr_x, scr_bc) = refs
    rows = lc * nchunk
    pad = SUBLANES
    if has_state:
        s2d = s_ref.at[0]
        scr_x[0:pad, :] = prev_ref[0, :, 0:D_SSM]
        scr_bc[0:pad, :] = prev_ref[0, :, D_SSM:CONV_DIM]
        s2d[...] = h0_ref[0]
    else:
        s2d = s_ref

        @pl.when(pl.program_id(0) == 0)
        def _():
            scr_x[0:pad, :] = jnp.zeros((pad, D_SSM), F32)
            scr_bc[0:pad, :] = jnp.zeros((pad, BC_DIM), F32)
            s_ref[...] = jnp.zeros(s_ref.shape, F32)

    scr_x[pad:pad + rows, :] = xs_ref[...]
    scr_bc[pad:pad + rows, :] = bc_ref[...]
    xc = cbias_ref[:, 0:D_SSM]
    bcc = cbias_ref[:, D_SSM:CONV_DIM]
    for tap in range(SSM_CONV):
        off = pad - (SSM_CONV - 1) + tap
        xc = xc + scr_x[off:off + rows, :] * cw_ref[tap:tap + 1, 0:D_SSM]
        bcc = bcc + scr_bc[off:off + rows, :] * cw_ref[tap:tap + 1, D_SSM:CONV_DIM]
    if not has_state:
        scr_x[0:pad, :] = scr_x[rows:rows + pad, :]
        scr_bc[0:pad, :] = scr_bc[rows:rows + pad, :]
    xs_act = _silu(xc)
    bc_act = _silu(bcc)
    dt_in = dt_ref[...] + dtb_ref[...]
    dt = jnp.maximum(dt_in, 0.0) + jnp.log1p(jnp.exp(-jnp.abs(dt_in)))
    a_row = -jnp.exp(alog_ref[...])
    a_seg = -jnp.exp(alogs_ref[...])
    nbm = SSM_GROUPS * SSM_STATE
    for c in range(nchunk):
        rs = slice(c * lc, (c + 1) * lc)
        y = _ssd_chunk(xs_act[rs], bc_act[rs, 0:nbm], bc_act[rs, nbm:2 * nbm], dt[rs],
                       z_ref[rs, :], s2d, a_row, a_seg, dskip_ref[...], gn_ref[...],
                       e3_ref[...], es3_ref[...], t3_ref[...], ones3_ref[...], lc)
        y_ref[rs, :] = y.astype(y_ref.dtype)


def _ssd_constants(lc):
    seg_w = SSM_HEADS * lc
    head_of_lane = np.arange(D_SSM) // SSM_HEAD_DIM
    e = (np.arange(LANES)[:, None] == head_of_lane[None, :]).astype(np.float32)
    es = (np.arange(LANES)[:, None] == (np.arange(seg_w) // lc)[None, :]).astype(np.float32)
    tri = np.tril(np.ones((lc, lc), np.float32))
    return (jnp.asarray(np.concatenate([e] * 3, axis=0), BF16),
            jnp.asarray(np.concatenate([es] * 3, axis=0), BF16),
            jnp.asarray(np.concatenate([tri] * 3, axis=1), BF16),
            jnp.ones((3 * lc, SSM_STATE), BF16))


def _ssd(xs, bc, dt, z, conv_w, conv_b, dt_bias, a_log, d_skip, gn, lc, nchunk,
         conv_prev=None, h0=None):
    m = xs.shape[0]
    rows = lc * nchunk
    has_state = h0 is not None
    seg_w = SSM_HEADS * lc
    e3, es3, t3, ones3 = _ssd_constants(lc)
    dtb = jnp.zeros((1, LANES), F32).at[0, :SSM_HEADS].set(dt_bias)
    alog_x = jnp.repeat(a_log, SSM_HEAD_DIM)[None, :]
    alog_s = jnp.repeat(a_log, lc)[None, :]
    dskip_x = jnp.repeat(d_skip, SSM_HEAD_DIM)[None, :]

    def const(shape):
        return pl.BlockSpec(shape, lambda i: (0,) * len(shape))

    def rowblk(width):
        return pl.BlockSpec((rows, width), lambda i: (i, 0))

    in_specs = [rowblk(D_SSM), rowblk(BC_DIM), rowblk(LANES), rowblk(D_SSM)]
    args = [xs, bc, dt, z]
    if has_state:
        nb = h0.shape[0]
        in_specs += [pl.BlockSpec((1, SUBLANES, CONV_DIM), lambda i: (i, 0, 0)),
                     pl.BlockSpec((1, D_SSM, SSM_STATE), lambda i: (i, 0, 0))]
        args += [conv_prev, h0]
        s_shape = jax.ShapeDtypeStruct((nb, D_SSM, SSM_STATE), F32)
        s_spec = pl.BlockSpec((1, D_SSM, SSM_STATE), lambda i: (i, 0, 0))
    else:
        s_shape = jax.ShapeDtypeStruct((D_SSM, SSM_STATE), F32)
        s_spec = const((D_SSM, SSM_STATE))
    in_specs += [const((SSM_CONV, CONV_DIM)), const((1, CONV_DIM)), const((1, LANES)),
                 const((1, D_SSM)), const((1, seg_w)), const((1, D_SSM)), const((1, D_SSM)),
                 const(e3.shape), const(es3.shape), const(t3.shape), const(ones3.shape)]
    args += [conv_w, conv_b[None, :], dtb, alog_x, alog_s, dskip_x, gn[None, :], e3, es3, t3, ones3]
    return pl.pallas_call(
        functools.partial(_ssd_kernel, lc=lc, nchunk=nchunk, has_state=has_state),
        grid=(m // rows,),
        in_specs=in_specs,
        out_specs=(rowblk(D_SSM), s_spec),
        out_shape=(jax.ShapeDtypeStruct((m, D_SSM), BF16), s_shape),
        scratch_shapes=[pltpu.VMEM((rows + SUBLANES, D_SSM), F32),
                        pltpu.VMEM((rows + SUBLANES, BC_DIM), F32)],
        compiler_params=_cparams(("arbitrary",)),
        name="ssd_sample" if has_state else "ssd_prompt",
    )(*args)


def _out_proj_kernel(o_ref, y_ref, x_ref, w_ref, g_ref, x2_ref, hf_ref):
    x2 = (x_ref[...]
          + jnp.dot(o_ref[...], w_ref[0:D_ATT, :], preferred_element_type=F32)
          + jnp.dot(y_ref[...], w_ref[D_ATT:D_ATT + D_SSM, :], preferred_element_type=F32))
    x2_ref[...] = x2
    hf_ref[...] = _rms(x2, g_ref[...]).astype(hf_ref.dtype)


def _out_proj(o, y, x, w, g, tm):
    m = x.shape[0]
    return pl.pallas_call(
        _out_proj_kernel,
        grid=(m // tm,),
        in_specs=[
            pl.BlockSpec((tm, D_ATT), lambda i: (i, 0)),
            pl.BlockSpec((tm, D_SSM), lambda i: (i, 0)),
            pl.BlockSpec((tm, D_MODEL), lambda i: (i, 0)),
            pl.BlockSpec((D_ATT + D_SSM, D_MODEL), lambda i: (0, 0)),
            pl.BlockSpec((1, D_MODEL), lambda i: (0, 0)),
        ],
        out_specs=(pl.BlockSpec((tm, D_MODEL), lambda i: (i, 0)),
                   pl.BlockSpec((tm, D_MODEL), lambda i: (i, 0))),
        out_shape=(jax.ShapeDtypeStruct((m, D_MODEL), F32),
                   jax.ShapeDtypeStruct((m, D_MODEL), BF16)),
        compiler_params=_cparams(("arbitrary",)),
        name="out_proj",
    )(o, y, x, w, g)


def _ffn_kernel(*refs, tm, seq, final_norm):
    if seq is None:
        (hf_ref, x2_ref, wg_ref, wu_ref, wd_ref, cw_ref, cb_ref, gfin_ref,
         out_ref, gl_ref, g_scr, carry_scr) = refs
    else:
        (hf_ref, x2_ref, wg_ref, wu_ref, wd_ref, cw_ref, cb_ref, gfin_ref, ov1_ref, ov2_ref,
         out_ref, gl_ref, g_scr) = refs
    i = pl.program_id(0)
    f = pl.program_id(1)
    pad = SUBLANES
    hf = hf_ref[...]
    gate = jnp.dot(hf, wg_ref[...], preferred_element_type=F32)
    up = jnp.dot(hf, wu_ref[...], preferred_element_type=F32)
    g_scr[pad:pad + tm, :] = gate
    if seq is None:
        @pl.when(i == 0)
        def _():
            g_scr[0:pad, :] = jnp.zeros((pad, gate.shape[1]), F32)

        @pl.when(i > 0)
        def _():
            g_scr[0:pad, :] = carry_scr[f]

        carry_scr[f] = g_scr[tm:tm + pad, :]
        gl_ref[...] = g_scr[tm:tm + pad, :]
        g1 = g_scr[pad - 1:pad - 1 + tm, :]
        g2 = g_scr[pad - 2:pad - 2 + tm, :]
    else:
        g_scr[0:pad, :] = jnp.zeros((pad, gate.shape[1]), F32)
        gl_ref[...] = gate
        pos = lax.broadcasted_iota(jnp.int32, gate.shape, 0) % seq
        g1 = jnp.where(pos == 0, ov1_ref[...], g_scr[pad - 1:pad - 1 + tm, :])
        g2 = jnp.where(pos < 2, ov2_ref[...], g_scr[pad - 2:pad - 2 + tm, :])
    conv = cb_ref[...] + g2 * cw_ref[0:1, :] + g1 * cw_ref[1:2, :] + gate * cw_ref[2:3, :]
    act = (_silu(conv) * up).astype(BF16)
    contrib = jnp.dot(act, wd_ref[...], preferred_element_type=F32)

    @pl.when(f == 0)
    def _():
        out_ref[...] = x2_ref[...] + contrib

    @pl.when(f > 0)
    def _():
        out_ref[...] += contrib

    if final_norm:
        @pl.when(f == pl.num_programs(1) - 1)
        def _():
            out_ref[...] = _rms(out_ref[...], gfin_ref[...])


def _ffn(hf, x2, wg, wu, wd, cw, cb, gfin, tm, tf, final_norm, seq=None, ov1=None, ov2=None):
    m = hf.shape[0]
    nf = D_FF // tf
    in_specs = [
        pl.BlockSpec((tm, D_MODEL), lambda i, f: (i, 0)),
        pl.BlockSpec((tm, D_MODEL), lambda i, f: (i, 0)),
        pl.BlockSpec((D_MODEL, tf), lambda i, f: (0, f)),
        pl.BlockSpec((D_MODEL, tf), lambda i, f: (0, f)),
        pl.BlockSpec((tf, D_MODEL), lambda i, f: (f, 0)),
        pl.BlockSpec((FFN_CONV, tf), lambda i, f: (0, f)),
        pl.BlockSpec((1, tf), lambda i, f: (0, f)),
        pl.BlockSpec((1, D_MODEL), lambda i, f: (0, 0)),
    ]
    args = [hf, x2, wg, wu, wd, cw, cb[None, :], gfin[None, :]]
    scratch = [pltpu.VMEM((tm + SUBLANES, tf), F32)]
    if seq is None:
        gl_shape = jax.ShapeDtypeStruct((m // tm * SUBLANES, D_FF), F32)
        gl_spec = pl.BlockSpec((SUBLANES, tf), lambda i, f: (i, f))
        scratch.append(pltpu.VMEM((nf, SUBLANES, tf), F32))
    else:
        in_specs += [pl.BlockSpec((tm, tf), lambda i, f: (i, f))] * 2
        args += [ov1, ov2]
        gl_shape = jax.ShapeDtypeStruct((m, D_FF), F32)
        gl_spec = pl.BlockSpec((tm, tf), lambda i, f: (i, f))
    return pl.pallas_call(
        functools.partial(_ffn_kernel, tm=tm, seq=seq, final_norm=final_norm),
        grid=(m // tm, nf),
        in_specs=in_specs,
        out_specs=(pl.BlockSpec((tm, D_MODEL), lambda i, f: (i, 0)), gl_spec),
        out_shape=(jax.ShapeDtypeStruct((m, D_MODEL), F32), gl_shape),
        scratch_shapes=scratch,
        compiler_params=_cparams(("arbitrary", "arbitrary")),
        name="ffn_sample" if seq is not None else "ffn_prompt",
    )(*args)


def _layer(x, w, lam_init, final_norm, gfin, *, batch, seq, state=None):
    m = x.shape[0]
    lamv = jnp.stack([w["lambda_q1"], w["lambda_k1"], w["lambda_q2"], w["lambda_k2"]])
    tm = min(512, m)
    q, k, v, z, xs, bc, dt = _in_proj(x, w["norm_mix_g"][None, :], w["w_in_main"], w["w_in_dt"], tm)

    if state is None:
        slopes = jnp.broadcast_to(
            jnp.asarray(2.0 ** (-8.0 * np.arange(1, ATT_HEADS + 1) / ATT_HEADS), F32)[:, None, None],
            (ATT_HEADS, 1, LANES))
        o = _attn_prompt(q, k, v, slopes, lamv, w["attn_subln_g"][None, :], lam_init, tile=256)
        y, s_new = _ssd(xs, bc, dt, z, w["conv_w"], w["conv_b"], w["dt_bias"], w["a_log"],
                        w["d_skip"], w["ssm_norm_g"], lc=CHUNK, nchunk=2)
        conv_new = jnp.concatenate([xs[m - (SSM_CONV - 1):], bc[m - (SSM_CONV - 1):]], axis=-1)[None]
        s_new = s_new[None]
    else:
        cache_k, cache_v, conv_prev, ssm_prev, ffn_prev = state
        past = cache_k.shape[1]
        o = _attn_sample(q, k, v, cache_k.reshape(batch, past, D_ATT), cache_v.reshape(batch, past, D_ATT),
                         lamv, w["attn_subln_g"][None, :], lam_init, seq)
        prev8 = jnp.pad(conv_prev, ((0, 0), (SUBLANES - (SSM_CONV - 1), 0), (0, 0)))
        y, s_new = _ssd(xs, bc, dt, z, w["conv_w"], w["conv_b"], w["dt_bias"], w["a_log"],
                        w["d_skip"], w["ssm_norm_g"], lc=seq, nchunk=1, conv_prev=prev8,
                        h0=ssm_prev.reshape(batch, D_SSM, SSM_STATE))
        conv_new = jnp.concatenate([xs.reshape(batch, seq, D_SSM)[:, seq - (SSM_CONV - 1):],
                                    bc.reshape(batch, seq, BC_DIM)[:, seq - (SSM_CONV - 1):]], axis=-1)

    x2, hf = _out_proj(o, y, x, w["w_out"], w["norm_ffn_g"][None, :], tm=min(256, m))

    ffn_args = (hf, x2, w["w_gate"], w["w_up"], w["w_down"], w["ffn_conv_w"], w["ffn_conv_b"], gfin)
    if state is None:
        x3, gl = _ffn(*ffn_args, tm=tm, tf=512, final_norm=final_norm)
        ffn_new = gl[None, gl.shape[0] - (FFN_CONV - 1):]
    else:
        zeros = jnp.zeros((batch, seq, D_FF), F32)
        ov1 = zeros.at[:, 0].set(ffn_prev[:, 1]).reshape(m, D_FF)
        ov2 = zeros.at[:, 0].set(ffn_prev[:, 0]).at[:, 1].set(ffn_prev[:, 1]).reshape(m, D_FF)
        x3, gl = _ffn(*ffn_args, tm=tm, tf=512, final_norm=final_norm, seq=seq, ov1=ov1, ov2=ov2)
        ffn_new = gl.reshape(batch, seq, D_FF)[:, seq - (FFN_CONV - 1):]
    k_new = k.reshape(batch, seq, ATT_HEADS, 2 * QK_DIM)
    v_new = v.reshape(batch, seq, ATT_HEADS, V_DIM)
    s_new = s_new.reshape(batch, SSM_HEADS, SSM_HEAD_DIM, SSM_STATE)
    return x3, k_new, v_new, conv_new, s_new, ffn_new


def kernel(x_prompt, x_sample, cache_k, cache_v, state_ssm_conv, state_ssm, state_ffn_conv, norm_mix_g, w_in, lambda_q1, lambda_k1, lambda_q2, lambda_k2, attn_subln_g, conv_w, conv_b, dt_bias, a_log, d_skip, ssm_norm_g, w_out, norm_ffn_g, w_gate, w_up, ffn_conv_w, ffn_conv_b, w_down, norm_final_g):
    depth = w_in.shape[0]
    pb, pl_len, _ = x_prompt.shape
    sb, sl_len, _ = x_sample.shape
    assert pb == 1 and pl_len % CHUNK == 0
    xp = x_prompt.reshape(pb * pl_len, D_MODEL)
    xs = x_sample.reshape(sb * sl_len, D_MODEL)
    n_main = 2 * D_ATT + D_ATT + D_SSM + CONV_DIM
    outs_p, outs_s = [], []
    for layer in range(depth):
        lam_init = 0.8 - 0.6 * math.exp(-0.3 * layer)
        w_in_l = w_in[layer].astype(BF16)
        w = dict(
            norm_mix_g=norm_mix_g[layer],
            w_in_main=w_in_l[:, :n_main],
            w_in_dt=jnp.pad(w_in_l[:, n_main:], ((0, 0), (0, LANES - SSM_HEADS))),
            lambda_q1=lambda_q1[layer], lambda_k1=lambda_k1[layer],
            lambda_q2=lambda_q2[layer], lambda_k2=lambda_k2[layer],
            attn_subln_g=attn_subln_g[layer], conv_w=conv_w[layer], conv_b=conv_b[layer],
            dt_bias=dt_bias[layer], a_log=a_log[layer], d_skip=d_skip[layer],
            ssm_norm_g=ssm_norm_g[layer], w_out=w_out[layer].astype(BF16),
            norm_ffn_g=norm_ffn_g[layer], w_gate=w_gate[layer].astype(BF16),
            w_up=w_up[layer].astype(BF16), ffn_conv_w=ffn_conv_w[layer],
            ffn_conv_b=ffn_conv_b[layer], w_down=w_down[layer].astype(BF16),
        )
        last = layer == depth - 1
        xp, *new_p = _layer(xp, w, lam_init, last, norm_final_g, batch=pb, seq=pl_len)
        xs, *new_s = _layer(xs, w, lam_init, last, norm_final_g, batch=sb, seq=sl_len,
                            state=(cache_k[layer], cache_v[layer], state_ssm_conv[layer],
                                   state_ssm[layer], state_ffn_conv[layer]))
        outs_p.append(new_p)
        outs_s.append(new_s)
    stack = lambda outs, idx: jnp.stack([o[idx] for o in outs])
    return (xp.reshape(pb, pl_len, D_MODEL), xs.reshape(sb, sl_len, D_MODEL),
            *[stack(outs_p, idx) for idx in range(5)],
            *[stack(outs_s, idx) for idx in range(5)])
```

```python
import functools
import math

import jax
import jax.numpy as jnp
import numpy as np
from jax import lax
from jax.experimental import pallas as pl
from jax.experimental.pallas import tpu as pltpu

F32 = jnp.float32
BF16 = jnp.bfloat16

D_MODEL = 2048
CHUNK = 64
ATT_HEADS = 8
QK_DIM = 64
V_DIM = 128
D_ATT = ATT_HEADS * V_DIM
SSM_HEADS = 16
SSM_HEAD_DIM = 64
D_SSM = SSM_HEADS * SSM_HEAD_DIM
SSM_GROUPS = 2
SSM_STATE = 128
SSM_CONV = 4
BC_DIM = 2 * SSM_GROUPS * SSM_STATE
CONV_DIM = D_SSM + BC_DIM
D_FF = 5632
FFN_CONV = 3
EPS = 1e-6
LOG2E = math.log2(math.e)
Q_SCALE = QK_DIM ** -0.5 * LOG2E
LANES = 128
SUBLANES = 8
VMEM_LIMIT = 52 * 1024 * 1024

_NT = (((1,), (1,)), ((), ()))
_TN = (((0,), (0,)), ((), ()))


def _cparams(sem):
    return pltpu.CompilerParams(dimension_semantics=sem, vmem_limit_bytes=VMEM_LIMIT)


def _silu(x):
    return x * jax.nn.sigmoid(x)


def _rms(x, g):
    return x * lax.rsqrt(jnp.mean(x * x, axis=-1, keepdims=True) + EPS) * g


_PROJ_TN = 512
_PROJ_SEGS = ((0, 2), (2, 4), (4, 6), (6, 8), (8, 10), (10, 11))


def _in_proj_kernel(x_ref, g_ref, w_ref, wdt_ref,
                    q_ref, k_ref, v_ref, z_ref, xs_ref, bc_ref, dt_ref, h_scr):
    j = pl.program_id(1)

    @pl.when(j == 0)
    def _():
        hb = _rms(x_ref[...], g_ref[...]).astype(BF16)
        h_scr[...] = hb
        dt_ref[...] = jnp.dot(hb, wdt_ref[...], preferred_element_type=F32)

    res = jnp.dot(h_scr[...], w_ref[...], preferred_element_type=F32)
    outs = (q_ref, k_ref, v_ref, z_ref, xs_ref, bc_ref)
    for (lo, hi), ref in zip(_PROJ_SEGS, outs):
        @pl.when((j >= lo) & (j < hi))
        def _(ref=ref):
            val = res * Q_SCALE if ref is q_ref else res
            ref[...] = val.astype(ref.dtype)


def _in_proj(x, g, w_main, w_dt, tm):
    m = x.shape[0]
    tn = _PROJ_TN
    nj = w_main.shape[1] // tn

    def seg_spec(lo, hi):
        return pl.BlockSpec((tm, tn), lambda i, j: (i, jnp.clip(j - lo, 0, hi - lo - 1)))

    out_shape = (
        jax.ShapeDtypeStruct((m, D_ATT), BF16),
        jax.ShapeDtypeStruct((m, D_ATT), F32),
        jax.ShapeDtypeStruct((m, D_ATT), F32),
        jax.ShapeDtypeStruct((m, D_SSM), F32),
        jax.ShapeDtypeStruct((m, D_SSM), F32),
        jax.ShapeDtypeStruct((m, BC_DIM), F32),
        jax.ShapeDtypeStruct((m, LANES), F32),
    )
    out_specs = tuple(seg_spec(lo, hi) for lo, hi in _PROJ_SEGS) + (
        pl.BlockSpec((tm, LANES), lambda i, j: (i, 0)),)
    return pl.pallas_call(
        _in_proj_kernel,
        grid=(m // tm, nj),
        in_specs=[
            pl.BlockSpec((tm, D_MODEL), lambda i, j: (i, 0)),
            pl.BlockSpec((1, D_MODEL), lambda i, j: (0, 0)),
            pl.BlockSpec((D_MODEL, tn), lambda i, j: (0, j)),
            pl.BlockSpec((D_MODEL, LANES), lambda i, j: (0, 0)),
        ],
        out_specs=out_specs,
        out_shape=out_shape,
        scratch_shapes=[pltpu.VMEM((tm, D_MODEL), BF16)],
        compiler_params=_cparams(("arbitrary", "arbitrary")),
        name="in_proj",
    )(x, g, w_main, w_dt)


def _lambda_value(lamv_ref, lam_init):
    lv = lamv_ref[...]
    s1 = jnp.sum(lv[0:1] * lv[1:2], axis=-1, keepdims=True)
    s2 = jnp.sum(lv[2:3] * lv[3:4], axis=-1, keepdims=True)
    return jnp.exp(s1) - jnp.exp(s2) + lam_init


def _split_maps(q):
    lane = lax.broadcasted_iota(jnp.int32, q.shape, 1)
    zero = jnp.zeros_like(q)
    return jnp.where(lane < QK_DIM, q, zero), jnp.where(lane >= QK_DIM, q, zero)


_VT_CHUNK = 512
_VT_PAD = 16


def _attn_prompt_kernel(q_ref, k_ref, v_ref, slope_ref, lamv_ref, g_ref, o_ref,
                        k_scr, vt_scr, acc_scr, t0_scr, t1_scr, p0_scr, p1_scr,
                        *, tile, lam_init):
    qi = pl.program_id(1)
    length = k_ref.shape[0]
    width = 2 * tile
    w = slope_ref[0][:, :1] * LOG2E

    @pl.when(qi == 0)
    def _():
        k_scr[:, 0:V_DIM] = k_ref[...].astype(BF16)
        lane = lax.broadcasted_iota(jnp.int32, (length, LANES), 1)
        koff = lax.broadcasted_iota(jnp.int32, (length, LANES), 0) % tile
        k_scr[:, V_DIM:V_DIM + LANES] = jnp.where(lane < 3, koff, 0).astype(F32).astype(BF16)
        per = _VT_CHUNK // tile
        extra = (lax.broadcasted_iota(jnp.int32, (_VT_PAD, tile), 0) == 0).astype(F32).astype(BF16)
        for c in range(length // _VT_CHUNK):
            vt = v_ref[c * _VT_CHUNK:(c + 1) * _VT_CHUNK, :].T.astype(BF16)
            for s in range(per):
                vt_scr[c * per + s, 0:V_DIM, :] = vt[:, s * tile:(s + 1) * tile]
                vt_scr[c * per + s, V_DIM:V_DIM + _VT_PAD, :] = extra

    lam = _lambda_value(lamv_ref, lam_init)
    q1, q2 = _split_maps(q_ref[...])
    q_t = jnp.concatenate([q1, q2], axis=0).astype(F32).T.astype(BF16)
    w_hi, w_mid, w_lo = (piece.astype(F32) for piece in _split3(w))
    brow = lax.broadcasted_iota(jnp.int32, (LANES, width), 0)
    q_bias = jnp.where(brow == 0, w_hi, jnp.where(brow == 1, w_mid, jnp.where(brow == 2, w_lo, 0.0)))
    q_aug = jnp.concatenate([q_t, q_bias.astype(BF16)], axis=0)

    key = lax.broadcasted_iota(jnp.int32, (tile, width), 0)
    qry = lax.broadcasted_iota(jnp.int32, (tile, width), 1) % tile
    rel = (qry - jnp.abs(qry - key)).astype(F32)
    kd = k_scr[pl.ds(pl.multiple_of(qi * tile, tile), tile), 0:V_DIM]
    t = jnp.dot(kd, q_t, preferred_element_type=F32) + w * rel
    t = jnp.where((key // CHUNK) <= (qry // CHUNK), t, -jnp.inf)
    m0 = jnp.max(t, axis=0, keepdims=True)
    acc_scr[...] = jnp.dot(vt_scr[qi], jnp.exp2(t - m0).astype(BF16), preferred_element_type=F32)

    t_slots = (t0_scr, t1_scr)
    p_slots = (p0_scr, p1_scr)

    def scores(j):
        k0 = pl.multiple_of(jnp.minimum(j, qi) * tile, tile)
        return jnp.dot(k_scr[pl.ds(k0, tile), :], q_aug, preferred_element_type=F32)

    def step(j, slot, alpha_prev, m_old):
        pv = jnp.dot(vt_scr[jnp.clip(j - 1, 0, qi)], p_slots[1 - slot][...],
                     preferred_element_type=F32)
        acc_scr[...] = alpha_prev * acc_scr[...] + pv
        t_slots[1 - slot][...] = scores(j + 1)
        off = jnp.where(j < qi, -w * ((qi - j) * tile).astype(F32), -jnp.inf)
        t_cur = t_slots[slot][...]
        m_new = jnp.maximum(m_old, jnp.max(t_cur, axis=0, keepdims=True) + off)
        p_slots[slot][...] = jnp.exp2(t_cur + (off - m_new)).astype(BF16)
        return jnp.exp2(m_old - m_new), m_new

    def body(jj, carry):
        carry = step(2 * jj, 0, *carry)
        return step(2 * jj + 1, 1, *carry)

    t0_scr[...] = scores(0)
    p1_scr[...] = jnp.zeros(p1_scr.shape, BF16)
    npairs = lax.shift_right_logical(qi + 1, 1)
    alpha_last, _ = lax.fori_loop(0, npairs, body, (jnp.ones((1, width), F32), m0))
    pv = jnp.dot(vt_scr[jnp.clip(2 * npairs - 1, 0, qi)], p1_scr[...], preferred_element_type=F32)
    acc = alpha_last * acc_scr[...] + pv

    on = acc[0:V_DIM] / acc[V_DIM:V_DIM + 1]
    o_t = on[:, :tile] - lam * on[:, tile:]
    o_t = o_t * lax.rsqrt(jnp.mean(o_t * o_t, axis=0, keepdims=True) + EPS)
    o_ref[...] = (o_t.T * g_ref[...] * (1.0 - lam_init)).astype(o_ref.dtype)


def _attn_prompt(q, k, v, slopes, lamv, g, lam_init, tile):
    length = q.shape[0]
    return pl.pallas_call(
        functools.partial(_attn_prompt_kernel, tile=tile, lam_init=lam_init),
        grid=(ATT_HEADS, length // tile),
        in_specs=[
            pl.BlockSpec((tile, V_DIM), lambda h, i: (i, h)),
            pl.BlockSpec((length, V_DIM), lambda h, i: (0, h)),
            pl.BlockSpec((length, V_DIM), lambda h, i: (0, h)),
            pl.BlockSpec((1, 1, LANES), lambda h, i: (h, 0, 0)),
            pl.BlockSpec((4, QK_DIM), lambda h, i: (0, 0)),
            pl.BlockSpec((1, V_DIM), lambda h, i: (0, 0)),
        ],
        out_specs=pl.BlockSpec((tile, V_DIM), lambda h, i: (i, h)),
        out_shape=jax.ShapeDtypeStruct((length, D_ATT), BF16),
        scratch_shapes=[
            pltpu.VMEM((length, V_DIM + LANES), BF16),
            pltpu.VMEM((length // tile, V_DIM + _VT_PAD, tile), BF16),
            pltpu.VMEM((V_DIM + _VT_PAD, 2 * tile), F32),
            pltpu.VMEM((tile, 2 * tile), F32),
            pltpu.VMEM((tile, 2 * tile), F32),
            pltpu.VMEM((tile, 2 * tile), BF16),
            pltpu.VMEM((tile, 2 * tile), BF16),
        ],
        compiler_params=_cparams(("arbitrary", "arbitrary")),
        name="attn_prompt",
    )(q, k, v, slopes, lamv, g)


def _attn_sample_kernel(q_ref, kn_ref, vn_ref, ck_ref, cv_ref, lamv_ref, g_ref, o_ref,
                        *, seq, past, lam_init):
    lam = _lambda_value(lamv_ref, lam_init)
    r_c = (lax.broadcasted_iota(jnp.int32, (2 * seq, past), 0) % seq
           - lax.broadcasted_iota(jnp.int32, (2 * seq, past), 1) + past).astype(F32)
    r_n = jnp.abs(lax.broadcasted_iota(jnp.int32, (2 * seq, seq), 0) % seq
                  - lax.broadcasted_iota(jnp.int32, (2 * seq, seq), 1)).astype(F32)
    for h in range(ATT_HEADS):
        w = 2.0 ** (-8.0 * (h + 1) / ATT_HEADS) * LOG2E
        sl = slice(h * V_DIM, (h + 1) * V_DIM)
        q1, q2 = _split_maps(q_ref[:, sl])
        qq = jnp.concatenate([q1, q2], axis=0)
        kc = ck_ref[0, pl.ds(h, past, stride=ATT_HEADS), :].astype(BF16)
        vc = cv_ref[0, pl.ds(h, past, stride=ATT_HEADS), :].astype(BF16)
        kn = kn_ref[:, sl].astype(BF16)
        vn = vn_ref[:, sl].astype(BF16)
        sc = lax.dot_general(qq, kc, _NT, preferred_element_type=F32) - w * r_c
        sn = lax.dot_general(qq, kn, _NT, preferred_element_type=F32) - w * r_n
        m = jnp.maximum(jnp.max(sc, axis=-1, keepdims=True), jnp.max(sn, axis=-1, keepdims=True))
        pc = jnp.exp2(sc - m)
        pn = jnp.exp2(sn - m)
        l = jnp.sum(pc, axis=-1, keepdims=True) + jnp.sum(pn, axis=-1, keepdims=True)
        acc = (jnp.dot(pc.astype(BF16), vc, preferred_element_type=F32)
               + jnp.dot(pn.astype(BF16), vn, preferred_element_type=F32))
        on = acc / l
        o = on[:seq] - lam * on[seq:]
        o_ref[:, sl] = (_rms(o, g_ref[...]) * (1.0 - lam_init)).astype(o_ref.dtype)


def _attn_sample(q, k_new, v_new, cache_k, cache_v, lamv, g, lam_init, seq):
    nb, past = cache_k.shape[0], cache_k.shape[1] // ATT_HEADS
    row_spec = pl.BlockSpec((seq, D_ATT), lambda b: (b, 0))
    cache_spec = pl.BlockSpec((1, past * ATT_HEADS, V_DIM), lambda b: (b, 0, 0))
    return pl.pallas_call(
        functools.partial(_attn_sample_kernel, seq=seq, past=past, lam_init=lam_init),
        grid=(nb,),
        in_specs=[row_spec, row_spec, row_spec, cache_spec, cache_spec,
                  pl.BlockSpec((4, QK_DIM), lambda b: (0, 0)),
                  pl.BlockSpec((1, V_DIM), lambda b: (0, 0))],
        out_specs=row_spec,
        out_shape=jax.ShapeDtypeStruct((nb * seq, D_ATT), BF16),
        compiler_params=_cparams(("arbitrary",)),
        name="attn_sample",
    )(q, k_new, v_new, cache_k, cache_v, lamv, g)


def _split3(x):
    hi = x.astype(BF16)
    r = x - hi.astype(F32)
    mid = r.astype(BF16)
    lo = (r - mid.astype(F32)).astype(BF16)
    return hi, mid, lo


def _ssd_chunk(xs, bm, cm, dt, z, s_ref, a_row, a_seg, dskip, gn, e3, es3, t3, ones3, lc):
    seg_w = SSM_HEADS * lc
    half = D_SSM // SSM_GROUPS
    d3 = jnp.concatenate(_split3(dt), axis=1)
    dt_x = jnp.dot(d3, e3, preferred_element_type=F32)
    la = dt_x * a_row
    if seg_w == D_SSM:
        la_s = la
    else:
        la_s = jnp.dot(d3, es3, preferred_element_type=F32) * a_seg
    la3 = jnp.concatenate(_split3(la), axis=0)
    acol = jnp.dot(t3, la3, preferred_element_type=F32)
    t_idx = lax.broadcasted_iota(jnp.int32, (lc, seg_w), 0)
    s_idx = lax.broadcasted_iota(jnp.int32, (lc, seg_w), 1) % lc
    w3 = jnp.concatenate(_split3(jnp.where(t_idx > s_idx, la_s, 0.0)), axis=0)
    seg = jnp.dot(t3, w3, preferred_element_type=F32)
    decay = jnp.where(t_idx >= s_idx, jnp.exp(seg), 0.0)

    cb16 = cm.astype(BF16)
    bb16 = bm.astype(BF16)
    hpg = SSM_HEADS // SSM_GROUPS
    cbs = []
    for g in range(SSM_GROUPS):
        gs = slice(g * SSM_STATE, (g + 1) * SSM_STATE)
        b_rep = jnp.concatenate([bb16[:, gs]] * hpg, axis=0)
        cbs.append(lax.dot_general(cb16[:, gs], b_rep, _NT, preferred_element_type=F32))
    mmat = (jnp.concatenate(cbs, axis=1) * decay).astype(BF16)

    xdt = xs * dt_x
    xdt16 = xdt.astype(BF16)
    hk = (2 * LANES) // lc
    wd = hk * SSM_HEAD_DIM
    blk = (lax.broadcasted_iota(jnp.int32, (hk * lc, wd), 0) // lc
           == lax.broadcasted_iota(jnp.int32, (hk * lc, wd), 1) // SSM_HEAD_DIM)
    parts = []
    for i in range(SSM_HEADS // hk):
        xd = xdt16[:, i * wd:(i + 1) * wd]
        bd = jnp.where(blk, jnp.concatenate([xd] * hk, axis=0), jnp.zeros((), BF16))
        parts.append(jnp.dot(mmat[:, i * hk * lc:(i + 1) * hk * lc], bd, preferred_element_type=F32))
    y_intra = jnp.concatenate(parts, axis=1) if len(parts) > 1 else parts[0]

    yi = []
    for g in range(SSM_GROUPS):
        sg = s_ref[g * half:(g + 1) * half, :].astype(BF16)
        yi.append(lax.dot_general(cb16[:, g * SSM_STATE:(g + 1) * SSM_STATE], sg, _NT,
                                  preferred_element_type=F32))
    y_inter = jnp.exp(acol) * jnp.concatenate(yi, axis=1)

    dec_end = jnp.exp(acol[lc - 1:lc, :] - acol)
    xd_end = (xdt * dec_end).astype(BF16)
    acl = lax.dot_general(la3, ones3, _TN, preferred_element_type=F32)
    for g in range(SSM_GROUPS):
        rows = slice(g * half, (g + 1) * half)
        upd = lax.dot_general(xd_end[:, rows], bb16[:, g * SSM_STATE:(g + 1) * SSM_STATE], _TN,
                              preferred_element_type=F32)
        s_ref[rows, :] = jnp.exp(acl[rows, :]) * s_ref[rows, :] + upd

    y = (y_intra + y_inter + dskip * xs) * _silu(z)
    outs = []
    for g in range(SSM_GROUPS):
        cs = slice(g * half, (g + 1) * half)
        outs.append(_rms(y[:, cs], gn[:, cs]))
    return jnp.concatenate(outs, axis=1)


def _ssd_kernel(*refs, lc, nchunk, has_state):
    if has_state:
        (xs_ref, bc_ref, dt_ref, z_ref, prev_ref, h0_ref, cw_ref, cbias_ref, dtb_ref, alog_ref,
         alogs_ref, dskip_ref, gn_ref, e3_ref, es3_ref, t3_ref, ones3_ref,
         y_ref, s_ref, scr_x, scr_bc) = refs
    else:
        (xs_ref, bc_ref, dt_ref, z_ref, cw_ref, cbias_ref, dtb_ref, alog_ref,
         alogs_ref, dskip_ref, gn_ref, e3_ref, es3_ref, t3_ref, ones3_ref,
         y_ref, s_ref, scr_x, scr_bc) = refs
    rows = lc * nchunk
    pad = SUBLANES
    if has_state:
        s2d = s_ref.at[0]
        scr_x[0:pad, :] = prev_ref[0, :, 0:D_SSM]
        scr_bc[0:pad, :] = prev_ref[0, :, D_SSM:CONV_DIM]
        s2d[...] = h0_ref[0]
    else:
        s2d = s_ref

        @pl.when(pl.program_id(0) == 0)
        def _():
            scr_x[0:pad, :] = jnp.zeros((pad, D_SSM), F32)
            scr_bc[0:pad, :] = jnp.zeros((pad, BC_DIM), F32)
            s_ref[...] = jnp.zeros(s_ref.shape, F32)

    scr_x[pad:pad + rows, :] = xs_ref[...]
    scr_bc[pad:pad + rows, :] = bc_ref[...]
    xc = cbias_ref[:, 0:D_SSM]
    bcc = cbias_ref[:, D_SSM:CONV_DIM]
    for tap in range(SSM_CONV):
        off = pad - (SSM_CONV - 1) + tap
        xc = xc + scr_x[off:off + rows, :] * cw_ref[tap:tap + 1, 0:D_SSM]
        bcc = bcc + scr_bc[off:off + rows, :] * cw_ref[tap:tap + 1, D_SSM:CONV_DIM]
    if not has_state:
        scr_x[0:pad, :] = scr_x[rows:rows + pad, :]
        scr_bc[0:pad, :] = scr_bc[rows:rows + pad, :]
    xs_act = _silu(xc)
    bc_act = _silu(bcc)
    dt_in = dt_ref[...] + dtb_ref[...]
    dt = jnp.maximum(dt_in, 0.0) + jnp.log1p(jnp.exp(-jnp.abs(dt_in)))
    a_row = -jnp.exp(alog_ref[...])
    a_seg = -jnp.exp(alogs_ref[...])
    nbm = SSM_GROUPS * SSM_STATE
    for c in range(nchunk):
        rs = slice(c * lc, (c + 1) * lc)
        y = _ssd_chunk(xs_act[rs], bc_act[rs, 0:nbm], bc_act[rs, nbm:2 * nbm], dt[rs],
                       z_ref[rs, :], s2d, a_row, a_seg, dskip_ref[...], gn_ref[...],
                       e3_ref[...], es3_ref[...], t3_ref[...], ones3_ref[...], lc)
        y_ref[rs, :] = y.astype(y_ref.dtype)


def _ssd_constants(lc):
    seg_w = SSM_HEADS * lc
    head_of_lane = np.arange(D_SSM) // SSM_HEAD_DIM
    e = (np.arange(LANES)[:, None] == head_of_lane[None, :]).astype(np.float32)
    es = (np.arange(LANES)[:, None] == (np.arange(seg_w) // lc)[None, :]).astype(np.float32)
    tri = np.tril(np.ones((lc, lc), np.float32))
    return (jnp.asarray(np.concatenate([e] * 3, axis=0), BF16),
            jnp.asarray(np.concatenate([es] * 3, axis=0), BF16),
            jnp.asarray(np.concatenate([tri] * 3, axis=1), BF16),
            jnp.ones((3 * lc, SSM_STATE), BF16))


def _ssd(xs, bc, dt, z, conv_w, conv_b, dt_bias, a_log, d_skip, gn, lc, nchunk,
         conv_prev=None, h0=None):
    m = xs.shape[0]
    rows = lc * nchunk
    has_state = h0 is not None
    seg_w = SSM_HEADS * lc
    e3, es3, t3, ones3 = _ssd_constants(lc)
    dtb = jnp.zeros((1, LANES), F32).at[0, :SSM_HEADS].set(dt_bias)
    alog_x = jnp.repeat(a_log, SSM_HEAD_DIM)[None, :]
    alog_s = jnp.repeat(a_log, lc)[None, :]
    dskip_x = jnp.repeat(d_skip, SSM_HEAD_DIM)[None, :]

    def const(shape):
        return pl.BlockSpec(shape, lambda i: (0,) * len(shape))

    def rowblk(width):
        return pl.BlockSpec((rows, width), lambda i: (i, 0))

    in_specs = [rowblk(D_SSM), rowblk(BC_DIM), rowblk(LANES), rowblk(D_SSM)]
    args = [xs, bc, dt, z]
    if has_state:
        nb = h0.shape[0]
        in_specs += [pl.BlockSpec((1, SUBLANES, CONV_DIM), lambda i: (i, 0, 0)),
                     pl.BlockSpec((1, D_SSM, SSM_STATE), lambda i: (i, 0, 0))]
        args += [conv_prev, h0]
        s_shape = jax.ShapeDtypeStruct((nb, D_SSM, SSM_STATE), F32)
        s_spec = pl.BlockSpec((1, D_SSM, SSM_STATE), lambda i: (i, 0, 0))
    else:
        s_shape = jax.ShapeDtypeStruct((D_SSM, SSM_STATE), F32)
        s_spec = const((D_SSM, SSM_STATE))
    in_specs += [const((SSM_CONV, CONV_DIM)), const((1, CONV_DIM)), const((1, LANES)),
                 const((1, D_SSM)), const((1, seg_w)), const((1, D_SSM)), const((1, D_SSM)),
                 const(e3.shape), const(es3.shape), const(t3.shape), const(ones3.shape)]
    args += [conv_w, conv_b[None, :], dtb, alog_x, alog_s, dskip_x, gn[None, :], e3, es3, t3, ones3]
    return pl.pallas_call(
        functools.partial(_ssd_kernel, lc=lc, nchunk=nchunk, has_state=has_state),
        grid=(m // rows,),
        in_specs=in_specs,
        out_specs=(rowblk(D_SSM), s_spec),
        out_shape=(jax.ShapeDtypeStruct((m, D_SSM), BF16), s_shape),
        scratch_shapes=[pltpu.VMEM((rows + SUBLANES, D_SSM), F32),
                        pltpu.VMEM((rows + SUBLANES, BC_DIM), F32)],
        compiler_params=_cparams(("arbitrary",)),
        name="ssd_sample" if has_state else "ssd_prompt",
    )(*args)


def _out_proj_kernel(o_ref, y_ref, x_ref, w_ref, g_ref, x2_ref, hf_ref):
    x2 = (x_ref[...]
          + jnp.dot(o_ref[...], w_ref[0:D_ATT, :], preferred_element_type=F32)
          + jnp.dot(y_ref[...], w_ref[D_ATT:D_ATT + D_SSM, :], preferred_element_type=F32))
    x2_ref[...] = x2
    hf_ref[...] = _rms(x2, g_ref[...]).astype(hf_ref.dtype)


def _out_proj(o, y, x, w, g, tm):
    m = x.shape[0]
    return pl.pallas_call(
        _out_proj_kernel,
        grid=(m // tm,),
        in_specs=[
            pl.BlockSpec((tm, D_ATT), lambda i: (i, 0)),
            pl.BlockSpec((tm, D_SSM), lambda i: (i, 0)),
            pl.BlockSpec((tm, D_MODEL), lambda i: (i, 0)),
            pl.BlockSpec((D_ATT + D_SSM, D_MODEL), lambda i: (0, 0)),
            pl.BlockSpec((1, D_MODEL), lambda i: (0, 0)),
        ],
        out_specs=(pl.BlockSpec((tm, D_MODEL), lambda i: (i, 0)),
                   pl.BlockSpec((tm, D_MODEL), lambda i: (i, 0))),
        out_shape=(jax.ShapeDtypeStruct((m, D_MODEL), F32),
                   jax.ShapeDtypeStruct((m, D_MODEL), BF16)),
        compiler_params=_cparams(("arbitrary",)),
        name="out_proj",
    )(o, y, x, w, g)


def _ffn_kernel(*refs, tm, seq, final_norm):
    if seq is None:
        (hf_ref, x2_ref, wg_ref, wu_ref, wd_ref, cw_ref, cb_ref, gfin_ref,
         out_ref, gl_ref, g_scr, carry_scr) = refs
    else:
        (hf_ref, x2_ref, wg_ref, wu_ref, wd_ref, cw_ref, cb_ref, gfin_ref, ov1_ref, ov2_ref,
         out_ref, gl_ref, g_scr) = refs
    i = pl.program_id(0)
    f = pl.program_id(1)
    pad = SUBLANES
    hf = hf_ref[...]
    gate = jnp.dot(hf, wg_ref[...], preferred_element_type=F32)
    up = jnp.dot(hf, wu_ref[...], preferred_element_type=F32)
    g_scr[pad:pad + tm, :] = gate
    if seq is None:
        @pl.when(i == 0)
        def _():
            g_scr[0:pad, :] = jnp.zeros((pad, gate.shape[1]), F32)

        @pl.when(i > 0)
        def _():
            g_scr[0:pad, :] = carry_scr[f]

        carry_scr[f] = g_scr[tm:tm + pad, :]
        gl_ref[...] = g_scr[tm:tm + pad, :]
        g1 = g_scr[pad - 1:pad - 1 + tm, :]
        g2 = g_scr[pad - 2:pad - 2 + tm, :]
    else:
        g_scr[0:pad, :] = jnp.zeros((pad, gate.shape[1]), F32)
        gl_ref[...] = gate
        pos = lax.broadcasted_iota(jnp.int32, gate.shape, 0) % seq
        g1 = jnp.where(pos == 0, ov1_ref[...], g_scr[pad - 1:pad - 1 + tm, :])
        g2 = jnp.where(pos < 2, ov2_ref[...], g_scr[pad - 2:pad - 2 + tm, :])
    conv = cb_ref[...] + g2 * cw_ref[0:1, :] + g1 * cw_ref[1:2, :] + gate * cw_ref[2:3, :]
    act = (_silu(conv) * up).astype(BF16)
    contrib = jnp.dot(act, wd_ref[...], preferred_element_type=F32)

    @pl.when(f == 0)
    def _():
        out_ref[...] = x2_ref[...] + contrib

    @pl.when(f > 0)
    def _():
        out_ref[...] += contrib

    if final_norm:
        @pl.when(f == pl.num_programs(1) - 1)
        def _():
            out_ref[...] = _rms(out_ref[...], gfin_ref[...])


def _ffn(hf, x2, wg, wu, wd, cw, cb, gfin, tm, tf, final_norm, seq=None, ov1=None, ov2=None):
    m = hf.shape[0]
    nf = D_FF // tf
    in_specs = [
        pl.BlockSpec((tm, D_MODEL), lambda i, f: (i, 0)),
        pl.BlockSpec((tm, D_MODEL), lambda i, f: (i, 0)),
        pl.BlockSpec((D_MODEL, tf), lambda i, f: (0, f)),
        pl.BlockSpec((D_MODEL, tf), lambda i, f: (0, f)),
        pl.BlockSpec((tf, D_MODEL), lambda i, f: (f, 0)),
        pl.BlockSpec((FFN_CONV, tf), lambda i, f: (0, f)),
        pl.BlockSpec((1, tf), lambda i, f: (0, f)),
        pl.BlockSpec((1, D_MODEL), lambda i, f: (0, 0)),
    ]
    args = [hf, x2, wg, wu, wd, cw, cb[None, :], gfin[None, :]]
    scratch = [pltpu.VMEM((tm + SUBLANES, tf), F32)]
    if seq is None:
        gl_shape = jax.ShapeDtypeStruct((m // tm * SUBLANES, D_FF), F32)
        gl_spec = pl.BlockSpec((SUBLANES, tf), lambda i, f: (i, f))
        scratch.append(pltpu.VMEM((nf, SUBLANES, tf), F32))
    else:
        in_specs += [pl.BlockSpec((tm, tf), lambda i, f: (i, f))] * 2
        args += [ov1, ov2]
        gl_shape = jax.ShapeDtypeStruct((m, D_FF), F32)
        gl_spec = pl.BlockSpec((tm, tf), lambda i, f: (i, f))
    return pl.pallas_call(
        functools.partial(_ffn_kernel, tm=tm, seq=seq, final_norm=final_norm),
        grid=(m // tm, nf),
        in_specs=in_specs,
        out_specs=(pl.BlockSpec((tm, D_MODEL), lambda i, f: (i, 0)), gl_spec),
        out_shape=(jax.ShapeDtypeStruct((m, D_MODEL), F32), gl_shape),
        scratch_shapes=scratch,
        compiler_params=_cparams(("arbitrary", "arbitrary")),
        name="ffn_sample" if seq is not None else "ffn_prompt",
    )(*args)


def _layer(x, w, lam_init, final_norm, gfin, *, batch, seq, state=None):
    m = x.shape[0]
    lamv = jnp.stack([w["lambda_q1"], w["lambda_k1"], w["lambda_q2"], w["lambda_k2"]])
    tm = min(512, m)
    q, k, v, z, xs, bc, dt = _in_proj(x, w["norm_mix_g"][None, :], w["w_in_main"], w["w_in_dt"], tm)

    if state is None:
        slopes = jnp.broadcast_to(
            jnp.asarray(2.0 ** (-8.0 * np.arange(1, ATT_HEADS + 1) / ATT_HEADS), F32)[:, None, None],
            (ATT_HEADS, 1, LANES))
        o = _attn_prompt(q, k, v, slopes, lamv, w["attn_subln_g"][None, :], lam_init, tile=256)
        y, s_new = _ssd(xs, bc, dt, z, w["conv_w"], w["conv_b"], w["dt_bias"], w["a_log"],
                        w["d_skip"], w["ssm_norm_g"], lc=CHUNK, nchunk=2)
        conv_new = jnp.concatenate([xs[m - (SSM_CONV - 1):], bc[m - (SSM_CONV - 1):]], axis=-1)[None]
        s_new = s_new[None]
    else:
        cache_k, cache_v, conv_prev, ssm_prev, ffn_prev = state
        past = cache_k.shape[1]
        o = _attn_sample(q, k, v, cache_k.reshape(batch, past * ATT_HEADS, V_DIM),
                         cache_v.reshape(batch, past * ATT_HEADS, V_DIM),
                         lamv, w["attn_subln_g"][None, :], lam_init, seq)
        prev8 = jnp.pad(conv_prev, ((0, 0), (SUBLANES - (SSM_CONV - 1), 0), (0, 0)))
        y, s_new = _ssd(xs, bc, dt, z, w["conv_w"], w["conv_b"], w["dt_bias"], w["a_log"],
                        w["d_skip"], w["ssm_norm_g"], lc=seq, nchunk=1, conv_prev=prev8,
                        h0=ssm_prev.reshape(batch, D_SSM, SSM_STATE))
        conv_new = jnp.concatenate([xs.reshape(batch, seq, D_SSM)[:, seq - (SSM_CONV - 1):],
                                    bc.reshape(batch, seq, BC_DIM)[:, seq - (SSM_CONV - 1):]], axis=-1)

    x2, hf = _out_proj(o, y, x, w["w_out"], w["norm_ffn_g"][None, :], tm=min(256, m))

    ffn_args = (hf, x2, w["w_gate"], w["w_up"], w["w_down"], w["ffn_conv_w"], w["ffn_conv_b"], gfin)
    if state is None:
        x3, gl = _ffn(*ffn_args, tm=tm, tf=512, final_norm=final_norm)
        ffn_new = gl[None, gl.shape[0] - (FFN_CONV - 1):]
    else:
        zeros = jnp.zeros((batch, seq, D_FF), F32)
        ov1 = zeros.at[:, 0].set(ffn_prev[:, 1]).reshape(m, D_FF)
        ov2 = zeros.at[:, 0].set(ffn_prev[:, 0]).at[:, 1].set(ffn_prev[:, 1]).reshape(m, D_FF)
        x3, gl = _ffn(*ffn_args, tm=tm, tf=512, final_norm=final_norm, seq=seq, ov1=ov1, ov2=ov2)
        ffn_new = gl.reshape(batch, seq, D_FF)[:, seq - (FFN_CONV - 1):]
    k_new = k.reshape(batch, seq, ATT_HEADS, 2 * QK_DIM)
    v_new = v.reshape(batch, seq, ATT_HEADS, V_DIM)
    s_new = s_new.reshape(batch, SSM_HEADS, SSM_HEAD_DIM, SSM_STATE)
    return x3, k_new, v_new, conv_new, s_new, ffn_new


def kernel(x_prompt, x_sample, cache_k, cache_v, state_ssm_conv, state_ssm, state_ffn_conv, norm_mix_g, w_in, lambda_q1, lambda_k1, lambda_q2, lambda_k2, attn_subln_g, conv_w, conv_b, dt_bias, a_log, d_skip, ssm_norm_g, w_out, norm_ffn_g, w_gate, w_up, ffn_conv_w, ffn_conv_b, w_down, norm_final_g):
    depth = w_in.shape[0]
    pb, pl_len, _ = x_prompt.shape
    sb, sl_len, _ = x_sample.shape
    assert pb == 1 and pl_len % CHUNK == 0
    xp = x_prompt.reshape(pb * pl_len, D_MODEL)
    xs = x_sample.reshape(sb * sl_len, D_MODEL)
    n_main = 2 * D_ATT + D_ATT + D_SSM + CONV_DIM
    outs_p, outs_s = [], []
    for layer in range(depth):
        lam_init = 0.8 - 0.6 * math.exp(-0.3 * layer)
        w_in_l = w_in[layer].astype(BF16)
        w = dict(
            norm_mix_g=norm_mix_g[layer],
            w_in_main=w_in_l[:, :n_main],
            w_in_dt=jnp.pad(w_in_l[:, n_main:], ((0, 0), (0, LANES - SSM_HEADS))),
            lambda_q1=lambda_q1[layer], lambda_k1=lambda_k1[layer],
            lambda_q2=lambda_q2[layer], lambda_k2=lambda_k2[layer],
            attn_subln_g=attn_subln_g[layer], conv_w=conv_w[layer], conv_b=conv_b[layer],
            dt_bias=dt_bias[layer], a_log=a_log[layer], d_skip=d_skip[layer],
            ssm_norm_g=ssm_norm_g[layer], w_out=w_out[layer].astype(BF16),
            norm_ffn_g=norm_ffn_g[layer], w_gate=w_gate[layer].astype(BF16),
            w_up=w_up[layer].astype(BF16), ffn_conv_w=ffn_conv_w[layer],
            ffn_conv_b=ffn_conv_b[layer], w_down=w_down[layer].astype(BF16),
        )
        last = layer == depth - 1
        xp, *new_p = _layer(xp, w, lam_init, last, norm_final_g, batch=pb, seq=pl_len)
        xs, *new_s = _layer(xs, w, lam_init, last, norm_final_g, batch=sb, seq=sl_len,
                            state=(cache_k[layer], cache_v[layer], state_ssm_conv[layer],
                                   state_ssm[layer], state_ffn_conv[layer]))
        outs_p.append(new_p)
        outs_s.append(new_s)
    stack = lambda outs, idx: jnp.stack([o[idx] for o in outs])
    return (xp.reshape(pb, pl_len, D_MODEL), xs.reshape(sb, sl_len, D_MODEL),
            *[stack(outs_p, idx) for idx in range(5)],
            *[stack(outs_s, idx) for idx in range(5)])
```

```python
import functools
import math

import jax
import jax.numpy as jnp
import numpy as np
from jax import lax
from jax.experimental import pallas as pl
from jax.experimental.pallas import tpu as pltpu

F32 = jnp.float32
BF16 = jnp.bfloat16

D_MODEL = 2048
CHUNK = 64
ATT_HEADS = 8
QK_DIM = 64
V_DIM = 128
D_ATT = ATT_HEADS * V_DIM
SSM_HEADS = 16
SSM_HEAD_DIM = 64
D_SSM = SSM_HEADS * SSM_HEAD_DIM
SSM_GROUPS = 2
SSM_STATE = 128
SSM_CONV = 4
BC_DIM = 2 * SSM_GROUPS * SSM_STATE
CONV_DIM = D_SSM + BC_DIM
D_FF = 5632
FFN_CONV = 3
EPS = 1e-6
LOG2E = math.log2(math.e)
Q_SCALE = QK_DIM ** -0.5 * LOG2E
LANES = 128
SUBLANES = 8
VMEM_LIMIT = 52 * 1024 * 1024

_NT = (((1,), (1,)), ((), ()))
_TN = (((0,), (0,)), ((), ()))


def _cparams(sem):
    return pltpu.CompilerParams(dimension_semantics=sem, vmem_limit_bytes=VMEM_LIMIT)


def _silu(x):
    return x * jax.nn.sigmoid(x)


def _rms(x, g):
    return x * lax.rsqrt(jnp.mean(x * x, axis=-1, keepdims=True) + EPS) * g


_PROJ_TN = 512
_PROJ_SEGS = ((0, 2), (2, 4), (4, 6), (6, 8), (8, 10), (10, 11))


def _in_proj_kernel(x_ref, g_ref, w_ref, wdt_ref,
                    q_ref, k_ref, v_ref, z_ref, xs_ref, bc_ref, dt_ref, h_scr):
    j = pl.program_id(1)

    @pl.when(j == 0)
    def _():
        hb = _rms(x_ref[...], g_ref[...]).astype(BF16)
        h_scr[...] = hb
        dt_ref[...] = jnp.dot(hb, wdt_ref[...], preferred_element_type=F32)

    res = jnp.dot(h_scr[...], w_ref[...], preferred_element_type=F32)
    outs = (q_ref, k_ref, v_ref, z_ref, xs_ref, bc_ref)
    for (lo, hi), ref in zip(_PROJ_SEGS, outs):
        @pl.when((j >= lo) & (j < hi))
        def _(ref=ref):
            val = res * Q_SCALE if ref is q_ref else res
            ref[...] = val.astype(ref.dtype)


def _in_proj(x, g, w_main, w_dt, tm):
    m = x.shape[0]
    tn = _PROJ_TN
    nj = w_main.shape[1] // tn

    def seg_spec(lo, hi):
        return pl.BlockSpec((tm, tn), lambda i, j: (i, jnp.clip(j - lo, 0, hi - lo - 1)))

    out_shape = (
        jax.ShapeDtypeStruct((m, D_ATT), BF16),
        jax.ShapeDtypeStruct((m, D_ATT), F32),
        jax.ShapeDtypeStruct((m, D_ATT), F32),
        jax.ShapeDtypeStruct((m, D_SSM), F32),
        jax.ShapeDtypeStruct((m, D_SSM), F32),
        jax.ShapeDtypeStruct((m, BC_DIM), F32),
        jax.ShapeDtypeStruct((m, LANES), F32),
    )
    out_specs = tuple(seg_spec(lo, hi) for lo, hi in _PROJ_SEGS) + (
        pl.BlockSpec((tm, LANES), lambda i, j: (i, 0)),)
    return pl.pallas_call(
        _in_proj_kernel,
        grid=(m // tm, nj),
        in_specs=[
            pl.BlockSpec((tm, D_MODEL), lambda i, j: (i, 0)),
            pl.BlockSpec((1, D_MODEL), lambda i, j: (0, 0)),
            pl.BlockSpec((D_MODEL, tn), lambda i, j: (0, j)),
            pl.BlockSpec((D_MODEL, LANES), lambda i, j: (0, 0)),
        ],
        out_specs=out_specs,
        out_shape=out_shape,
        scratch_shapes=[pltpu.VMEM((tm, D_MODEL), BF16)],
        compiler_params=_cparams(("arbitrary", "arbitrary")),
        name="in_proj",
    )(x, g, w_main, w_dt)


def _lambda_value(lamv_ref, lam_init):
    lv = lamv_ref[...]
    s1 = jnp.sum(lv[0:1] * lv[1:2], axis=-1, keepdims=True)
    s2 = jnp.sum(lv[2:3] * lv[3:4], axis=-1, keepdims=True)
    return jnp.exp(s1) - jnp.exp(s2) + lam_init


def _split_maps(q):
    lane = lax.broadcasted_iota(jnp.int32, q.shape, 1)
    zero = jnp.zeros_like(q)
    return jnp.where(lane < QK_DIM, q, zero), jnp.where(lane >= QK_DIM, q, zero)


_VT_CHUNK = 512
_VT_PAD = 16
_UNROLL_SHIFT = 2
_UNROLL = 1 << _UNROLL_SHIFT


def _attn_prompt_kernel(q_ref, k_ref, v_ref, slope_ref, lamv_ref, g_ref, dmask_ref, dneg_ref, o_ref,
                        k_scr, vt_scr, acc_scr, t0_scr, t1_scr, p0_scr, p1_scr,
                        *, tile, lam_init):
    qi = pl.program_id(1)
    length = k_ref.shape[0]
    width = 2 * tile
    w = slope_ref[0][:, :1] * LOG2E

    @pl.when(qi == 0)
    def _():
        k_scr[:, 0:V_DIM] = k_ref[...].astype(BF16)
        lane = lax.broadcasted_iota(jnp.int32, (length, LANES), 1)
        koff = lax.broadcasted_iota(jnp.int32, (length, LANES), 0) % tile
        k_scr[:, V_DIM:V_DIM + LANES] = jnp.where(lane < 3, koff, 0).astype(F32).astype(BF16)
        per = _VT_CHUNK // tile
        extra = (lax.broadcasted_iota(jnp.int32, (_VT_PAD, tile), 0) == 0).astype(F32).astype(BF16)
        for c in range(length // _VT_CHUNK):
            vt = v_ref[c * _VT_CHUNK:(c + 1) * _VT_CHUNK, :].T.astype(BF16)
            for s in range(per):
                vt_scr[c * per + s, 0:V_DIM, :] = vt[:, s * tile:(s + 1) * tile]
                vt_scr[c * per + s, V_DIM:V_DIM + _VT_PAD, :] = extra

    lam = _lambda_value(lamv_ref, lam_init)
    q1, q2 = _split_maps(q_ref[...])
    q_t = jnp.concatenate([q1, q2], axis=0).astype(F32).T.astype(BF16)
    w_hi, w_mid, w_lo = (piece.astype(F32) for piece in _split3(w))
    brow = lax.broadcasted_iota(jnp.int32, (LANES, width), 0)
    q_bias = jnp.where(brow == 0, w_hi, jnp.where(brow == 1, w_mid, jnp.where(brow == 2, w_lo, 0.0)))
    q_aug = jnp.concatenate([q_t, q_bias.astype(BF16)], axis=0)

    t_slots = (t0_scr, t1_scr)
    p_slots = (p0_scr, p1_scr)

    def scores(j):
        k0 = pl.multiple_of(jnp.minimum(j, qi) * tile, tile)
        return jnp.dot(k_scr[pl.ds(k0, tile), :], q_aug, preferred_element_type=F32)

    def step(j, slot, alpha_prev, m_old):
        pv = jnp.dot(vt_scr[jnp.clip(j - 1, 0, qi)], p_slots[1 - slot][...],
                     preferred_element_type=F32)
        acc_scr[...] = alpha_prev * acc_scr[...] + pv
        t_slots[1 - slot][...] = scores(j + 1)
        off = jnp.where(j < qi, -w * ((qi - j) * tile).astype(F32), -jnp.inf)
        t_cur = t_slots[slot][...]
        m_new = jnp.maximum(m_old, jnp.max(t_cur, axis=0, keepdims=True) + off)
        p_slots[slot][...] = jnp.exp2(t_cur + (off - m_new)).astype(BF16)
        return jnp.exp2(m_old - m_new), m_new

    def run(first, ntiles_per_iter, niter, carry):
        def body(i, carry):
            for s in range(ntiles_per_iter):
                carry = step(first + ntiles_per_iter * i + s, s % 2, *carry)
            return carry
        return lax.fori_loop(0, niter, body, carry)

    acc_scr[...] = jnp.zeros(acc_scr.shape, F32)
    p1_scr[...] = jnp.zeros(p1_scr.shape, BF16)
    t0_scr[...] = scores(0)
    carry = (jnp.ones((1, width), F32), jnp.full((1, width), -0.5 * float(jnp.finfo(F32).max), F32))
    n_main = lax.shift_right_logical(qi, _UNROLL_SHIFT)
    done = n_main * _UNROLL
    n_rest = lax.shift_right_logical(qi - done + 1, 1)
    carry = run(0, _UNROLL, n_main, carry)
    alpha_last, m_old = run(done, 2, n_rest, carry)
    done = done + 2 * n_rest
    pv = jnp.dot(vt_scr[jnp.clip(done - 1, 0, qi)], p1_scr[...], preferred_element_type=F32)
    acc = alpha_last * acc_scr[...] + pv

    t = t0_scr[...] + (w * dmask_ref[...] + dneg_ref[...])
    m_new = jnp.maximum(m_old, jnp.max(t, axis=0, keepdims=True))
    p = jnp.exp2(t - m_new).astype(BF16)
    acc = jnp.exp2(m_old - m_new) * acc + jnp.dot(vt_scr[qi], p, preferred_element_type=F32)

    on = acc[0:V_DIM] / acc[V_DIM:V_DIM + 1]
    o_t = on[:, :tile] - lam * on[:, tile:]
    o_t = o_t * lax.rsqrt(jnp.mean(o_t * o_t, axis=0, keepdims=True) + EPS)
    o_ref[...] = (o_t.T * g_ref[...] * (1.0 - lam_init)).astype(o_ref.dtype)


def _attn_prompt(q, k, v, slopes, lamv, g, lam_init, tile):
    length = q.shape[0]
    key = np.arange(tile)[:, None]
    qry = np.tile(np.arange(tile), 2)[None, :]
    dmask = jnp.asarray((qry - np.abs(qry - key)) - key, F32)
    dneg = jnp.asarray(np.where(key // CHUNK <= qry // CHUNK, 0.0, -np.inf), F32)
    diag_spec = pl.BlockSpec((tile, 2 * tile), lambda h, i: (0, 0))
    return pl.pallas_call(
        functools.partial(_attn_prompt_kernel, tile=tile, lam_init=lam_init),
        grid=(ATT_HEADS, length // tile),
        in_specs=[
            pl.BlockSpec((tile, V_DIM), lambda h, i: (i, h)),
            pl.BlockSpec((length, V_DIM), lambda h, i: (0, h)),
            pl.BlockSpec((length, V_DIM), lambda h, i: (0, h)),
            pl.BlockSpec((1, 1, LANES), lambda h, i: (h, 0, 0)),
            pl.BlockSpec((4, QK_DIM), lambda h, i: (0, 0)),
            pl.BlockSpec((1, V_DIM), lambda h, i: (0, 0)),
            diag_spec, diag_spec,
        ],
        out_specs=pl.BlockSpec((tile, V_DIM), lambda h, i: (i, h)),
        out_shape=jax.ShapeDtypeStruct((length, D_ATT), BF16),
        scratch_shapes=[
            pltpu.VMEM((length, V_DIM + LANES), BF16),
            pltpu.VMEM((length // tile, V_DIM + _VT_PAD, tile), BF16),
            pltpu.VMEM((V_DIM + _VT_PAD, 2 * tile), F32),
            pltpu.VMEM((tile, 2 * tile), F32),
            pltpu.VMEM((tile, 2 * tile), F32),
            pltpu.VMEM((tile, 2 * tile), BF16),
            pltpu.VMEM((tile, 2 * tile), BF16),
        ],
        compiler_params=_cparams(("arbitrary", "arbitrary")),
        name="attn_prompt",
    )(q, k, v, slopes, lamv, g, dmask, dneg)


def _attn_sample_kernel(q_ref, kn_ref, vn_ref, ck_ref, cv_ref, lamv_ref, g_ref, o_ref,
                        *, seq, past, lam_init):
    lam = _lambda_value(lamv_ref, lam_init)
    r_c = (lax.broadcasted_iota(jnp.int32, (2 * seq, past), 0) % seq
           - lax.broadcasted_iota(jnp.int32, (2 * seq, past), 1) + past).astype(F32)
    r_n = jnp.abs(lax.broadcasted_iota(jnp.int32, (2 * seq, seq), 0) % seq
                  - lax.broadcasted_iota(jnp.int32, (2 * seq, seq), 1)).astype(F32)
    for h in range(ATT_HEADS):
        w = 2.0 ** (-8.0 * (h + 1) / ATT_HEADS) * LOG2E
        sl = slice(h * V_DIM, (h + 1) * V_DIM)
        q1, q2 = _split_maps(q_ref[:, sl])
        qq = jnp.concatenate([q1, q2], axis=0)
        kc = ck_ref[0, pl.ds(h, past, stride=ATT_HEADS), :].astype(BF16)
        vc = cv_ref[0, pl.ds(h, past, stride=ATT_HEADS), :].astype(BF16)
        kn = kn_ref[:, sl].astype(BF16)
        vn = vn_ref[:, sl].astype(BF16)
        sc = lax.dot_general(qq, kc, _NT, preferred_element_type=F32) - w * r_c
        sn = lax.dot_general(qq, kn, _NT, preferred_element_type=F32) - w * r_n
        m = jnp.maximum(jnp.max(sc, axis=-1, keepdims=True), jnp.max(sn, axis=-1, keepdims=True))
        pc = jnp.exp2(sc - m)
        pn = jnp.exp2(sn - m)
        l = jnp.sum(pc, axis=-1, keepdims=True) + jnp.sum(pn, axis=-1, keepdims=True)
        acc = (jnp.dot(pc.astype(BF16), vc, preferred_element_type=F32)
               + jnp.dot(pn.astype(BF16), vn, preferred_element_type=F32))
        on = acc / l
        o = on[:seq] - lam * on[seq:]
        o_ref[:, sl] = (_rms(o, g_ref[...]) * (1.0 - lam_init)).astype(o_ref.dtype)


def _attn_sample(q, k_new, v_new, cache_k, cache_v, lamv, g, lam_init, seq):
    nb, past = cache_k.shape[0], cache_k.shape[1] // ATT_HEADS
    row_spec = pl.BlockSpec((seq, D_ATT), lambda b: (b, 0))
    cache_spec = pl.BlockSpec((1, past * ATT_HEADS, V_DIM), lambda b: (b, 0, 0))
    return pl.pallas_call(
        functools.partial(_attn_sample_kernel, seq=seq, past=past, lam_init=lam_init),
        grid=(nb,),
        in_specs=[row_spec, row_spec, row_spec, cache_spec, cache_spec,
                  pl.BlockSpec((4, QK_DIM), lambda b: (0, 0)),
                  pl.BlockSpec((1, V_DIM), lambda b: (0, 0))],
        out_specs=row_spec,
        out_shape=jax.ShapeDtypeStruct((nb * seq, D_ATT), BF16),
        compiler_params=_cparams(("arbitrary",)),
        name="attn_sample",
    )(q, k_new, v_new, cache_k, cache_v, lamv, g)


def _split3(x):
    hi = x.astype(BF16)
    r = x - hi.astype(F32)
    mid = r.astype(BF16)
    lo = (r - mid.astype(F32)).astype(BF16)
    return hi, mid, lo


def _ssd_chunk(xs, bm, cm, dt, z, s_ref, a_row, a_seg, dskip, gn, e3, es3, t3, ones3, lc):
    seg_w = SSM_HEADS * lc
    half = D_SSM // SSM_GROUPS
    d3 = jnp.concatenate(_split3(dt), axis=1)
    dt_x = jnp.dot(d3, e3, preferred_element_type=F32)
    la = dt_x * a_row
    if seg_w == D_SSM:
        la_s = la
    else:
        la_s = jnp.dot(d3, es3, preferred_element_type=F32) * a_seg
    la3 = jnp.concatenate(_split3(la), axis=0)
    acol = jnp.dot(t3, la3, preferred_element_type=F32)
    t_idx = lax.broadcasted_iota(jnp.int32, (lc, seg_w), 0)
    s_idx = lax.broadcasted_iota(jnp.int32, (lc, seg_w), 1) % lc
    w3 = jnp.concatenate(_split3(jnp.where(t_idx > s_idx, la_s, 0.0)), axis=0)
    seg = jnp.dot(t3, w3, preferred_element_type=F32)
    decay = jnp.where(t_idx >= s_idx, jnp.exp(seg), 0.0)

    cb16 = cm.astype(BF16)
    bb16 = bm.astype(BF16)
    hpg = SSM_HEADS // SSM_GROUPS
    cbs = []
    for g in range(SSM_GROUPS):
        gs = slice(g * SSM_STATE, (g + 1) * SSM_STATE)
        b_rep = jnp.concatenate([bb16[:, gs]] * hpg, axis=0)
        cbs.append(lax.dot_general(cb16[:, gs], b_rep, _NT, preferred_element_type=F32))
    mmat = (jnp.concatenate(cbs, axis=1) * decay).astype(BF16)

    xdt = xs * dt_x
    xdt16 = xdt.astype(BF16)
    hk = (2 * LANES) // lc
    wd = hk * SSM_HEAD_DIM
    blk = (lax.broadcasted_iota(jnp.int32, (hk * lc, wd), 0) // lc
           == lax.broadcasted_iota(jnp.int32, (hk * lc, wd), 1) // SSM_HEAD_DIM)
    parts = []
    for i in range(SSM_HEADS // hk):
        xd = xdt16[:, i * wd:(i + 1) * wd]
        bd = jnp.where(blk, jnp.concatenate([xd] * hk, axis=0), jnp.zeros((), BF16))
        parts.append(jnp.dot(mmat[:, i * hk * lc:(i + 1) * hk * lc], bd, preferred_element_type=F32))
    y_intra = jnp.concatenate(parts, axis=1) if len(parts) > 1 else parts[0]

    yi = []
    for g in range(SSM_GROUPS):
        sg = s_ref[g * half:(g + 1) * half, :].astype(BF16)
        yi.append(lax.dot_general(cb16[:, g * SSM_STATE:(g + 1) * SSM_STATE], sg, _NT,
                                  preferred_element_type=F32))
    y_inter = jnp.exp(acol) * jnp.concatenate(yi, axis=1)

    dec_end = jnp.exp(acol[lc - 1:lc, :] - acol)
    xd_end = (xdt * dec_end).astype(BF16)
    acl = lax.dot_general(la3, ones3, _TN, preferred_element_type=F32)
    for g in range(SSM_GROUPS):
        rows = slice(g * half, (g + 1) * half)
        upd = lax.dot_general(xd_end[:, rows], bb16[:, g * SSM_STATE:(g + 1) * SSM_STATE], _TN,
                              preferred_element_type=F32)
        s_ref[rows, :] = jnp.exp(acl[rows, :]) * s_ref[rows, :] + upd

    y = (y_intra + y_inter + dskip * xs) * _silu(z)
    outs = []
    for g in range(SSM_GROUPS):
        cs = slice(g * half, (g + 1) * half)
        outs.append(_rms(y[:, cs], gn[:, cs]))
    return jnp.concatenate(outs, axis=1)


def _ssd_kernel(*refs, lc, nchunk, has_state):
    if has_state:
        (xs_ref, bc_ref, dt_ref, z_ref, prev_ref, h0_ref, cw_ref, cbias_ref, dtb_ref, alog_ref,
         alogs_ref, dskip_ref, gn_ref, e3_ref, es3_ref, t3_ref, ones3_ref,
         y_ref, s_ref, scr_x, scr_bc) = refs
    else:
        (xs_ref, bc_ref, dt_ref, z_ref, cw_ref, cbias_ref, dtb_ref, alog_ref,
         alogs_ref, dskip_ref, gn_ref, e3_ref, es3_ref, t3_ref, ones3_ref,
         y_ref, s_ref, scr_x, scr_bc) = refs
    rows = lc * nchunk
    pad = SUBLANES
    if has_state:
        s2d = s_ref.at[0]
        scr_x[0:pad, :] = prev_ref[0, :, 0:D_SSM]
        scr_bc[0:pad, :] = prev_ref[0, :, D_SSM:CONV_DIM]
        s2d[...] = h0_ref[0]
    else:
        s2d = s_ref

        @pl.when(pl.program_id(0) == 0)
        def _():
            scr_x[0:pad, :] = jnp.zeros((pad, D_SSM), F32)
            scr_bc[0:pad, :] = jnp.zeros((pad, BC_DIM), F32)
            s_ref[...] = jnp.zeros(s_ref.shape, F32)

    scr_x[pad:pad + rows, :] = xs_ref[...]
    scr_bc[pad:pad + rows, :] = bc_ref[...]
    xc = cbias_ref[:, 0:D_SSM]
    bcc = cbias_ref[:, D_SSM:CONV_DIM]
    for tap in range(SSM_CONV):
        off = pad - (SSM_CONV - 1) + tap
        xc = xc + scr_x[off:off + rows, :] * cw_ref[tap:tap + 1, 0:D_SSM]
        bcc = bcc + scr_bc[off:off + rows, :] * cw_ref[tap:tap + 1, D_SSM:CONV_DIM]
    if not has_state:
        scr_x[0:pad, :] = scr_x[rows:rows + pad, :]
        scr_bc[0:pad, :] = scr_bc[rows:rows + pad, :]
    xs_act = _silu(xc)
    bc_act = _silu(bcc)
    dt_in = dt_ref[...] + dtb_ref[...]
    dt = jnp.maximum(dt_in, 0.0) + jnp.log1p(jnp.exp(-jnp.abs(dt_in)))
    a_row = -jnp.exp(alog_ref[...])
    a_seg = -jnp.exp(alogs_ref[...])
    nbm = SSM_GROUPS * SSM_STATE
    for c in range(nchunk):
        rs = slice(c * lc, (c + 1) * lc)
        y = _ssd_chunk(xs_act[rs], bc_act[rs, 0:nbm], bc_act[rs, nbm:2 * nbm], dt[rs],
                       z_ref[rs, :], s2d, a_row, a_seg, dskip_ref[...], gn_ref[...],
                       e3_ref[...], es3_ref[...], t3_ref[...], ones3_ref[...], lc)
        y_ref[rs, :] = y.astype(y_ref.dtype)


def _ssd_constants(lc):
    seg_w = SSM_HEADS * lc
    head_of_lane = np.arange(D_SSM) // SSM_HEAD_DIM
    e = (np.arange(LANES)[:, None] == head_of_lane[None, :]).astype(np.float32)
    es = (np.arange(LANES)[:, None] == (np.arange(seg_w) // lc)[None, :]).astype(np.float32)
    tri = np.tril(np.ones((lc, lc), np.float32))
    return (jnp.asarray(np.concatenate([e] * 3, axis=0), BF16),
            jnp.asarray(np.concatenate([es] * 3, axis=0), BF16),
            jnp.asarray(np.concatenate([tri] * 3, axis=1), BF16),
            jnp.ones((3 * lc, SSM_STATE), BF16))


def _ssd(xs, bc, dt, z, conv_w, conv_b, dt_bias, a_log, d_skip, gn, lc, nchunk,
         conv_prev=None, h0=None):
    m = xs.shape[0]
    rows = lc * nchunk
    has_state = h0 is not None
    seg_w = SSM_HEADS * lc
    e3, es3, t3, ones3 = _ssd_constants(lc)
    dtb = jnp.zeros((1, LANES), F32).at[0, :SSM_HEADS].set(dt_bias)
    alog_x = jnp.repeat(a_log, SSM_HEAD_DIM)[None, :]
    alog_s = jnp.repeat(a_log, lc)[None, :]
    dskip_x = jnp.repeat(d_skip, SSM_HEAD_DIM)[None, :]

    def const(shape):
        return pl.BlockSpec(shape, lambda i: (0,) * len(shape))

    def rowblk(width):
        return pl.BlockSpec((rows, width), lambda i: (i, 0))

    in_specs = [rowblk(D_SSM), rowblk(BC_DIM), rowblk(LANES), rowblk(D_SSM)]
    args = [xs, bc, dt, z]
    if has_state:
        nb = h0.shape[0]
        in_specs += [pl.BlockSpec((1, SUBLANES, CONV_DIM), lambda i: (i, 0, 0)),
                     pl.BlockSpec((1, D_SSM, SSM_STATE), lambda i: (i, 0, 0))]
        args += [conv_prev, h0]
        s_shape = jax.ShapeDtypeStruct((nb, D_SSM, SSM_STATE), F32)
        s_spec = pl.BlockSpec((1, D_SSM, SSM_STATE), lambda i: (i, 0, 0))
    else:
        s_shape = jax.ShapeDtypeStruct((D_SSM, SSM_STATE), F32)
        s_spec = const((D_SSM, SSM_STATE))
    in_specs += [const((SSM_CONV, CONV_DIM)), const((1, CONV_DIM)), const((1, LANES)),
                 const((1, D_SSM)), const((1, seg_w)), const((1, D_SSM)), const((1, D_SSM)),
                 const(e3.shape), const(es3.shape), const(t3.shape), const(ones3.shape)]
    args += [conv_w, conv_b[None, :], dtb, alog_x, alog_s, dskip_x, gn[None, :], e3, es3, t3, ones3]
    return pl.pallas_call(
        functools.partial(_ssd_kernel, lc=lc, nchunk=nchunk, has_state=has_state),
        grid=(m // rows,),
        in_specs=in_specs,
        out_specs=(rowblk(D_SSM), s_spec),
        out_shape=(jax.ShapeDtypeStruct((m, D_SSM), BF16), s_shape),
        scratch_shapes=[pltpu.VMEM((rows + SUBLANES, D_SSM), F32),
                        pltpu.VMEM((rows + SUBLANES, BC_DIM), F32)],
        compiler_params=_cparams(("arbitrary",)),
        name="ssd_sample" if has_state else "ssd_prompt",
    )(*args)


def _out_proj_kernel(o_ref, y_ref, x_ref, w_ref, g_ref, x2_ref, hf_ref):
    x2 = (x_ref[...]
          + jnp.dot(o_ref[...], w_ref[0:D_ATT, :], preferred_element_type=F32)
          + jnp.dot(y_ref[...], w_ref[D_ATT:D_ATT + D_SSM, :], preferred_element_type=F32))
    x2_ref[...] = x2
    hf_ref[...] = _rms(x2, g_ref[...]).astype(hf_ref.dtype)


def _out_proj(o, y, x, w, g, tm):
    m = x.shape[0]
    return pl.pallas_call(
        _out_proj_kernel,
        grid=(m // tm,),
        in_specs=[
            pl.BlockSpec((tm, D_ATT), lambda i: (i, 0)),
            pl.BlockSpec((tm, D_SSM), lambda i: (i, 0)),
            pl.BlockSpec((tm, D_MODEL), lambda i: (i, 0)),
            pl.BlockSpec((D_ATT + D_SSM, D_MODEL), lambda i: (0, 0)),
            pl.BlockSpec((1, D_MODEL), lambda i: (0, 0)),
        ],
        out_specs=(pl.BlockSpec((tm, D_MODEL), lambda i: (i, 0)),
                   pl.BlockSpec((tm, D_MODEL), lambda i: (i, 0))),
        out_shape=(jax.ShapeDtypeStruct((m, D_MODEL), F32),
                   jax.ShapeDtypeStruct((m, D_MODEL), BF16)),
        compiler_params=_cparams(("arbitrary",)),
        name="out_proj",
    )(o, y, x, w, g)


def _ffn_kernel(*refs, tm, seq, final_norm):
    if seq is None:
        (hf_ref, x2_ref, wg_ref, wu_ref, wd_ref, cw_ref, cb_ref, gfin_ref,
         out_ref, gl_ref, g_scr, act_scr, carry_scr) = refs
    else:
        (hf_ref, x2_ref, wg_ref, wu_ref, wd_ref, cw_ref, cb_ref, gfin_ref, ov1_ref, ov2_ref,
         out_ref, gl_ref, g_scr, act_scr) = refs
    i = pl.program_id(0)
    f = pl.program_id(1)
    nf = pl.num_programs(1) - 1
    pad = SUBLANES

    def gate_up():
        hf = hf_ref[...]
        gate = jnp.dot(hf, wg_ref[...], preferred_element_type=F32)
        up = jnp.dot(hf, wu_ref[...], preferred_element_type=F32)
        g_scr[pad:pad + tm, :] = gate
        if seq is None:
            g_scr[0:pad, :] = carry_scr[f]
            carry_scr[f] = g_scr[tm:tm + pad, :]
            gl_ref[...] = g_scr[tm:tm + pad, :]
            g1 = g_scr[pad - 1:pad - 1 + tm, :]
            g2 = g_scr[pad - 2:pad - 2 + tm, :]
        else:
            g_scr[0:pad, :] = jnp.zeros((pad, gate.shape[1]), F32)
            gl_ref[...] = gate
            pos = lax.broadcasted_iota(jnp.int32, gate.shape, 0) % seq
            g1 = jnp.where(pos == 0, ov1_ref[...], g_scr[pad - 1:pad - 1 + tm, :])
            g2 = jnp.where(pos < 2, ov2_ref[...], g_scr[pad - 2:pad - 2 + tm, :])
        conv = cb_ref[...] + g2 * cw_ref[0:1, :] + g1 * cw_ref[1:2, :] + gate * cw_ref[2:3, :]
        act_scr[...] = (_silu(conv) * up).astype(BF16)

    def down():
        return jnp.dot(act_scr[...], wd_ref[...], preferred_element_type=F32)

    @pl.when(f == 0)
    def _():
        if seq is None:
            @pl.when(i == 0)
            def _():
                carry_scr[...] = jnp.zeros(carry_scr.shape, F32)
        out_ref[...] = x2_ref[...]
        gate_up()

    @pl.when((f > 0) & (f < nf))
    def _():
        contrib = down()
        gate_up()
        out_ref[...] += contrib

    @pl.when(f == nf)
    def _():
        res = out_ref[...] + down()
        out_ref[...] = _rms(res, gfin_ref[...]) if final_norm else res


def _ffn(hf, x2, wg, wu, wd, cw, cb, gfin, tm, tf, final_norm, seq=None, ov1=None, ov2=None):
    m = hf.shape[0]
    nf = D_FF // tf
    cur = lambda f: jnp.minimum(f, nf - 1)
    prev = lambda f: jnp.maximum(f - 1, 0)
    in_specs = [
        pl.BlockSpec((tm, D_MODEL), lambda i, f: (i, 0)),
        pl.BlockSpec((tm, D_MODEL), lambda i, f: (i, 0)),
        pl.BlockSpec((D_MODEL, tf), lambda i, f: (0, cur(f))),
        pl.BlockSpec((D_MODEL, tf), lambda i, f: (0, cur(f))),
        pl.BlockSpec((tf, D_MODEL), lambda i, f: (prev(f), 0)),
        pl.BlockSpec((FFN_CONV, tf), lambda i, f: (0, cur(f))),
        pl.BlockSpec((1, tf), lambda i, f: (0, cur(f))),
        pl.BlockSpec((1, D_MODEL), lambda i, f: (0, 0)),
    ]
    args = [hf, x2, wg, wu, wd, cw, cb[None, :], gfin[None, :]]
    scratch = [pltpu.VMEM((tm + SUBLANES, tf), F32), pltpu.VMEM((tm, tf), BF16)]
    if seq is None:
        gl_shape = jax.ShapeDtypeStruct((m // tm * SUBLANES, D_FF), F32)
        gl_spec = pl.BlockSpec((SUBLANES, tf), lambda i, f: (i, cur(f)))
        scratch.append(pltpu.VMEM((nf, SUBLANES, tf), F32))
    else:
        in_specs += [pl.BlockSpec((tm, tf), lambda i, f: (i, cur(f)))] * 2
        args += [ov1, ov2]
        gl_shape = jax.ShapeDtypeStruct((m, D_FF), F32)
        gl_spec = pl.BlockSpec((tm, tf), lambda i, f: (i, cur(f)))
    return pl.pallas_call(
        functools.partial(_ffn_kernel, tm=tm, seq=seq, final_norm=final_norm),
        grid=(m // tm, nf + 1),
        in_specs=in_specs,
        out_specs=(pl.BlockSpec((tm, D_MODEL), lambda i, f: (i, 0)), gl_spec),
        out_shape=(jax.ShapeDtypeStruct((m, D_MODEL), F32), gl_shape),
        scratch_shapes=scratch,
        compiler_params=_cparams(("arbitrary", "arbitrary")),
        name="ffn_sample" if seq is not None else "ffn_prompt",
    )(*args)


def _layer(x, w, lam_init, final_norm, gfin, *, batch, seq, state=None):
    m = x.shape[0]
    lamv = jnp.stack([w["lambda_q1"], w["lambda_k1"], w["lambda_q2"], w["lambda_k2"]])
    tm = min(512, m)
    q, k, v, z, xs, bc, dt = _in_proj(x, w["norm_mix_g"][None, :], w["w_in_main"], w["w_in_dt"],
                                      tm=min(1024, m))

    if state is None:
        slopes = jnp.broadcast_to(
            jnp.asarray(2.0 ** (-8.0 * np.arange(1, ATT_HEADS + 1) / ATT_HEADS), F32)[:, None, None],
            (ATT_HEADS, 1, LANES))
        o = _attn_prompt(q, k, v, slopes, lamv, w["attn_subln_g"][None, :], lam_init, tile=256)
        y, s_new = _ssd(xs, bc, dt, z, w["conv_w"], w["conv_b"], w["dt_bias"], w["a_log"],
                        w["d_skip"], w["ssm_norm_g"], lc=CHUNK, nchunk=2)
        conv_new = jnp.concatenate([xs[m - (SSM_CONV - 1):], bc[m - (SSM_CONV - 1):]], axis=-1)[None]
        s_new = s_new[None]
    else:
        cache_k, cache_v, conv_prev, ssm_prev, ffn_prev = state
        past = cache_k.shape[1]
        o = _attn_sample(q, k, v, cache_k.reshape(batch, past * ATT_HEADS, V_DIM),
                         cache_v.reshape(batch, past * ATT_HEADS, V_DIM),
                         lamv, w["attn_subln_g"][None, :], lam_init, seq)
        prev8 = jnp.pad(conv_prev, ((0, 0), (SUBLANES - (SSM_CONV - 1), 0), (0, 0)))
        y, s_new = _ssd(xs, bc, dt, z, w["conv_w"], w["conv_b"], w["dt_bias"], w["a_log"],
                        w["d_skip"], w["ssm_norm_g"], lc=seq, nchunk=1, conv_prev=prev8,
                        h0=ssm_prev.reshape(batch, D_SSM, SSM_STATE))
        conv_new = jnp.concatenate([xs.reshape(batch, seq, D_SSM)[:, seq - (SSM_CONV - 1):],
                                    bc.reshape(batch, seq, BC_DIM)[:, seq - (SSM_CONV - 1):]], axis=-1)

    x2, hf = _out_proj(o, y, x, w["w_out"], w["norm_ffn_g"][None, :], tm=min(256, m))

    ffn_args = (hf, x2, w["w_gate"], w["w_up"], w["w_down"], w["ffn_conv_w"], w["ffn_conv_b"], gfin)
    if state is None:
        x3, gl = _ffn(*ffn_args, tm=tm, tf=512, final_norm=final_norm)
        ffn_new = gl[None, gl.shape[0] - (FFN_CONV - 1):]
    else:
        zeros = jnp.zeros((batch, seq, D_FF), F32)
        ov1 = zeros.at[:, 0].set(ffn_prev[:, 1]).reshape(m, D_FF)
        ov2 = zeros.at[:, 0].set(ffn_prev[:, 0]).at[:, 1].set(ffn_prev[:, 1]).reshape(m, D_FF)
        x3, gl = _ffn(*ffn_args, tm=tm, tf=512, final_norm=final_norm, seq=seq, ov1=ov1, ov2=ov2)
        ffn_new = gl.reshape(batch, seq, D_FF)[:, seq - (FFN_CONV - 1):]
    k_new = k.reshape(batch, seq, ATT_HEADS, 2 * QK_DIM)
    v_new = v.reshape(batch, seq, ATT_HEADS, V_DIM)
    s_new = s_new.reshape(batch, SSM_HEADS, SSM_HEAD_DIM, SSM_STATE)
    return x3, k_new, v_new, conv_new, s_new, ffn_new


def kernel(x_prompt, x_sample, cache_k, cache_v, state_ssm_conv, state_ssm, state_ffn_conv, norm_mix_g, w_in, lambda_q1, lambda_k1, lambda_q2, lambda_k2, attn_subln_g, conv_w, conv_b, dt_bias, a_log, d_skip, ssm_norm_g, w_out, norm_ffn_g, w_gate, w_up, ffn_conv_w, ffn_conv_b, w_down, norm_final_g):
    depth = w_in.shape[0]
    pb, pl_len, _ = x_prompt.shape
    sb, sl_len, _ = x_sample.shape
    assert pb == 1 and pl_len % CHUNK == 0
    xp = x_prompt.reshape(pb * pl_len, D_MODEL)
    xs = x_sample.reshape(sb * sl_len, D_MODEL)
    n_main = 2 * D_ATT + D_ATT + D_SSM + CONV_DIM
    outs_p, outs_s = [], []
    for layer in range(depth):
        lam_init = 0.8 - 0.6 * math.exp(-0.3 * layer)
        w_in_l = w_in[layer].astype(BF16)
        w = dict(
            norm_mix_g=norm_mix_g[layer],
            w_in_main=w_in_l[:, :n_main],
            w_in_dt=jnp.pad(w_in_l[:, n_main:], ((0, 0), (0, LANES - SSM_HEADS))),
            lambda_q1=lambda_q1[layer], lambda_k1=lambda_k1[layer],
            lambda_q2=lambda_q2[layer], lambda_k2=lambda_k2[layer],
            attn_subln_g=attn_subln_g[layer], conv_w=conv_w[layer], conv_b=conv_b[layer],
            dt_bias=dt_bias[layer], a_log=a_log[layer], d_skip=d_skip[layer],
            ssm_norm_g=ssm_norm_g[layer], w_out=w_out[layer].astype(BF16),
            norm_ffn_g=norm_ffn_g[layer], w_gate=w_gate[layer].astype(BF16),
            w_up=w_up[layer].astype(BF16), ffn_conv_w=ffn_conv_w[layer],
            ffn_conv_b=ffn_conv_b[layer], w_down=w_down[layer].astype(BF16),
        )
        last = layer == depth - 1
        xp, *new_p = _layer(xp, w, lam_init, last, norm_final_g, batch=pb, seq=pl_len)
        xs, *new_s = _layer(xs, w, lam_init, last, norm_final_g, batch=sb, seq=sl_len,
                            state=(cache_k[layer], cache_v[layer], state_ssm_conv[layer],
                                   state_ssm[layer], state_ffn_conv[layer]))
        outs_p.append(new_p)
        outs_s.append(new_s)
    stack = lambda outs, idx: jnp.stack([o[idx] for o in outs])
    return (xp.reshape(pb, pl_len, D_MODEL), xs.reshape(sb, sl_len, D_MODEL),
            *[stack(outs_p, idx) for idx in range(5)],
            *[stack(outs_s, idx) for idx in range(5)])
```

```python
import functools
import math

import jax
import jax.numpy as jnp
import numpy as np
from jax import lax
from jax.experimental import pallas as pl
from jax.experimental.pallas import tpu as pltpu

F32 = jnp.float32
BF16 = jnp.bfloat16

D_MODEL = 2048
CHUNK = 64
ATT_HEADS = 8
QK_DIM = 64
V_DIM = 128
D_ATT = ATT_HEADS * V_DIM
SSM_HEADS = 16
SSM_HEAD_DIM = 64
D_SSM = SSM_HEADS * SSM_HEAD_DIM
SSM_GROUPS = 2
SSM_STATE = 128
SSM_CONV = 4
BC_DIM = 2 * SSM_GROUPS * SSM_STATE
CONV_DIM = D_SSM + BC_DIM
D_FF = 5632
FFN_CONV = 3
EPS = 1e-6
LOG2E = math.log2(math.e)
Q_SCALE = QK_DIM ** -0.5 * LOG2E
LANES = 128
SUBLANES = 8
VMEM_LIMIT = 52 * 1024 * 1024

_NT = (((1,), (1,)), ((), ()))
_TN = (((0,), (0,)), ((), ()))


def _cparams(sem):
    return pltpu.CompilerParams(dimension_semantics=sem, vmem_limit_bytes=VMEM_LIMIT)


def _silu(x):
    return x * jax.nn.sigmoid(x)


def _rms(x, g):
    return x * lax.rsqrt(jnp.mean(x * x, axis=-1, keepdims=True) + EPS) * g


_PROJ_TN = 512
_PROJ_SEGS = ((0, 2), (2, 4), (4, 6), (6, 8), (8, 10), (10, 11))


def _in_proj_kernel(x_ref, g_ref, w_ref, wdt_ref,
                    q_ref, k_ref, v_ref, z_ref, xs_ref, bc_ref, dt_ref, h_scr):
    j = pl.program_id(1)

    @pl.when(j == 0)
    def _():
        hb = _rms(x_ref[...], g_ref[...]).astype(BF16)
        h_scr[...] = hb
        dt_ref[...] = jnp.dot(hb, wdt_ref[...], preferred_element_type=F32)

    res = jnp.dot(h_scr[...], w_ref[...], preferred_element_type=F32)
    outs = (q_ref, k_ref, v_ref, z_ref, xs_ref, bc_ref)
    for (lo, hi), ref in zip(_PROJ_SEGS, outs):
        @pl.when((j >= lo) & (j < hi))
        def _(ref=ref):
            val = res * Q_SCALE if ref is q_ref else res
            ref[...] = val.astype(ref.dtype)


def _in_proj(x, g, w_main, w_dt, tm):
    m = x.shape[0]
    tn = _PROJ_TN
    nj = w_main.shape[1] // tn

    def seg_spec(lo, hi):
        return pl.BlockSpec((tm, tn), lambda i, j: (i, jnp.clip(j - lo, 0, hi - lo - 1)))

    out_shape = (
        jax.ShapeDtypeStruct((m, D_ATT), BF16),
        jax.ShapeDtypeStruct((m, D_ATT), F32),
        jax.ShapeDtypeStruct((m, D_ATT), F32),
        jax.ShapeDtypeStruct((m, D_SSM), F32),
        jax.ShapeDtypeStruct((m, D_SSM), F32),
        jax.ShapeDtypeStruct((m, BC_DIM), F32),
        jax.ShapeDtypeStruct((m, LANES), F32),
    )
    out_specs = tuple(seg_spec(lo, hi) for lo, hi in _PROJ_SEGS) + (
        pl.BlockSpec((tm, LANES), lambda i, j: (i, 0)),)
    return pl.pallas_call(
        _in_proj_kernel,
        grid=(m // tm, nj),
        in_specs=[
            pl.BlockSpec((tm, D_MODEL), lambda i, j: (i, 0)),
            pl.BlockSpec((1, D_MODEL), lambda i, j: (0, 0)),
            pl.BlockSpec((D_MODEL, tn), lambda i, j: (0, j)),
            pl.BlockSpec((D_MODEL, LANES), lambda i, j: (0, 0)),
        ],
        out_specs=out_specs,
        out_shape=out_shape,
        scratch_shapes=[pltpu.VMEM((tm, D_MODEL), BF16)],
        compiler_params=_cparams(("arbitrary", "arbitrary")),
        name="in_proj",
    )(x, g, w_main, w_dt)


def _lambda_value(lamv_ref, lam_init):
    lv = lamv_ref[...]
    s1 = jnp.sum(lv[0:1] * lv[1:2], axis=-1, keepdims=True)
    s2 = jnp.sum(lv[2:3] * lv[3:4], axis=-1, keepdims=True)
    return jnp.exp(s1) - jnp.exp(s2) + lam_init


def _split_maps(q):
    lane = lax.broadcasted_iota(jnp.int32, q.shape, 1)
    zero = jnp.zeros_like(q)
    return jnp.where(lane < QK_DIM, q, zero), jnp.where(lane >= QK_DIM, q, zero)


_VT_CHUNK = 512
_VT_PAD = 16
_UNROLL_SHIFT = 2
_UNROLL = 1 << _UNROLL_SHIFT


def _attn_prompt_kernel(q_ref, k_ref, v_ref, slope_ref, lamv_ref, g_ref, dmask_ref, dneg_ref, o_ref,
                        k_scr, vt_scr, acc_scr, t0_scr, t1_scr, p0_scr, p1_scr,
                        *, tile, lam_init):
    length = k_ref.shape[0]
    width = 2 * tile
    w = slope_ref[0][:, :1] * LOG2E
    lam = _lambda_value(lamv_ref, lam_init)

    k_scr[:, 0:V_DIM] = k_ref[...].astype(BF16)
    lane = lax.broadcasted_iota(jnp.int32, (length, LANES), 1)
    koff = lax.broadcasted_iota(jnp.int32, (length, LANES), 0) % tile
    k_scr[:, V_DIM:V_DIM + LANES] = jnp.where(lane < 3, koff, 0).astype(F32).astype(BF16)
    per = _VT_CHUNK // tile
    extra = (lax.broadcasted_iota(jnp.int32, (_VT_PAD, tile), 0) == 0).astype(F32).astype(BF16)
    for c in range(length // _VT_CHUNK):
        vt = v_ref[c * _VT_CHUNK:(c + 1) * _VT_CHUNK, :].T.astype(BF16)
        for s in range(per):
            vt_scr[c * per + s, 0:V_DIM, :] = vt[:, s * tile:(s + 1) * tile]
            vt_scr[c * per + s, V_DIM:V_DIM + _VT_PAD, :] = extra
    w_hi, w_mid, w_lo = (piece.astype(F32) for piece in _split3(w))
    brow = lax.broadcasted_iota(jnp.int32, (LANES, width), 0)
    q_bias = jnp.where(brow == 0, w_hi, jnp.where(brow == 1, w_mid, jnp.where(brow == 2, w_lo, 0.0)))
    q_bias = q_bias.astype(BF16)

    def q_tile(qi, _):
        _attn_prompt_tile(qi, w, lam, q_bias, q_ref, g_ref, dmask_ref, dneg_ref, o_ref,
                          k_scr, vt_scr, acc_scr, (t0_scr, t1_scr), (p0_scr, p1_scr),
                          tile=tile, lam_init=lam_init)
        return 0

    lax.fori_loop(0, length // tile, q_tile, 0)


def _attn_prompt_tile(qi, w, lam, q_bias, q_ref, g_ref, dmask_ref, dneg_ref, o_ref,
                      k_scr, vt_scr, acc_scr, t_slots, p_slots, *, tile, lam_init):
    width = 2 * tile
    t0_scr, p1_scr = t_slots[0], p_slots[1]
    rows = pl.ds(pl.multiple_of(qi * tile, tile), tile)
    q1, q2 = _split_maps(q_ref[rows, :])
    q_t = jnp.concatenate([q1, q2], axis=0).astype(F32).T.astype(BF16)
    q_aug = jnp.concatenate([q_t, q_bias], axis=0)

    def scores(j):
        k0 = pl.multiple_of(jnp.minimum(j, qi) * tile, tile)
        return jnp.dot(k_scr[pl.ds(k0, tile), :], q_aug, preferred_element_type=F32)

    def step(j, slot, alpha_prev, m_old, tmax_cur):
        pv = jnp.dot(vt_scr[jnp.clip(j - 1, 0, qi)], p_slots[1 - slot][...],
                     preferred_element_type=F32)
        acc_scr[...] = alpha_prev * acc_scr[...] + pv
        t_next = scores(j + 1)
        t_slots[1 - slot][...] = t_next
        tmax_next = jnp.max(t_next, axis=0, keepdims=True)
        off = jnp.where(j < qi, -w * ((qi - j) * tile).astype(F32), -jnp.inf)
        m_new = jnp.maximum(m_old, tmax_cur + off)
        p_slots[slot][...] = jnp.exp2(t_slots[slot][...] + (off - m_new)).astype(BF16)
        return jnp.exp2(m_old - m_new), m_new, tmax_next

    def run(first, ntiles_per_iter, niter, carry):
        def body(i, carry):
            for s in range(ntiles_per_iter):
                carry = step(first + ntiles_per_iter * i + s, s % 2, *carry)
            return carry
        return lax.fori_loop(0, niter, body, carry)

    acc_scr[...] = jnp.zeros(acc_scr.shape, F32)
    p1_scr[...] = jnp.zeros(p1_scr.shape, BF16)
    t_first = scores(0)
    t0_scr[...] = t_first
    carry = (jnp.ones((1, width), F32), jnp.full((1, width), -0.5 * float(jnp.finfo(F32).max), F32),
             jnp.max(t_first, axis=0, keepdims=True))
    n_main = lax.shift_right_logical(qi, _UNROLL_SHIFT)
    done = n_main * _UNROLL
    n_rest = lax.shift_right_logical(qi - done + 1, 1)
    carry = run(0, _UNROLL, n_main, carry)
    alpha_last, m_old, _ = run(done, 2, n_rest, carry)
    done = done + 2 * n_rest
    pv = jnp.dot(vt_scr[jnp.clip(done - 1, 0, qi)], p1_scr[...], preferred_element_type=F32)
    acc = alpha_last * acc_scr[...] + pv

    t = t0_scr[...] + (w * dmask_ref[...] + dneg_ref[...])
    m_new = jnp.maximum(m_old, jnp.max(t, axis=0, keepdims=True))
    p = jnp.exp2(t - m_new).astype(BF16)
    acc = jnp.exp2(m_old - m_new) * acc + jnp.dot(vt_scr[qi], p, preferred_element_type=F32)

    on = acc[0:V_DIM] / acc[V_DIM:V_DIM + 1]
    o_t = on[:, :tile] - lam * on[:, tile:]
    o_t = o_t * lax.rsqrt(jnp.mean(o_t * o_t, axis=0, keepdims=True) + EPS)
    o_ref[rows, :] = (o_t.T * g_ref[...] * (1.0 - lam_init)).astype(o_ref.dtype)


def _attn_prompt(q, k, v, slopes, lamv, g, lam_init, tile):
    length = q.shape[0]
    key = np.arange(tile)[:, None]
    qry = np.tile(np.arange(tile), 2)[None, :]
    dmask = jnp.asarray((qry - np.abs(qry - key)) - key, F32)
    dneg = jnp.asarray(np.where(key // CHUNK <= qry // CHUNK, 0.0, -np.inf), F32)
    diag_spec = pl.BlockSpec((tile, 2 * tile), lambda h: (0, 0))
    head_spec = pl.BlockSpec((length, V_DIM), lambda h: (0, h))
    return pl.pallas_call(
        functools.partial(_attn_prompt_kernel, tile=tile, lam_init=lam_init),
        grid=(ATT_HEADS,),
        in_specs=[
            head_spec, head_spec, head_spec,
            pl.BlockSpec((1, 1, LANES), lambda h: (h, 0, 0)),
            pl.BlockSpec((4, QK_DIM), lambda h: (0, 0)),
            pl.BlockSpec((1, V_DIM), lambda h: (0, 0)),
            diag_spec, diag_spec,
        ],
        out_specs=head_spec,
        out_shape=jax.ShapeDtypeStruct((length, D_ATT), BF16),
        scratch_shapes=[
            pltpu.VMEM((length, V_DIM + LANES), BF16),
            pltpu.VMEM((length // tile, V_DIM + _VT_PAD, tile), BF16),
            pltpu.VMEM((V_DIM + _VT_PAD, 2 * tile), F32),
            pltpu.VMEM((tile, 2 * tile), F32),
            pltpu.VMEM((tile, 2 * tile), F32),
            pltpu.VMEM((tile, 2 * tile), BF16),
            pltpu.VMEM((tile, 2 * tile), BF16),
        ],
        compiler_params=_cparams(("arbitrary",)),
        name="attn_prompt",
    )(q, k, v, slopes, lamv, g, dmask, dneg)


def _attn_sample_kernel(q_ref, kn_ref, vn_ref, ck_ref, cv_ref, lamv_ref, g_ref, o_ref,
                        *, seq, past, lam_init):
    lam = _lambda_value(lamv_ref, lam_init)
    r_c = (lax.broadcasted_iota(jnp.int32, (2 * seq, past), 0) % seq
           - lax.broadcasted_iota(jnp.int32, (2 * seq, past), 1) + past).astype(F32)
    r_n = jnp.abs(lax.broadcasted_iota(jnp.int32, (2 * seq, seq), 0) % seq
                  - lax.broadcasted_iota(jnp.int32, (2 * seq, seq), 1)).astype(F32)
    for h in range(ATT_HEADS):
        w = 2.0 ** (-8.0 * (h + 1) / ATT_HEADS) * LOG2E
        sl = slice(h * V_DIM, (h + 1) * V_DIM)
        q1, q2 = _split_maps(q_ref[:, sl])
        qq = jnp.concatenate([q1, q2], axis=0)
        kc = ck_ref[0, pl.ds(h, past, stride=ATT_HEADS), :].astype(BF16)
        vc = cv_ref[0, pl.ds(h, past, stride=ATT_HEADS), :].astype(BF16)
        kn = kn_ref[:, sl].astype(BF16)
        vn = vn_ref[:, sl].astype(BF16)
        sc = lax.dot_general(qq, kc, _NT, preferred_element_type=F32) - w * r_c
        sn = lax.dot_general(qq, kn, _NT, preferred_element_type=F32) - w * r_n
        m = jnp.maximum(jnp.max(sc, axis=-1, keepdims=True), jnp.max(sn, axis=-1, keepdims=True))
        pc = jnp.exp2(sc - m)
        pn = jnp.exp2(sn - m)
        l = jnp.sum(pc, axis=-1, keepdims=True) + jnp.sum(pn, axis=-1, keepdims=True)
        acc = (jnp.dot(pc.astype(BF16), vc, preferred_element_type=F32)
               + jnp.dot(pn.astype(BF16), vn, preferred_element_type=F32))
        on = acc / l
        o = on[:seq] - lam * on[seq:]
        o_ref[:, sl] = (_rms(o, g_ref[...]) * (1.0 - lam_init)).astype(o_ref.dtype)


def _attn_sample(q, k_new, v_new, cache_k, cache_v, lamv, g, lam_init, seq):
    nb, past = cache_k.shape[0], cache_k.shape[1] // ATT_HEADS
    row_spec = pl.BlockSpec((seq, D_ATT), lambda b: (b, 0))
    cache_spec = pl.BlockSpec((1, past * ATT_HEADS, V_DIM), lambda b: (b, 0, 0))
    return pl.pallas_call(
        functools.partial(_attn_sample_kernel, seq=seq, past=past, lam_init=lam_init),
        grid=(nb,),
        in_specs=[row_spec, row_spec, row_spec, cache_spec, cache_spec,
                  pl.BlockSpec((4, QK_DIM), lambda b: (0, 0)),
                  pl.BlockSpec((1, V_DIM), lambda b: (0, 0))],
        out_specs=row_spec,
        out_shape=jax.ShapeDtypeStruct((nb * seq, D_ATT), BF16),
        compiler_params=_cparams(("arbitrary",)),
        name="attn_sample",
    )(q, k_new, v_new, cache_k, cache_v, lamv, g)


def _split3(x):
    hi = x.astype(BF16)
    r = x - hi.astype(F32)
    mid = r.astype(BF16)
    lo = (r - mid.astype(F32)).astype(BF16)
    return hi, mid, lo


def _ssd_chunk(xs, bm, cm, dt, z, s_ref, a_heads, dskip, gn, e3, es3, t3, ones_k, lc):
    seg_w = SSM_HEADS * lc
    half = D_SSM // SSM_GROUPS
    d3 = jnp.concatenate(_split3(dt), axis=1)
    dt_x = jnp.dot(d3, e3, preferred_element_type=F32)
    la3 = jnp.concatenate(_split3(dt * a_heads), axis=0)
    ac3 = jnp.concatenate(_split3(jnp.dot(t3, la3, preferred_element_type=F32)), axis=1)
    acol = jnp.dot(ac3, e3, preferred_element_type=F32)
    acol_s = acol if seg_w == D_SSM else jnp.dot(ac3, es3, preferred_element_type=F32)
    t_idx = lax.broadcasted_iota(jnp.int32, (lc, seg_w), 0)
    s_idx = lax.broadcasted_iota(jnp.int32, (lc, seg_w), 1) % lc
    arow = jnp.sum(jnp.where(t_idx == s_idx, acol_s, 0.0), axis=0, keepdims=True)
    decay = jnp.where(t_idx >= s_idx, jnp.exp(acol_s - arow), 0.0)

    cb16 = cm.astype(BF16)
    bb16 = bm.astype(BF16)
    hpg = SSM_HEADS // SSM_GROUPS
    cbs = []
    for g in range(SSM_GROUPS):
        gs = slice(g * SSM_STATE, (g + 1) * SSM_STATE)
        b_rep = jnp.concatenate([bb16[:, gs]] * hpg, axis=0)
        cbs.append(lax.dot_general(cb16[:, gs], b_rep, _NT, preferred_element_type=F32))
    mmat = (jnp.concatenate(cbs, axis=1) * decay).astype(BF16)

    xdt = xs * dt_x
    xdt16 = xdt.astype(BF16)
    hk = (2 * LANES) // lc
    wd = hk * SSM_HEAD_DIM
    blk = (lax.broadcasted_iota(jnp.int32, (hk * lc, wd), 0) // lc
           == lax.broadcasted_iota(jnp.int32, (hk * lc, wd), 1) // SSM_HEAD_DIM)
    parts = []
    for i in range(SSM_HEADS // hk):
        xd = xdt16[:, i * wd:(i + 1) * wd]
        bd = jnp.where(blk, jnp.concatenate([xd] * hk, axis=0), jnp.zeros((), BF16))
        parts.append(jnp.dot(mmat[:, i * hk * lc:(i + 1) * hk * lc], bd, preferred_element_type=F32))
    y_intra = jnp.concatenate(parts, axis=1) if len(parts) > 1 else parts[0]

    yi = []
    for g in range(SSM_GROUPS):
        sg = s_ref[g * half:(g + 1) * half, :].astype(BF16)
        yi.append(lax.dot_general(cb16[:, g * SSM_STATE:(g + 1) * SSM_STATE], sg, _NT,
                                  preferred_element_type=F32))
    y_inter = jnp.exp(acol) * jnp.concatenate(yi, axis=1)

    alast = acol[lc - 1:lc, :]
    dec_end = jnp.exp(alast - acol)
    xd_end = (xdt * dec_end).astype(BF16)
    krow = lax.broadcasted_iota(jnp.int32, (ones_k.shape[0], D_SSM), 0)
    a_hi, a_mid, a_lo = (piece.astype(F32) for piece in _split3(alast))
    pieces = jnp.where(krow == 0, a_hi, jnp.where(krow == 1, a_mid, jnp.where(krow == 2, a_lo, 0.0)))
    acl = lax.dot_general(pieces.astype(BF16), ones_k, _TN, preferred_element_type=F32)
    for g in range(SSM_GROUPS):
        rows = slice(g * half, (g + 1) * half)
        upd = lax.dot_general(xd_end[:, rows], bb16[:, g * SSM_STATE:(g + 1) * SSM_STATE], _TN,
                              preferred_element_type=F32)
        s_ref[rows, :] = jnp.exp(acl[rows, :]) * s_ref[rows, :] + upd

    y = (y_intra + y_inter + dskip * xs) * _silu(z)
    outs = []
    for g in range(SSM_GROUPS):
        cs = slice(g * half, (g + 1) * half)
        outs.append(_rms(y[:, cs], gn[:, cs]))
    return jnp.concatenate(outs, axis=1)


def _ssd_kernel(*refs, lc, nchunk, has_state):
    if has_state:
        (xs_ref, bc_ref, dt_ref, z_ref, prev_ref, h0_ref, cw_ref, cbias_ref, dtb_ref, alog_ref,
         dskip_ref, gn_ref, e3_ref, es3_ref, t3_ref, ones_ref,
         y_ref, s_ref, scr_x, scr_bc) = refs
    else:
        (xs_ref, bc_ref, dt_ref, z_ref, cw_ref, cbias_ref, dtb_ref, alog_ref,
         dskip_ref, gn_ref, e3_ref, es3_ref, t3_ref, ones_ref,
         y_ref, s_ref, scr_x, scr_bc) = refs
    rows = lc * nchunk
    pad = SUBLANES
    if has_state:
        s2d = s_ref.at[0]
        scr_x[0:pad, :] = prev_ref[0, :, 0:D_SSM]
        scr_bc[0:pad, :] = prev_ref[0, :, D_SSM:CONV_DIM]
        s2d[...] = h0_ref[0]
    else:
        s2d = s_ref

        @pl.when(pl.program_id(0) == 0)
        def _():
            scr_x[0:pad, :] = jnp.zeros((pad, D_SSM), F32)
            scr_bc[0:pad, :] = jnp.zeros((pad, BC_DIM), F32)
            s_ref[...] = jnp.zeros(s_ref.shape, F32)

    scr_x[pad:pad + rows, :] = xs_ref[...]
    scr_bc[pad:pad + rows, :] = bc_ref[...]
    xc = cbias_ref[:, 0:D_SSM]
    bcc = cbias_ref[:, D_SSM:CONV_DIM]
    for tap in range(SSM_CONV):
        off = pad - (SSM_CONV - 1) + tap
        xc = xc + scr_x[off:off + rows, :] * cw_ref[tap:tap + 1, 0:D_SSM]
        bcc = bcc + scr_bc[off:off + rows, :] * cw_ref[tap:tap + 1, D_SSM:CONV_DIM]
    if not has_state:
        scr_x[0:pad, :] = scr_x[rows:rows + pad, :]
        scr_bc[0:pad, :] = scr_bc[rows:rows + pad, :]
    xs_act = _silu(xc)
    bc_act = _silu(bcc)
    dt_in = dt_ref[...] + dtb_ref[...]
    dt = jnp.maximum(dt_in, 0.0) + jnp.log1p(jnp.exp(-jnp.abs(dt_in)))
    a_heads = -jnp.exp(alog_ref[...])
    nbm = SSM_GROUPS * SSM_STATE
    for c in range(nchunk):
        rs = slice(c * lc, (c + 1) * lc)
        y = _ssd_chunk(xs_act[rs], bc_act[rs, 0:nbm], bc_act[rs, nbm:2 * nbm], dt[rs],
                       z_ref[rs, :], s2d, a_heads, dskip_ref[...], gn_ref[...],
                       e3_ref[...], es3_ref[...], t3_ref[...], ones_ref[...], lc)
        y_ref[rs, :] = y.astype(y_ref.dtype)


def _ssd_constants(lc):
    seg_w = SSM_HEADS * lc
    head_of_lane = np.arange(D_SSM) // SSM_HEAD_DIM
    e = (np.arange(LANES)[:, None] == head_of_lane[None, :]).astype(np.float32)
    es = (np.arange(LANES)[:, None] == (np.arange(seg_w) // lc)[None, :]).astype(np.float32)
    tri = np.tril(np.ones((lc, lc), np.float32))
    return (jnp.asarray(np.concatenate([e] * 3, axis=0), BF16),
            jnp.asarray(np.concatenate([es] * 3, axis=0), BF16),
            jnp.asarray(np.concatenate([tri] * 3, axis=1), BF16),
            jnp.ones((2 * SUBLANES, SSM_STATE), BF16))


def _ssd(xs, bc, dt, z, conv_w, conv_b, dt_bias, a_log, d_skip, gn, lc, nchunk,
         conv_prev=None, h0=None):
    m = xs.shape[0]
    rows = lc * nchunk
    has_state = h0 is not None
    e3, es3, t3, ones_k = _ssd_constants(lc)
    dtb = jnp.zeros((1, LANES), F32).at[0, :SSM_HEADS].set(dt_bias)
    alog = jnp.zeros((1, LANES), F32).at[0, :SSM_HEADS].set(a_log)
    dskip_x = jnp.repeat(d_skip, SSM_HEAD_DIM)[None, :]

    def const(shape):
        return pl.BlockSpec(shape, lambda i: (0,) * len(shape))

    def rowblk(width):
        return pl.BlockSpec((rows, width), lambda i: (i, 0))

    in_specs = [rowblk(D_SSM), rowblk(BC_DIM), rowblk(LANES), rowblk(D_SSM)]
    args = [xs, bc, dt, z]
    if has_state:
        nb = h0.shape[0]
        in_specs += [pl.BlockSpec((1, SUBLANES, CONV_DIM), lambda i: (i, 0, 0)),
                     pl.BlockSpec((1, D_SSM, SSM_STATE), lambda i: (i, 0, 0))]
        args += [conv_prev, h0]
        s_shape = jax.ShapeDtypeStruct((nb, D_SSM, SSM_STATE), F32)
        s_spec = pl.BlockSpec((1, D_SSM, SSM_STATE), lambda i: (i, 0, 0))
    else:
        s_shape = jax.ShapeDtypeStruct((D_SSM, SSM_STATE), F32)
        s_spec = const((D_SSM, SSM_STATE))
    in_specs += [const((SSM_CONV, CONV_DIM)), const((1, CONV_DIM)), const((1, LANES)),
                 const((1, LANES)), const((1, D_SSM)), const((1, D_SSM)),
                 const(e3.shape), const(es3.shape), const(t3.shape), const(ones_k.shape)]
    args += [conv_w, conv_b[None, :], dtb, alog, dskip_x, gn[None, :], e3, es3, t3, ones_k]
    return pl.pallas_call(
        functools.partial(_ssd_kernel, lc=lc, nchunk=nchunk, has_state=has_state),
        grid=(m // rows,),
        in_specs=in_specs,
        out_specs=(rowblk(D_SSM), s_spec),
        out_shape=(jax.ShapeDtypeStruct((m, D_SSM), BF16), s_shape),
        scratch_shapes=[pltpu.VMEM((rows + SUBLANES, D_SSM), F32),
                        pltpu.VMEM((rows + SUBLANES, BC_DIM), F32)],
        compiler_params=_cparams(("arbitrary",)),
        name="ssd_sample" if has_state else "ssd_prompt",
    )(*args)


def _out_proj_kernel(o_ref, y_ref, x_ref, w_ref, g_ref, x2_ref, hf_ref):
    x2 = (x_ref[...]
          + jnp.dot(o_ref[...], w_ref[0:D_ATT, :], preferred_element_type=F32)
          + jnp.dot(y_ref[...], w_ref[D_ATT:D_ATT + D_SSM, :], preferred_element_type=F32))
    x2_ref[...] = x2
    hf_ref[...] = _rms(x2, g_ref[...]).astype(hf_ref.dtype)


def _out_proj(o, y, x, w, g, tm):
    m = x.shape[0]
    return pl.pallas_call(
        _out_proj_kernel,
        grid=(m // tm,),
        in_specs=[
            pl.BlockSpec((tm, D_ATT), lambda i: (i, 0)),
            pl.BlockSpec((tm, D_SSM), lambda i: (i, 0)),
            pl.BlockSpec((tm, D_MODEL), lambda i: (i, 0)),
            pl.BlockSpec((D_ATT + D_SSM, D_MODEL), lambda i: (0, 0)),
            pl.BlockSpec((1, D_MODEL), lambda i: (0, 0)),
        ],
        out_specs=(pl.BlockSpec((tm, D_MODEL), lambda i: (i, 0)),
                   pl.BlockSpec((tm, D_MODEL), lambda i: (i, 0))),
        out_shape=(jax.ShapeDtypeStruct((m, D_MODEL), F32),
                   jax.ShapeDtypeStruct((m, D_MODEL), BF16)),
        compiler_params=_cparams(("arbitrary",)),
        name="out_proj",
    )(o, y, x, w, g)


def _ffn_kernel(*refs, tm, seq, final_norm):
    if seq is None:
        (hf_ref, x2_ref, wg_ref, wu_ref, wd_ref, cw_ref, cb_ref, gfin_ref,
         out_ref, gl_ref, g_scr, act_scr, carry_scr) = refs
    else:
        (hf_ref, x2_ref, wg_ref, wu_ref, wd_ref, cw_ref, cb_ref, gfin_ref, prev_ref, sel1_ref, sel2_ref,
         out_ref, gl_ref, g_scr, act_scr) = refs
    i = pl.program_id(0)
    f = pl.program_id(1)
    nf = pl.num_programs(1) - 1
    pad = SUBLANES

    def gate_up():
        hf = hf_ref[...]
        gate = jnp.dot(hf, wg_ref[...], preferred_element_type=F32)
        up = jnp.dot(hf, wu_ref[...], preferred_element_type=F32)
        g_scr[pad:pad + tm, :] = gate
        if seq is None:
            g_scr[0:pad, :] = carry_scr[f]
            carry_scr[f] = g_scr[tm:tm + pad, :]
            gl_ref[...] = g_scr[tm:tm + pad, :]
            g1 = g_scr[pad - 1:pad - 1 + tm, :]
            g2 = g_scr[pad - 2:pad - 2 + tm, :]
        else:
            g_scr[0:pad, :] = jnp.zeros((pad, gate.shape[1]), F32)
            gl_ref[...] = gate
            prev3 = jnp.concatenate(_split3(prev_ref[...]), axis=0)
            ov1 = jnp.dot(sel1_ref[...], prev3, preferred_element_type=F32)
            ov2 = jnp.dot(sel2_ref[...], prev3, preferred_element_type=F32)
            pos = lax.broadcasted_iota(jnp.int32, gate.shape, 0) % seq
            g1 = jnp.where(pos == 0, ov1, g_scr[pad - 1:pad - 1 + tm, :])
            g2 = jnp.where(pos < 2, ov2, g_scr[pad - 2:pad - 2 + tm, :])
        conv = cb_ref[...] + g2 * cw_ref[0:1, :] + g1 * cw_ref[1:2, :] + gate * cw_ref[2:3, :]
        act_scr[...] = (_silu(conv) * up).astype(BF16)

    def down():
        return jnp.dot(act_scr[...], wd_ref[...], preferred_element_type=F32)

    @pl.when(f == 0)
    def _():
        if seq is None:
            @pl.when(i == 0)
            def _():
                carry_scr[...] = jnp.zeros(carry_scr.shape, F32)
        out_ref[...] = x2_ref[...]
        gate_up()

    @pl.when((f > 0) & (f < nf))
    def _():
        contrib = down()
        gate_up()
        out_ref[...] += contrib

    @pl.when(f == nf)
    def _():
        res = out_ref[...] + down()
        out_ref[...] = _rms(res, gfin_ref[...]) if final_norm else res


def _ffn_prev_selectors(m, seq, nprev):
    sel1 = np.zeros((m, 3 * nprev), np.float32)
    sel2 = np.zeros((m, 3 * nprev), np.float32)
    nstate = FFN_CONV - 1
    for b in range(m // seq):
        for piece in range(3):
            base = piece * nprev + b * nstate
            sel1[b * seq, base + 1] = 1.0
            sel2[b * seq, base + 0] = 1.0
            sel2[b * seq + 1, base + 1] = 1.0
    return jnp.asarray(sel1, BF16), jnp.asarray(sel2, BF16)


def _ffn(hf, x2, wg, wu, wd, cw, cb, gfin, tm, tf, final_norm, seq=None, prev=None):
    m = hf.shape[0]
    nf = D_FF // tf
    cur = lambda f: jnp.minimum(f, nf - 1)
    last = lambda f: jnp.maximum(f - 1, 0)
    in_specs = [
        pl.BlockSpec((tm, D_MODEL), lambda i, f: (i, 0)),
        pl.BlockSpec((tm, D_MODEL), lambda i, f: (i, 0)),
        pl.BlockSpec((D_MODEL, tf), lambda i, f: (0, cur(f))),
        pl.BlockSpec((D_MODEL, tf), lambda i, f: (0, cur(f))),
        pl.BlockSpec((tf, D_MODEL), lambda i, f: (last(f), 0)),
        pl.BlockSpec((FFN_CONV, tf), lambda i, f: (0, cur(f))),
        pl.BlockSpec((1, tf), lambda i, f: (0, cur(f))),
        pl.BlockSpec((1, D_MODEL), lambda i, f: (0, 0)),
    ]
    args = [hf, x2, wg, wu, wd, cw, cb[None, :], gfin[None, :]]
    scratch = [pltpu.VMEM((tm + SUBLANES, tf), F32), pltpu.VMEM((tm, tf), BF16)]
    if seq is None:
        gl_shape = jax.ShapeDtypeStruct((m // tm * SUBLANES, D_FF), F32)
        gl_spec = pl.BlockSpec((SUBLANES, tf), lambda i, f: (i, cur(f)))
        scratch.append(pltpu.VMEM((nf, SUBLANES, tf), F32))
    else:
        assert m == tm, "sample FFN handles all sequences in one row tile"
        nprev = prev.shape[0]
        sel1, sel2 = _ffn_prev_selectors(m, seq, nprev)
        in_specs += [pl.BlockSpec((nprev, tf), lambda i, f: (0, cur(f))),
                     pl.BlockSpec(sel1.shape, lambda i, f: (0, 0)),
                     pl.BlockSpec(sel2.shape, lambda i, f: (0, 0))]
        args += [prev, sel1, sel2]
        gl_shape = jax.ShapeDtypeStruct((m, D_FF), F32)
        gl_spec = pl.BlockSpec((tm, tf), lambda i, f: (i, cur(f)))
    return pl.pallas_call(
        functools.partial(_ffn_kernel, tm=tm, seq=seq, final_norm=final_norm),
        grid=(m // tm, nf + 1),
        in_specs=in_specs,
        out_specs=(pl.BlockSpec((tm, D_MODEL), lambda i, f: (i, 0)), gl_spec),
        out_shape=(jax.ShapeDtypeStruct((m, D_MODEL), F32), gl_shape),
        scratch_shapes=scratch,
        compiler_params=_cparams(("arbitrary", "arbitrary")),
        name="ffn_sample" if seq is not None else "ffn_prompt",
    )(*args)


def _layer(x, w, lam_init, final_norm, gfin, *, batch, seq, state=None):
    m = x.shape[0]
    lamv = jnp.stack([w["lambda_q1"], w["lambda_k1"], w["lambda_q2"], w["lambda_k2"]])
    tm = min(512, m)
    q, k, v, z, xs, bc, dt = _in_proj(x, w["norm_mix_g"][None, :], w["w_in_main"], w["w_in_dt"],
                                      tm=min(1024, m))

    if state is None:
        slopes = jnp.broadcast_to(
            jnp.asarray(2.0 ** (-8.0 * np.arange(1, ATT_HEADS + 1) / ATT_HEADS), F32)[:, None, None],
            (ATT_HEADS, 1, LANES))
        o = _attn_prompt(q, k, v, slopes, lamv, w["attn_subln_g"][None, :], lam_init, tile=256)
        y, s_new = _ssd(xs, bc, dt, z, w["conv_w"], w["conv_b"], w["dt_bias"], w["a_log"],
                        w["d_skip"], w["ssm_norm_g"], lc=CHUNK, nchunk=4)
        conv_new = jnp.concatenate([xs[m - (SSM_CONV - 1):], bc[m - (SSM_CONV - 1):]], axis=-1)[None]
        s_new = s_new[None]
    else:
        cache_k, cache_v, conv_prev, ssm_prev, ffn_prev = state
        past = cache_k.shape[1]
        o = _attn_sample(q, k, v, cache_k.reshape(batch, past * ATT_HEADS, V_DIM),
                         cache_v.reshape(batch, past * ATT_HEADS, V_DIM),
                         lamv, w["attn_subln_g"][None, :], lam_init, seq)
        prev8 = jnp.pad(conv_prev, ((0, 0), (SUBLANES - (SSM_CONV - 1), 0), (0, 0)))
        y, s_new = _ssd(xs, bc, dt, z, w["conv_w"], w["conv_b"], w["dt_bias"], w["a_log"],
                        w["d_skip"], w["ssm_norm_g"], lc=seq, nchunk=1, conv_prev=prev8,
                        h0=ssm_prev.reshape(batch, D_SSM, SSM_STATE))
        conv_new = jnp.concatenate([xs.reshape(batch, seq, D_SSM)[:, seq - (SSM_CONV - 1):],
                                    bc.reshape(batch, seq, BC_DIM)[:, seq - (SSM_CONV - 1):]], axis=-1)

    x2, hf = _out_proj(o, y, x, w["w_out"], w["norm_ffn_g"][None, :], tm=tm)

    ffn_args = (hf, x2, w["w_gate"], w["w_up"], w["w_down"], w["ffn_conv_w"], w["ffn_conv_b"], gfin)
    if state is None:
        x3, gl = _ffn(*ffn_args, tm=tm, tf=512, final_norm=final_norm)
        ffn_new = gl[None, gl.shape[0] - (FFN_CONV - 1):]
    else:
        x3, gl = _ffn(*ffn_args, tm=tm, tf=512, final_norm=final_norm, seq=seq,
                      prev=ffn_prev.reshape(batch * (FFN_CONV - 1), D_FF))
        ffn_new = gl.reshape(batch, seq, D_FF)[:, seq - (FFN_CONV - 1):]
    k_new = k.reshape(batch, seq, ATT_HEADS, 2 * QK_DIM)
    v_new = v.reshape(batch, seq, ATT_HEADS, V_DIM)
    s_new = s_new.reshape(batch, SSM_HEADS, SSM_HEAD_DIM, SSM_STATE)
    return x3, k_new, v_new, conv_new, s_new, ffn_new


def kernel(x_prompt, x_sample, cache_k, cache_v, state_ssm_conv, state_ssm, state_ffn_conv, norm_mix_g, w_in, lambda_q1, lambda_k1, lambda_q2, lambda_k2, attn_subln_g, conv_w, conv_b, dt_bias, a_log, d_skip, ssm_norm_g, w_out, norm_ffn_g, w_gate, w_up, ffn_conv_w, ffn_conv_b, w_down, norm_final_g):
    depth = w_in.shape[0]
    pb, pl_len, _ = x_prompt.shape
    sb, sl_len, _ = x_sample.shape
    assert pb == 1 and pl_len % CHUNK == 0
    xp = x_prompt.reshape(pb * pl_len, D_MODEL)
    xs = x_sample.reshape(sb * sl_len, D_MODEL)
    n_main = 2 * D_ATT + D_ATT + D_SSM + CONV_DIM
    outs_p, outs_s = [], []
    for layer in range(depth):
        lam_init = 0.8 - 0.6 * math.exp(-0.3 * layer)
        w_in_l = w_in[layer].astype(BF16)
        w = dict(
            norm_mix_g=norm_mix_g[layer],
            w_in_main=w_in_l,
            w_in_dt=jnp.pad(w_in_l[:, n_main:], ((0, 0), (0, LANES - SSM_HEADS))),
            lambda_q1=lambda_q1[layer], lambda_k1=lambda_k1[layer],
            lambda_q2=lambda_q2[layer], lambda_k2=lambda_k2[layer],
            attn_subln_g=attn_subln_g[layer], conv_w=conv_w[layer], conv_b=conv_b[layer],
            dt_bias=dt_bias[layer], a_log=a_log[layer], d_skip=d_skip[layer],
            ssm_norm_g=ssm_norm_g[layer], w_out=w_out[layer].astype(BF16),
            norm_ffn_g=norm_ffn_g[layer], w_gate=w_gate[layer].astype(BF16),
            w_up=w_up[layer].astype(BF16), ffn_conv_w=ffn_conv_w[layer],
            ffn_conv_b=ffn_conv_b[layer], w_down=w_down[layer].astype(BF16),
        )
        last = layer == depth - 1
        xp, *new_p = _layer(xp, w, lam_init, last, norm_final_g, batch=pb, seq=pl_len)
        xs, *new_s = _layer(xs, w, lam_init, last, norm_final_g, batch=sb, seq=sl_len,
                            state=(cache_k[layer], cache_v[layer], state_ssm_conv[layer],
                                   state_ssm[layer], state_ffn_conv[layer]))
        outs_p.append(new_p)
        outs_s.append(new_s)
    stack = lambda outs, idx: jnp.stack([o[idx] for o in outs])
    return (xp.reshape(pb, pl_len, D_MODEL), xs.reshape(sb, sl_len, D_MODEL),
            *[stack(outs_p, idx) for idx in range(5)],
            *[stack(outs_s, idx) for idx in range(5)])
```

```python
import functools
import math

import jax
import jax.numpy as jnp
import numpy as np
from jax import lax
from jax.experimental import pallas as pl
from jax.experimental.pallas import tpu as pltpu

F32 = jnp.float32
BF16 = jnp.bfloat16

D_MODEL = 2048
CHUNK = 64
ATT_HEADS = 8
QK_DIM = 64
V_DIM = 128
D_ATT = ATT_HEADS * V_DIM
SSM_HEADS = 16
SSM_HEAD_DIM = 64
D_SSM = SSM_HEADS * SSM_HEAD_DIM
SSM_GROUPS = 2
SSM_STATE = 128
SSM_CONV = 4
BC_DIM = 2 * SSM_GROUPS * SSM_STATE
CONV_DIM = D_SSM + BC_DIM
D_FF = 5632
FFN_CONV = 3
EPS = 1e-6
LOG2E = math.log2(math.e)
Q_SCALE = QK_DIM ** -0.5 * LOG2E
LANES = 128
SUBLANES = 8
VMEM_LIMIT = 52 * 1024 * 1024

_NT = (((1,), (1,)), ((), ()))
_TN = (((0,), (0,)), ((), ()))


def _cparams(sem):
    return pltpu.CompilerParams(dimension_semantics=sem, vmem_limit_bytes=VMEM_LIMIT)


def _silu(x):
    return x * jax.nn.sigmoid(x)


def _rms(x, g):
    return x * lax.rsqrt(jnp.mean(x * x, axis=-1, keepdims=True) + EPS) * g


_PROJ_TN = 512
_PROJ_SEGS = ((0, 2), (2, 4), (4, 6), (6, 8), (8, 10), (10, 11))


def _in_proj_kernel(x_ref, g_ref, w_ref, wdt_ref,
                    q_ref, k_ref, v_ref, z_ref, xs_ref, bc_ref, dt_ref, h_scr):
    j = pl.program_id(1)

    @pl.when(j == 0)
    def _():
        hb = _rms(x_ref[...], g_ref[...]).astype(BF16)
        h_scr[...] = hb
        dt_ref[...] = jnp.dot(hb, wdt_ref[...], preferred_element_type=F32)

    res = jnp.dot(h_scr[...], w_ref[...], preferred_element_type=F32)
    outs = (q_ref, k_ref, v_ref, z_ref, xs_ref, bc_ref)
    for (lo, hi), ref in zip(_PROJ_SEGS, outs):
        @pl.when((j >= lo) & (j < hi))
        def _(ref=ref):
            val = res * Q_SCALE if ref is q_ref else res
            ref[...] = val.astype(ref.dtype)


def _in_proj(x, g, w_main, w_dt, tm):
    m = x.shape[0]
    tn = _PROJ_TN
    nj = w_main.shape[1] // tn

    def seg_spec(lo, hi):
        return pl.BlockSpec((tm, tn), lambda i, j: (i, jnp.clip(j - lo, 0, hi - lo - 1)))

    out_shape = (
        jax.ShapeDtypeStruct((m, D_ATT), BF16),
        jax.ShapeDtypeStruct((m, D_ATT), F32),
        jax.ShapeDtypeStruct((m, D_ATT), F32),
        jax.ShapeDtypeStruct((m, D_SSM), F32),
        jax.ShapeDtypeStruct((m, D_SSM), F32),
        jax.ShapeDtypeStruct((m, BC_DIM), F32),
        jax.ShapeDtypeStruct((m, LANES), F32),
    )
    out_specs = tuple(seg_spec(lo, hi) for lo, hi in _PROJ_SEGS) + (
        pl.BlockSpec((tm, LANES), lambda i, j: (i, 0)),)
    return pl.pallas_call(
        _in_proj_kernel,
        grid=(m // tm, nj),
        in_specs=[
            pl.BlockSpec((tm, D_MODEL), lambda i, j: (i, 0)),
            pl.BlockSpec((1, D_MODEL), lambda i, j: (0, 0)),
            pl.BlockSpec((D_MODEL, tn), lambda i, j: (0, j)),
            pl.BlockSpec((D_MODEL, LANES), lambda i, j: (0, 0)),
        ],
        out_specs=out_specs,
        out_shape=out_shape,
        scratch_shapes=[pltpu.VMEM((tm, D_MODEL), BF16)],
        compiler_params=_cparams(("arbitrary", "arbitrary")),
        name="in_proj",
    )(x, g, w_main, w_dt)


def _lambda_value(lamv_ref, lam_init):
    lv = lamv_ref[...]
    s1 = jnp.sum(lv[0:1] * lv[1:2], axis=-1, keepdims=True)
    s2 = jnp.sum(lv[2:3] * lv[3:4], axis=-1, keepdims=True)
    return jnp.exp(s1) - jnp.exp(s2) + lam_init


def _split_maps(q):
    lane = lax.broadcasted_iota(jnp.int32, q.shape, 1)
    zero = jnp.zeros_like(q)
    return jnp.where(lane < QK_DIM, q, zero), jnp.where(lane >= QK_DIM, q, zero)


_VT_CHUNK = 512
_VT_PAD = 16
_UNROLL_SHIFT = 2
_UNROLL = 1 << _UNROLL_SHIFT
_HEADS_PER_STEP = 2


def _attn_prompt_kernel(q_ref, k_ref, v_ref, slope_ref, lamv_ref, g_ref, dmask_ref, dneg_ref, o_ref,
                        k_scr, vt_scr, acc_scr, t0_scr, t1_scr, p0_scr, p1_scr,
                        *, tile, lam_init):
    length = k_ref.shape[0]
    width = 2 * tile
    nheads = k_scr.shape[0]
    lam = _lambda_value(lamv_ref, lam_init)

    lane = lax.broadcasted_iota(jnp.int32, (length, LANES), 1)
    koff = lax.broadcasted_iota(jnp.int32, (length, LANES), 0) % tile
    koff = jnp.where(lane < 3, koff, 0).astype(F32).astype(BF16)
    extra = (lax.broadcasted_iota(jnp.int32, (_VT_PAD, tile), 0) == 0).astype(F32).astype(BF16)
    brow = lax.broadcasted_iota(jnp.int32, (LANES, width), 0)
    per = _VT_CHUNK // tile
    ws, q_biases = [], []
    for hh in range(nheads):
        cols = slice(hh * V_DIM, (hh + 1) * V_DIM)
        k_scr[hh, :, 0:V_DIM] = k_ref[:, cols].astype(BF16)
        k_scr[hh, :, V_DIM:V_DIM + LANES] = koff
        for c in range(length // _VT_CHUNK):
            vt = v_ref[c * _VT_CHUNK:(c + 1) * _VT_CHUNK, cols].T.astype(BF16)
            for s in range(per):
                vt_scr[hh, c * per + s, 0:V_DIM, :] = vt[:, s * tile:(s + 1) * tile]
                vt_scr[hh, c * per + s, V_DIM:V_DIM + _VT_PAD, :] = extra
        w = slope_ref[hh][:, :1] * LOG2E
        w_hi, w_mid, w_lo = (piece.astype(F32) for piece in _split3(w))
        q_bias = jnp.where(brow == 0, w_hi, jnp.where(brow == 1, w_mid, jnp.where(brow == 2, w_lo, 0.0)))
        ws.append(w)
        q_biases.append(q_bias.astype(BF16))

    def q_tile(qi, _):
        _attn_prompt_tile(qi, ws, lam, q_biases, q_ref, g_ref, dmask_ref, dneg_ref, o_ref,
                          k_scr, vt_scr, acc_scr, (t0_scr, t1_scr), (p0_scr, p1_scr),
                          tile=tile, lam_init=lam_init)
        return 0

    lax.fori_loop(0, length // tile, q_tile, 0)


def _attn_prompt_tile(qi, ws, lam, q_biases, q_ref, g_ref, dmask_ref, dneg_ref, o_ref,
                      k_scr, vt_scr, acc_scr, t_slots, p_slots, *, tile, lam_init):
    width = 2 * tile
    nheads = len(ws)
    rows = pl.ds(pl.multiple_of(qi * tile, tile), tile)
    q_augs = []
    for hh in range(nheads):
        q1, q2 = _split_maps(q_ref[rows, hh * V_DIM:(hh + 1) * V_DIM])
        q_t = jnp.concatenate([q1, q2], axis=0).astype(F32).T.astype(BF16)
        q_augs.append(jnp.concatenate([q_t, q_biases[hh]], axis=0))

    def scores(hh, j):
        k0 = pl.multiple_of(jnp.minimum(j, qi) * tile, tile)
        return jnp.dot(k_scr[hh, pl.ds(k0, tile), :], q_augs[hh], preferred_element_type=F32)

    def step(hh, j, slot, alpha_prev, m_old, tmax_cur):
        pv = jnp.dot(vt_scr[hh, jnp.clip(j - 1, 0, qi)], p_slots[1 - slot][hh],
                     preferred_element_type=F32)
        acc_scr[hh] = alpha_prev * acc_scr[hh] + pv
        t_next = scores(hh, j + 1)
        t_slots[1 - slot][hh] = t_next
        tmax_next = jnp.max(t_next, axis=0, keepdims=True)
        off = jnp.where(j < qi, -ws[hh] * ((qi - j) * tile).astype(F32), -jnp.inf)
        m_new = jnp.maximum(m_old, tmax_cur + off)
        p_slots[slot][hh] = jnp.exp2(t_slots[slot][hh] + (off - m_new)).astype(BF16)
        return jnp.exp2(m_old - m_new), m_new, tmax_next

    def run(first, ntiles_per_iter, niter, carry):
        def body(i, carry):
            for s in range(ntiles_per_iter):
                carry = tuple(step(hh, first + ntiles_per_iter * i + s, s % 2, *carry[hh])
                              for hh in range(nheads))
            return carry
        return lax.fori_loop(0, niter, body, carry)

    carry = []
    for hh in range(nheads):
        acc_scr[hh] = jnp.zeros(acc_scr.shape[1:], F32)
        p_slots[1][hh] = jnp.zeros(p_slots[1].shape[1:], BF16)
        t_first = scores(hh, 0)
        t_slots[0][hh] = t_first
        carry.append((jnp.ones((1, width), F32),
                      jnp.full((1, width), -0.5 * float(jnp.finfo(F32).max), F32),
                      jnp.max(t_first, axis=0, keepdims=True)))
    n_main = lax.shift_right_logical(qi, _UNROLL_SHIFT)
    done = n_main * _UNROLL
    n_rest = lax.shift_right_logical(qi - done + 1, 1)
    carry = run(0, _UNROLL, n_main, tuple(carry))
    carry = run(done, 2, n_rest, carry)
    done = done + 2 * n_rest

    for hh in range(nheads):
        alpha_last, m_old, _ = carry[hh]
        pv = jnp.dot(vt_scr[hh, jnp.clip(done - 1, 0, qi)], p_slots[1][hh], preferred_element_type=F32)
        acc = alpha_last * acc_scr[hh] + pv
        t = t_slots[0][hh] + (ws[hh] * dmask_ref[...] + dneg_ref[...])
        m_new = jnp.maximum(m_old, jnp.max(t, axis=0, keepdims=True))
        p = jnp.exp2(t - m_new).astype(BF16)
        acc = jnp.exp2(m_old - m_new) * acc + jnp.dot(vt_scr[hh, qi], p, preferred_element_type=F32)

        on = acc[0:V_DIM] / acc[V_DIM:V_DIM + 1]
        o_t = on[:, :tile] - lam * on[:, tile:]
        o_t = o_t * lax.rsqrt(jnp.mean(o_t * o_t, axis=0, keepdims=True) + EPS)
        o_ref[rows, hh * V_DIM:(hh + 1) * V_DIM] = (
            o_t.T * g_ref[...] * (1.0 - lam_init)).astype(o_ref.dtype)


def _attn_prompt(q, k, v, slopes, lamv, g, lam_init, tile):
    length = q.shape[0]
    key = np.arange(tile)[:, None]
    qry = np.tile(np.arange(tile), 2)[None, :]
    dmask = jnp.asarray((qry - np.abs(qry - key)) - key, F32)
    dneg = jnp.asarray(np.where(key // CHUNK <= qry // CHUNK, 0.0, -np.inf), F32)
    nh = _HEADS_PER_STEP
    diag_spec = pl.BlockSpec((tile, 2 * tile), lambda h: (0, 0))
    head_spec = pl.BlockSpec((length, nh * V_DIM), lambda h: (0, h))
    kv_spec = pl.BlockSpec((length, nh * V_DIM), lambda h: (0, h), pipeline_mode=pl.Buffered(1))
    return pl.pallas_call(
        functools.partial(_attn_prompt_kernel, tile=tile, lam_init=lam_init),
        grid=(ATT_HEADS // nh,),
        in_specs=[
            kv_spec, kv_spec, kv_spec,
            pl.BlockSpec((nh, 1, LANES), lambda h: (h, 0, 0)),
            pl.BlockSpec((4, QK_DIM), lambda h: (0, 0)),
            pl.BlockSpec((1, V_DIM), lambda h: (0, 0)),
            diag_spec, diag_spec,
        ],
        out_specs=head_spec,
        out_shape=jax.ShapeDtypeStruct((length, D_ATT), BF16),
        scratch_shapes=[
            pltpu.VMEM((nh, length, V_DIM + LANES), BF16),
            pltpu.VMEM((nh, length // tile, V_DIM + _VT_PAD, tile), BF16),
            pltpu.VMEM((nh, V_DIM + _VT_PAD, 2 * tile), F32),
            pltpu.VMEM((nh, tile, 2 * tile), F32),
            pltpu.VMEM((nh, tile, 2 * tile), F32),
            pltpu.VMEM((nh, tile, 2 * tile), BF16),
            pltpu.VMEM((nh, tile, 2 * tile), BF16),
        ],
        compiler_params=_cparams(("arbitrary",)),
        name="attn_prompt",
    )(q, k, v, slopes, lamv, g, dmask, dneg)


def _attn_sample_kernel(q_ref, kn_ref, vn_ref, ck_ref, cv_ref, lamv_ref, g_ref, o_ref,
                        *, seq, past, lam_init):
    lam = _lambda_value(lamv_ref, lam_init)
    r_c = (lax.broadcasted_iota(jnp.int32, (2 * seq, past), 0) % seq
           - lax.broadcasted_iota(jnp.int32, (2 * seq, past), 1) + past).astype(F32)
    r_n = jnp.abs(lax.broadcasted_iota(jnp.int32, (2 * seq, seq), 0) % seq
                  - lax.broadcasted_iota(jnp.int32, (2 * seq, seq), 1)).astype(F32)
    for h in range(ATT_HEADS):
        w = 2.0 ** (-8.0 * (h + 1) / ATT_HEADS) * LOG2E
        sl = slice(h * V_DIM, (h + 1) * V_DIM)
        q1, q2 = _split_maps(q_ref[:, sl])
        qq = jnp.concatenate([q1, q2], axis=0)
        kc = ck_ref[0, pl.ds(h, past, stride=ATT_HEADS), :].astype(BF16)
        vc = cv_ref[0, pl.ds(h, past, stride=ATT_HEADS), :].astype(BF16)
        kn = kn_ref[:, sl].astype(BF16)
        vn = vn_ref[:, sl].astype(BF16)
        sc = lax.dot_general(qq, kc, _NT, preferred_element_type=F32) - w * r_c
        sn = lax.dot_general(qq, kn, _NT, preferred_element_type=F32) - w * r_n
        m = jnp.maximum(jnp.max(sc, axis=-1, keepdims=True), jnp.max(sn, axis=-1, keepdims=True))
        pc = jnp.exp2(sc - m)
        pn = jnp.exp2(sn - m)
        l = jnp.sum(pc, axis=-1, keepdims=True) + jnp.sum(pn, axis=-1, keepdims=True)
        acc = (jnp.dot(pc.astype(BF16), vc, preferred_element_type=F32)
               + jnp.dot(pn.astype(BF16), vn, preferred_element_type=F32))
        on = acc / l
        o = on[:seq] - lam * on[seq:]
        o_ref[:, sl] = (_rms(o, g_ref[...]) * (1.0 - lam_init)).astype(o_ref.dtype)


def _attn_sample(q, k_new, v_new, cache_k, cache_v, lamv, g, lam_init, seq):
    nb, past = cache_k.shape[0], cache_k.shape[1] // ATT_HEADS
    row_spec = pl.BlockSpec((seq, D_ATT), lambda b: (b, 0))
    cache_spec = pl.BlockSpec((1, past * ATT_HEADS, V_DIM), lambda b: (b, 0, 0))
    return pl.pallas_call(
        functools.partial(_attn_sample_kernel, seq=seq, past=past, lam_init=lam_init),
        grid=(nb,),
        in_specs=[row_spec, row_spec, row_spec, cache_spec, cache_spec,
                  pl.BlockSpec((4, QK_DIM), lambda b: (0, 0)),
                  pl.BlockSpec((1, V_DIM), lambda b: (0, 0))],
        out_specs=row_spec,
        out_shape=jax.ShapeDtypeStruct((nb * seq, D_ATT), BF16),
        compiler_params=_cparams(("arbitrary",)),
        name="attn_sample",
    )(q, k_new, v_new, cache_k, cache_v, lamv, g)


def _split3(x):
    hi = x.astype(BF16)
    r = x - hi.astype(F32)
    mid = r.astype(BF16)
    lo = (r - mid.astype(F32)).astype(BF16)
    return hi, mid, lo


def _ssd_chunk(xs, bm, cm, dt, z, s_ref, a_heads, dskip, gn, e3, es3, t3, ones_k, lc):
    seg_w = SSM_HEADS * lc
    half = D_SSM // SSM_GROUPS
    d3 = jnp.concatenate(_split3(dt), axis=1)
    dt_x = jnp.dot(d3, e3, preferred_element_type=F32)
    la3 = jnp.concatenate(_split3(dt * a_heads), axis=0)
    ac3 = jnp.concatenate(_split3(jnp.dot(t3, la3, preferred_element_type=F32)), axis=1)
    acol = jnp.dot(ac3, e3, preferred_element_type=F32)
    acol_s = acol if seg_w == D_SSM else jnp.dot(ac3, es3, preferred_element_type=F32)
    t_idx = lax.broadcasted_iota(jnp.int32, (lc, seg_w), 0)
    s_idx = lax.broadcasted_iota(jnp.int32, (lc, seg_w), 1) % lc
    arow = jnp.sum(jnp.where(t_idx == s_idx, acol_s, 0.0), axis=0, keepdims=True)
    decay = jnp.where(t_idx >= s_idx, jnp.exp(acol_s - arow), 0.0)

    cb16 = cm.astype(BF16)
    bb16 = bm.astype(BF16)
    hpg = SSM_HEADS // SSM_GROUPS
    cbs = []
    for g in range(SSM_GROUPS):
        gs = slice(g * SSM_STATE, (g + 1) * SSM_STATE)
        b_rep = jnp.concatenate([bb16[:, gs]] * hpg, axis=0)
        cbs.append(lax.dot_general(cb16[:, gs], b_rep, _NT, preferred_element_type=F32))
    mmat = (jnp.concatenate(cbs, axis=1) * decay).astype(BF16)

    xdt = xs * dt_x
    xdt16 = xdt.astype(BF16)
    hk = (2 * LANES) // lc
    wd = hk * SSM_HEAD_DIM
    blk = (lax.broadcasted_iota(jnp.int32, (hk * lc, wd), 0) // lc
           == lax.broadcasted_iota(jnp.int32, (hk * lc, wd), 1) // SSM_HEAD_DIM)
    parts = []
    for i in range(SSM_HEADS // hk):
        xd = xdt16[:, i * wd:(i + 1) * wd]
        bd = jnp.where(blk, jnp.concatenate([xd] * hk, axis=0), jnp.zeros((), BF16))
        parts.append(jnp.dot(mmat[:, i * hk * lc:(i + 1) * hk * lc], bd, preferred_element_type=F32))
    y_intra = jnp.concatenate(parts, axis=1) if len(parts) > 1 else parts[0]

    yi = []
    for g in range(SSM_GROUPS):
        sg = s_ref[g * half:(g + 1) * half, :].astype(BF16)
        yi.append(lax.dot_general(cb16[:, g * SSM_STATE:(g + 1) * SSM_STATE], sg, _NT,
                                  preferred_element_type=F32))
    y_inter = jnp.exp(acol) * jnp.concatenate(yi, axis=1)

    alast = acol[lc - 1:lc, :]
    dec_end = jnp.exp(alast - acol)
    xd_end = (xdt * dec_end).astype(BF16)
    krow = lax.broadcasted_iota(jnp.int32, (ones_k.shape[0], D_SSM), 0)
    a_hi, a_mid, a_lo = (piece.astype(F32) for piece in _split3(alast))
    pieces = jnp.where(krow == 0, a_hi, jnp.where(krow == 1, a_mid, jnp.where(krow == 2, a_lo, 0.0)))
    acl = lax.dot_general(pieces.astype(BF16), ones_k, _TN, preferred_element_type=F32)
    for g in range(SSM_GROUPS):
        rows = slice(g * half, (g + 1) * half)
        upd = lax.dot_general(xd_end[:, rows], bb16[:, g * SSM_STATE:(g + 1) * SSM_STATE], _TN,
                              preferred_element_type=F32)
        s_ref[rows, :] = jnp.exp(acl[rows, :]) * s_ref[rows, :] + upd

    y = (y_intra + y_inter + dskip * xs) * _silu(z)
    outs = []
    for g in range(SSM_GROUPS):
        cs = slice(g * half, (g + 1) * half)
        outs.append(_rms(y[:, cs], gn[:, cs]))
    return jnp.concatenate(outs, axis=1)


def _ssd_kernel(*refs, lc, nchunk, has_state):
    if has_state:
        (xs_ref, bc_ref, dt_ref, z_ref, prev_ref, h0_ref, cw_ref, cbias_ref, dtb_ref, alog_ref,
         dskip_ref, gn_ref, e3_ref, es3_ref, t3_ref, ones_ref,
         y_ref, s_ref, scr_x, scr_bc) = refs
    else:
        (xs_ref, bc_ref, dt_ref, z_ref, cw_ref, cbias_ref, dtb_ref, alog_ref,
         dskip_ref, gn_ref, e3_ref, es3_ref, t3_ref, ones_ref,
         y_ref, s_ref, scr_x, scr_bc) = refs
    rows = lc * nchunk
    pad = SUBLANES
    if has_state:
        s2d = s_ref.at[0]
        scr_x[0:pad, :] = prev_ref[0, :, 0:D_SSM]
        scr_bc[0:pad, :] = prev_ref[0, :, D_SSM:CONV_DIM]
        s2d[...] = h0_ref[0]
    else:
        s2d = s_ref

        @pl.when(pl.program_id(0) == 0)
        def _():
            scr_x[0:pad, :] = jnp.zeros((pad, D_SSM), F32)
            scr_bc[0:pad, :] = jnp.zeros((pad, BC_DIM), F32)
            s_ref[...] = jnp.zeros(s_ref.shape, F32)

    scr_x[pad:pad + rows, :] = xs_ref[...]
    scr_bc[pad:pad + rows, :] = bc_ref[...]
    xc = cbias_ref[:, 0:D_SSM]
    bcc = cbias_ref[:, D_SSM:CONV_DIM]
    for tap in range(SSM_CONV):
        off = pad - (SSM_CONV - 1) + tap
        xc = xc + scr_x[off:off + rows, :] * cw_ref[tap:tap + 1, 0:D_SSM]
        bcc = bcc + scr_bc[off:off + rows, :] * cw_ref[tap:tap + 1, D_SSM:CONV_DIM]
    if not has_state:
        scr_x[0:pad, :] = scr_x[rows:rows + pad, :]
        scr_bc[0:pad, :] = scr_bc[rows:rows + pad, :]
    xs_act = _silu(xc)
    bc_act = _silu(bcc)
    dt_in = dt_ref[...] + dtb_ref[...]
    dt = jnp.maximum(dt_in, 0.0) + jnp.log1p(jnp.exp(-jnp.abs(dt_in)))
    a_heads = -jnp.exp(alog_ref[...])
    nbm = SSM_GROUPS * SSM_STATE
    for c in range(nchunk):
        rs = slice(c * lc, (c + 1) * lc)
        y = _ssd_chunk(xs_act[rs], bc_act[rs, 0:nbm], bc_act[rs, nbm:2 * nbm], dt[rs],
                       z_ref[rs, :], s2d, a_heads, dskip_ref[...], gn_ref[...],
                       e3_ref[...], es3_ref[...], t3_ref[...], ones_ref[...], lc)
        y_ref[rs, :] = y.astype(y_ref.dtype)


def _ssd_constants(lc):
    seg_w = SSM_HEADS * lc
    head_of_lane = np.arange(D_SSM) // SSM_HEAD_DIM
    e = (np.arange(LANES)[:, None] == head_of_lane[None, :]).astype(np.float32)
    es = (np.arange(LANES)[:, None] == (np.arange(seg_w) // lc)[None, :]).astype(np.float32)
    tri = np.tril(np.ones((lc, lc), np.float32))
    return (jnp.asarray(np.concatenate([e] * 3, axis=0), BF16),
            jnp.asarray(np.concatenate([es] * 3, axis=0), BF16),
            jnp.asarray(np.concatenate([tri] * 3, axis=1), BF16),
            jnp.ones((2 * SUBLANES, SSM_STATE), BF16))


def _ssd(xs, bc, dt, z, conv_w, conv_b, dt_bias, a_log, d_skip, gn, lc, nchunk,
         conv_prev=None, h0=None):
    m = xs.shape[0]
    rows = lc * nchunk
    has_state = h0 is not None
    e3, es3, t3, ones_k = _ssd_constants(lc)
    dtb = jnp.zeros((1, LANES), F32).at[0, :SSM_HEADS].set(dt_bias)
    alog = jnp.zeros((1, LANES), F32).at[0, :SSM_HEADS].set(a_log)
    dskip_x = jnp.repeat(d_skip, SSM_HEAD_DIM)[None, :]

    def const(shape):
        return pl.BlockSpec(shape, lambda i: (0,) * len(shape))

    def rowblk(width):
        return pl.BlockSpec((rows, width), lambda i: (i, 0))

    in_specs = [rowblk(D_SSM), rowblk(BC_DIM), rowblk(LANES), rowblk(D_SSM)]
    args = [xs, bc, dt, z]
    if has_state:
        nb = h0.shape[0]
        in_specs += [pl.BlockSpec((1, SUBLANES, CONV_DIM), lambda i: (i, 0, 0)),
                     pl.BlockSpec((1, D_SSM, SSM_STATE), lambda i: (i, 0, 0))]
        args += [conv_prev, h0]
        s_shape = jax.ShapeDtypeStruct((nb, D_SSM, SSM_STATE), F32)
        s_spec = pl.BlockSpec((1, D_SSM, SSM_STATE), lambda i: (i, 0, 0))
    else:
        s_shape = jax.ShapeDtypeStruct((D_SSM, SSM_STATE), F32)
        s_spec = const((D_SSM, SSM_STATE))
    in_specs += [const((SSM_CONV, CONV_DIM)), const((1, CONV_DIM)), const((1, LANES)),
                 const((1, LANES)), const((1, D_SSM)), const((1, D_SSM)),
                 const(e3.shape), const(es3.shape), const(t3.shape), const(ones_k.shape)]
    args += [conv_w, conv_b[None, :], dtb, alog, dskip_x, gn[None, :], e3, es3, t3, ones_k]
    return pl.pallas_call(
        functools.partial(_ssd_kernel, lc=lc, nchunk=nchunk, has_state=has_state),
        grid=(m // rows,),
        in_specs=in_specs,
        out_specs=(rowblk(D_SSM), s_spec),
        out_shape=(jax.ShapeDtypeStruct((m, D_SSM), BF16), s_shape),
        scratch_shapes=[pltpu.VMEM((rows + SUBLANES, D_SSM), F32),
                        pltpu.VMEM((rows + SUBLANES, BC_DIM), F32)],
        compiler_params=_cparams(("arbitrary",)),
        name="ssd_sample" if has_state else "ssd_prompt",
    )(*args)


def _out_proj_kernel(o_ref, y_ref, x_ref, w_ref, g_ref, x2_ref, hf_ref):
    x2 = (x_ref[...]
          + jnp.dot(o_ref[...], w_ref[0:D_ATT, :], preferred_element_type=F32)
          + jnp.dot(y_ref[...], w_ref[D_ATT:D_ATT + D_SSM, :], preferred_element_type=F32))
    x2_ref[...] = x2
    hf_ref[...] = _rms(x2, g_ref[...]).astype(hf_ref.dtype)


def _out_proj(o, y, x, w, g, tm):
    m = x.shape[0]
    return pl.pallas_call(
        _out_proj_kernel,
        grid=(m // tm,),
        in_specs=[
            pl.BlockSpec((tm, D_ATT), lambda i: (i, 0)),
            pl.BlockSpec((tm, D_SSM), lambda i: (i, 0)),
            pl.BlockSpec((tm, D_MODEL), lambda i: (i, 0)),
            pl.BlockSpec((D_ATT + D_SSM, D_MODEL), lambda i: (0, 0)),
            pl.BlockSpec((1, D_MODEL), lambda i: (0, 0)),
        ],
        out_specs=(pl.BlockSpec((tm, D_MODEL), lambda i: (i, 0)),
                   pl.BlockSpec((tm, D_MODEL), lambda i: (i, 0))),
        out_shape=(jax.ShapeDtypeStruct((m, D_MODEL), F32),
                   jax.ShapeDtypeStruct((m, D_MODEL), BF16)),
        compiler_params=_cparams(("arbitrary",)),
        name="out_proj",
    )(o, y, x, w, g)


def _ffn_kernel(*refs, tm, seq, final_norm):
    if seq is None:
        (hf_ref, x2_ref, wg_ref, wu_ref, wd_ref, cw_ref, cb_ref, gfin_ref,
         out_ref, gl_ref, g_scr, act_scr, carry_scr) = refs
    else:
        (hf_ref, x2_ref, wg_ref, wu_ref, wd_ref, cw_ref, cb_ref, gfin_ref, prev_ref, sel1_ref, sel2_ref,
         out_ref, gl_ref, g_scr, act_scr) = refs
    i = pl.program_id(0)
    f = pl.program_id(1)
    nf = pl.num_programs(1) - 1
    pad = SUBLANES

    def gate_up():
        hf = hf_ref[...]
        gate = jnp.dot(hf, wg_ref[...], preferred_element_type=F32)
        up = jnp.dot(hf, wu_ref[...], preferred_element_type=F32)
        g_scr[pad:pad + tm, :] = gate
        if seq is None:
            g_scr[0:pad, :] = carry_scr[f]
            carry_scr[f] = g_scr[tm:tm + pad, :]
            gl_ref[...] = g_scr[tm:tm + pad, :]
            g1 = g_scr[pad - 1:pad - 1 + tm, :]
            g2 = g_scr[pad - 2:pad - 2 + tm, :]
        else:
            g_scr[0:pad, :] = jnp.zeros((pad, gate.shape[1]), F32)
            gl_ref[...] = gate
            prev3 = jnp.concatenate(_split3(prev_ref[...]), axis=0)
            ov1 = jnp.dot(sel1_ref[...], prev3, preferred_element_type=F32)
            ov2 = jnp.dot(sel2_ref[...], prev3, preferred_element_type=F32)
            pos = lax.broadcasted_iota(jnp.int32, gate.shape, 0) % seq
            g1 = jnp.where(pos == 0, ov1, g_scr[pad - 1:pad - 1 + tm, :])
            g2 = jnp.where(pos < 2, ov2, g_scr[pad - 2:pad - 2 + tm, :])
        conv = cb_ref[...] + g2 * cw_ref[0:1, :] + g1 * cw_ref[1:2, :] + gate * cw_ref[2:3, :]
        act_scr[...] = (_silu(conv) * up).astype(BF16)

    def down():
        return jnp.dot(act_scr[...], wd_ref[...], preferred_element_type=F32)

    @pl.when(f == 0)
    def _():
        if seq is None:
            @pl.when(i == 0)
            def _():
                carry_scr[...] = jnp.zeros(carry_scr.shape, F32)
        out_ref[...] = x2_ref[...]
        gate_up()

    @pl.when((f > 0) & (f < nf))
    def _():
        contrib = down()
        gate_up()
        out_ref[...] += contrib

    @pl.when(f == nf)
    def _():
        res = out_ref[...] + down()
        out_ref[...] = _rms(res, gfin_ref[...]) if final_norm else res


def _ffn_prev_selectors(m, seq, nprev):
    sel1 = np.zeros((m, 3 * nprev), np.float32)
    sel2 = np.zeros((m, 3 * nprev), np.float32)
    nstate = FFN_CONV - 1
    for b in range(m // seq):
        for piece in range(3):
            base = piece * nprev + b * nstate
            sel1[b * seq, base + 1] = 1.0
            sel2[b * seq, base + 0] = 1.0
            sel2[b * seq + 1, base + 1] = 1.0
    return jnp.asarray(sel1, BF16), jnp.asarray(sel2, BF16)


def _ffn(hf, x2, wg, wu, wd, cw, cb, gfin, tm, tf, final_norm, seq=None, prev=None):
    m = hf.shape[0]
    nf = D_FF // tf
    cur = lambda f: jnp.minimum(f, nf - 1)
    last = lambda f: jnp.maximum(f - 1, 0)
    in_specs = [
        pl.BlockSpec((tm, D_MODEL), lambda i, f: (i, 0)),
        pl.BlockSpec((tm, D_MODEL), lambda i, f: (i, 0)),
        pl.BlockSpec((D_MODEL, tf), lambda i, f: (0, cur(f))),
        pl.BlockSpec((D_MODEL, tf), lambda i, f: (0, cur(f))),
        pl.BlockSpec((tf, D_MODEL), lambda i, f: (last(f), 0)),
        pl.BlockSpec((FFN_CONV, tf), lambda i, f: (0, cur(f))),
        pl.BlockSpec((1, tf), lambda i, f: (0, cur(f))),
        pl.BlockSpec((1, D_MODEL), lambda i, f: (0, 0)),
    ]
    args = [hf, x2, wg, wu, wd, cw, cb[None, :], gfin[None, :]]
    scratch = [pltpu.VMEM((tm + SUBLANES, tf), F32), pltpu.VMEM((tm, tf), BF16)]
    if seq is None:
        gl_shape = jax.ShapeDtypeStruct((m // tm * SUBLANES, D_FF), F32)
        gl_spec = pl.BlockSpec((SUBLANES, tf), lambda i, f: (i, cur(f)))
        scratch.append(pltpu.VMEM((nf, SUBLANES, tf), F32))
    else:
        assert m == tm, "sample FFN handles all sequences in one row tile"
        nprev = prev.shape[0]
        sel1, sel2 = _ffn_prev_selectors(m, seq, nprev)
        in_specs += [pl.BlockSpec((nprev, tf), lambda i, f: (0, cur(f))),
                     pl.BlockSpec(sel1.shape, lambda i, f: (0, 0)),
                     pl.BlockSpec(sel2.shape, lambda i, f: (0, 0))]
        args += [prev, sel1, sel2]
        gl_shape = jax.ShapeDtypeStruct((m, D_FF), F32)
        gl_spec = pl.BlockSpec((tm, tf), lambda i, f: (i, cur(f)))
    return pl.pallas_call(
        functools.partial(_ffn_kernel, tm=tm, seq=seq, final_norm=final_norm),
        grid=(m // tm, nf + 1),
        in_specs=in_specs,
        out_specs=(pl.BlockSpec((tm, D_MODEL), lambda i, f: (i, 0)), gl_spec),
        out_shape=(jax.ShapeDtypeStruct((m, D_MODEL), F32), gl_shape),
        scratch_shapes=scratch,
        compiler_params=_cparams(("arbitrary", "arbitrary")),
        name="ffn_sample" if seq is not None else "ffn_prompt",
    )(*args)


def _layer(x, w, lam_init, final_norm, gfin, *, batch, seq, state=None):
    m = x.shape[0]
    lamv = jnp.stack([w["lambda_q1"], w["lambda_k1"], w["lambda_q2"], w["lambda_k2"]])
    tm = min(512, m)
    q, k, v, z, xs, bc, dt = _in_proj(x, w["norm_mix_g"][None, :], w["w_in_main"], w["w_in_dt"],
                                      tm=min(1024, m))

    if state is None:
        slopes = jnp.broadcast_to(
            jnp.asarray(2.0 ** (-8.0 * np.arange(1, ATT_HEADS + 1) / ATT_HEADS), F32)[:, None, None],
            (ATT_HEADS, 1, LANES))
        o = _attn_prompt(q, k, v, slopes, lamv, w["attn_subln_g"][None, :], lam_init, tile=256)
        y, s_new = _ssd(xs, bc, dt, z, w["conv_w"], w["conv_b"], w["dt_bias"], w["a_log"],
                        w["d_skip"], w["ssm_norm_g"], lc=CHUNK, nchunk=4)
        conv_new = jnp.concatenate([xs[m - (SSM_CONV - 1):], bc[m - (SSM_CONV - 1):]], axis=-1)[None]
        s_new = s_new[None]
    else:
        cache_k, cache_v, conv_prev, ssm_prev, ffn_prev = state
        past = cache_k.shape[1]
        o = _attn_sample(q, k, v, cache_k.reshape(batch, past * ATT_HEADS, V_DIM),
                         cache_v.reshape(batch, past * ATT_HEADS, V_DIM),
                         lamv, w["attn_subln_g"][None, :], lam_init, seq)
        prev8 = jnp.pad(conv_prev, ((0, 0), (SUBLANES - (SSM_CONV - 1), 0), (0, 0)))
        y, s_new = _ssd(xs, bc, dt, z, w["conv_w"], w["conv_b"], w["dt_bias"], w["a_log"],
                        w["d_skip"], w["ssm_norm_g"], lc=seq, nchunk=1, conv_prev=prev8,
                        h0=ssm_prev.reshape(batch, D_SSM, SSM_STATE))
        conv_new = jnp.concatenate([xs.reshape(batch, seq, D_SSM)[:, seq - (SSM_CONV - 1):],
                                    bc.reshape(batch, seq, BC_DIM)[:, seq - (SSM_CONV - 1):]], axis=-1)

    x2, hf = _out_proj(o, y, x, w["w_out"], w["norm_ffn_g"][None, :], tm=tm)

    ffn_args = (hf, x2, w["w_gate"], w["w_up"], w["w_down"], w["ffn_conv_w"], w["ffn_conv_b"], gfin)
    if state is None:
        x3, gl = _ffn(*ffn_args, tm=tm, tf=512, final_norm=final_norm)
        ffn_new = gl[None, gl.shape[0] - (FFN_CONV - 1):]
    else:
        x3, gl = _ffn(*ffn_args, tm=tm, tf=512, final_norm=final_norm, seq=seq,
                      prev=ffn_prev.reshape(batch * (FFN_CONV - 1), D_FF))
        ffn_new = gl.reshape(batch, seq, D_FF)[:, seq - (FFN_CONV - 1):]
    k_new = k.reshape(batch, seq, ATT_HEADS, 2 * QK_DIM)
    v_new = v.reshape(batch, seq, ATT_HEADS, V_DIM)
    s_new = s_new.reshape(batch, SSM_HEADS, SSM_HEAD_DIM, SSM_STATE)
    return x3, k_new, v_new, conv_new, s_new, ffn_new


def kernel(x_prompt, x_sample, cache_k, cache_v, state_ssm_conv, state_ssm, state_ffn_conv, norm_mix_g, w_in, lambda_q1, lambda_k1, lambda_q2, lambda_k2, attn_subln_g, conv_w, conv_b, dt_bias, a_log, d_skip, ssm_norm_g, w_out, norm_ffn_g, w_gate, w_up, ffn_conv_w, ffn_conv_b, w_down, norm_final_g):
    depth = w_in.shape[0]
    pb, pl_len, _ = x_prompt.shape
    sb, sl_len, _ = x_sample.shape
    assert pb == 1 and pl_len % CHUNK == 0
    xp = x_prompt.reshape(pb * pl_len, D_MODEL)
    xs = x_sample.reshape(sb * sl_len, D_MODEL)
    n_main = 2 * D_ATT + D_ATT + D_SSM + CONV_DIM
    outs_p, outs_s = [], []
    for layer in range(depth):
        lam_init = 0.8 - 0.6 * math.exp(-0.3 * layer)
        w_in_l = w_in[layer].astype(BF16)
        w = dict(
            norm_mix_g=norm_mix_g[layer],
            w_in_main=w_in_l,
            w_in_dt=jnp.pad(w_in_l[:, n_main:], ((0, 0), (0, LANES - SSM_HEADS))),
            lambda_q1=lambda_q1[layer], lambda_k1=lambda_k1[layer],
            lambda_q2=lambda_q2[layer], lambda_k2=lambda_k2[layer],
            attn_subln_g=attn_subln_g[layer], conv_w=conv_w[layer], conv_b=conv_b[layer],
            dt_bias=dt_bias[layer], a_log=a_log[layer], d_skip=d_skip[layer],
            ssm_norm_g=ssm_norm_g[layer], w_out=w_out[layer].astype(BF16),
            norm_ffn_g=norm_ffn_g[layer], w_gate=w_gate[layer].astype(BF16),
            w_up=w_up[layer].astype(BF16), ffn_conv_w=ffn_conv_w[layer],
            ffn_conv_b=ffn_conv_b[layer], w_down=w_down[layer].astype(BF16),
        )
        last = layer == depth - 1
        xp, *new_p = _layer(xp, w, lam_init, last, norm_final_g, batch=pb, seq=pl_len)
        xs, *new_s = _layer(xs, w, lam_init, last, norm_final_g, batch=sb, seq=sl_len,
                            state=(cache_k[layer], cache_v[layer], state_ssm_conv[layer],
                                   state_ssm[layer], state_ffn_conv[layer]))
        outs_p.append(new_p)
        outs_s.append(new_s)
    stack = lambda outs, idx: jnp.stack([o[idx] for o in outs])
    return (xp.reshape(pb, pl_len, D_MODEL), xs.reshape(sb, sl_len, D_MODEL),
            *[stack(outs_p, idx) for idx in range(5)],
            *[stack(outs_s, idx) for idx in range(5)])
```

```python
import functools
import math

import jax
import jax.numpy as jnp
import numpy as np
from jax import lax
from jax.experimental import pallas as pl
from jax.experimental.pallas import tpu as pltpu

F32 = jnp.float32
BF16 = jnp.bfloat16

D_MODEL = 2048
CHUNK = 64
ATT_HEADS = 8
QK_DIM = 64
V_DIM = 128
D_ATT = ATT_HEADS * V_DIM
SSM_HEADS = 16
SSM_HEAD_DIM = 64
D_SSM = SSM_HEADS * SSM_HEAD_DIM
SSM_GROUPS = 2
SSM_STATE = 128
SSM_CONV = 4
BC_DIM = 2 * SSM_GROUPS * SSM_STATE
CONV_DIM = D_SSM + BC_DIM
D_FF = 5632
FFN_CONV = 3
EPS = 1e-6
LOG2E = math.log2(math.e)
Q_SCALE = QK_DIM ** -0.5 * LOG2E
LANES = 128
SUBLANES = 8
VMEM_LIMIT = 56 * 1024 * 1024

_NT = (((1,), (1,)), ((), ()))
_TN = (((0,), (0,)), ((), ()))


def _cparams(sem):
    return pltpu.CompilerParams(dimension_semantics=sem, vmem_limit_bytes=VMEM_LIMIT)


def _silu(x):
    return x * jax.nn.sigmoid(x)


def _rms(x, g):
    return x * lax.rsqrt(jnp.mean(x * x, axis=-1, keepdims=True) + EPS) * g


_PROJ_TN = 512
_PROJ_SEGS = ((0, 2), (2, 4), (4, 6), (6, 8), (8, 10), (10, 11))


def _in_proj_kernel(x_ref, g_ref, w_ref, wdt_ref,
                    q_ref, k_ref, v_ref, z_ref, xs_ref, bc_ref, dt_ref, *rest):
    w16_ref, h_scr = rest if len(rest) == 2 else (None, rest[0])
    j = pl.program_id(1)

    @pl.when(j == 0)
    def _():
        hb = _rms(x_ref[...], g_ref[...]).astype(BF16)
        h_scr[...] = hb
        dt_ref[...] = jnp.dot(hb, wdt_ref[...], preferred_element_type=F32)

    w = w_ref[...]
    if w16_ref is not None:
        w = w.astype(BF16)
        w16_ref[...] = w
    res = jnp.dot(h_scr[...], w, preferred_element_type=F32)
    outs = (q_ref, k_ref, v_ref, z_ref, xs_ref, bc_ref)
    for (lo, hi), ref in zip(_PROJ_SEGS, outs):
        @pl.when((j >= lo) & (j < hi))
        def _(ref=ref):
            val = res * Q_SCALE if ref is q_ref else res
            ref[...] = val.astype(ref.dtype)


def _in_proj(x, g, w_main, w_dt, tm):
    m = x.shape[0]
    tn = _PROJ_TN
    nj = w_main.shape[1] // tn

    def seg_spec(lo, hi):
        return pl.BlockSpec((tm, tn), lambda i, j: (i, jnp.clip(j - lo, 0, hi - lo - 1)))

    out_shape = (
        jax.ShapeDtypeStruct((m, D_ATT), BF16),
        jax.ShapeDtypeStruct((m, D_ATT), F32),
        jax.ShapeDtypeStruct((m, D_ATT), F32),
        jax.ShapeDtypeStruct((m, D_SSM), F32),
        jax.ShapeDtypeStruct((m, D_SSM), F32),
        jax.ShapeDtypeStruct((m, BC_DIM), F32),
        jax.ShapeDtypeStruct((m, LANES), F32),
    )
    out_specs = tuple(seg_spec(lo, hi) for lo, hi in _PROJ_SEGS) + (
        pl.BlockSpec((tm, LANES), lambda i, j: (i, 0)),)
    if w_main.dtype != BF16:
        assert m == tm, "the bf16 weight copy is written once per tile: needs a single row tile"
        out_shape += (jax.ShapeDtypeStruct((D_MODEL, nj * tn), BF16),)
        out_specs += (pl.BlockSpec((D_MODEL, tn), lambda i, j: (0, j)),)
    return pl.pallas_call(
        _in_proj_kernel,
        grid=(m // tm, nj),
        in_specs=[
            pl.BlockSpec((tm, D_MODEL), lambda i, j: (i, 0)),
            pl.BlockSpec((1, D_MODEL), lambda i, j: (0, 0)),
            pl.BlockSpec((D_MODEL, tn), lambda i, j: (0, j)),
            pl.BlockSpec((D_MODEL, LANES), lambda i, j: (0, 0)),
        ],
        out_specs=out_specs,
        out_shape=out_shape,
        scratch_shapes=[pltpu.VMEM((tm, D_MODEL), BF16)],
        compiler_params=_cparams(("arbitrary", "arbitrary")),
        name="in_proj",
    )(x, g, w_main, w_dt)


def _lambda_value(lamv_ref, lam_init):
    lv = lamv_ref[...]
    s1 = jnp.sum(lv[0:1] * lv[1:2], axis=-1, keepdims=True)
    s2 = jnp.sum(lv[2:3] * lv[3:4], axis=-1, keepdims=True)
    return jnp.exp(s1) - jnp.exp(s2) + lam_init


def _split_maps(q):
    lane = lax.broadcasted_iota(jnp.int32, q.shape, 1)
    zero = jnp.zeros_like(q)
    return jnp.where(lane < QK_DIM, q, zero), jnp.where(lane >= QK_DIM, q, zero)


_VT_CHUNK = 512
_VT_PAD = 16
_UNROLL_SHIFT = 2
_UNROLL = 1 << _UNROLL_SHIFT
_HEADS_PER_STEP = 2


def _attn_prompt_kernel(q_ref, k_ref, v_ref, slope_ref, lamv_ref, g_ref, dmask_ref, dneg_ref, o_ref,
                        k_scr, vt_scr, acc_scr, t0_scr, t1_scr, p0_scr, p1_scr,
                        *, tile, lam_init):
    length = k_ref.shape[0]
    width = 2 * tile
    nheads = k_scr.shape[0]
    lam = _lambda_value(lamv_ref, lam_init)

    lane = lax.broadcasted_iota(jnp.int32, (length, LANES), 1)
    koff = lax.broadcasted_iota(jnp.int32, (length, LANES), 0) % tile
    koff = jnp.where(lane < 3, koff, 0).astype(F32).astype(BF16)
    extra = (lax.broadcasted_iota(jnp.int32, (_VT_PAD, tile), 0) == 0).astype(F32).astype(BF16)
    brow = lax.broadcasted_iota(jnp.int32, (LANES, width), 0)
    per = _VT_CHUNK // tile
    ws, q_biases = [], []
    for hh in range(nheads):
        cols = slice(hh * V_DIM, (hh + 1) * V_DIM)
        k_scr[hh, :, 0:V_DIM] = k_ref[:, cols].astype(BF16)
        k_scr[hh, :, V_DIM:V_DIM + LANES] = koff
        for c in range(length // _VT_CHUNK):
            vt = v_ref[c * _VT_CHUNK:(c + 1) * _VT_CHUNK, cols].T.astype(BF16)
            for s in range(per):
                vt_scr[hh, c * per + s, 0:V_DIM, :] = vt[:, s * tile:(s + 1) * tile]
                vt_scr[hh, c * per + s, V_DIM:V_DIM + _VT_PAD, :] = extra
        w = slope_ref[hh][:, :1] * LOG2E
        w_hi, w_mid, w_lo = (piece.astype(F32) for piece in _split3(w))
        q_bias = jnp.where(brow == 0, w_hi, jnp.where(brow == 1, w_mid, jnp.where(brow == 2, w_lo, 0.0)))
        ws.append(w)
        q_biases.append(q_bias.astype(BF16))

    def q_tile(qi, _):
        _attn_prompt_tile(qi, ws, lam, q_biases, q_ref, g_ref, dmask_ref, dneg_ref, o_ref,
                          k_scr, vt_scr, acc_scr, (t0_scr, t1_scr), (p0_scr, p1_scr),
                          tile=tile, lam_init=lam_init)
        return 0

    lax.fori_loop(0, length // tile, q_tile, 0)


def _attn_prompt_tile(qi, ws, lam, q_biases, q_ref, g_ref, dmask_ref, dneg_ref, o_ref,
                      k_scr, vt_scr, acc_scr, t_slots, p_slots, *, tile, lam_init):
    width = 2 * tile
    nheads = len(ws)
    rows = pl.ds(pl.multiple_of(qi * tile, tile), tile)
    q_augs = []
    for hh in range(nheads):
        q1, q2 = _split_maps(q_ref[rows, hh * V_DIM:(hh + 1) * V_DIM])
        q_t = jnp.concatenate([q1, q2], axis=0).astype(F32).T.astype(BF16)
        q_augs.append(jnp.concatenate([q_t, q_biases[hh]], axis=0))

    def scores(hh, j):
        k0 = pl.multiple_of(jnp.minimum(j, qi) * tile, tile)
        return jnp.dot(k_scr[hh, pl.ds(k0, tile), :], q_augs[hh], preferred_element_type=F32)

    def step(hh, j, slot, alpha_prev, m_old, tmax_cur):
        pv = jnp.dot(vt_scr[hh, jnp.clip(j - 1, 0, qi)], p_slots[1 - slot][hh],
                     preferred_element_type=F32)
        acc_scr[hh] = alpha_prev * acc_scr[hh] + pv
        t_next = scores(hh, j + 1)
        t_slots[1 - slot][hh] = t_next
        tmax_next = jnp.max(t_next, axis=0, keepdims=True)
        off = jnp.where(j < qi, -ws[hh] * ((qi - j) * tile).astype(F32), -jnp.inf)
        m_new = jnp.maximum(m_old, tmax_cur + off)
        p_slots[slot][hh] = jnp.exp2(t_slots[slot][hh] + (off - m_new)).astype(BF16)
        return jnp.exp2(m_old - m_new), m_new, tmax_next

    def run(first, ntiles_per_iter, niter, carry):
        def body(i, carry):
            for s in range(ntiles_per_iter):
                carry = tuple(step(hh, first + ntiles_per_iter * i + s, s % 2, *carry[hh])
                              for hh in range(nheads))
            return carry
        return lax.fori_loop(0, niter, body, carry)

    carry = []
    for hh in range(nheads):
        acc_scr[hh] = jnp.zeros(acc_scr.shape[1:], F32)
        p_slots[1][hh] = jnp.zeros(p_slots[1].shape[1:], BF16)
        t_first = scores(hh, 0)
        t_slots[0][hh] = t_first
        carry.append((jnp.ones((1, width), F32),
                      jnp.full((1, width), -0.5 * float(jnp.finfo(F32).max), F32),
                      jnp.max(t_first, axis=0, keepdims=True)))
    n_main = lax.shift_right_logical(qi, _UNROLL_SHIFT)
    done = n_main * _UNROLL
    n_rest = lax.shift_right_logical(qi - done + 1, 1)
    carry = run(0, _UNROLL, n_main, tuple(carry))
    carry = run(done, 2, n_rest, carry)
    done = done + 2 * n_rest

    for hh in range(nheads):
        alpha_last, m_old, _ = carry[hh]
        pv = jnp.dot(vt_scr[hh, jnp.clip(done - 1, 0, qi)], p_slots[1][hh], preferred_element_type=F32)
        acc = alpha_last * acc_scr[hh] + pv
        t = t_slots[0][hh] + (ws[hh] * dmask_ref[...] + dneg_ref[...])
        m_new = jnp.maximum(m_old, jnp.max(t, axis=0, keepdims=True))
        p = jnp.exp2(t - m_new).astype(BF16)
        acc = jnp.exp2(m_old - m_new) * acc + jnp.dot(vt_scr[hh, qi], p, preferred_element_type=F32)

        on = acc[0:V_DIM] / acc[V_DIM:V_DIM + 1]
        o_t = on[:, :tile] - lam * on[:, tile:]
        o_t = o_t * lax.rsqrt(jnp.mean(o_t * o_t, axis=0, keepdims=True) + EPS)
        o_ref[rows, hh * V_DIM:(hh + 1) * V_DIM] = (
            o_t.T * g_ref[...] * (1.0 - lam_init)).astype(o_ref.dtype)


def _attn_prompt(q, k, v, slopes, lamv, g, lam_init, tile):
    length = q.shape[0]
    key = np.arange(tile)[:, None]
    qry = np.tile(np.arange(tile), 2)[None, :]
    dmask = jnp.asarray((qry - np.abs(qry - key)) - key, F32)
    dneg = jnp.asarray(np.where(key // CHUNK <= qry // CHUNK, 0.0, -np.inf), F32)
    nh = _HEADS_PER_STEP
    diag_spec = pl.BlockSpec((tile, 2 * tile), lambda h: (0, 0))
    head_spec = pl.BlockSpec((length, nh * V_DIM), lambda h: (0, h))
    kv_spec = pl.BlockSpec((length, nh * V_DIM), lambda h: (0, h), pipeline_mode=pl.Buffered(1))
    return pl.pallas_call(
        functools.partial(_attn_prompt_kernel, tile=tile, lam_init=lam_init),
        grid=(ATT_HEADS // nh,),
        in_specs=[
            kv_spec, kv_spec, kv_spec,
            pl.BlockSpec((nh, 1, LANES), lambda h: (h, 0, 0)),
            pl.BlockSpec((4, QK_DIM), lambda h: (0, 0)),
            pl.BlockSpec((1, V_DIM), lambda h: (0, 0)),
            diag_spec, diag_spec,
        ],
        out_specs=head_spec,
        out_shape=jax.ShapeDtypeStruct((length, D_ATT), BF16),
        scratch_shapes=[
            pltpu.VMEM((nh, length, V_DIM + LANES), BF16),
            pltpu.VMEM((nh, length // tile, V_DIM + _VT_PAD, tile), BF16),
            pltpu.VMEM((nh, V_DIM + _VT_PAD, 2 * tile), F32),
            pltpu.VMEM((nh, tile, 2 * tile), F32),
            pltpu.VMEM((nh, tile, 2 * tile), F32),
            pltpu.VMEM((nh, tile, 2 * tile), BF16),
            pltpu.VMEM((nh, tile, 2 * tile), BF16),
        ],
        compiler_params=_cparams(("arbitrary",)),
        name="attn_prompt",
    )(q, k, v, slopes, lamv, g, dmask, dneg)


def _attn_sample_kernel(q_ref, kn_ref, vn_ref, ck_ref, cv_ref, lamv_ref, g_ref, o_ref,
                        *, seq, past, lam_init):
    lam = _lambda_value(lamv_ref, lam_init)
    r_c = (lax.broadcasted_iota(jnp.int32, (2 * seq, past), 0) % seq
           - lax.broadcasted_iota(jnp.int32, (2 * seq, past), 1) + past).astype(F32)
    r_n = jnp.abs(lax.broadcasted_iota(jnp.int32, (2 * seq, seq), 0) % seq
                  - lax.broadcasted_iota(jnp.int32, (2 * seq, seq), 1)).astype(F32)
    for h in range(ATT_HEADS):
        w = 2.0 ** (-8.0 * (h + 1) / ATT_HEADS) * LOG2E
        sl = slice(h * V_DIM, (h + 1) * V_DIM)
        q1, q2 = _split_maps(q_ref[:, sl])
        qq = jnp.concatenate([q1, q2], axis=0)
        kc = ck_ref[0, pl.ds(h, past, stride=ATT_HEADS), :].astype(BF16)
        vc = cv_ref[0, pl.ds(h, past, stride=ATT_HEADS), :].astype(BF16)
        kn = kn_ref[:, sl].astype(BF16)
        vn = vn_ref[:, sl].astype(BF16)
        sc = lax.dot_general(qq, kc, _NT, preferred_element_type=F32) - w * r_c
        sn = lax.dot_general(qq, kn, _NT, preferred_element_type=F32) - w * r_n
        m = jnp.maximum(jnp.max(sc, axis=-1, keepdims=True), jnp.max(sn, axis=-1, keepdims=True))
        pc = jnp.exp2(sc - m)
        pn = jnp.exp2(sn - m)
        l = jnp.sum(pc, axis=-1, keepdims=True) + jnp.sum(pn, axis=-1, keepdims=True)
        acc = (jnp.dot(pc.astype(BF16), vc, preferred_element_type=F32)
               + jnp.dot(pn.astype(BF16), vn, preferred_element_type=F32))
        on = acc / l
        o = on[:seq] - lam * on[seq:]
        o_ref[:, sl] = (_rms(o, g_ref[...]) * (1.0 - lam_init)).astype(o_ref.dtype)


def _attn_sample(q, k_new, v_new, cache_k, cache_v, lamv, g, lam_init, seq):
    nb, past = cache_k.shape[0], cache_k.shape[1] // ATT_HEADS
    row_spec = pl.BlockSpec((seq, D_ATT), lambda b: (b, 0))
    cache_spec = pl.BlockSpec((1, past * ATT_HEADS, V_DIM), lambda b: (b, 0, 0))
    return pl.pallas_call(
        functools.partial(_attn_sample_kernel, seq=seq, past=past, lam_init=lam_init),
        grid=(nb,),
        in_specs=[row_spec, row_spec, row_spec, cache_spec, cache_spec,
                  pl.BlockSpec((4, QK_DIM), lambda b: (0, 0)),
                  pl.BlockSpec((1, V_DIM), lambda b: (0, 0))],
        out_specs=row_spec,
        out_shape=jax.ShapeDtypeStruct((nb * seq, D_ATT), BF16),
        compiler_params=_cparams(("arbitrary",)),
        name="attn_sample",
    )(q, k_new, v_new, cache_k, cache_v, lamv, g)


def _split3(x):
    hi = x.astype(BF16)
    r = x - hi.astype(F32)
    mid = r.astype(BF16)
    lo = (r - mid.astype(F32)).astype(BF16)
    return hi, mid, lo


def _ssd_chunk(xs, bm, cm, dt, z, s_ref, a_heads, dskip, gn, e3, es3, t3, ones_k, lc):
    seg_w = SSM_HEADS * lc
    half = D_SSM // SSM_GROUPS
    d3 = jnp.concatenate(_split3(dt), axis=1)
    dt_x = jnp.dot(d3, e3, preferred_element_type=F32)
    la3 = jnp.concatenate(_split3(dt * a_heads), axis=0)
    ac3 = jnp.concatenate(_split3(jnp.dot(t3, la3, preferred_element_type=F32)), axis=1)
    acol = jnp.dot(ac3, e3, preferred_element_type=F32)
    acol_s = acol if seg_w == D_SSM else jnp.dot(ac3, es3, preferred_element_type=F32)
    t_idx = lax.broadcasted_iota(jnp.int32, (lc, seg_w), 0)
    s_idx = lax.broadcasted_iota(jnp.int32, (lc, seg_w), 1) % lc
    arow = jnp.sum(jnp.where(t_idx == s_idx, acol_s, 0.0), axis=0, keepdims=True)
    decay = jnp.where(t_idx >= s_idx, jnp.exp(acol_s - arow), 0.0)

    cb16 = cm.astype(BF16)
    bb16 = bm.astype(BF16)
    hpg = SSM_HEADS // SSM_GROUPS
    cbs = []
    for g in range(SSM_GROUPS):
        gs = slice(g * SSM_STATE, (g + 1) * SSM_STATE)
        b_rep = jnp.concatenate([bb16[:, gs]] * hpg, axis=0)
        cbs.append(lax.dot_general(cb16[:, gs], b_rep, _NT, preferred_element_type=F32))
    mmat = (jnp.concatenate(cbs, axis=1) * decay).astype(BF16)

    xdt = xs * dt_x
    xdt16 = xdt.astype(BF16)
    hk = (2 * LANES) // lc
    wd = hk * SSM_HEAD_DIM
    blk = (lax.broadcasted_iota(jnp.int32, (hk * lc, wd), 0) // lc
           == lax.broadcasted_iota(jnp.int32, (hk * lc, wd), 1) // SSM_HEAD_DIM)
    parts = []
    for i in range(SSM_HEADS // hk):
        xd = xdt16[:, i * wd:(i + 1) * wd]
        bd = jnp.where(blk, jnp.concatenate([xd] * hk, axis=0), jnp.zeros((), BF16))
        parts.append(jnp.dot(mmat[:, i * hk * lc:(i + 1) * hk * lc], bd, preferred_element_type=F32))
    y_intra = jnp.concatenate(parts, axis=1) if len(parts) > 1 else parts[0]

    yi = []
    for g in range(SSM_GROUPS):
        sg = s_ref[g * half:(g + 1) * half, :].astype(BF16)
        yi.append(lax.dot_general(cb16[:, g * SSM_STATE:(g + 1) * SSM_STATE], sg, _NT,
                                  preferred_element_type=F32))
    y_inter = jnp.exp(acol) * jnp.concatenate(yi, axis=1)

    alast = acol[lc - 1:lc, :]
    dec_end = jnp.exp(alast - acol)
    xd_end = (xdt * dec_end).astype(BF16)
    krow = lax.broadcasted_iota(jnp.int32, (ones_k.shape[0], D_SSM), 0)
    a_hi, a_mid, a_lo = (piece.astype(F32) for piece in _split3(alast))
    pieces = jnp.where(krow == 0, a_hi, jnp.where(krow == 1, a_mid, jnp.where(krow == 2, a_lo, 0.0)))
    acl = lax.dot_general(pieces.astype(BF16), ones_k, _TN, preferred_element_type=F32)
    for g in range(SSM_GROUPS):
        rows = slice(g * half, (g + 1) * half)
        upd = lax.dot_general(xd_end[:, rows], bb16[:, g * SSM_STATE:(g + 1) * SSM_STATE], _TN,
                              preferred_element_type=F32)
        s_ref[rows, :] = jnp.exp(acl[rows, :]) * s_ref[rows, :] + upd

    y = (y_intra + y_inter + dskip * xs) * _silu(z)
    outs = []
    for g in range(SSM_GROUPS):
        cs = slice(g * half, (g + 1) * half)
        outs.append(_rms(y[:, cs], gn[:, cs]))
    return jnp.concatenate(outs, axis=1)


def _ssd_kernel(*refs, lc, nchunk, has_state):
    if has_state:
        (xs_ref, bc_ref, dt_ref, z_ref, prev_ref, h0_ref, cw_ref, cbias_ref, dtb_ref, alog_ref,
         dskip_ref, gn_ref, e3_ref, es3_ref, t3_ref, ones_ref,
         y_ref, s_ref, scr_x, scr_bc) = refs
    else:
        (xs_ref, bc_ref, dt_ref, z_ref, cw_ref, cbias_ref, dtb_ref, alog_ref,
         dskip_ref, gn_ref, e3_ref, es3_ref, t3_ref, ones_ref,
         y_ref, s_ref, scr_x, scr_bc) = refs
    rows = lc * nchunk
    pad = SUBLANES
    if has_state:
        s2d = s_ref.at[0]
        scr_x[0:pad, :] = prev_ref[0, :, 0:D_SSM]
        scr_bc[0:pad, :] = prev_ref[0, :, D_SSM:CONV_DIM]
        s2d[...] = h0_ref[0]
    else:
        s2d = s_ref

        @pl.when(pl.program_id(0) == 0)
        def _():
            scr_x[0:pad, :] = jnp.zeros((pad, D_SSM), F32)
            scr_bc[0:pad, :] = jnp.zeros((pad, BC_DIM), F32)
            s_ref[...] = jnp.zeros(s_ref.shape, F32)

    scr_x[pad:pad + rows, :] = xs_ref[...]
    scr_bc[pad:pad + rows, :] = bc_ref[...]
    xc = cbias_ref[:, 0:D_SSM]
    bcc = cbias_ref[:, D_SSM:CONV_DIM]
    for tap in range(SSM_CONV):
        off = pad - (SSM_CONV - 1) + tap
        xc = xc + scr_x[off:off + rows, :] * cw_ref[tap:tap + 1, 0:D_SSM]
        bcc = bcc + scr_bc[off:off + rows, :] * cw_ref[tap:tap + 1, D_SSM:CONV_DIM]
    if not has_state:
        scr_x[0:pad, :] = scr_x[rows:rows + pad, :]
        scr_bc[0:pad, :] = scr_bc[rows:rows + pad, :]
    xs_act = _silu(xc)
    bc_act = _silu(bcc)
    dt_in = dt_ref[...] + dtb_ref[...]
    dt = jnp.maximum(dt_in, 0.0) + jnp.log1p(jnp.exp(-jnp.abs(dt_in)))
    a_heads = -jnp.exp(alog_ref[...])
    nbm = SSM_GROUPS * SSM_STATE
    for c in range(nchunk):
        rs = slice(c * lc, (c + 1) * lc)
        y = _ssd_chunk(xs_act[rs], bc_act[rs, 0:nbm], bc_act[rs, nbm:2 * nbm], dt[rs],
                       z_ref[rs, :], s2d, a_heads, dskip_ref[...], gn_ref[...],
                       e3_ref[...], es3_ref[...], t3_ref[...], ones_ref[...], lc)
        y_ref[rs, :] = y.astype(y_ref.dtype)


def _ssd_constants(lc):
    seg_w = SSM_HEADS * lc
    head_of_lane = np.arange(D_SSM) // SSM_HEAD_DIM
    e = (np.arange(LANES)[:, None] == head_of_lane[None, :]).astype(np.float32)
    es = (np.arange(LANES)[:, None] == (np.arange(seg_w) // lc)[None, :]).astype(np.float32)
    tri = np.tril(np.ones((lc, lc), np.float32))
    return (jnp.asarray(np.concatenate([e] * 3, axis=0), BF16),
            jnp.asarray(np.concatenate([es] * 3, axis=0), BF16),
            jnp.asarray(np.concatenate([tri] * 3, axis=1), BF16),
            jnp.ones((2 * SUBLANES, SSM_STATE), BF16))


def _ssd(xs, bc, dt, z, conv_w, conv_b, dt_bias, a_log, d_skip, gn, lc, nchunk,
         conv_prev=None, h0=None):
    m = xs.shape[0]
    rows = lc * nchunk
    has_state = h0 is not None
    e3, es3, t3, ones_k = _ssd_constants(lc)
    dtb = jnp.zeros((1, LANES), F32).at[0, :SSM_HEADS].set(dt_bias)
    alog = jnp.zeros((1, LANES), F32).at[0, :SSM_HEADS].set(a_log)
    dskip_x = jnp.repeat(d_skip, SSM_HEAD_DIM)[None, :]

    def const(shape):
        return pl.BlockSpec(shape, lambda i: (0,) * len(shape))

    def rowblk(width):
        return pl.BlockSpec((rows, width), lambda i: (i, 0))

    in_specs = [rowblk(D_SSM), rowblk(BC_DIM), rowblk(LANES), rowblk(D_SSM)]
    args = [xs, bc, dt, z]
    if has_state:
        nb = h0.shape[0]
        in_specs += [pl.BlockSpec((1, SUBLANES, CONV_DIM), lambda i: (i, 0, 0)),
                     pl.BlockSpec((1, D_SSM, SSM_STATE), lambda i: (i, 0, 0))]
        args += [conv_prev, h0]
        s_shape = jax.ShapeDtypeStruct((nb, D_SSM, SSM_STATE), F32)
        s_spec = pl.BlockSpec((1, D_SSM, SSM_STATE), lambda i: (i, 0, 0))
    else:
        s_shape = jax.ShapeDtypeStruct((D_SSM, SSM_STATE), F32)
        s_spec = const((D_SSM, SSM_STATE))
    in_specs += [const((SSM_CONV, CONV_DIM)), const((1, CONV_DIM)), const((1, LANES)),
                 const((1, LANES)), const((1, D_SSM)), const((1, D_SSM)),
                 const(e3.shape), const(es3.shape), const(t3.shape), const(ones_k.shape)]
    args += [conv_w, conv_b[None, :], dtb, alog, dskip_x, gn[None, :], e3, es3, t3, ones_k]
    return pl.pallas_call(
        functools.partial(_ssd_kernel, lc=lc, nchunk=nchunk, has_state=has_state),
        grid=(m // rows,),
        in_specs=in_specs,
        out_specs=(rowblk(D_SSM), s_spec),
        out_shape=(jax.ShapeDtypeStruct((m, D_SSM), BF16), s_shape),
        scratch_shapes=[pltpu.VMEM((rows + SUBLANES, D_SSM), F32),
                        pltpu.VMEM((rows + SUBLANES, BC_DIM), F32)],
        compiler_params=_cparams(("arbitrary",)),
        name="ssd_sample" if has_state else "ssd_prompt",
    )(*args)


def _out_proj_kernel(o_ref, y_ref, x_ref, w_ref, g_ref, x2_ref, hf_ref, w16_ref=None):
    if w16_ref is not None:
        w16_ref[...] = w_ref[...].astype(BF16)
        w_ref = w16_ref
    x2 = (x_ref[...]
          + jnp.dot(o_ref[...], w_ref[0:D_ATT, :], preferred_element_type=F32)
          + jnp.dot(y_ref[...], w_ref[D_ATT:D_ATT + D_SSM, :], preferred_element_type=F32))
    x2_ref[...] = x2
    hf_ref[...] = _rms(x2, g_ref[...]).astype(hf_ref.dtype)


def _out_proj(o, y, x, w, g, tm):
    m = x.shape[0]
    w_spec = pl.BlockSpec((D_ATT + D_SSM, D_MODEL), lambda i: (0, 0), pipeline_mode=pl.Buffered(1))
    out_specs = (pl.BlockSpec((tm, D_MODEL), lambda i: (i, 0)),
                 pl.BlockSpec((tm, D_MODEL), lambda i: (i, 0)))
    out_shape = (jax.ShapeDtypeStruct((m, D_MODEL), F32),
                 jax.ShapeDtypeStruct((m, D_MODEL), BF16))
    if w.dtype != BF16:
        assert m == tm, "the bf16 weight copy is written once: needs a single row tile"
        out_specs += (pl.BlockSpec((D_ATT + D_SSM, D_MODEL), lambda i: (0, 0)),)
        out_shape += (jax.ShapeDtypeStruct(w.shape, BF16),)
    return pl.pallas_call(
        _out_proj_kernel,
        grid=(m // tm,),
        in_specs=[
            pl.BlockSpec((tm, D_ATT), lambda i: (i, 0)),
            pl.BlockSpec((tm, D_SSM), lambda i: (i, 0)),
            pl.BlockSpec((tm, D_MODEL), lambda i: (i, 0)),
            w_spec,
            pl.BlockSpec((1, D_MODEL), lambda i: (0, 0)),
        ],
        out_specs=out_specs,
        out_shape=out_shape,
        compiler_params=_cparams(("arbitrary",)),
        name="out_proj",
    )(o, y, x, w, g)


def _ffn_kernel(*refs, tm, seq, final_norm):
    if seq is None:
        (hf_ref, x2_ref, wg_ref, wu_ref, wd_ref, cw_ref, cb_ref, gfin_ref,
         out_ref, gl_ref, g_scr, act_scr, carry_scr) = refs
    else:
        (hf_ref, x2_ref, wg_ref, wu_ref, wd_ref, cw_ref, cb_ref, gfin_ref, prev_ref, sel1_ref, sel2_ref,
         out_ref, gl_ref, wg16_ref, wu16_ref, wd16_ref, g_scr, act_scr) = refs
    i = pl.program_id(0)
    f = pl.program_id(1)
    nf = pl.num_programs(1) - 1
    pad = SUBLANES

    def weight(w_ref, w16_ref):
        if seq is None:
            return w_ref[...]
        w16 = w_ref[...].astype(BF16)
        w16_ref[...] = w16
        return w16

    def gate_up():
        hf = hf_ref[...]
        gate = jnp.dot(hf, weight(wg_ref, None if seq is None else wg16_ref), preferred_element_type=F32)
        up = jnp.dot(hf, weight(wu_ref, None if seq is None else wu16_ref), preferred_element_type=F32)
        g_scr[pad:pad + tm, :] = gate
        if seq is None:
            g_scr[0:pad, :] = carry_scr[f]
            carry_scr[f] = g_scr[tm:tm + pad, :]
            gl_ref[...] = g_scr[tm:tm + pad, :]
            g1 = g_scr[pad - 1:pad - 1 + tm, :]
            g2 = g_scr[pad - 2:pad - 2 + tm, :]
        else:
            g_scr[0:pad, :] = jnp.zeros((pad, gate.shape[1]), F32)
            gl_ref[...] = gate
            prev3 = jnp.concatenate(_split3(prev_ref[...]), axis=0)
            ov1 = jnp.dot(sel1_ref[...], prev3, preferred_element_type=F32)
            ov2 = jnp.dot(sel2_ref[...], prev3, preferred_element_type=F32)
            pos = lax.broadcasted_iota(jnp.int32, gate.shape, 0) % seq
            g1 = jnp.where(pos == 0, ov1, g_scr[pad - 1:pad - 1 + tm, :])
            g2 = jnp.where(pos < 2, ov2, g_scr[pad - 2:pad - 2 + tm, :])
        conv = cb_ref[...] + g2 * cw_ref[0:1, :] + g1 * cw_ref[1:2, :] + gate * cw_ref[2:3, :]
        act_scr[...] = (_silu(conv) * up).astype(BF16)

    def down():
        return jnp.dot(act_scr[...], weight(wd_ref, None if seq is None else wd16_ref),
                       preferred_element_type=F32)

    @pl.when(f == 0)
    def _():
        if seq is None:
            @pl.when(i == 0)
            def _():
                carry_scr[...] = jnp.zeros(carry_scr.shape, F32)
        out_ref[...] = x2_ref[...]
        gate_up()

    @pl.when((f > 0) & (f < nf))
    def _():
        contrib = down()
        gate_up()
        out_ref[...] += contrib

    @pl.when(f == nf)
    def _():
        res = out_ref[...] + down()
        out_ref[...] = _rms(res, gfin_ref[...]) if final_norm else res


def _ffn_prev_selectors(m, seq, nprev):
    sel1 = np.zeros((m, 3 * nprev), np.float32)
    sel2 = np.zeros((m, 3 * nprev), np.float32)
    nstate = FFN_CONV - 1
    for b in range(m // seq):
        for piece in range(3):
            base = piece * nprev + b * nstate
            sel1[b * seq, base + 1] = 1.0
            sel2[b * seq, base + 0] = 1.0
            sel2[b * seq + 1, base + 1] = 1.0
    return jnp.asarray(sel1, BF16), jnp.asarray(sel2, BF16)


def _ffn(hf, x2, wg, wu, wd, cw, cb, gfin, tm, tf, final_norm, seq=None, prev=None):
    m = hf.shape[0]
    nf = D_FF // tf
    cur = lambda f: jnp.minimum(f, nf - 1)
    last = lambda f: jnp.maximum(f - 1, 0)
    in_specs = [
        pl.BlockSpec((tm, D_MODEL), lambda i, f: (i, 0)),
        pl.BlockSpec((tm, D_MODEL), lambda i, f: (i, 0), pipeline_mode=pl.Buffered(1)),
        pl.BlockSpec((D_MODEL, tf), lambda i, f: (0, cur(f))),
        pl.BlockSpec((D_MODEL, tf), lambda i, f: (0, cur(f))),
        pl.BlockSpec((tf, D_MODEL), lambda i, f: (last(f), 0)),
        pl.BlockSpec((FFN_CONV, tf), lambda i, f: (0, cur(f))),
        pl.BlockSpec((1, tf), lambda i, f: (0, cur(f))),
        pl.BlockSpec((1, D_MODEL), lambda i, f: (0, 0)),
    ]
    args = [hf, x2, wg, wu, wd, cw, cb[None, :], gfin[None, :]]
    scratch = [pltpu.VMEM((tm + SUBLANES, tf), F32), pltpu.VMEM((tm, tf), BF16)]
    if seq is None:
        gl_shape = jax.ShapeDtypeStruct((m // tm * SUBLANES, D_FF), F32)
        gl_spec = pl.BlockSpec((SUBLANES, tf), lambda i, f: (i, cur(f)))
        scratch.append(pltpu.VMEM((nf, SUBLANES, tf), F32))
    else:
        assert m == tm, "sample FFN handles all sequences in one row tile"
        nprev = prev.shape[0]
        sel1, sel2 = _ffn_prev_selectors(m, seq, nprev)
        in_specs += [pl.BlockSpec((nprev, tf), lambda i, f: (0, cur(f))),
                     pl.BlockSpec(sel1.shape, lambda i, f: (0, 0)),
                     pl.BlockSpec(sel2.shape, lambda i, f: (0, 0))]
        args += [prev, sel1, sel2]
        gl_shape = jax.ShapeDtypeStruct((m, D_FF), F32)
        gl_spec = pl.BlockSpec((tm, tf), lambda i, f: (i, cur(f)))
    out_specs = [pl.BlockSpec((tm, D_MODEL), lambda i, f: (i, 0)), gl_spec]
    out_shape = [jax.ShapeDtypeStruct((m, D_MODEL), F32), gl_shape]
    if seq is not None:
        out_specs += [in_specs[2], in_specs[3], in_specs[4]]
        out_shape += [jax.ShapeDtypeStruct(wg.shape, BF16), jax.ShapeDtypeStruct(wu.shape, BF16),
                      jax.ShapeDtypeStruct(wd.shape, BF16)]
    return pl.pallas_call(
        functools.partial(_ffn_kernel, tm=tm, seq=seq, final_norm=final_norm),
        grid=(m // tm, nf + 1),
        in_specs=in_specs,
        out_specs=tuple(out_specs),
        out_shape=tuple(out_shape),
        scratch_shapes=scratch,
        compiler_params=_cparams(("arbitrary", "arbitrary")),
        name="ffn_sample" if seq is not None else "ffn_prompt",
    )(*args)


def _layer(x, w, lam_init, final_norm, gfin, *, batch, seq, state=None):
    m = x.shape[0]
    lamv = jnp.stack([w["lambda_q1"], w["lambda_k1"], w["lambda_q2"], w["lambda_k2"]])
    tm = min(512, m)
    w16 = {}
    q, k, v, z, xs, bc, dt, *extra = _in_proj(x, w["norm_mix_g"][None, :], w["w_in_main"], w["w_in_dt"],
                                              tm=min(1024, m))
    if extra:
        w16["w_in_main"] = extra[0]

    if state is None:
        slopes = jnp.broadcast_to(
            jnp.asarray(2.0 ** (-8.0 * np.arange(1, ATT_HEADS + 1) / ATT_HEADS), F32)[:, None, None],
            (ATT_HEADS, 1, LANES))
        o = _attn_prompt(q, k, v, slopes, lamv, w["attn_subln_g"][None, :], lam_init, tile=256)
        y, s_new = _ssd(xs, bc, dt, z, w["conv_w"], w["conv_b"], w["dt_bias"], w["a_log"],
                        w["d_skip"], w["ssm_norm_g"], lc=CHUNK, nchunk=4)
        conv_new = jnp.concatenate([xs[m - (SSM_CONV - 1):], bc[m - (SSM_CONV - 1):]], axis=-1)[None]
        s_new = s_new[None]
    else:
        cache_k, cache_v, conv_prev, ssm_prev, ffn_prev = state
        past = cache_k.shape[1]
        o = _attn_sample(q, k, v, cache_k.reshape(batch, past * ATT_HEADS, V_DIM),
                         cache_v.reshape(batch, past * ATT_HEADS, V_DIM),
                         lamv, w["attn_subln_g"][None, :], lam_init, seq)
        prev8 = jnp.pad(conv_prev, ((0, 0), (SUBLANES - (SSM_CONV - 1), 0), (0, 0)))
        y, s_new = _ssd(xs, bc, dt, z, w["conv_w"], w["conv_b"], w["dt_bias"], w["a_log"],
                        w["d_skip"], w["ssm_norm_g"], lc=seq, nchunk=1, conv_prev=prev8,
                        h0=ssm_prev.reshape(batch, D_SSM, SSM_STATE))
        conv_new = jnp.concatenate([xs.reshape(batch, seq, D_SSM)[:, seq - (SSM_CONV - 1):],
                                    bc.reshape(batch, seq, BC_DIM)[:, seq - (SSM_CONV - 1):]], axis=-1)

    x2, hf, *extra = _out_proj(o, y, x, w["w_out"], w["norm_ffn_g"][None, :], tm=tm)
    if extra:
        w16["w_out"] = extra[0]

    ffn_args = (hf, x2, w["w_gate"], w["w_up"], w["w_down"], w["ffn_conv_w"], w["ffn_conv_b"], gfin)
    if state is None:
        x3, gl = _ffn(*ffn_args, tm=min(1024, m), tf=512, final_norm=final_norm)
        ffn_new = gl[None, gl.shape[0] - (FFN_CONV - 1):]
    else:
        x3, gl, w16["w_gate"], w16["w_up"], w16["w_down"] = _ffn(
            *ffn_args, tm=tm, tf=256, final_norm=final_norm, seq=seq,
            prev=ffn_prev.reshape(batch * (FFN_CONV - 1), D_FF))
        ffn_new = gl.reshape(batch, seq, D_FF)[:, seq - (FFN_CONV - 1):]
    k_new = k.reshape(batch, seq, ATT_HEADS, 2 * QK_DIM)
    v_new = v.reshape(batch, seq, ATT_HEADS, V_DIM)
    s_new = s_new.reshape(batch, SSM_HEADS, SSM_HEAD_DIM, SSM_STATE)
    return (x3, k_new, v_new, conv_new, s_new, ffn_new), w16


def kernel(x_prompt, x_sample, cache_k, cache_v, state_ssm_conv, state_ssm, state_ffn_conv, norm_mix_g, w_in, lambda_q1, lambda_k1, lambda_q2, lambda_k2, attn_subln_g, conv_w, conv_b, dt_bias, a_log, d_skip, ssm_norm_g, w_out, norm_ffn_g, w_gate, w_up, ffn_conv_w, ffn_conv_b, w_down, norm_final_g):
    depth = w_in.shape[0]
    pb, pl_len, _ = x_prompt.shape
    sb, sl_len, _ = x_sample.shape
    assert pb == 1 and pl_len % CHUNK == 0
    xp = x_prompt.reshape(pb * pl_len, D_MODEL)
    xs = x_sample.reshape(sb * sl_len, D_MODEL)
    n_main = 2 * D_ATT + D_ATT + D_SSM + CONV_DIM
    outs_p, outs_s = [], []
    for layer in range(depth):
        lam_init = 0.8 - 0.6 * math.exp(-0.3 * layer)
        w = dict(
            norm_mix_g=norm_mix_g[layer],
            w_in_main=w_in[layer],
            w_in_dt=jnp.pad(w_in[layer][:, n_main:].astype(BF16), ((0, 0), (0, LANES - SSM_HEADS))),
            lambda_q1=lambda_q1[layer], lambda_k1=lambda_k1[layer],
            lambda_q2=lambda_q2[layer], lambda_k2=lambda_k2[layer],
            attn_subln_g=attn_subln_g[layer], conv_w=conv_w[layer], conv_b=conv_b[layer],
            dt_bias=dt_bias[layer], a_log=a_log[layer], d_skip=d_skip[layer],
            ssm_norm_g=ssm_norm_g[layer], w_out=w_out[layer],
            norm_ffn_g=norm_ffn_g[layer], w_gate=w_gate[layer],
            w_up=w_up[layer], ffn_conv_w=ffn_conv_w[layer],
            ffn_conv_b=ffn_conv_b[layer], w_down=w_down[layer],
        )
        last = layer == depth - 1
        (xs, *new_s), w16 = _layer(xs, w, lam_init, last, norm_final_g, batch=sb, seq=sl_len,
                                   state=(cache_k[layer], cache_v[layer], state_ssm_conv[layer],
                                          state_ssm[layer], state_ffn_conv[layer]))
        (xp, *new_p), _ = _layer(xp, {**w, **w16}, lam_init, last, norm_final_g, batch=pb, seq=pl_len)
        outs_p.append(new_p)
        outs_s.append(new_s)
    stack = lambda outs, idx: jnp.stack([o[idx] for o in outs])
    return (xp.reshape(pb, pl_len, D_MODEL), xs.reshape(sb, sl_len, D_MODEL),
            *[stack(outs_p, idx) for idx in range(5)],
            *[stack(outs_s, idx) for idx in range(5)])
```

```python
import functools
import math

import jax
import jax.numpy as jnp
import numpy as np
from jax import lax
from jax.experimental import pallas as pl
from jax.experimental.pallas import tpu as pltpu

F32 = jnp.float32
BF16 = jnp.bfloat16

D_MODEL = 2048
CHUNK = 64
ATT_HEADS = 8
QK_DIM = 64
V_DIM = 128
D_ATT = ATT_HEADS * V_DIM
SSM_HEADS = 16
SSM_HEAD_DIM = 64
D_SSM = SSM_HEADS * SSM_HEAD_DIM
SSM_GROUPS = 2
SSM_STATE = 128
SSM_CONV = 4
BC_DIM = 2 * SSM_GROUPS * SSM_STATE
CONV_DIM = D_SSM + BC_DIM
D_FF = 5632
FFN_CONV = 3
EPS = 1e-6
LOG2E = math.log2(math.e)
Q_SCALE = QK_DIM ** -0.5 * LOG2E
LANES = 128
SUBLANES = 8
VMEM_LIMIT = 56 * 1024 * 1024

_NT = (((1,), (1,)), ((), ()))
_TN = (((0,), (0,)), ((), ()))


def _cparams(sem):
    return pltpu.CompilerParams(dimension_semantics=sem, vmem_limit_bytes=VMEM_LIMIT)


def _silu(x):
    return x * jax.nn.sigmoid(x)


def _rms(x, g):
    return x * lax.rsqrt(jnp.mean(x * x, axis=-1, keepdims=True) + EPS) * g


_PROJ_TN = 512
_PROJ_SEGS = ((0, 2), (2, 4), (4, 6), (6, 8), (8, 10), (10, 11))


def _in_proj_kernel(x_ref, g_ref, w_ref, wdt_ref,
                    q_ref, k_ref, v_ref, z_ref, xs_ref, bc_ref, dt_ref, *rest):
    w16_ref, h_scr = rest if len(rest) == 2 else (None, rest[0])
    j = pl.program_id(1)

    @pl.when(j == 0)
    def _():
        hb = _rms(x_ref[...], g_ref[...]).astype(BF16)
        h_scr[...] = hb
        dt_ref[...] = lax.dot_general(hb, wdt_ref[...], _NT, preferred_element_type=F32)

    w = w_ref[...]
    if w16_ref is not None:
        w = w.astype(BF16)
        w16_ref[...] = w
    res = lax.dot_general(h_scr[...], w, _NT, preferred_element_type=F32)
    outs = (q_ref, k_ref, v_ref, z_ref, xs_ref, bc_ref)
    for (lo, hi), ref in zip(_PROJ_SEGS, outs):
        @pl.when((j >= lo) & (j < hi))
        def _(ref=ref):
            val = res * Q_SCALE if ref is q_ref else res
            ref[...] = val.astype(ref.dtype)


def _in_proj(x, g, w_main, w_dt, tm):
    m = x.shape[0]
    tn = _PROJ_TN
    nj = w_main.shape[0] // tn

    def seg_spec(lo, hi):
        return pl.BlockSpec((tm, tn), lambda i, j: (i, jnp.clip(j - lo, 0, hi - lo - 1)))

    out_shape = (
        jax.ShapeDtypeStruct((m, D_ATT), BF16),
        jax.ShapeDtypeStruct((m, D_ATT), F32),
        jax.ShapeDtypeStruct((m, D_ATT), F32),
        jax.ShapeDtypeStruct((m, D_SSM), F32),
        jax.ShapeDtypeStruct((m, D_SSM), F32),
        jax.ShapeDtypeStruct((m, BC_DIM), F32),
        jax.ShapeDtypeStruct((m, LANES), F32),
    )
    out_specs = tuple(seg_spec(lo, hi) for lo, hi in _PROJ_SEGS) + (
        pl.BlockSpec((tm, LANES), lambda i, j: (i, 0)),)
    if w_main.dtype != BF16:
        assert m == tm, "the bf16 weight copy is written once per tile: needs a single row tile"
        out_shape += (jax.ShapeDtypeStruct((nj * tn, D_MODEL), BF16),)
        out_specs += (pl.BlockSpec((tn, D_MODEL), lambda i, j: (j, 0)),)
    return pl.pallas_call(
        _in_proj_kernel,
        grid=(m // tm, nj),
        in_specs=[
            pl.BlockSpec((tm, D_MODEL), lambda i, j: (i, 0)),
            pl.BlockSpec((1, D_MODEL), lambda i, j: (0, 0)),
            pl.BlockSpec((tn, D_MODEL), lambda i, j: (j, 0)),
            pl.BlockSpec((LANES, D_MODEL), lambda i, j: (0, 0)),
        ],
        out_specs=out_specs,
        out_shape=out_shape,
        scratch_shapes=[pltpu.VMEM((tm, D_MODEL), BF16)],
        compiler_params=_cparams(("arbitrary", "arbitrary")),
        name="in_proj",
    )(x, g, w_main, w_dt)


def _lambda_value(lamv_ref, lam_init):
    lv = lamv_ref[...]
    s1 = jnp.sum(lv[0:1] * lv[1:2], axis=-1, keepdims=True)
    s2 = jnp.sum(lv[2:3] * lv[3:4], axis=-1, keepdims=True)
    return jnp.exp(s1) - jnp.exp(s2) + lam_init


def _split_maps(q):
    lane = lax.broadcasted_iota(jnp.int32, q.shape, 1)
    zero = jnp.zeros_like(q)
    return jnp.where(lane < QK_DIM, q, zero), jnp.where(lane >= QK_DIM, q, zero)


_VT_CHUNK = 512
_VT_PAD = 16
_UNROLL_SHIFT = 2
_UNROLL = 1 << _UNROLL_SHIFT
_HEADS_PER_STEP = 2


def _attn_prompt_kernel(q_ref, k_ref, v_ref, slope_ref, lamv_ref, g_ref, dmask_ref, dneg_ref, o_ref,
                        k_scr, vt_scr, acc_scr, t0_scr, t1_scr, p0_scr, p1_scr, dbias_scr,
                        *, tile, lam_init):
    length = k_ref.shape[0]
    width = 2 * tile
    nheads = k_scr.shape[0]
    lam = _lambda_value(lamv_ref, lam_init)

    lane = lax.broadcasted_iota(jnp.int32, (length, LANES), 1)
    koff = lax.broadcasted_iota(jnp.int32, (length, LANES), 0) % tile
    koff = jnp.where(lane < 3, koff, 0).astype(F32).astype(BF16)
    extra = (lax.broadcasted_iota(jnp.int32, (_VT_PAD, tile), 0) == 0).astype(F32).astype(BF16)
    brow = lax.broadcasted_iota(jnp.int32, (LANES, width), 0)
    per = _VT_CHUNK // tile
    ws, q_biases = [], []
    for hh in range(nheads):
        cols = slice(hh * V_DIM, (hh + 1) * V_DIM)
        k_scr[hh, :, 0:V_DIM] = k_ref[:, cols].astype(BF16)
        k_scr[hh, :, V_DIM:V_DIM + LANES] = koff
        for c in range(length // _VT_CHUNK):
            vt = v_ref[c * _VT_CHUNK:(c + 1) * _VT_CHUNK, cols].T.astype(BF16)
            for s in range(per):
                vt_scr[hh, c * per + s, 0:V_DIM, :] = vt[:, s * tile:(s + 1) * tile]
                vt_scr[hh, c * per + s, V_DIM:V_DIM + _VT_PAD, :] = extra
        w = slope_ref[hh][:, :1] * LOG2E
        w_hi, w_mid, w_lo = (piece.astype(F32) for piece in _split3(w))
        q_bias = jnp.where(brow == 0, w_hi, jnp.where(brow == 1, w_mid, jnp.where(brow == 2, w_lo, 0.0)))
        ws.append(w)
        q_biases.append(q_bias.astype(BF16))
        dbias_scr[hh] = w * dmask_ref[...] + dneg_ref[...]

    def q_tile(qi, _):
        _attn_prompt_tile(qi, ws, lam, q_biases, q_ref, g_ref, o_ref,
                          k_scr, vt_scr, acc_scr, (t0_scr, t1_scr), (p0_scr, p1_scr),
                          dbias_scr, tile=tile, lam_init=lam_init)
        return 0

    lax.fori_loop(0, length // tile, q_tile, 0)


def _attn_prompt_tile(qi, ws, lam, q_biases, q_ref, g_ref, o_ref,
                      k_scr, vt_scr, acc_scr, t_slots, p_slots, dbias_scr, *, tile, lam_init):
    width = 2 * tile
    nheads = len(ws)
    rows = pl.ds(pl.multiple_of(qi * tile, tile), tile)
    q_augs = []
    for hh in range(nheads):
        q1, q2 = _split_maps(q_ref[rows, hh * V_DIM:(hh + 1) * V_DIM])
        q_t = jnp.concatenate([q1, q2], axis=0).astype(F32).T.astype(BF16)
        q_augs.append(jnp.concatenate([q_t, q_biases[hh]], axis=0))

    def scores(hh, j):
        k0 = pl.multiple_of(jnp.minimum(j, qi) * tile, tile)
        return jnp.dot(k_scr[hh, pl.ds(k0, tile), :], q_augs[hh], preferred_element_type=F32)

    def step(hh, j, slot, alpha_prev, m_old, tmax_cur):
        pv = jnp.dot(vt_scr[hh, jnp.clip(j - 1, 0, qi)], p_slots[1 - slot][hh],
                     preferred_element_type=F32)
        acc_scr[hh] = alpha_prev * acc_scr[hh] + pv
        t_next = scores(hh, j + 1)
        t_slots[1 - slot][hh] = t_next
        tmax_next = jnp.max(t_next, axis=0, keepdims=True)
        off = jnp.where(j < qi, -ws[hh] * ((qi - j) * tile).astype(F32), -jnp.inf)
        m_new = jnp.maximum(m_old, tmax_cur + off)
        p_slots[slot][hh] = jnp.exp2(t_slots[slot][hh] + (off - m_new)).astype(BF16)
        return jnp.exp2(m_old - m_new), m_new, tmax_next

    def run(first, ntiles_per_iter, niter, carry):
        def body(i, carry):
            for s in range(ntiles_per_iter):
                carry = tuple(step(hh, first + ntiles_per_iter * i + s, s % 2, *carry[hh])
                              for hh in range(nheads))
            return carry
        return lax.fori_loop(0, niter, body, carry)

    carry = []
    for hh in range(nheads):
        acc_scr[hh] = jnp.zeros(acc_scr.shape[1:], F32)
        p_slots[1][hh] = jnp.zeros(p_slots[1].shape[1:], BF16)
        t_first = scores(hh, 0)
        t_slots[0][hh] = t_first
        carry.append((jnp.ones((1, width), F32),
                      jnp.full((1, width), -0.5 * float(jnp.finfo(F32).max), F32),
                      jnp.max(t_first, axis=0, keepdims=True)))
    n_main = lax.shift_right_logical(qi, _UNROLL_SHIFT)
    done = n_main * _UNROLL
    n_rest = lax.shift_right_logical(qi - done + 1, 1)
    carry = run(0, _UNROLL, n_main, tuple(carry))
    carry = run(done, 2, n_rest, carry)
    done = done + 2 * n_rest

    for hh in range(nheads):
        alpha_last, m_old, _ = carry[hh]
        pv = jnp.dot(vt_scr[hh, jnp.clip(done - 1, 0, qi)], p_slots[1][hh], preferred_element_type=F32)
        acc = alpha_last * acc_scr[hh] + pv
        t = t_slots[0][hh] + dbias_scr[hh]
        m_new = jnp.maximum(m_old, jnp.max(t, axis=0, keepdims=True))
        p = jnp.exp2(t - m_new).astype(BF16)
        acc = jnp.exp2(m_old - m_new) * acc + jnp.dot(vt_scr[hh, qi], p, preferred_element_type=F32)

        on = acc[0:V_DIM] / acc[V_DIM:V_DIM + 1]
        o_t = on[:, :tile] - lam * on[:, tile:]
        o_t = o_t * lax.rsqrt(jnp.mean(o_t * o_t, axis=0, keepdims=True) + EPS)
        o_ref[rows, hh * V_DIM:(hh + 1) * V_DIM] = (
            o_t.T * g_ref[...] * (1.0 - lam_init)).astype(o_ref.dtype)


def _attn_prompt(q, k, v, slopes, lamv, g, lam_init, tile):
    length = q.shape[0]
    key = np.arange(tile)[:, None]
    qry = np.tile(np.arange(tile), 2)[None, :]
    dmask = jnp.asarray((qry - np.abs(qry - key)) - key, F32)
    dneg = jnp.asarray(np.where(key // CHUNK <= qry // CHUNK, 0.0, -np.inf), F32)
    nh = _HEADS_PER_STEP
    diag_spec = pl.BlockSpec((tile, 2 * tile), lambda h: (0, 0))
    head_spec = pl.BlockSpec((length, nh * V_DIM), lambda h: (0, h))
    kv_spec = pl.BlockSpec((length, nh * V_DIM), lambda h: (0, h), pipeline_mode=pl.Buffered(1))
    return pl.pallas_call(
        functools.partial(_attn_prompt_kernel, tile=tile, lam_init=lam_init),
        grid=(ATT_HEADS // nh,),
        in_specs=[
            kv_spec, kv_spec, kv_spec,
            pl.BlockSpec((nh, 1, LANES), lambda h: (h, 0, 0)),
            pl.BlockSpec((4, QK_DIM), lambda h: (0, 0)),
            pl.BlockSpec((1, V_DIM), lambda h: (0, 0)),
            diag_spec, diag_spec,
        ],
        out_specs=head_spec,
        out_shape=jax.ShapeDtypeStruct((length, D_ATT), BF16),
        scratch_shapes=[
            pltpu.VMEM((nh, length, V_DIM + LANES), BF16),
            pltpu.VMEM((nh, length // tile, V_DIM + _VT_PAD, tile), BF16),
            pltpu.VMEM((nh, V_DIM + _VT_PAD, 2 * tile), F32),
            pltpu.VMEM((nh, tile, 2 * tile), F32),
            pltpu.VMEM((nh, tile, 2 * tile), F32),
            pltpu.VMEM((nh, tile, 2 * tile), BF16),
            pltpu.VMEM((nh, tile, 2 * tile), BF16),
            pltpu.VMEM((nh, tile, 2 * tile), F32),
        ],
        compiler_params=_cparams(("arbitrary",)),
        name="attn_prompt",
    )(q, k, v, slopes, lamv, g, dmask, dneg)


def _attn_sample_kernel(q_ref, kn_ref, vn_ref, ck_ref, cv_ref, lamv_ref, g_ref, o_ref,
                        *, seq, past, lam_init):
    lam = _lambda_value(lamv_ref, lam_init)
    r_c = (lax.broadcasted_iota(jnp.int32, (2 * seq, past), 0) % seq
           - lax.broadcasted_iota(jnp.int32, (2 * seq, past), 1) + past).astype(F32)
    r_n = jnp.abs(lax.broadcasted_iota(jnp.int32, (2 * seq, seq), 0) % seq
                  - lax.broadcasted_iota(jnp.int32, (2 * seq, seq), 1)).astype(F32)
    for h in range(ATT_HEADS):
        w = 2.0 ** (-8.0 * (h + 1) / ATT_HEADS) * LOG2E
        sl = slice(h * V_DIM, (h + 1) * V_DIM)
        q1, q2 = _split_maps(q_ref[:, sl])
        qq = jnp.concatenate([q1, q2], axis=0)
        kc = ck_ref[0, pl.ds(h, past, stride=ATT_HEADS), :].astype(BF16)
        vc = cv_ref[0, pl.ds(h, past, stride=ATT_HEADS), :].astype(BF16)
        kn = kn_ref[:, sl].astype(BF16)
        vn = vn_ref[:, sl].astype(BF16)
        sc = lax.dot_general(qq, kc, _NT, preferred_element_type=F32) - w * r_c
        sn = lax.dot_general(qq, kn, _NT, preferred_element_type=F32) - w * r_n
        m = jnp.maximum(jnp.max(sc, axis=-1, keepdims=True), jnp.max(sn, axis=-1, keepdims=True))
        pc = jnp.exp2(sc - m)
        pn = jnp.exp2(sn - m)
        l = jnp.sum(pc, axis=-1, keepdims=True) + jnp.sum(pn, axis=-1, keepdims=True)
        acc = (jnp.dot(pc.astype(BF16), vc, preferred_element_type=F32)
               + jnp.dot(pn.astype(BF16), vn, preferred_element_type=F32))
        on = acc / l
        o = on[:seq] - lam * on[seq:]
        o_ref[:, sl] = (_rms(o, g_ref[...]) * (1.0 - lam_init)).astype(o_ref.dtype)


def _attn_sample(q, k_new, v_new, cache_k, cache_v, lamv, g, lam_init, seq):
    nb, past = cache_k.shape[0], cache_k.shape[1] // ATT_HEADS
    row_spec = pl.BlockSpec((seq, D_ATT), lambda b: (b, 0))
    cache_spec = pl.BlockSpec((1, past * ATT_HEADS, V_DIM), lambda b: (b, 0, 0))
    return pl.pallas_call(
        functools.partial(_attn_sample_kernel, seq=seq, past=past, lam_init=lam_init),
        grid=(nb,),
        in_specs=[row_spec, row_spec, row_spec, cache_spec, cache_spec,
                  pl.BlockSpec((4, QK_DIM), lambda b: (0, 0)),
                  pl.BlockSpec((1, V_DIM), lambda b: (0, 0))],
        out_specs=row_spec,
        out_shape=jax.ShapeDtypeStruct((nb * seq, D_ATT), BF16),
        compiler_params=_cparams(("arbitrary",)),
        name="attn_sample",
    )(q, k_new, v_new, cache_k, cache_v, lamv, g)


def _split3(x):
    hi = x.astype(BF16)
    r = x - hi.astype(F32)
    mid = r.astype(BF16)
    lo = (r - mid.astype(F32)).astype(BF16)
    return hi, mid, lo


def _ssd_chunk(xs, bm, cm, dt, z, s_ref, a_heads, dskip, gn, e3, es3, t3, ones_k, lc):
    seg_w = SSM_HEADS * lc
    half = D_SSM // SSM_GROUPS
    d3 = jnp.concatenate(_split3(dt), axis=1)
    dt_x = jnp.dot(d3, e3, preferred_element_type=F32)
    la3 = jnp.concatenate(_split3(dt * a_heads), axis=0)
    ac3 = jnp.concatenate(_split3(jnp.dot(t3, la3, preferred_element_type=F32)), axis=1)
    acol = jnp.dot(ac3, e3, preferred_element_type=F32)
    acol_s = acol if seg_w == D_SSM else jnp.dot(ac3, es3, preferred_element_type=F32)
    t_idx = lax.broadcasted_iota(jnp.int32, (lc, seg_w), 0)
    s_idx = lax.broadcasted_iota(jnp.int32, (lc, seg_w), 1) % lc
    arow = jnp.sum(jnp.where(t_idx == s_idx, acol_s, 0.0), axis=0, keepdims=True)
    decay = jnp.where(t_idx >= s_idx, jnp.exp(acol_s - arow), 0.0)

    cb16 = cm.astype(BF16)
    bb16 = bm.astype(BF16)
    hpg = SSM_HEADS // SSM_GROUPS
    cbs = []
    for g in range(SSM_GROUPS):
        gs = slice(g * SSM_STATE, (g + 1) * SSM_STATE)
        b_rep = jnp.concatenate([bb16[:, gs]] * hpg, axis=0)
        cbs.append(lax.dot_general(cb16[:, gs], b_rep, _NT, preferred_element_type=F32))
    mmat = (jnp.concatenate(cbs, axis=1) * decay).astype(BF16)

    xdt = xs * dt_x
    xdt16 = xdt.astype(BF16)
    hk = (2 * LANES) // lc
    wd = hk * SSM_HEAD_DIM
    blk = (lax.broadcasted_iota(jnp.int32, (hk * lc, wd), 0) // lc
           == lax.broadcasted_iota(jnp.int32, (hk * lc, wd), 1) // SSM_HEAD_DIM)
    parts = []
    for i in range(SSM_HEADS // hk):
        xd = xdt16[:, i * wd:(i + 1) * wd]
        bd = jnp.where(blk, jnp.concatenate([xd] * hk, axis=0), jnp.zeros((), BF16))
        parts.append(jnp.dot(mmat[:, i * hk * lc:(i + 1) * hk * lc], bd, preferred_element_type=F32))
    y_intra = jnp.concatenate(parts, axis=1) if len(parts) > 1 else parts[0]

    yi = []
    for g in range(SSM_GROUPS):
        sg = s_ref[g * half:(g + 1) * half, :].astype(BF16)
        yi.append(lax.dot_general(cb16[:, g * SSM_STATE:(g + 1) * SSM_STATE], sg, _NT,
                                  preferred_element_type=F32))
    y_inter = jnp.exp(acol) * jnp.concatenate(yi, axis=1)

    alast = acol[lc - 1:lc, :]
    dec_end = jnp.exp(alast - acol)
    xd_end = (xdt * dec_end).astype(BF16)
    krow = lax.broadcasted_iota(jnp.int32, (ones_k.shape[0], D_SSM), 0)
    a_hi, a_mid, a_lo = (piece.astype(F32) for piece in _split3(alast))
    pieces = jnp.where(krow == 0, a_hi, jnp.where(krow == 1, a_mid, jnp.where(krow == 2, a_lo, 0.0)))
    acl = lax.dot_general(pieces.astype(BF16), ones_k, _TN, preferred_element_type=F32)
    for g in range(SSM_GROUPS):
        rows = slice(g * half, (g + 1) * half)
        upd = lax.dot_general(xd_end[:, rows], bb16[:, g * SSM_STATE:(g + 1) * SSM_STATE], _TN,
                              preferred_element_type=F32)
        s_ref[rows, :] = jnp.exp(acl[rows, :]) * s_ref[rows, :] + upd

    y = (y_intra + y_inter + dskip * xs) * _silu(z)
    outs = []
    for g in range(SSM_GROUPS):
        cs = slice(g * half, (g + 1) * half)
        outs.append(_rms(y[:, cs], gn[:, cs]))
    return jnp.concatenate(outs, axis=1)


def _ssd_kernel(*refs, lc, nchunk, has_state):
    if has_state:
        (xs_ref, bc_ref, dt_ref, z_ref, prev_ref, h0_ref, cw_ref, cbias_ref, dtb_ref, alog_ref,
         dskip_ref, gn_ref, e3_ref, es3_ref, t3_ref, ones_ref,
         y_ref, s_ref, scr_x, scr_bc) = refs
    else:
        (xs_ref, bc_ref, dt_ref, z_ref, cw_ref, cbias_ref, dtb_ref, alog_ref,
         dskip_ref, gn_ref, e3_ref, es3_ref, t3_ref, ones_ref,
         y_ref, s_ref, scr_x, scr_bc) = refs
    rows = lc * nchunk
    pad = SUBLANES
    if has_state:
        s2d = s_ref.at[0]
        scr_x[0:pad, :] = prev_ref[0, :, 0:D_SSM]
        scr_bc[0:pad, :] = prev_ref[0, :, D_SSM:CONV_DIM]
        s2d[...] = h0_ref[0]
    else:
        s2d = s_ref

        @pl.when(pl.program_id(0) == 0)
        def _():
            scr_x[0:pad, :] = jnp.zeros((pad, D_SSM), F32)
            scr_bc[0:pad, :] = jnp.zeros((pad, BC_DIM), F32)
            s_ref[...] = jnp.zeros(s_ref.shape, F32)

    scr_x[pad:pad + rows, :] = xs_ref[...]
    scr_bc[pad:pad + rows, :] = bc_ref[...]
    xc = cbias_ref[:, 0:D_SSM]
    bcc = cbias_ref[:, D_SSM:CONV_DIM]
    for tap in range(SSM_CONV):
        off = pad - (SSM_CONV - 1) + tap
        xc = xc + scr_x[off:off + rows, :] * cw_ref[tap:tap + 1, 0:D_SSM]
        bcc = bcc + scr_bc[off:off + rows, :] * cw_ref[tap:tap + 1, D_SSM:CONV_DIM]
    if not has_state:
        scr_x[0:pad, :] = scr_x[rows:rows + pad, :]
        scr_bc[0:pad, :] = scr_bc[rows:rows + pad, :]
    xs_act = _silu(xc)
    bc_act = _silu(bcc)
    dt_in = dt_ref[...] + dtb_ref[...]
    dt = jnp.maximum(dt_in, 0.0) + jnp.log1p(jnp.exp(-jnp.abs(dt_in)))
    a_heads = -jnp.exp(alog_ref[...])
    nbm = SSM_GROUPS * SSM_STATE
    for c in range(nchunk):
        rs = slice(c * lc, (c + 1) * lc)
        y = _ssd_chunk(xs_act[rs], bc_act[rs, 0:nbm], bc_act[rs, nbm:2 * nbm], dt[rs],
                       z_ref[rs, :], s2d, a_heads, dskip_ref[...], gn_ref[...],
                       e3_ref[...], es3_ref[...], t3_ref[...], ones_ref[...], lc)
        y_ref[rs, :] = y.astype(y_ref.dtype)


def _ssd_constants(lc):
    seg_w = SSM_HEADS * lc
    head_of_lane = np.arange(D_SSM) // SSM_HEAD_DIM
    e = (np.arange(LANES)[:, None] == head_of_lane[None, :]).astype(np.float32)
    es = (np.arange(LANES)[:, None] == (np.arange(seg_w) // lc)[None, :]).astype(np.float32)
    tri = np.tril(np.ones((lc, lc), np.float32))
    return (jnp.asarray(np.concatenate([e] * 3, axis=0), BF16),
            jnp.asarray(np.concatenate([es] * 3, axis=0), BF16),
            jnp.asarray(np.concatenate([tri] * 3, axis=1), BF16),
            jnp.ones((2 * SUBLANES, SSM_STATE), BF16))


def _ssd(xs, bc, dt, z, conv_w, conv_b, dt_bias, a_log, d_skip, gn, lc, nchunk,
         conv_prev=None, h0=None):
    m = xs.shape[0]
    rows = lc * nchunk
    has_state = h0 is not None
    e3, es3, t3, ones_k = _ssd_constants(lc)
    dtb = jnp.zeros((1, LANES), F32).at[0, :SSM_HEADS].set(dt_bias)
    alog = jnp.zeros((1, LANES), F32).at[0, :SSM_HEADS].set(a_log)
    dskip_x = jnp.repeat(d_skip, SSM_HEAD_DIM)[None, :]

    def const(shape):
        return pl.BlockSpec(shape, lambda i: (0,) * len(shape))

    def rowblk(width):
        return pl.BlockSpec((rows, width), lambda i: (i, 0))

    in_specs = [rowblk(D_SSM), rowblk(BC_DIM), rowblk(LANES), rowblk(D_SSM)]
    args = [xs, bc, dt, z]
    if has_state:
        nb = h0.shape[0]
        in_specs += [pl.BlockSpec((1, SUBLANES, CONV_DIM), lambda i: (i, 0, 0)),
                     pl.BlockSpec((1, D_SSM, SSM_STATE), lambda i: (i, 0, 0))]
        args += [conv_prev, h0]
        s_shape = jax.ShapeDtypeStruct((nb, D_SSM, SSM_STATE), F32)
        s_spec = pl.BlockSpec((1, D_SSM, SSM_STATE), lambda i: (i, 0, 0))
    else:
        s_shape = jax.ShapeDtypeStruct((D_SSM, SSM_STATE), F32)
        s_spec = const((D_SSM, SSM_STATE))
    in_specs += [const((SSM_CONV, CONV_DIM)), const((1, CONV_DIM)), const((1, LANES)),
                 const((1, LANES)), const((1, D_SSM)), const((1, D_SSM)),
                 const(e3.shape), const(es3.shape), const(t3.shape), const(ones_k.shape)]
    args += [conv_w, conv_b[None, :], dtb, alog, dskip_x, gn[None, :], e3, es3, t3, ones_k]
    return pl.pallas_call(
        functools.partial(_ssd_kernel, lc=lc, nchunk=nchunk, has_state=has_state),
        grid=(m // rows,),
        in_specs=in_specs,
        out_specs=(rowblk(D_SSM), s_spec),
        out_shape=(jax.ShapeDtypeStruct((m, D_SSM), BF16), s_shape),
        scratch_shapes=[pltpu.VMEM((rows + SUBLANES, D_SSM), F32),
                        pltpu.VMEM((rows + SUBLANES, BC_DIM), F32)],
        compiler_params=_cparams(("arbitrary",)),
        name="ssd_sample" if has_state else "ssd_prompt",
    )(*args)


def _out_proj_kernel(o_ref, y_ref, x_ref, w_ref, g_ref, x2_ref, hf_ref, w16_ref=None):
    if w16_ref is not None:
        w16_ref[...] = w_ref[...].astype(BF16)
        w_ref = w16_ref
    x2 = (x_ref[...]
          + jnp.dot(o_ref[...], w_ref[0:D_ATT, :], preferred_element_type=F32)
          + jnp.dot(y_ref[...], w_ref[D_ATT:D_ATT + D_SSM, :], preferred_element_type=F32))
    x2_ref[...] = x2
    hf_ref[...] = _rms(x2, g_ref[...]).astype(hf_ref.dtype)


def _out_proj(o, y, x, w, g, tm):
    m = x.shape[0]
    w_spec = pl.BlockSpec((D_ATT + D_SSM, D_MODEL), lambda i: (0, 0), pipeline_mode=pl.Buffered(1))
    out_specs = (pl.BlockSpec((tm, D_MODEL), lambda i: (i, 0)),
                 pl.BlockSpec((tm, D_MODEL), lambda i: (i, 0)))
    out_shape = (jax.ShapeDtypeStruct((m, D_MODEL), F32),
                 jax.ShapeDtypeStruct((m, D_MODEL), BF16))
    if w.dtype != BF16:
        assert m == tm, "the bf16 weight copy is written once: needs a single row tile"
        out_specs += (pl.BlockSpec((D_ATT + D_SSM, D_MODEL), lambda i: (0, 0)),)
        out_shape += (jax.ShapeDtypeStruct(w.shape, BF16),)
    return pl.pallas_call(
        _out_proj_kernel,
        grid=(m // tm,),
        in_specs=[
            pl.BlockSpec((tm, D_ATT), lambda i: (i, 0)),
            pl.BlockSpec((tm, D_SSM), lambda i: (i, 0)),
            pl.BlockSpec((tm, D_MODEL), lambda i: (i, 0)),
            w_spec,
            pl.BlockSpec((1, D_MODEL), lambda i: (0, 0)),
        ],
        out_specs=out_specs,
        out_shape=out_shape,
        compiler_params=_cparams(("arbitrary",)),
        name="out_proj",
    )(o, y, x, w, g)


def _ffn_kernel(*refs, tm, seq, final_norm):
    if seq is None:
        (hf_ref, x2_ref, wg_ref, wu_ref, wd_ref, cw_ref, cb_ref, gfin_ref,
         out_ref, gl_ref, g_scr, act_scr, carry_scr) = refs
    else:
        (hf_ref, x2_ref, wg_ref, wu_ref, wd_ref, cw_ref, cb_ref, gfin_ref, prev_ref, sel1_ref, sel2_ref,
         out_ref, gl_ref, wg16_ref, wu16_ref, wd16_ref, g_scr, act_scr) = refs
    i = pl.program_id(0)
    f = pl.program_id(1)
    nf = pl.num_programs(1) - 1
    pad = SUBLANES

    def weight(w_ref, w16_ref):
        if seq is None:
            return w_ref[...]
        w16 = w_ref[...].astype(BF16)
        w16_ref[...] = w16
        return w16

    def gate_up():
        hf = hf_ref[...]
        gate = jnp.dot(hf, weight(wg_ref, None if seq is None else wg16_ref), preferred_element_type=F32)
        up = jnp.dot(hf, weight(wu_ref, None if seq is None else wu16_ref), preferred_element_type=F32)
        g_scr[pad:pad + tm, :] = gate
        if seq is None:
            g_scr[0:pad, :] = carry_scr[f]
            carry_scr[f] = g_scr[tm:tm + pad, :]
            gl_ref[...] = g_scr[tm:tm + pad, :]
            g1 = g_scr[pad - 1:pad - 1 + tm, :]
            g2 = g_scr[pad - 2:pad - 2 + tm, :]
        else:
            g_scr[0:pad, :] = jnp.zeros((pad, gate.shape[1]), F32)
            gl_ref[...] = gate
            prev3 = jnp.concatenate(_split3(prev_ref[...]), axis=0)
            ov1 = jnp.dot(sel1_ref[...], prev3, preferred_element_type=F32)
            ov2 = jnp.dot(sel2_ref[...], prev3, preferred_element_type=F32)
            pos = lax.broadcasted_iota(jnp.int32, gate.shape, 0) % seq
            g1 = jnp.where(pos == 0, ov1, g_scr[pad - 1:pad - 1 + tm, :])
            g2 = jnp.where(pos < 2, ov2, g_scr[pad - 2:pad - 2 + tm, :])
        conv = cb_ref[...] + g2 * cw_ref[0:1, :] + g1 * cw_ref[1:2, :] + gate * cw_ref[2:3, :]
        act_scr[...] = (_silu(conv) * up).astype(BF16)

    def down():
        return jnp.dot(act_scr[...], weight(wd_ref, None if seq is None else wd16_ref),
                       preferred_element_type=F32)

    @pl.when(f == 0)
    def _():
        if seq is None:
            @pl.when(i == 0)
            def _():
                carry_scr[...] = jnp.zeros(carry_scr.shape, F32)
        out_ref[...] = x2_ref[...]
        gate_up()

    @pl.when((f > 0) & (f < nf))
    def _():
        contrib = down()
        gate_up()
        out_ref[...] += contrib

    @pl.when(f == nf)
    def _():
        res = out_ref[...] + down()
        out_ref[...] = _rms(res, gfin_ref[...]) if final_norm else res


def _ffn_prev_selectors(m, seq, nprev):
    sel1 = np.zeros((m, 3 * nprev), np.float32)
    sel2 = np.zeros((m, 3 * nprev), np.float32)
    nstate = FFN_CONV - 1
    for b in range(m // seq):
        for piece in range(3):
            base = piece * nprev + b * nstate
            sel1[b * seq, base + 1] = 1.0
            sel2[b * seq, base + 0] = 1.0
            sel2[b * seq + 1, base + 1] = 1.0
    return jnp.asarray(sel1, BF16), jnp.asarray(sel2, BF16)


def _ffn(hf, x2, wg, wu, wd, cw, cb, gfin, tm, tf, final_norm, seq=None, prev=None):
    m = hf.shape[0]
    nf = D_FF // tf
    cur = lambda f: jnp.minimum(f, nf - 1)
    last = lambda f: jnp.maximum(f - 1, 0)
    in_specs = [
        pl.BlockSpec((tm, D_MODEL), lambda i, f: (i, 0)),
        pl.BlockSpec((tm, D_MODEL), lambda i, f: (i, 0), pipeline_mode=pl.Buffered(1)),
        pl.BlockSpec((D_MODEL, tf), lambda i, f: (0, cur(f))),
        pl.BlockSpec((D_MODEL, tf), lambda i, f: (0, cur(f))),
        pl.BlockSpec((tf, D_MODEL), lambda i, f: (last(f), 0)),
        pl.BlockSpec((FFN_CONV, tf), lambda i, f: (0, cur(f))),
        pl.BlockSpec((1, tf), lambda i, f: (0, cur(f))),
        pl.BlockSpec((1, D_MODEL), lambda i, f: (0, 0)),
    ]
    args = [hf, x2, wg, wu, wd, cw, cb[None, :], gfin[None, :]]
    scratch = [pltpu.VMEM((tm + SUBLANES, tf), F32), pltpu.VMEM((tm, tf), BF16)]
    if seq is None:
        gl_shape = jax.ShapeDtypeStruct((m // tm * SUBLANES, D_FF), F32)
        gl_spec = pl.BlockSpec((SUBLANES, tf), lambda i, f: (i, cur(f)))
        scratch.append(pltpu.VMEM((nf, SUBLANES, tf), F32))
    else:
        assert m == tm, "sample FFN handles all sequences in one row tile"
        nprev = prev.shape[0]
        sel1, sel2 = _ffn_prev_selectors(m, seq, nprev)
        in_specs += [pl.BlockSpec((nprev, tf), lambda i, f: (0, cur(f))),
                     pl.BlockSpec(sel1.shape, lambda i, f: (0, 0)),
                     pl.BlockSpec(sel2.shape, lambda i, f: (0, 0))]
        args += [prev, sel1, sel2]
        gl_shape = jax.ShapeDtypeStruct((m, D_FF), F32)
        gl_spec = pl.BlockSpec((tm, tf), lambda i, f: (i, cur(f)))
    out_specs = [pl.BlockSpec((tm, D_MODEL), lambda i, f: (i, 0)), gl_spec]
    out_shape = [jax.ShapeDtypeStruct((m, D_MODEL), F32), gl_shape]
    if seq is not None:
        out_specs += [in_specs[2], in_specs[3], in_specs[4]]
        out_shape += [jax.ShapeDtypeStruct(wg.shape, BF16), jax.ShapeDtypeStruct(wu.shape, BF16),
                      jax.ShapeDtypeStruct(wd.shape, BF16)]
    return pl.pallas_call(
        functools.partial(_ffn_kernel, tm=tm, seq=seq, final_norm=final_norm),
        grid=(m // tm, nf + 1),
        in_specs=in_specs,
        out_specs=tuple(out_specs),
        out_shape=tuple(out_shape),
        scratch_shapes=scratch,
        compiler_params=_cparams(("arbitrary", "arbitrary")),
        name="ffn_sample" if seq is not None else "ffn_prompt",
    )(*args)


def _layer(x, w, lam_init, final_norm, gfin, *, batch, seq, state=None):
    m = x.shape[0]
    lamv = jnp.stack([w["lambda_q1"], w["lambda_k1"], w["lambda_q2"], w["lambda_k2"]])
    tm = min(512, m)
    w16 = {}
    q, k, v, z, xs, bc, dt, *extra = _in_proj(x, w["norm_mix_g"][None, :], w["w_in_main"], w["w_in_dt"],
                                              tm=min(1024, m))
    if extra:
        w16["w_in_main"] = extra[0]

    if state is None:
        slopes = jnp.broadcast_to(
            jnp.asarray(2.0 ** (-8.0 * np.arange(1, ATT_HEADS + 1) / ATT_HEADS), F32)[:, None, None],
            (ATT_HEADS, 1, LANES))
        o = _attn_prompt(q, k, v, slopes, lamv, w["attn_subln_g"][None, :], lam_init, tile=256)
        y, s_new = _ssd(xs, bc, dt, z, w["conv_w"], w["conv_b"], w["dt_bias"], w["a_log"],
                        w["d_skip"], w["ssm_norm_g"], lc=CHUNK, nchunk=4)
        conv_new = jnp.concatenate([xs[m - (SSM_CONV - 1):], bc[m - (SSM_CONV - 1):]], axis=-1)[None]
        s_new = s_new[None]
    else:
        cache_k, cache_v, conv_prev, ssm_prev, ffn_prev = state
        past = cache_k.shape[1]
        o = _attn_sample(q, k, v, cache_k.reshape(batch, past * ATT_HEADS, V_DIM),
                         cache_v.reshape(batch, past * ATT_HEADS, V_DIM),
                         lamv, w["attn_subln_g"][None, :], lam_init, seq)
        prev8 = jnp.pad(conv_prev, ((0, 0), (SUBLANES - (SSM_CONV - 1), 0), (0, 0)))
        y, s_new = _ssd(xs, bc, dt, z, w["conv_w"], w["conv_b"], w["dt_bias"], w["a_log"],
                        w["d_skip"], w["ssm_norm_g"], lc=seq, nchunk=1, conv_prev=prev8,
                        h0=ssm_prev.reshape(batch, D_SSM, SSM_STATE))
        conv_new = jnp.concatenate([xs.reshape(batch, seq, D_SSM)[:, seq - (SSM_CONV - 1):],
                                    bc.reshape(batch, seq, BC_DIM)[:, seq - (SSM_CONV - 1):]], axis=-1)

    x2, hf, *extra = _out_proj(o, y, x, w["w_out"], w["norm_ffn_g"][None, :], tm=tm)
    if extra:
        w16["w_out"] = extra[0]

    ffn_args = (hf, x2, w["w_gate"], w["w_up"], w["w_down"], w["ffn_conv_w"], w["ffn_conv_b"], gfin)
    if state is None:
        x3, gl = _ffn(*ffn_args, tm=min(1024, m), tf=512, final_norm=final_norm)
        ffn_new = gl[None, gl.shape[0] - (FFN_CONV - 1):]
    else:
        x3, gl, w16["w_gate"], w16["w_up"], w16["w_down"] = _ffn(
            *ffn_args, tm=tm, tf=256, final_norm=final_norm, seq=seq,
            prev=ffn_prev.reshape(batch * (FFN_CONV - 1), D_FF))
        ffn_new = gl.reshape(batch, seq, D_FF)[:, seq - (FFN_CONV - 1):]
    k_new = k.reshape(batch, seq, ATT_HEADS, 2 * QK_DIM)
    v_new = v.reshape(batch, seq, ATT_HEADS, V_DIM)
    s_new = s_new.reshape(batch, SSM_HEADS, SSM_HEAD_DIM, SSM_STATE)
    return (x3, k_new, v_new, conv_new, s_new, ffn_new), w16


def kernel(x_prompt, x_sample, cache_k, cache_v, state_ssm_conv, state_ssm, state_ffn_conv, norm_mix_g, w_in, lambda_q1, lambda_k1, lambda_q2, lambda_k2, attn_subln_g, conv_w, conv_b, dt_bias, a_log, d_skip, ssm_norm_g, w_out, norm_ffn_g, w_gate, w_up, ffn_conv_w, ffn_conv_b, w_down, norm_final_g):
    depth = w_in.shape[0]
    pb, pl_len, _ = x_prompt.shape
    sb, sl_len, _ = x_sample.shape
    assert pb == 1 and pl_len % CHUNK == 0
    xp = x_prompt.reshape(pb * pl_len, D_MODEL)
    xs = x_sample.reshape(sb * sl_len, D_MODEL)
    n_main = 2 * D_ATT + D_ATT + D_SSM + CONV_DIM
    outs_p, outs_s = [], []
    for layer in range(depth):
        lam_init = 0.8 - 0.6 * math.exp(-0.3 * layer)
        w_in_t = jnp.swapaxes(w_in[layer], 0, 1)
        w = dict(
            norm_mix_g=norm_mix_g[layer],
            w_in_main=w_in_t,
            w_in_dt=jnp.pad(w_in_t[n_main:].astype(BF16), ((0, LANES - SSM_HEADS), (0, 0))),
            lambda_q1=lambda_q1[layer], lambda_k1=lambda_k1[layer],
            lambda_q2=lambda_q2[layer], lambda_k2=lambda_k2[layer],
            attn_subln_g=attn_subln_g[layer], conv_w=conv_w[layer], conv_b=conv_b[layer],
            dt_bias=dt_bias[layer], a_log=a_log[layer], d_skip=d_skip[layer],
            ssm_norm_g=ssm_norm_g[layer], w_out=w_out[layer],
            norm_ffn_g=norm_ffn_g[layer], w_gate=w_gate[layer],
            w_up=w_up[layer], ffn_conv_w=ffn_conv_w[layer],
            ffn_conv_b=ffn_conv_b[layer], w_down=w_down[layer],
        )
        last = layer == depth - 1
        (xs, *new_s), w16 = _layer(xs, w, lam_init, last, norm_final_g, batch=sb, seq=sl_len,
                                   state=(cache_k[layer], cache_v[layer], state_ssm_conv[layer],
                                          state_ssm[layer], state_ffn_conv[layer]))
        (xp, *new_p), _ = _layer(xp, {**w, **w16}, lam_init, last, norm_final_g, batch=pb, seq=pl_len)
        outs_p.append(new_p)
        outs_s.append(new_s)
    stack = lambda outs, idx: jnp.stack([o[idx] for o in outs])
    return (xp.reshape(pb, pl_len, D_MODEL), xs.reshape(sb, sl_len, D_MODEL),
            *[stack(outs_p, idx) for idx in range(5)],
            *[stack(outs_s, idx) for idx in range(5)])
```

```python
import functools
import math

import jax
import jax.numpy as jnp
import numpy as np
from jax import lax
from jax.experimental import pallas as pl
from jax.experimental.pallas import tpu as pltpu

F32 = jnp.float32
BF16 = jnp.bfloat16

D_MODEL = 2048
CHUNK = 64
ATT_HEADS = 8
QK_DIM = 64
V_DIM = 128
D_ATT = ATT_HEADS * V_DIM
SSM_HEADS = 16
SSM_HEAD_DIM = 64
D_SSM = SSM_HEADS * SSM_HEAD_DIM
SSM_GROUPS = 2
SSM_STATE = 128
SSM_CONV = 4
BC_DIM = 2 * SSM_GROUPS * SSM_STATE
CONV_DIM = D_SSM + BC_DIM
D_FF = 5632
FFN_CONV = 3
EPS = 1e-6
LOG2E = math.log2(math.e)
Q_SCALE = QK_DIM ** -0.5 * LOG2E
LANES = 128
SUBLANES = 8
VMEM_LIMIT = 56 * 1024 * 1024

_NT = (((1,), (1,)), ((), ()))
_TN = (((0,), (0,)), ((), ()))


def _cparams(sem):
    return pltpu.CompilerParams(dimension_semantics=sem, vmem_limit_bytes=VMEM_LIMIT)


def _silu(x):
    return x * jax.nn.sigmoid(x)


def _rms(x, g):
    return x * lax.rsqrt(jnp.mean(x * x, axis=-1, keepdims=True) + EPS) * g


_PROJ_TN = 512
_PROJ_SEGS = ((0, 2), (2, 4), (4, 6), (6, 8), (8, 10), (10, 11))


def _in_proj_kernel(x_ref, g_ref, w_ref, wdt_ref,
                    q_ref, k_ref, v_ref, z_ref, xs_ref, bc_ref, dt_ref, *rest):
    w16_ref, h_scr = rest if len(rest) == 2 else (None, rest[0])
    j = pl.program_id(1)

    @pl.when(j == 0)
    def _():
        hb = _rms(x_ref[...], g_ref[...]).astype(BF16)
        h_scr[...] = hb
        dt_ref[...] = lax.dot_general(hb, wdt_ref[...], _NT, preferred_element_type=F32)

    w = w_ref[...]
    if w16_ref is not None:
        w = w.astype(BF16)
        w16_ref[...] = w
    res = lax.dot_general(h_scr[...], w, _NT, preferred_element_type=F32)
    outs = (q_ref, k_ref, v_ref, z_ref, xs_ref, bc_ref)
    for (lo, hi), ref in zip(_PROJ_SEGS, outs):
        @pl.when((j >= lo) & (j < hi))
        def _(ref=ref):
            val = res * Q_SCALE if ref is q_ref else res
            ref[...] = val.astype(ref.dtype)


def _in_proj(x, g, w_main, w_dt, tm):
    m = x.shape[0]
    tn = _PROJ_TN
    nj = w_main.shape[0] // tn

    def seg_spec(lo, hi):
        return pl.BlockSpec((tm, tn), lambda i, j: (i, jnp.clip(j - lo, 0, hi - lo - 1)))

    out_shape = (
        jax.ShapeDtypeStruct((m, D_ATT), BF16),
        jax.ShapeDtypeStruct((m, D_ATT), F32),
        jax.ShapeDtypeStruct((m, D_ATT), F32),
        jax.ShapeDtypeStruct((m, D_SSM), F32),
        jax.ShapeDtypeStruct((m, D_SSM), F32),
        jax.ShapeDtypeStruct((m, BC_DIM), F32),
        jax.ShapeDtypeStruct((m, LANES), F32),
    )
    out_specs = tuple(seg_spec(lo, hi) for lo, hi in _PROJ_SEGS) + (
        pl.BlockSpec((tm, LANES), lambda i, j: (i, 0)),)
    if w_main.dtype != BF16:
        assert m == tm, "the bf16 weight copy is written once per tile: needs a single row tile"
        out_shape += (jax.ShapeDtypeStruct((nj * tn, D_MODEL), BF16),)
        out_specs += (pl.BlockSpec((tn, D_MODEL), lambda i, j: (j, 0)),)
    return pl.pallas_call(
        _in_proj_kernel,
        grid=(m // tm, nj),
        in_specs=[
            pl.BlockSpec((tm, D_MODEL), lambda i, j: (i, 0)),
            pl.BlockSpec((1, D_MODEL), lambda i, j: (0, 0)),
            pl.BlockSpec((tn, D_MODEL), lambda i, j: (j, 0)),
            pl.BlockSpec((LANES, D_MODEL), lambda i, j: (0, 0)),
        ],
        out_specs=out_specs,
        out_shape=out_shape,
        scratch_shapes=[pltpu.VMEM((tm, D_MODEL), BF16)],
        compiler_params=_cparams(("arbitrary", "arbitrary")),
        name="in_proj",
    )(x, g, w_main, w_dt)


def _lambda_value(lamv_ref, lam_init):
    lv = lamv_ref[...]
    s1 = jnp.sum(lv[0:1] * lv[1:2], axis=-1, keepdims=True)
    s2 = jnp.sum(lv[2:3] * lv[3:4], axis=-1, keepdims=True)
    return jnp.exp(s1) - jnp.exp(s2) + lam_init


def _split_maps(q):
    lane = lax.broadcasted_iota(jnp.int32, q.shape, 1)
    zero = jnp.zeros_like(q)
    return jnp.where(lane < QK_DIM, q, zero), jnp.where(lane >= QK_DIM, q, zero)


_VT_CHUNK = 512
_VT_PAD = 16
_UNROLL_SHIFT = 2
_UNROLL = 1 << _UNROLL_SHIFT
_HEADS_PER_STEP = 2


def _attn_prompt_kernel(q_ref, k_ref, v_ref, slope_ref, lamv_ref, g_ref, dmask_ref, dneg_ref, o_ref,
                        k_scr, vt_scr, acc_scr, t0_scr, t1_scr, p0_scr, p1_scr, dbias_scr, stat_scr,
                        *, tile, lam_init):
    length = k_ref.shape[0]
    width = 2 * tile
    nheads = k_scr.shape[0]
    lam = _lambda_value(lamv_ref, lam_init)

    lane = lax.broadcasted_iota(jnp.int32, (length, LANES), 1)
    koff = lax.broadcasted_iota(jnp.int32, (length, LANES), 0) % tile
    koff = jnp.where(lane < 3, koff, 0).astype(F32).astype(BF16)
    extra = (lax.broadcasted_iota(jnp.int32, (_VT_PAD, tile), 0) == 0).astype(F32).astype(BF16)
    brow = lax.broadcasted_iota(jnp.int32, (LANES, width), 0)
    per = _VT_CHUNK // tile
    ws, q_biases = [], []
    for hh in range(nheads):
        cols = slice(hh * V_DIM, (hh + 1) * V_DIM)
        k_scr[hh, :, 0:V_DIM] = k_ref[:, cols].astype(BF16)
        k_scr[hh, :, V_DIM:V_DIM + LANES] = koff
        for c in range(length // _VT_CHUNK):
            vt = v_ref[c * _VT_CHUNK:(c + 1) * _VT_CHUNK, cols].T.astype(BF16)
            for s in range(per):
                vt_scr[hh, c * per + s, 0:V_DIM, :] = vt[:, s * tile:(s + 1) * tile]
                vt_scr[hh, c * per + s, V_DIM:V_DIM + _VT_PAD, :] = extra
        w = slope_ref[hh][:, :1] * LOG2E
        w_hi, w_mid, w_lo = (piece.astype(F32) for piece in _split3(w))
        q_bias = jnp.where(brow == 0, w_hi, jnp.where(brow == 1, w_mid, jnp.where(brow == 2, w_lo, 0.0)))
        ws.append(w)
        q_biases.append(q_bias.astype(BF16))
        dbias_scr[hh] = w * dmask_ref[...] + dneg_ref[...]

    def q_pair(qp, _):
        _attn_prompt_pair(qp, ws, lam, q_biases, q_ref, g_ref, o_ref,
                          k_scr, vt_scr, acc_scr, (t0_scr, t1_scr), (p0_scr, p1_scr),
                          dbias_scr, stat_scr, tile=tile, lam_init=lam_init)
        return 0

    lax.fori_loop(0, length // (2 * tile), q_pair, 0)


def _attn_prompt_pair(qp, ws, lam, q_biases, q_ref, g_ref, o_ref,
                      k_scr, vt_scr, acc_scr, t_slots, p_slots, dbias_scr, stat_scr, *, tile, lam_init):
    width = 2 * tile
    nheads = len(ws)
    streams = [(hh, u) for hh in range(nheads) for u in range(2)]
    nfull = 2 * qp
    qis = [nfull + u for _, u in streams]
    rows = [pl.ds(pl.multiple_of(qi * tile, tile), tile) for qi in qis]
    q_augs = []
    for s, (hh, _) in enumerate(streams):
        q1, q2 = _split_maps(q_ref[rows[s], hh * V_DIM:(hh + 1) * V_DIM])
        q_t = jnp.concatenate([q1, q2], axis=0).astype(F32).T.astype(BF16)
        q_augs.append(jnp.concatenate([q_t, q_biases[hh]], axis=0))

    def scores(s, j):
        k0 = pl.multiple_of(jnp.minimum(j, qis[s]) * tile, tile)
        return jnp.dot(k_scr[streams[s][0], pl.ds(k0, tile), :], q_augs[s], preferred_element_type=F32)

    ALPHA, TMAX = 0, 1

    def step(s, j, slot, m_old):
        hh = streams[s][0]
        pv = jnp.dot(vt_scr[hh, jnp.maximum(j - 1, 0)], p_slots[1 - slot][s],
                     preferred_element_type=F32)
        acc_scr[s] = stat_scr[s, ALPHA:ALPHA + 1, :] * acc_scr[s] + pv
        off = -ws[hh] * ((qis[s] - j) * tile).astype(F32)
        m_new = jnp.maximum(m_old, stat_scr[s, TMAX:TMAX + 1, :] + off)
        t_next = scores(s, j + 1)
        t_slots[1 - slot][s] = t_next
        stat_scr[s, TMAX:TMAX + 1, :] = jnp.max(t_next, axis=0, keepdims=True)
        p_slots[slot][s] = jnp.exp2(t_slots[slot][s] + (off - m_new)).astype(BF16)
        stat_scr[s, ALPHA:ALPHA + 1, :] = jnp.exp2(m_old - m_new)
        return m_new

    def run(first, ntiles_per_iter, niter, carry):
        def body(i, carry):
            for k in range(ntiles_per_iter):
                carry = tuple(step(s, first + ntiles_per_iter * i + k, k % 2, carry[s])
                              for s in range(len(streams)))
            return carry
        return lax.fori_loop(0, niter, body, carry)

    carry = []
    for s in range(len(streams)):
        acc_scr[s] = jnp.zeros(acc_scr.shape[1:], F32)
        p_slots[1][s] = jnp.zeros(p_slots[1].shape[1:], BF16)
        t_first = scores(s, 0)
        t_slots[0][s] = t_first
        stat_scr[s, ALPHA:ALPHA + 1, :] = jnp.ones((1, width), F32)
        stat_scr[s, TMAX:TMAX + 1, :] = jnp.max(t_first, axis=0, keepdims=True)
        carry.append(jnp.full((1, width), -0.5 * float(jnp.finfo(F32).max), F32))
    n_main = lax.shift_right_logical(nfull, _UNROLL_SHIFT)
    done = n_main * _UNROLL
    carry = run(0, _UNROLL, n_main, tuple(carry))
    carry = run(done, 2, lax.shift_right_logical(nfull - done, 1), carry)

    for s, (hh, u) in enumerate(streams):
        m_old = carry[s]
        diag_slot = 0
        if u == 1:
            m_old = step(s, nfull, 0, m_old)
            diag_slot = 1
        alpha_last = stat_scr[s, ALPHA:ALPHA + 1, :]
        pv = jnp.dot(vt_scr[hh, jnp.maximum(qis[s] - 1, 0)], p_slots[1 - diag_slot][s],
                     preferred_element_type=F32)
        acc = alpha_last * acc_scr[s] + pv
        t = t_slots[diag_slot][s] + dbias_scr[hh]
        m_new = jnp.maximum(m_old, jnp.max(t, axis=0, keepdims=True))
        p = jnp.exp2(t - m_new).astype(BF16)
        acc = jnp.exp2(m_old - m_new) * acc + jnp.dot(vt_scr[hh, qis[s]], p, preferred_element_type=F32)

        on = acc[0:V_DIM] / acc[V_DIM:V_DIM + 1]
        o_t = on[:, :tile] - lam * on[:, tile:]
        o_t = o_t * lax.rsqrt(jnp.mean(o_t * o_t, axis=0, keepdims=True) + EPS)
        o_ref[rows[s], hh * V_DIM:(hh + 1) * V_DIM] = (
            o_t.T * g_ref[...] * (1.0 - lam_init)).astype(o_ref.dtype)


def _attn_prompt(q, k, v, slopes, lamv, g, lam_init, tile):
    length = q.shape[0]
    key = np.arange(tile)[:, None]
    qry = np.tile(np.arange(tile), 2)[None, :]
    dmask = jnp.asarray((qry - np.abs(qry - key)) - key, F32)
    dneg = jnp.asarray(np.where(key // CHUNK <= qry // CHUNK, 0.0, -np.inf), F32)
    nh = _HEADS_PER_STEP
    diag_spec = pl.BlockSpec((tile, 2 * tile), lambda h: (0, 0), pipeline_mode=pl.Buffered(1))
    head_spec = pl.BlockSpec((length, nh * V_DIM), lambda h: (0, h), pipeline_mode=pl.Buffered(1))
    kv_spec = pl.BlockSpec((length, nh * V_DIM), lambda h: (0, h), pipeline_mode=pl.Buffered(1))
    return pl.pallas_call(
        functools.partial(_attn_prompt_kernel, tile=tile, lam_init=lam_init),
        grid=(ATT_HEADS // nh,),
        in_specs=[
            kv_spec, kv_spec, kv_spec,
            pl.BlockSpec((nh, 1, LANES), lambda h: (h, 0, 0)),
            pl.BlockSpec((4, QK_DIM), lambda h: (0, 0)),
            pl.BlockSpec((1, V_DIM), lambda h: (0, 0)),
            diag_spec, diag_spec,
        ],
        out_specs=head_spec,
        out_shape=jax.ShapeDtypeStruct((length, D_ATT), BF16),
        scratch_shapes=[
            pltpu.VMEM((nh, length, V_DIM + LANES), BF16),
            pltpu.VMEM((nh, length // tile, V_DIM + _VT_PAD, tile), BF16),
            pltpu.VMEM((2 * nh, V_DIM + _VT_PAD, 2 * tile), F32),
            pltpu.VMEM((2 * nh, tile, 2 * tile), F32),
            pltpu.VMEM((2 * nh, tile, 2 * tile), F32),
            pltpu.VMEM((2 * nh, tile, 2 * tile), BF16),
            pltpu.VMEM((2 * nh, tile, 2 * tile), BF16),
            pltpu.VMEM((nh, tile, 2 * tile), F32),
            pltpu.VMEM((2 * nh, SUBLANES, 2 * tile), F32),
        ],
        compiler_params=_cparams(("arbitrary",)),
        name="attn_prompt",
    )(q, k, v, slopes, lamv, g, dmask, dneg)


def _attn_sample_kernel(q_ref, kn_ref, vn_ref, ck_ref, cv_ref, lamv_ref, g_ref, o_ref,
                        *, seq, past, lam_init):
    lam = _lambda_value(lamv_ref, lam_init)
    r_c = (lax.broadcasted_iota(jnp.int32, (2 * seq, past), 0) % seq
           - lax.broadcasted_iota(jnp.int32, (2 * seq, past), 1) + past).astype(F32)
    r_n = jnp.abs(lax.broadcasted_iota(jnp.int32, (2 * seq, seq), 0) % seq
                  - lax.broadcasted_iota(jnp.int32, (2 * seq, seq), 1)).astype(F32)
    for h in range(ATT_HEADS):
        w = 2.0 ** (-8.0 * (h + 1) / ATT_HEADS) * LOG2E
        sl = slice(h * V_DIM, (h + 1) * V_DIM)
        q1, q2 = _split_maps(q_ref[:, sl])
        qq = jnp.concatenate([q1, q2], axis=0)
        kc = ck_ref[0, pl.ds(h, past, stride=ATT_HEADS), :].astype(BF16)
        vc = cv_ref[0, pl.ds(h, past, stride=ATT_HEADS), :].astype(BF16)
        kn = kn_ref[:, sl].astype(BF16)
        vn = vn_ref[:, sl].astype(BF16)
        sc = lax.dot_general(qq, kc, _NT, preferred_element_type=F32) - w * r_c
        sn = lax.dot_general(qq, kn, _NT, preferred_element_type=F32) - w * r_n
        m = jnp.maximum(jnp.max(sc, axis=-1, keepdims=True), jnp.max(sn, axis=-1, keepdims=True))
        pc = jnp.exp2(sc - m)
        pn = jnp.exp2(sn - m)
        l = jnp.sum(pc, axis=-1, keepdims=True) + jnp.sum(pn, axis=-1, keepdims=True)
        acc = (jnp.dot(pc.astype(BF16), vc, preferred_element_type=F32)
               + jnp.dot(pn.astype(BF16), vn, preferred_element_type=F32))
        on = acc / l
        o = on[:seq] - lam * on[seq:]
        o_ref[:, sl] = (_rms(o, g_ref[...]) * (1.0 - lam_init)).astype(o_ref.dtype)


def _attn_sample(q, k_new, v_new, cache_k, cache_v, lamv, g, lam_init, seq):
    nb, past = cache_k.shape[0], cache_k.shape[1] // ATT_HEADS
    row_spec = pl.BlockSpec((seq, D_ATT), lambda b: (b, 0))
    cache_spec = pl.BlockSpec((1, past * ATT_HEADS, V_DIM), lambda b: (b, 0, 0))
    return pl.pallas_call(
        functools.partial(_attn_sample_kernel, seq=seq, past=past, lam_init=lam_init),
        grid=(nb,),
        in_specs=[row_spec, row_spec, row_spec, cache_spec, cache_spec,
                  pl.BlockSpec((4, QK_DIM), lambda b: (0, 0)),
                  pl.BlockSpec((1, V_DIM), lambda b: (0, 0))],
        out_specs=row_spec,
        out_shape=jax.ShapeDtypeStruct((nb * seq, D_ATT), BF16),
        compiler_params=_cparams(("arbitrary",)),
        name="attn_sample",
    )(q, k_new, v_new, cache_k, cache_v, lamv, g)


def _split3(x):
    hi = x.astype(BF16)
    r = x - hi.astype(F32)
    mid = r.astype(BF16)
    lo = (r - mid.astype(F32)).astype(BF16)
    return hi, mid, lo


def _ssd_chunk(xs, bm, cm, dt, z, s_ref, a_heads, dskip, gn, e3, es3, t3, ones_k, lc):
    seg_w = SSM_HEADS * lc
    half = D_SSM // SSM_GROUPS
    d3 = jnp.concatenate(_split3(dt), axis=1)
    dt_x = jnp.dot(d3, e3, preferred_element_type=F32)
    la3 = jnp.concatenate(_split3(dt * a_heads), axis=0)
    ac3 = jnp.concatenate(_split3(jnp.dot(t3, la3, preferred_element_type=F32)), axis=1)
    acol = jnp.dot(ac3, e3, preferred_element_type=F32)
    acol_s = acol if seg_w == D_SSM else jnp.dot(ac3, es3, preferred_element_type=F32)
    t_idx = lax.broadcasted_iota(jnp.int32, (lc, seg_w), 0)
    s_idx = lax.broadcasted_iota(jnp.int32, (lc, seg_w), 1) % lc
    arow = jnp.sum(jnp.where(t_idx == s_idx, acol_s, 0.0), axis=0, keepdims=True)
    decay = jnp.where(t_idx >= s_idx, jnp.exp(acol_s - arow), 0.0)

    cb16 = cm.astype(BF16)
    bb16 = bm.astype(BF16)
    hpg = SSM_HEADS // SSM_GROUPS
    cbs = []
    for g in range(SSM_GROUPS):
        gs = slice(g * SSM_STATE, (g + 1) * SSM_STATE)
        b_rep = jnp.concatenate([bb16[:, gs]] * hpg, axis=0)
        cbs.append(lax.dot_general(cb16[:, gs], b_rep, _NT, preferred_element_type=F32))
    mmat = (jnp.concatenate(cbs, axis=1) * decay).astype(BF16)

    xdt = xs * dt_x
    xdt16 = xdt.astype(BF16)
    hk = (2 * LANES) // lc
    wd = hk * SSM_HEAD_DIM
    blk = (lax.broadcasted_iota(jnp.int32, (hk * lc, wd), 0) // lc
           == lax.broadcasted_iota(jnp.int32, (hk * lc, wd), 1) // SSM_HEAD_DIM)
    parts = []
    for i in range(SSM_HEADS // hk):
        xd = xdt16[:, i * wd:(i + 1) * wd]
        bd = jnp.where(blk, jnp.concatenate([xd] * hk, axis=0), jnp.zeros((), BF16))
        parts.append(jnp.dot(mmat[:, i * hk * lc:(i + 1) * hk * lc], bd, preferred_element_type=F32))
    y_intra = jnp.concatenate(parts, axis=1) if len(parts) > 1 else parts[0]

    yi = []
    for g in range(SSM_GROUPS):
        sg = s_ref[g * half:(g + 1) * half, :].astype(BF16)
        yi.append(lax.dot_general(cb16[:, g * SSM_STATE:(g + 1) * SSM_STATE], sg, _NT,
                                  preferred_element_type=F32))
    y_inter = jnp.exp(acol) * jnp.concatenate(yi, axis=1)

    alast = acol[lc - 1:lc, :]
    dec_end = jnp.exp(alast - acol)
    xd_end = (xdt * dec_end).astype(BF16)
    krow = lax.broadcasted_iota(jnp.int32, (ones_k.shape[0], D_SSM), 0)
    a_hi, a_mid, a_lo = (piece.astype(F32) for piece in _split3(alast))
    pieces = jnp.where(krow == 0, a_hi, jnp.where(krow == 1, a_mid, jnp.where(krow == 2, a_lo, 0.0)))
    acl = lax.dot_general(pieces.astype(BF16), ones_k, _TN, preferred_element_type=F32)
    for g in range(SSM_GROUPS):
        rows = slice(g * half, (g + 1) * half)
        upd = lax.dot_general(xd_end[:, rows], bb16[:, g * SSM_STATE:(g + 1) * SSM_STATE], _TN,
                              preferred_element_type=F32)
        s_ref[rows, :] = jnp.exp(acl[rows, :]) * s_ref[rows, :] + upd

    y = (y_intra + y_inter + dskip * xs) * _silu(z)
    outs = []
    for g in range(SSM_GROUPS):
        cs = slice(g * half, (g + 1) * half)
        outs.append(_rms(y[:, cs], gn[:, cs]))
    return jnp.concatenate(outs, axis=1)


def _ssd_kernel(*refs, lc, nchunk, has_state):
    if has_state:
        (xs_ref, bc_ref, dt_ref, z_ref, prev_ref, h0_ref, cw_ref, cbias_ref, dtb_ref, alog_ref,
         dskip_ref, gn_ref, e3_ref, es3_ref, t3_ref, ones_ref,
         y_ref, s_ref, scr_x, scr_bc) = refs
    else:
        (xs_ref, bc_ref, dt_ref, z_ref, cw_ref, cbias_ref, dtb_ref, alog_ref,
         dskip_ref, gn_ref, e3_ref, es3_ref, t3_ref, ones_ref,
         y_ref, s_ref, scr_x, scr_bc) = refs
    rows = lc * nchunk
    pad = SUBLANES
    if has_state:
        s2d = s_ref.at[0]
        scr_x[0:pad, :] = prev_ref[0, :, 0:D_SSM]
        scr_bc[0:pad, :] = prev_ref[0, :, D_SSM:CONV_DIM]
        s2d[...] = h0_ref[0]
    else:
        s2d = s_ref

        @pl.when(pl.program_id(0) == 0)
        def _():
            scr_x[0:pad, :] = jnp.zeros((pad, D_SSM), F32)
            scr_bc[0:pad, :] = jnp.zeros((pad, BC_DIM), F32)
            s_ref[...] = jnp.zeros(s_ref.shape, F32)

    scr_x[pad:pad + rows, :] = xs_ref[...]
    scr_bc[pad:pad + rows, :] = bc_ref[...]
    xc = cbias_ref[:, 0:D_SSM]
    bcc = cbias_ref[:, D_SSM:CONV_DIM]
    for tap in range(SSM_CONV):
        off = pad - (SSM_CONV - 1) + tap
        xc = xc + scr_x[off:off + rows, :] * cw_ref[tap:tap + 1, 0:D_SSM]
        bcc = bcc + scr_bc[off:off + rows, :] * cw_ref[tap:tap + 1, D_SSM:CONV_DIM]
    if not has_state:
        scr_x[0:pad, :] = scr_x[rows:rows + pad, :]
        scr_bc[0:pad, :] = scr_bc[rows:rows + pad, :]
    xs_act = _silu(xc)
    bc_act = _silu(bcc)
    dt_in = dt_ref[...] + dtb_ref[...]
    dt = jnp.maximum(dt_in, 0.0) + jnp.log1p(jnp.exp(-jnp.abs(dt_in)))
    a_heads = -jnp.exp(alog_ref[...])
    nbm = SSM_GROUPS * SSM_STATE
    for c in range(nchunk):
        rs = slice(c * lc, (c + 1) * lc)
        y = _ssd_chunk(xs_act[rs], bc_act[rs, 0:nbm], bc_act[rs, nbm:2 * nbm], dt[rs],
                       z_ref[rs, :], s2d, a_heads, dskip_ref[...], gn_ref[...],
                       e3_ref[...], es3_ref[...], t3_ref[...], ones_ref[...], lc)
        y_ref[rs, :] = y.astype(y_ref.dtype)


def _ssd_constants(lc):
    seg_w = SSM_HEADS * lc
    head_of_lane = np.arange(D_SSM) // SSM_HEAD_DIM
    e = (np.arange(LANES)[:, None] == head_of_lane[None, :]).astype(np.float32)
    es = (np.arange(LANES)[:, None] == (np.arange(seg_w) // lc)[None, :]).astype(np.float32)
    tri = np.tril(np.ones((lc, lc), np.float32))
    return (jnp.asarray(np.concatenate([e] * 3, axis=0), BF16),
            jnp.asarray(np.concatenate([es] * 3, axis=0), BF16),
            jnp.asarray(np.concatenate([tri] * 3, axis=1), BF16),
            jnp.ones((2 * SUBLANES, SSM_STATE), BF16))


def _ssd(xs, bc, dt, z, conv_w, conv_b, dt_bias, a_log, d_skip, gn, lc, nchunk,
         conv_prev=None, h0=None):
    m = xs.shape[0]
    rows = lc * nchunk
    has_state = h0 is not None
    e3, es3, t3, ones_k = _ssd_constants(lc)
    dtb = jnp.zeros((1, LANES), F32).at[0, :SSM_HEADS].set(dt_bias)
    alog = jnp.zeros((1, LANES), F32).at[0, :SSM_HEADS].set(a_log)
    dskip_x = jnp.repeat(d_skip, SSM_HEAD_DIM)[None, :]

    def const(shape):
        return pl.BlockSpec(shape, lambda i: (0,) * len(shape))

    def rowblk(width):
        return pl.BlockSpec((rows, width), lambda i: (i, 0))

    in_specs = [rowblk(D_SSM), rowblk(BC_DIM), rowblk(LANES), rowblk(D_SSM)]
    args = [xs, bc, dt, z]
    if has_state:
        nb = h0.shape[0]
        in_specs += [pl.BlockSpec((1, SUBLANES, CONV_DIM), lambda i: (i, 0, 0)),
                     pl.BlockSpec((1, D_SSM, SSM_STATE), lambda i: (i, 0, 0))]
        args += [conv_prev, h0]
        s_shape = jax.ShapeDtypeStruct((nb, D_SSM, SSM_STATE), F32)
        s_spec = pl.BlockSpec((1, D_SSM, SSM_STATE), lambda i: (i, 0, 0))
    else:
        s_shape = jax.ShapeDtypeStruct((D_SSM, SSM_STATE), F32)
        s_spec = const((D_SSM, SSM_STATE))
    in_specs += [const((SSM_CONV, CONV_DIM)), const((1, CONV_DIM)), const((1, LANES)),
                 const((1, LANES)), const((1, D_SSM)), const((1, D_SSM)),
                 const(e3.shape), const(es3.shape), const(t3.shape), const(ones_k.shape)]
    args += [conv_w, conv_b[None, :], dtb, alog, dskip_x, gn[None, :], e3, es3, t3, ones_k]
    return pl.pallas_call(
        functools.partial(_ssd_kernel, lc=lc, nchunk=nchunk, has_state=has_state),
        grid=(m // rows,),
        in_specs=in_specs,
        out_specs=(rowblk(D_SSM), s_spec),
        out_shape=(jax.ShapeDtypeStruct((m, D_SSM), BF16), s_shape),
        scratch_shapes=[pltpu.VMEM((rows + SUBLANES, D_SSM), F32),
                        pltpu.VMEM((rows + SUBLANES, BC_DIM), F32)],
        compiler_params=_cparams(("arbitrary",)),
        name="ssd_sample" if has_state else "ssd_prompt",
    )(*args)


def _out_proj_kernel(o_ref, y_ref, x_ref, w_ref, g_ref, x2_ref, hf_ref, w16_ref=None):
    if w16_ref is not None:
        w16_ref[...] = w_ref[...].astype(BF16)
        w_ref = w16_ref
    x2 = (x_ref[...]
          + jnp.dot(o_ref[...], w_ref[0:D_ATT, :], preferred_element_type=F32)
          + jnp.dot(y_ref[...], w_ref[D_ATT:D_ATT + D_SSM, :], preferred_element_type=F32))
    x2_ref[...] = x2
    hf_ref[...] = _rms(x2, g_ref[...]).astype(hf_ref.dtype)


def _out_proj(o, y, x, w, g, tm):
    m = x.shape[0]
    w_spec = pl.BlockSpec((D_ATT + D_SSM, D_MODEL), lambda i: (0, 0), pipeline_mode=pl.Buffered(1))
    out_specs = (pl.BlockSpec((tm, D_MODEL), lambda i: (i, 0)),
                 pl.BlockSpec((tm, D_MODEL), lambda i: (i, 0)))
    out_shape = (jax.ShapeDtypeStruct((m, D_MODEL), F32),
                 jax.ShapeDtypeStruct((m, D_MODEL), BF16))
    if w.dtype != BF16:
        assert m == tm, "the bf16 weight copy is written once: needs a single row tile"
        out_specs += (pl.BlockSpec((D_ATT + D_SSM, D_MODEL), lambda i: (0, 0)),)
        out_shape += (jax.ShapeDtypeStruct(w.shape, BF16),)
    return pl.pallas_call(
        _out_proj_kernel,
        grid=(m // tm,),
        in_specs=[
            pl.BlockSpec((tm, D_ATT), lambda i: (i, 0)),
            pl.BlockSpec((tm, D_SSM), lambda i: (i, 0)),
            pl.BlockSpec((tm, D_MODEL), lambda i: (i, 0)),
            w_spec,
            pl.BlockSpec((1, D_MODEL), lambda i: (0, 0)),
        ],
        out_specs=out_specs,
        out_shape=out_shape,
        compiler_params=_cparams(("arbitrary",)),
        name="out_proj",
    )(o, y, x, w, g)


def _ffn_kernel(*refs, tm, seq, final_norm):
    if seq is None:
        (hf_ref, x2_ref, wg_ref, wu_ref, wd_ref, cw_ref, cb_ref, gfin_ref,
         out_ref, gl_ref, g_scr, act_scr, carry_scr) = refs
    else:
        (hf_ref, x2_ref, wg_ref, wu_ref, wd_ref, cw_ref, cb_ref, gfin_ref, prev_ref, sel1_ref, sel2_ref,
         out_ref, gl_ref, wg16_ref, wu16_ref, wd16_ref, g_scr, act_scr) = refs
    i = pl.program_id(0)
    f = pl.program_id(1)
    nf = pl.num_programs(1) - 1
    pad = SUBLANES

    def weight(w_ref, w16_ref):
        if seq is None:
            return w_ref[...]
        w16 = w_ref[...].astype(BF16)
        w16_ref[...] = w16
        return w16

    def gate_up():
        hf = hf_ref[...]
        gate = jnp.dot(hf, weight(wg_ref, None if seq is None else wg16_ref), preferred_element_type=F32)
        up = jnp.dot(hf, weight(wu_ref, None if seq is None else wu16_ref), preferred_element_type=F32)
        g_scr[pad:pad + tm, :] = gate
        if seq is None:
            g_scr[0:pad, :] = carry_scr[f]
            carry_scr[f] = g_scr[tm:tm + pad, :]
            gl_ref[...] = g_scr[tm:tm + pad, :]
            g1 = g_scr[pad - 1:pad - 1 + tm, :]
            g2 = g_scr[pad - 2:pad - 2 + tm, :]
        else:
            g_scr[0:pad, :] = jnp.zeros((pad, gate.shape[1]), F32)
            gl_ref[...] = gate
            prev3 = jnp.concatenate(_split3(prev_ref[...]), axis=0)
            ov1 = jnp.dot(sel1_ref[...], prev3, preferred_element_type=F32)
            ov2 = jnp.dot(sel2_ref[...], prev3, preferred_element_type=F32)
            pos = lax.broadcasted_iota(jnp.int32, gate.shape, 0) % seq
            g1 = jnp.where(pos == 0, ov1, g_scr[pad - 1:pad - 1 + tm, :])
            g2 = jnp.where(pos < 2, ov2, g_scr[pad - 2:pad - 2 + tm, :])
        conv = cb_ref[...] + g2 * cw_ref[0:1, :] + g1 * cw_ref[1:2, :] + gate * cw_ref[2:3, :]
        act_scr[...] = (_silu(conv) * up).astype(BF16)

    def down():
        return jnp.dot(act_scr[...], weight(wd_ref, None if seq is None else wd16_ref),
                       preferred_element_type=F32)

    @pl.when(f == 0)
    def _():
        if seq is None:
            @pl.when(i == 0)
            def _():
                carry_scr[...] = jnp.zeros(carry_scr.shape, F32)
        out_ref[...] = x2_ref[...]
        gate_up()

    @pl.when((f > 0) & (f < nf))
    def _():
        contrib = down()
        gate_up()
        out_ref[...] += contrib

    @pl.when(f == nf)
    def _():
        res = out_ref[...] + down()
        out_ref[...] = _rms(res, gfin_ref[...]) if final_norm else res


def _ffn_prev_selectors(m, seq, nprev):
    sel1 = np.zeros((m, 3 * nprev), np.float32)
    sel2 = np.zeros((m, 3 * nprev), np.float32)
    nstate = FFN_CONV - 1
    for b in range(m // seq):
        for piece in range(3):
            base = piece * nprev + b * nstate
            sel1[b * seq, base + 1] = 1.0
            sel2[b * seq, base + 0] = 1.0
            sel2[b * seq + 1, base + 1] = 1.0
    return jnp.asarray(sel1, BF16), jnp.asarray(sel2, BF16)


def _ffn(hf, x2, wg, wu, wd, cw, cb, gfin, tm, tf, final_norm, seq=None, prev=None):
    m = hf.shape[0]
    nf = D_FF // tf
    cur = lambda f: jnp.minimum(f, nf - 1)
    last = lambda f: jnp.maximum(f - 1, 0)
    in_specs = [
        pl.BlockSpec((tm, D_MODEL), lambda i, f: (i, 0)),
        pl.BlockSpec((tm, D_MODEL), lambda i, f: (i, 0), pipeline_mode=pl.Buffered(1)),
        pl.BlockSpec((D_MODEL, tf), lambda i, f: (0, cur(f))),
        pl.BlockSpec((D_MODEL, tf), lambda i, f: (0, cur(f))),
        pl.BlockSpec((tf, D_MODEL), lambda i, f: (last(f), 0)),
        pl.BlockSpec((FFN_CONV, tf), lambda i, f: (0, cur(f))),
        pl.BlockSpec((1, tf), lambda i, f: (0, cur(f))),
        pl.BlockSpec((1, D_MODEL), lambda i, f: (0, 0)),
    ]
    args = [hf, x2, wg, wu, wd, cw, cb[None, :], gfin[None, :]]
    scratch = [pltpu.VMEM((tm + SUBLANES, tf), F32), pltpu.VMEM((tm, tf), BF16)]
    if seq is None:
        gl_shape = jax.ShapeDtypeStruct((m // tm * SUBLANES, D_FF), F32)
        gl_spec = pl.BlockSpec((SUBLANES, tf), lambda i, f: (i, cur(f)))
        scratch.append(pltpu.VMEM((nf, SUBLANES, tf), F32))
    else:
        assert m == tm, "sample FFN handles all sequences in one row tile"
        nprev = prev.shape[0]
        sel1, sel2 = _ffn_prev_selectors(m, seq, nprev)
        in_specs += [pl.BlockSpec((nprev, tf), lambda i, f: (0, cur(f))),
                     pl.BlockSpec(sel1.shape, lambda i, f: (0, 0)),
                     pl.BlockSpec(sel2.shape, lambda i, f: (0, 0))]
        args += [prev, sel1, sel2]
        gl_shape = jax.ShapeDtypeStruct((m, D_FF), F32)
        gl_spec = pl.BlockSpec((tm, tf), lambda i, f: (i, cur(f)))
    out_specs = [pl.BlockSpec((tm, D_MODEL), lambda i, f: (i, 0)), gl_spec]
    out_shape = [jax.ShapeDtypeStruct((m, D_MODEL), F32), gl_shape]
    if seq is not None:
        out_specs += [in_specs[2], in_specs[3], in_specs[4]]
        out_shape += [jax.ShapeDtypeStruct(wg.shape, BF16), jax.ShapeDtypeStruct(wu.shape, BF16),
                      jax.ShapeDtypeStruct(wd.shape, BF16)]
    return pl.pallas_call(
        functools.partial(_ffn_kernel, tm=tm, seq=seq, final_norm=final_norm),
        grid=(m // tm, nf + 1),
        in_specs=in_specs,
        out_specs=tuple(out_specs),
        out_shape=tuple(out_shape),
        scratch_shapes=scratch,
        compiler_params=_cparams(("arbitrary", "arbitrary")),
        name="ffn_sample" if seq is not None else "ffn_prompt",
    )(*args)


def _layer(x, w, lam_init, final_norm, gfin, *, batch, seq, state=None):
    m = x.shape[0]
    lamv = jnp.stack([w["lambda_q1"], w["lambda_k1"], w["lambda_q2"], w["lambda_k2"]])
    tm = min(512, m)
    w16 = {}
    q, k, v, z, xs, bc, dt, *extra = _in_proj(x, w["norm_mix_g"][None, :], w["w_in_main"], w["w_in_dt"],
                                              tm=min(1024, m))
    if extra:
        w16["w_in_main"] = extra[0]

    if state is None:
        slopes = jnp.broadcast_to(
            jnp.asarray(2.0 ** (-8.0 * np.arange(1, ATT_HEADS + 1) / ATT_HEADS), F32)[:, None, None],
            (ATT_HEADS, 1, LANES))
        o = _attn_prompt(q, k, v, slopes, lamv, w["attn_subln_g"][None, :], lam_init, tile=256)
        y, s_new = _ssd(xs, bc, dt, z, w["conv_w"], w["conv_b"], w["dt_bias"], w["a_log"],
                        w["d_skip"], w["ssm_norm_g"], lc=CHUNK, nchunk=4)
        conv_new = jnp.concatenate([xs[m - (SSM_CONV - 1):], bc[m - (SSM_CONV - 1):]], axis=-1)[None]
        s_new = s_new[None]
    else:
        cache_k, cache_v, conv_prev, ssm_prev, ffn_prev = state
        past = cache_k.shape[1]
        o = _attn_sample(q, k, v, cache_k.reshape(batch, past * ATT_HEADS, V_DIM),
                         cache_v.reshape(batch, past * ATT_HEADS, V_DIM),
                         lamv, w["attn_subln_g"][None, :], lam_init, seq)
        prev8 = jnp.pad(conv_prev, ((0, 0), (SUBLANES - (SSM_CONV - 1), 0), (0, 0)))
        y, s_new = _ssd(xs, bc, dt, z, w["conv_w"], w["conv_b"], w["dt_bias"], w["a_log"],
                        w["d_skip"], w["ssm_norm_g"], lc=seq, nchunk=1, conv_prev=prev8,
                        h0=ssm_prev.reshape(batch, D_SSM, SSM_STATE))
        conv_new = jnp.concatenate([xs.reshape(batch, seq, D_SSM)[:, seq - (SSM_CONV - 1):],
                                    bc.reshape(batch, seq, BC_DIM)[:, seq - (SSM_CONV - 1):]], axis=-1)

    x2, hf, *extra = _out_proj(o, y, x, w["w_out"], w["norm_ffn_g"][None, :], tm=tm)
    if extra:
        w16["w_out"] = extra[0]

    ffn_args = (hf, x2, w["w_gate"], w["w_up"], w["w_down"], w["ffn_conv_w"], w["ffn_conv_b"], gfin)
    if state is None:
        x3, gl = _ffn(*ffn_args, tm=min(1024, m), tf=512, final_norm=final_norm)
        ffn_new = gl[None, gl.shape[0] - (FFN_CONV - 1):]
    else:
        x3, gl, w16["w_gate"], w16["w_up"], w16["w_down"] = _ffn(
            *ffn_args, tm=tm, tf=256, final_norm=final_norm, seq=seq,
            prev=ffn_prev.reshape(batch * (FFN_CONV - 1), D_FF))
        ffn_new = gl.reshape(batch, seq, D_FF)[:, seq - (FFN_CONV - 1):]
    k_new = k.reshape(batch, seq, ATT_HEADS, 2 * QK_DIM)
    v_new = v.reshape(batch, seq, ATT_HEADS, V_DIM)
    s_new = s_new.reshape(batch, SSM_HEADS, SSM_HEAD_DIM, SSM_STATE)
    return (x3, k_new, v_new, conv_new, s_new, ffn_new), w16


def kernel(x_prompt, x_sample, cache_k, cache_v, state_ssm_conv, state_ssm, state_ffn_conv, norm_mix_g, w_in, lambda_q1, lambda_k1, lambda_q2, lambda_k2, attn_subln_g, conv_w, conv_b, dt_bias, a_log, d_skip, ssm_norm_g, w_out, norm_ffn_g, w_gate, w_up, ffn_conv_w, ffn_conv_b, w_down, norm_final_g):
    depth = w_in.shape[0]
    pb, pl_len, _ = x_prompt.shape
    sb, sl_len, _ = x_sample.shape
    assert pb == 1 and pl_len % CHUNK == 0
    xp = x_prompt.reshape(pb * pl_len, D_MODEL)
    xs = x_sample.reshape(sb * sl_len, D_MODEL)
    n_main = 2 * D_ATT + D_ATT + D_SSM + CONV_DIM
    outs_p, outs_s = [], []
    for layer in range(depth):
        lam_init = 0.8 - 0.6 * math.exp(-0.3 * layer)
        w_in_t = jnp.swapaxes(w_in[layer], 0, 1)
        w = dict(
            norm_mix_g=norm_mix_g[layer],
            w_in_main=w_in_t,
            w_in_dt=jnp.pad(w_in_t[n_main:].astype(BF16), ((0, LANES - SSM_HEADS), (0, 0))),
            lambda_q1=lambda_q1[layer], lambda_k1=lambda_k1[layer],
            lambda_q2=lambda_q2[layer], lambda_k2=lambda_k2[layer],
            attn_subln_g=attn_subln_g[layer], conv_w=conv_w[layer], conv_b=conv_b[layer],
            dt_bias=dt_bias[layer], a_log=a_log[layer], d_skip=d_skip[layer],
            ssm_norm_g=ssm_norm_g[layer], w_out=w_out[layer],
            norm_ffn_g=norm_ffn_g[layer], w_gate=w_gate[layer],
            w_up=w_up[layer], ffn_conv_w=ffn_conv_w[layer],
            ffn_conv_b=ffn_conv_b[layer], w_down=w_down[layer],
        )
        last = layer == depth - 1
        (xs, *new_s), w16 = _layer(xs, w, lam_init, last, norm_final_g, batch=sb, seq=sl_len,
                                   state=(cache_k[layer], cache_v[layer], state_ssm_conv[layer],
                                          state_ssm[layer], state_ffn_conv[layer]))
        (xp, *new_p), _ = _layer(xp, {**w, **w16}, lam_init, last, norm_final_g, batch=pb, seq=pl_len)
        outs_p.append(new_p)
        outs_s.append(new_s)
    stack = lambda outs, idx: jnp.stack([o[idx] for o in outs])
    return (xp.reshape(pb, pl_len, D_MODEL), xs.reshape(sb, sl_len, D_MODEL),
            *[stack(outs_p, idx) for idx in range(5)],
            *[stack(outs_s, idx) for idx in range(5)])
```

```python
import functools
import math

import jax
import jax.numpy as jnp
import numpy as np
from jax import lax
from jax.experimental import pallas as pl
from jax.experimental.pallas import tpu as pltpu

F32 = jnp.float32
BF16 = jnp.bfloat16

D_MODEL = 2048
CHUNK = 64
ATT_HEADS = 8
QK_DIM = 64
V_DIM = 128
D_ATT = ATT_HEADS * V_DIM
SSM_HEADS = 16
SSM_HEAD_DIM = 64
D_SSM = SSM_HEADS * SSM_HEAD_DIM
SSM_GROUPS = 2
SSM_STATE = 128
SSM_CONV = 4
BC_DIM = 2 * SSM_GROUPS * SSM_STATE
CONV_DIM = D_SSM + BC_DIM
D_FF = 5632
FFN_CONV = 3
EPS = 1e-6
LOG2E = math.log2(math.e)
Q_SCALE = QK_DIM ** -0.5 * LOG2E
LANES = 128
SUBLANES = 8
VMEM_LIMIT = 56 * 1024 * 1024

_NT = (((1,), (1,)), ((), ()))
_TN = (((0,), (0,)), ((), ()))


def _cparams(sem):
    return pltpu.CompilerParams(dimension_semantics=sem, vmem_limit_bytes=VMEM_LIMIT)


def _silu(x):
    return x * jax.nn.sigmoid(x)


def _rms(x, g):
    return x * lax.rsqrt(jnp.mean(x * x, axis=-1, keepdims=True) + EPS) * g


_PROJ_TN = 512
_PROJ_SEGS = ((0, 2), (2, 4), (4, 6), (6, 8), (8, 10), (10, 11))


def _in_proj_kernel(x_ref, g_ref, w_ref, wdt_ref,
                    q_ref, k_ref, v_ref, z_ref, xs_ref, bc_ref, dt_ref, *rest):
    w16_ref, h_scr = rest if len(rest) == 2 else (None, rest[0])
    j = pl.program_id(1)

    @pl.when(j == 0)
    def _():
        hb = _rms(x_ref[...], g_ref[...]).astype(BF16)
        h_scr[...] = hb
        dt_ref[...] = lax.dot_general(hb, wdt_ref[...], _NT, preferred_element_type=F32)

    outs = (q_ref, k_ref, v_ref, z_ref, xs_ref, bc_ref)
    for (lo, hi), ref in zip(_PROJ_SEGS, outs):
        @pl.when((j >= lo) & (j < hi))
        def _(ref=ref):
            w = w_ref[...]
            if w16_ref is not None:
                w = w.astype(BF16)
                w16_ref[...] = w
            res = lax.dot_general(h_scr[...], w, _NT, preferred_element_type=F32)
            val = res * Q_SCALE if ref is q_ref else res
            ref[...] = val.astype(ref.dtype)


def _in_proj(x, g, w_main, w_dt, tm):
    m = x.shape[0]
    tn = _PROJ_TN
    nj = w_main.shape[0] // tn

    def seg_spec(lo, hi):
        return pl.BlockSpec((tm, tn), lambda i, j: (i, jnp.clip(j - lo, 0, hi - lo - 1)))

    out_shape = (
        jax.ShapeDtypeStruct((m, D_ATT), BF16),
        jax.ShapeDtypeStruct((m, D_ATT), F32),
        jax.ShapeDtypeStruct((m, D_ATT), F32),
        jax.ShapeDtypeStruct((m, D_SSM), F32),
        jax.ShapeDtypeStruct((m, D_SSM), F32),
        jax.ShapeDtypeStruct((m, BC_DIM), F32),
        jax.ShapeDtypeStruct((m, LANES), F32),
    )
    out_specs = tuple(seg_spec(lo, hi) for lo, hi in _PROJ_SEGS) + (
        pl.BlockSpec((tm, LANES), lambda i, j: (i, 0)),)
    if w_main.dtype != BF16:
        assert m == tm, "the bf16 weight copy is written once per tile: needs a single row tile"
        out_shape += (jax.ShapeDtypeStruct((nj * tn, D_MODEL), BF16),)
        out_specs += (pl.BlockSpec((tn, D_MODEL), lambda i, j: (j, 0)),)
    return pl.pallas_call(
        _in_proj_kernel,
        grid=(m // tm, nj),
        in_specs=[
            pl.BlockSpec((tm, D_MODEL), lambda i, j: (i, 0)),
            pl.BlockSpec((1, D_MODEL), lambda i, j: (0, 0)),
            pl.BlockSpec((tn, D_MODEL), lambda i, j: (j, 0)),
            pl.BlockSpec((LANES, D_MODEL), lambda i, j: (0, 0)),
        ],
        out_specs=out_specs,
        out_shape=out_shape,
        scratch_shapes=[pltpu.VMEM((tm, D_MODEL), BF16)],
        compiler_params=_cparams(("arbitrary", "arbitrary")),
        name="in_proj",
    )(x, g, w_main, w_dt)


def _lambda_value(lamv_ref, lam_init):
    lv = lamv_ref[...]
    s1 = jnp.sum(lv[0:1] * lv[1:2], axis=-1, keepdims=True)
    s2 = jnp.sum(lv[2:3] * lv[3:4], axis=-1, keepdims=True)
    return jnp.exp(s1) - jnp.exp(s2) + lam_init


def _split_maps(q):
    lane = lax.broadcasted_iota(jnp.int32, q.shape, 1)
    zero = jnp.zeros_like(q)
    return jnp.where(lane < QK_DIM, q, zero), jnp.where(lane >= QK_DIM, q, zero)


_VT_CHUNK = 512
_VT_PAD = 16
_UNROLL_SHIFT = 2
_UNROLL = 1 << _UNROLL_SHIFT
_HEADS_PER_STEP = 2


def _attn_prompt_kernel(q_ref, k_ref, v_ref, slope_ref, lamv_ref, g_ref, dmask_ref, dneg_ref, o_ref,
                        k_scr, vt_scr, acc_scr, t0_scr, t1_scr, p0_scr, p1_scr, dbias_scr, stat_scr,
                        *, tile, lam_init):
    length = k_ref.shape[0]
    width = 2 * tile
    nheads = k_scr.shape[0]
    lam = _lambda_value(lamv_ref, lam_init)

    lane = lax.broadcasted_iota(jnp.int32, (length, LANES), 1)
    koff = lax.broadcasted_iota(jnp.int32, (length, LANES), 0) % tile
    koff = jnp.where(lane < 3, koff, 0).astype(F32).astype(BF16)
    extra = (lax.broadcasted_iota(jnp.int32, (_VT_PAD, tile), 0) == 0).astype(F32).astype(BF16)
    brow = lax.broadcasted_iota(jnp.int32, (LANES, width), 0)
    per = _VT_CHUNK // tile
    ws, q_biases = [], []
    for hh in range(nheads):
        cols = slice(hh * V_DIM, (hh + 1) * V_DIM)
        k_scr[hh, :, 0:V_DIM] = k_ref[:, cols].astype(BF16)
        k_scr[hh, :, V_DIM:V_DIM + LANES] = koff
        for c in range(length // _VT_CHUNK):
            vt = v_ref[c * _VT_CHUNK:(c + 1) * _VT_CHUNK, cols].T.astype(BF16)
            for s in range(per):
                vt_scr[hh, c * per + s, 0:V_DIM, :] = vt[:, s * tile:(s + 1) * tile]
                vt_scr[hh, c * per + s, V_DIM:V_DIM + _VT_PAD, :] = extra
        w = slope_ref[hh][:, :1] * LOG2E
        w_hi, w_mid, w_lo = (piece.astype(F32) for piece in _split3(w))
        q_bias = jnp.where(brow == 0, w_hi, jnp.where(brow == 1, w_mid, jnp.where(brow == 2, w_lo, 0.0)))
        ws.append(w)
        q_biases.append(q_bias.astype(BF16))
        dbias_scr[hh] = w * dmask_ref[...] + dneg_ref[...]

    def q_pair(qp, _):
        _attn_prompt_pair(qp, ws, lam, q_biases, q_ref, g_ref, o_ref,
                          k_scr, vt_scr, acc_scr, (t0_scr, t1_scr), (p0_scr, p1_scr),
                          dbias_scr, stat_scr, tile=tile, lam_init=lam_init)
        return 0

    lax.fori_loop(0, length // (2 * tile), q_pair, 0)


def _attn_prompt_pair(qp, ws, lam, q_biases, q_ref, g_ref, o_ref,
                      k_scr, vt_scr, acc_scr, t_slots, p_slots, dbias_scr, stat_scr, *, tile, lam_init):
    width = 2 * tile
    nheads = len(ws)
    streams = [(hh, u) for hh in range(nheads) for u in range(2)]
    nfull = 2 * qp
    qis = [nfull + u for _, u in streams]
    rows = [pl.ds(pl.multiple_of(qi * tile, tile), tile) for qi in qis]
    q_augs = []
    for s, (hh, _) in enumerate(streams):
        q1, q2 = _split_maps(q_ref[rows[s], hh * V_DIM:(hh + 1) * V_DIM])
        q_t = jnp.concatenate([q1, q2], axis=0).astype(F32).T.astype(BF16)
        q_augs.append(jnp.concatenate([q_t, q_biases[hh]], axis=0))

    def scores(s, j):
        k0 = pl.multiple_of(jnp.minimum(j, qis[s]) * tile, tile)
        return jnp.dot(k_scr[streams[s][0], pl.ds(k0, tile), :], q_augs[s], preferred_element_type=F32)

    ALPHA, TMAX = 0, 1

    def step(s, j, slot, m_old):
        hh = streams[s][0]
        pv = jnp.dot(vt_scr[hh, jnp.maximum(j - 1, 0)], p_slots[1 - slot][s],
                     preferred_element_type=F32)
        acc_scr[s] = stat_scr[s, ALPHA:ALPHA + 1, :] * acc_scr[s] + pv
        off = -ws[hh] * ((qis[s] - j) * tile).astype(F32)
        m_new = jnp.maximum(m_old, stat_scr[s, TMAX:TMAX + 1, :] + off)
        t_next = scores(s, j + 1)
        t_slots[1 - slot][s] = t_next
        stat_scr[s, TMAX:TMAX + 1, :] = jnp.max(t_next, axis=0, keepdims=True)
        p_slots[slot][s] = jnp.exp2(t_slots[slot][s] + (off - m_new)).astype(BF16)
        stat_scr[s, ALPHA:ALPHA + 1, :] = jnp.exp2(m_old - m_new)
        return m_new

    def run(first, ntiles_per_iter, niter, carry):
        def body(i, carry):
            for k in range(ntiles_per_iter):
                carry = tuple(step(s, first + ntiles_per_iter * i + k, k % 2, carry[s])
                              for s in range(len(streams)))
            return carry
        return lax.fori_loop(0, niter, body, carry)

    carry = []
    for s in range(len(streams)):
        acc_scr[s] = jnp.zeros(acc_scr.shape[1:], F32)
        p_slots[1][s] = jnp.zeros(p_slots[1].shape[1:], BF16)
        t_first = scores(s, 0)
        t_slots[0][s] = t_first
        stat_scr[s, ALPHA:ALPHA + 1, :] = jnp.ones((1, width), F32)
        stat_scr[s, TMAX:TMAX + 1, :] = jnp.max(t_first, axis=0, keepdims=True)
        carry.append(jnp.full((1, width), -0.5 * float(jnp.finfo(F32).max), F32))
    n_main = lax.shift_right_logical(nfull, _UNROLL_SHIFT)
    done = n_main * _UNROLL
    carry = run(0, _UNROLL, n_main, tuple(carry))
    carry = run(done, 2, lax.shift_right_logical(nfull - done, 1), carry)

    for s, (hh, u) in enumerate(streams):
        m_old = carry[s]
        diag_slot = 0
        if u == 1:
            m_old = step(s, nfull, 0, m_old)
            diag_slot = 1
        alpha_last = stat_scr[s, ALPHA:ALPHA + 1, :]
        pv = jnp.dot(vt_scr[hh, jnp.maximum(qis[s] - 1, 0)], p_slots[1 - diag_slot][s],
                     preferred_element_type=F32)
        acc = alpha_last * acc_scr[s] + pv
        t = t_slots[diag_slot][s] + dbias_scr[hh]
        m_new = jnp.maximum(m_old, jnp.max(t, axis=0, keepdims=True))
        p = jnp.exp2(t - m_new).astype(BF16)
        acc = jnp.exp2(m_old - m_new) * acc + jnp.dot(vt_scr[hh, qis[s]], p, preferred_element_type=F32)

        on = acc[0:V_DIM] / acc[V_DIM:V_DIM + 1]
        o_t = on[:, :tile] - lam * on[:, tile:]
        o_t = o_t * lax.rsqrt(jnp.mean(o_t * o_t, axis=0, keepdims=True) + EPS)
        o_ref[rows[s], hh * V_DIM:(hh + 1) * V_DIM] = (
            o_t.T * g_ref[...] * (1.0 - lam_init)).astype(o_ref.dtype)


def _attn_prompt(q, k, v, slopes, lamv, g, lam_init, tile):
    length = q.shape[0]
    key = np.arange(tile)[:, None]
    qry = np.tile(np.arange(tile), 2)[None, :]
    dmask = jnp.asarray((qry - np.abs(qry - key)) - key, F32)
    dneg = jnp.asarray(np.where(key // CHUNK <= qry // CHUNK, 0.0, -np.inf), F32)
    nh = _HEADS_PER_STEP
    diag_spec = pl.BlockSpec((tile, 2 * tile), lambda h: (0, 0), pipeline_mode=pl.Buffered(1))
    head_spec = pl.BlockSpec((length, nh * V_DIM), lambda h: (0, h), pipeline_mode=pl.Buffered(1))
    kv_spec = pl.BlockSpec((length, nh * V_DIM), lambda h: (0, h), pipeline_mode=pl.Buffered(1))
    return pl.pallas_call(
        functools.partial(_attn_prompt_kernel, tile=tile, lam_init=lam_init),
        grid=(ATT_HEADS // nh,),
        in_specs=[
            kv_spec, kv_spec, kv_spec,
            pl.BlockSpec((nh, 1, LANES), lambda h: (h, 0, 0)),
            pl.BlockSpec((4, QK_DIM), lambda h: (0, 0)),
            pl.BlockSpec((1, V_DIM), lambda h: (0, 0)),
            diag_spec, diag_spec,
        ],
        out_specs=head_spec,
        out_shape=jax.ShapeDtypeStruct((length, D_ATT), BF16),
        scratch_shapes=[
            pltpu.VMEM((nh, length, V_DIM + LANES), BF16),
            pltpu.VMEM((nh, length // tile, V_DIM + _VT_PAD, tile), BF16),
            pltpu.VMEM((2 * nh, V_DIM + _VT_PAD, 2 * tile), F32),
            pltpu.VMEM((2 * nh, tile, 2 * tile), F32),
            pltpu.VMEM((2 * nh, tile, 2 * tile), F32),
            pltpu.VMEM((2 * nh, tile, 2 * tile), BF16),
            pltpu.VMEM((2 * nh, tile, 2 * tile), BF16),
            pltpu.VMEM((nh, tile, 2 * tile), F32),
            pltpu.VMEM((2 * nh, SUBLANES, 2 * tile), F32),
        ],
        compiler_params=_cparams(("arbitrary",)),
        name="attn_prompt",
    )(q, k, v, slopes, lamv, g, dmask, dneg)


def _attn_sample_kernel(q_ref, kn_ref, vn_ref, ck_ref, cv_ref, lamv_ref, g_ref, bias_c_ref, bias_n_ref,
                        o_ref, *, seq, past, lam_init):
    lam = _lambda_value(lamv_ref, lam_init)
    rows = 2 * seq
    sc, sn = [], []
    for h in range(ATT_HEADS):
        sl = slice(h * V_DIM, (h + 1) * V_DIM)
        q1, q2 = _split_maps(q_ref[:, sl])
        qq = jnp.concatenate([q1, q2], axis=0)
        kc = ck_ref[0, pl.ds(h, past, stride=ATT_HEADS), :].astype(BF16)
        sc.append(lax.dot_general(qq, kc, _NT, preferred_element_type=F32))
        sn.append(lax.dot_general(qq, kn_ref[:, sl].astype(BF16), _NT, preferred_element_type=F32))
    sc = jnp.concatenate(sc, axis=0) - bias_c_ref[...]
    sn = jnp.concatenate(sn, axis=0) - bias_n_ref[...]
    m = jnp.maximum(jnp.max(sc, axis=-1, keepdims=True), jnp.max(sn, axis=-1, keepdims=True))
    pc = jnp.exp2(sc - m)
    pn = jnp.exp2(sn - m)
    inv_l = 1.0 / (jnp.sum(pc, axis=-1, keepdims=True) + jnp.sum(pn, axis=-1, keepdims=True))
    pc = pc.astype(BF16)
    pn = pn.astype(BF16)
    for h in range(ATT_HEADS):
        sl = slice(h * V_DIM, (h + 1) * V_DIM)
        hr = slice(h * rows, (h + 1) * rows)
        vc = cv_ref[0, pl.ds(h, past, stride=ATT_HEADS), :].astype(BF16)
        acc = (jnp.dot(pc[hr], vc, preferred_element_type=F32)
               + jnp.dot(pn[hr], vn_ref[:, sl].astype(BF16), preferred_element_type=F32))
        on = acc * inv_l[hr]
        o = on[:seq] - lam * on[seq:]
        o_ref[:, sl] = (_rms(o, g_ref[...]) * (1.0 - lam_init)).astype(o_ref.dtype)


def _attn_sample(q, k_new, v_new, cache_k, cache_v, lamv, g, lam_init, seq):
    nb, past = cache_k.shape[0], cache_k.shape[1] // ATT_HEADS
    slope = np.repeat(2.0 ** (-8.0 * np.arange(1, ATT_HEADS + 1) / ATT_HEADS), 2 * seq)[:, None] * LOG2E
    qpos = np.tile(np.arange(seq), 2 * ATT_HEADS)[:, None]
    bias_c = jnp.asarray(slope * (qpos + past - np.arange(past)[None, :]), F32)
    bias_n = jnp.asarray(slope * np.abs(qpos - np.arange(seq)[None, :]), F32)
    row_spec = pl.BlockSpec((seq, D_ATT), lambda b: (b, 0))
    cache_spec = pl.BlockSpec((1, past * ATT_HEADS, V_DIM), lambda b: (b, 0, 0))

    def const(arr):
        return pl.BlockSpec(arr.shape, lambda b: (0, 0), pipeline_mode=pl.Buffered(1))

    return pl.pallas_call(
        functools.partial(_attn_sample_kernel, seq=seq, past=past, lam_init=lam_init),
        grid=(nb,),
        in_specs=[row_spec, row_spec, row_spec, cache_spec, cache_spec,
                  pl.BlockSpec((4, QK_DIM), lambda b: (0, 0)),
                  pl.BlockSpec((1, V_DIM), lambda b: (0, 0)),
                  const(bias_c), const(bias_n)],
        out_specs=row_spec,
        out_shape=jax.ShapeDtypeStruct((nb * seq, D_ATT), BF16),
        compiler_params=_cparams(("arbitrary",)),
        name="attn_sample",
    )(q, k_new, v_new, cache_k, cache_v, lamv, g, bias_c, bias_n)


def _split3(x):
    hi = x.astype(BF16)
    r = x - hi.astype(F32)
    mid = r.astype(BF16)
    lo = (r - mid.astype(F32)).astype(BF16)
    return hi, mid, lo


def _ssd_chunk(xs, bm, cm, dt, z, s_ref, a_heads, dskip, gn, e3, es3, t3, ones_k, lc):
    seg_w = SSM_HEADS * lc
    half = D_SSM // SSM_GROUPS
    d3 = jnp.concatenate(_split3(dt), axis=1)
    dt_x = jnp.dot(d3, e3, preferred_element_type=F32)
    la3 = jnp.concatenate(_split3(dt * a_heads), axis=0)
    ac3 = jnp.concatenate(_split3(jnp.dot(t3, la3, preferred_element_type=F32)), axis=1)
    acol = jnp.dot(ac3, e3, preferred_element_type=F32)
    acol_s = acol if seg_w == D_SSM else jnp.dot(ac3, es3, preferred_element_type=F32)
    t_idx = lax.broadcasted_iota(jnp.int32, (lc, seg_w), 0)
    s_idx = lax.broadcasted_iota(jnp.int32, (lc, seg_w), 1) % lc
    arow = jnp.sum(jnp.where(t_idx == s_idx, acol_s, 0.0), axis=0, keepdims=True)
    decay = jnp.where(t_idx >= s_idx, jnp.exp(acol_s - arow), 0.0)

    cb16 = cm.astype(BF16)
    bb16 = bm.astype(BF16)
    hpg = SSM_HEADS // SSM_GROUPS
    cbs = []
    for g in range(SSM_GROUPS):
        gs = slice(g * SSM_STATE, (g + 1) * SSM_STATE)
        b_rep = jnp.concatenate([bb16[:, gs]] * hpg, axis=0)
        cbs.append(lax.dot_general(cb16[:, gs], b_rep, _NT, preferred_element_type=F32))
    mmat = (jnp.concatenate(cbs, axis=1) * decay).astype(BF16)

    xdt = xs * dt_x
    xdt16 = xdt.astype(BF16)
    hk = (2 * LANES) // lc
    wd = hk * SSM_HEAD_DIM
    blk = (lax.broadcasted_iota(jnp.int32, (hk * lc, wd), 0) // lc
           == lax.broadcasted_iota(jnp.int32, (hk * lc, wd), 1) // SSM_HEAD_DIM)
    parts = []
    for i in range(SSM_HEADS // hk):
        xd = xdt16[:, i * wd:(i + 1) * wd]
        bd = jnp.where(blk, jnp.concatenate([xd] * hk, axis=0), jnp.zeros((), BF16))
        parts.append(jnp.dot(mmat[:, i * hk * lc:(i + 1) * hk * lc], bd, preferred_element_type=F32))
    y_intra = jnp.concatenate(parts, axis=1) if len(parts) > 1 else parts[0]

    yi = []
    for g in range(SSM_GROUPS):
        sg = s_ref[g * half:(g + 1) * half, :].astype(BF16)
        yi.append(lax.dot_general(cb16[:, g * SSM_STATE:(g + 1) * SSM_STATE], sg, _NT,
                                  preferred_element_type=F32))
    y_inter = jnp.exp(acol) * jnp.concatenate(yi, axis=1)

    alast = acol[lc - 1:lc, :]
    dec_end = jnp.exp(alast - acol)
    xd_end = (xdt * dec_end).astype(BF16)
    krow = lax.broadcasted_iota(jnp.int32, (ones_k.shape[0], D_SSM), 0)
    a_hi, a_mid, a_lo = (piece.astype(F32) for piece in _split3(alast))
    pieces = jnp.where(krow == 0, a_hi, jnp.where(krow == 1, a_mid, jnp.where(krow == 2, a_lo, 0.0)))
    acl = lax.dot_general(pieces.astype(BF16), ones_k, _TN, preferred_element_type=F32)
    for g in range(SSM_GROUPS):
        rows = slice(g * half, (g + 1) * half)
        upd = lax.dot_general(xd_end[:, rows], bb16[:, g * SSM_STATE:(g + 1) * SSM_STATE], _TN,
                              preferred_element_type=F32)
        s_ref[rows, :] = jnp.exp(acl[rows, :]) * s_ref[rows, :] + upd

    y = (y_intra + y_inter + dskip * xs) * _silu(z)
    outs = []
    for g in range(SSM_GROUPS):
        cs = slice(g * half, (g + 1) * half)
        outs.append(_rms(y[:, cs], gn[:, cs]))
    return jnp.concatenate(outs, axis=1)


def _ssd_kernel(*refs, lc, nchunk, has_state):
    if has_state:
        (xs_ref, bc_ref, dt_ref, z_ref, prev_ref, h0_ref, cw_ref, cbias_ref, dtb_ref, alog_ref,
         dskip_ref, gn_ref, e3_ref, es3_ref, t3_ref, ones_ref,
         y_ref, s_ref, scr_x, scr_bc) = refs
    else:
        (xs_ref, bc_ref, dt_ref, z_ref, cw_ref, cbias_ref, dtb_ref, alog_ref,
         dskip_ref, gn_ref, e3_ref, es3_ref, t3_ref, ones_ref,
         y_ref, s_ref, scr_x, scr_bc) = refs
    rows = lc * nchunk
    pad = SUBLANES
    if has_state:
        s2d = s_ref.at[0]
        scr_x[0:pad, :] = prev_ref[0, :, 0:D_SSM]
        scr_bc[0:pad, :] = prev_ref[0, :, D_SSM:CONV_DIM]
        s2d[...] = h0_ref[0]
    else:
        s2d = s_ref

        @pl.when(pl.program_id(0) == 0)
        def _():
            scr_x[0:pad, :] = jnp.zeros((pad, D_SSM), F32)
            scr_bc[0:pad, :] = jnp.zeros((pad, BC_DIM), F32)
            s_ref[...] = jnp.zeros(s_ref.shape, F32)

    scr_x[pad:pad + rows, :] = xs_ref[...]
    scr_bc[pad:pad + rows, :] = bc_ref[...]
    xc = cbias_ref[:, 0:D_SSM]
    bcc = cbias_ref[:, D_SSM:CONV_DIM]
    for tap in range(SSM_CONV):
        off = pad - (SSM_CONV - 1) + tap
        xc = xc + scr_x[off:off + rows, :] * cw_ref[tap:tap + 1, 0:D_SSM]
        bcc = bcc + scr_bc[off:off + rows, :] * cw_ref[tap:tap + 1, D_SSM:CONV_DIM]
    if not has_state:
        scr_x[0:pad, :] = scr_x[rows:rows + pad, :]
        scr_bc[0:pad, :] = scr_bc[rows:rows + pad, :]
    xs_act = _silu(xc)
    bc_act = _silu(bcc)
    dt_in = dt_ref[...] + dtb_ref[...]
    dt = jnp.maximum(dt_in, 0.0) + jnp.log1p(jnp.exp(-jnp.abs(dt_in)))
    a_heads = -jnp.exp(alog_ref[...])
    nbm = SSM_GROUPS * SSM_STATE
    for c in range(nchunk):
        rs = slice(c * lc, (c + 1) * lc)
        y = _ssd_chunk(xs_act[rs], bc_act[rs, 0:nbm], bc_act[rs, nbm:2 * nbm], dt[rs],
                       z_ref[rs, :], s2d, a_heads, dskip_ref[...], gn_ref[...],
                       e3_ref[...], es3_ref[...], t3_ref[...], ones_ref[...], lc)
        y_ref[rs, :] = y.astype(y_ref.dtype)


def _ssd_constants(lc):
    seg_w = SSM_HEADS * lc
    head_of_lane = np.arange(D_SSM) // SSM_HEAD_DIM
    e = (np.arange(LANES)[:, None] == head_of_lane[None, :]).astype(np.float32)
    es = (np.arange(LANES)[:, None] == (np.arange(seg_w) // lc)[None, :]).astype(np.float32)
    tri = np.tril(np.ones((lc, lc), np.float32))
    return (jnp.asarray(np.concatenate([e] * 3, axis=0), BF16),
            jnp.asarray(np.concatenate([es] * 3, axis=0), BF16),
            jnp.asarray(np.concatenate([tri] * 3, axis=1), BF16),
            jnp.ones((2 * SUBLANES, SSM_STATE), BF16))


def _ssd(xs, bc, dt, z, conv_w, conv_b, dt_bias, a_log, d_skip, gn, lc, nchunk,
         conv_prev=None, h0=None):
    m = xs.shape[0]
    rows = lc * nchunk
    has_state = h0 is not None
    e3, es3, t3, ones_k = _ssd_constants(lc)
    dtb = jnp.zeros((1, LANES), F32).at[0, :SSM_HEADS].set(dt_bias)
    alog = jnp.zeros((1, LANES), F32).at[0, :SSM_HEADS].set(a_log)
    dskip_x = jnp.repeat(d_skip, SSM_HEAD_DIM)[None, :]

    def const(shape):
        return pl.BlockSpec(shape, lambda i: (0,) * len(shape))

    def rowblk(width):
        return pl.BlockSpec((rows, width), lambda i: (i, 0))

    in_specs = [rowblk(D_SSM), rowblk(BC_DIM), rowblk(LANES), rowblk(D_SSM)]
    args = [xs, bc, dt, z]
    if has_state:
        nb = h0.shape[0]
        in_specs += [pl.BlockSpec((1, SUBLANES, CONV_DIM), lambda i: (i, 0, 0)),
                     pl.BlockSpec((1, D_SSM, SSM_STATE), lambda i: (i, 0, 0))]
        args += [conv_prev, h0]
        s_shape = jax.ShapeDtypeStruct((nb, D_SSM, SSM_STATE), F32)
        s_spec = pl.BlockSpec((1, D_SSM, SSM_STATE), lambda i: (i, 0, 0))
    else:
        s_shape = jax.ShapeDtypeStruct((D_SSM, SSM_STATE), F32)
        s_spec = const((D_SSM, SSM_STATE))
    in_specs += [const((SSM_CONV, CONV_DIM)), const((1, CONV_DIM)), const((1, LANES)),
                 const((1, LANES)), const((1, D_SSM)), const((1, D_SSM)),
                 const(e3.shape), const(es3.shape), const(t3.shape), const(ones_k.shape)]
    args += [conv_w, conv_b[None, :], dtb, alog, dskip_x, gn[None, :], e3, es3, t3, ones_k]
    return pl.pallas_call(
        functools.partial(_ssd_kernel, lc=lc, nchunk=nchunk, has_state=has_state),
        grid=(m // rows,),
        in_specs=in_specs,
        out_specs=(rowblk(D_SSM), s_spec),
        out_shape=(jax.ShapeDtypeStruct((m, D_SSM), BF16), s_shape),
        scratch_shapes=[pltpu.VMEM((rows + SUBLANES, D_SSM), F32),
                        pltpu.VMEM((rows + SUBLANES, BC_DIM), F32)],
        compiler_params=_cparams(("arbitrary",)),
        name="ssd_sample" if has_state else "ssd_prompt",
    )(*args)


def _out_proj_kernel(o_ref, y_ref, x_ref, w_ref, g_ref, x2_ref, hf_ref, w16_ref=None):
    if w16_ref is not None:
        w16_ref[...] = w_ref[...].astype(BF16)
        w_ref = w16_ref
    x2 = (x_ref[...]
          + jnp.dot(o_ref[...], w_ref[0:D_ATT, :], preferred_element_type=F32)
          + jnp.dot(y_ref[...], w_ref[D_ATT:D_ATT + D_SSM, :], preferred_element_type=F32))
    x2_ref[...] = x2
    hf_ref[...] = _rms(x2, g_ref[...]).astype(hf_ref.dtype)


def _out_proj(o, y, x, w, g, tm):
    m = x.shape[0]
    w_spec = pl.BlockSpec((D_ATT + D_SSM, D_MODEL), lambda i: (0, 0), pipeline_mode=pl.Buffered(1))
    out_specs = (pl.BlockSpec((tm, D_MODEL), lambda i: (i, 0)),
                 pl.BlockSpec((tm, D_MODEL), lambda i: (i, 0)))
    out_shape = (jax.ShapeDtypeStruct((m, D_MODEL), F32),
                 jax.ShapeDtypeStruct((m, D_MODEL), BF16))
    if w.dtype != BF16:
        assert m == tm, "the bf16 weight copy is written once: needs a single row tile"
        out_specs += (pl.BlockSpec((D_ATT + D_SSM, D_MODEL), lambda i: (0, 0)),)
        out_shape += (jax.ShapeDtypeStruct(w.shape, BF16),)
    return pl.pallas_call(
        _out_proj_kernel,
        grid=(m // tm,),
        in_specs=[
            pl.BlockSpec((tm, D_ATT), lambda i: (i, 0)),
            pl.BlockSpec((tm, D_SSM), lambda i: (i, 0)),
            pl.BlockSpec((tm, D_MODEL), lambda i: (i, 0)),
            w_spec,
            pl.BlockSpec((1, D_MODEL), lambda i: (0, 0)),
        ],
        out_specs=out_specs,
        out_shape=out_shape,
        compiler_params=_cparams(("arbitrary",)),
        name="out_proj",
    )(o, y, x, w, g)


def _ffn_kernel(*refs, tm, seq, final_norm):
    if seq is None:
        (hf_ref, x2_ref, wg_ref, wu_ref, wd_ref, cw_ref, cb_ref, gfin_ref,
         out_ref, gl_ref, g_scr, act_scr, carry_scr) = refs
    else:
        (hf_ref, x2_ref, wg_ref, wu_ref, wd_ref, cw_ref, cb_ref, gfin_ref, prev_ref, sel1_ref, sel2_ref,
         out_ref, gl_ref, wg16_ref, wu16_ref, wd16_ref, g_scr, act_scr) = refs
    i = pl.program_id(0)
    f = pl.program_id(1)
    nf = pl.num_programs(1) - 1
    pad = SUBLANES

    def weight(w_ref, w16_ref):
        if seq is None:
            return w_ref[...]
        w16 = w_ref[...].astype(BF16)
        w16_ref[...] = w16
        return w16

    def gate_up():
        hf = hf_ref[...]
        gate = jnp.dot(hf, weight(wg_ref, None if seq is None else wg16_ref), preferred_element_type=F32)
        up = jnp.dot(hf, weight(wu_ref, None if seq is None else wu16_ref), preferred_element_type=F32)
        g_scr[pad:pad + tm, :] = gate
        if seq is None:
            g_scr[0:pad, :] = carry_scr[f]
            carry_scr[f] = g_scr[tm:tm + pad, :]
            gl_ref[...] = g_scr[tm:tm + pad, :]
            g1 = g_scr[pad - 1:pad - 1 + tm, :]
            g2 = g_scr[pad - 2:pad - 2 + tm, :]
        else:
            g_scr[0:pad, :] = jnp.zeros((pad, gate.shape[1]), F32)
            gl_ref[...] = gate
            prev3 = jnp.concatenate(_split3(prev_ref[...]), axis=0)
            ov1 = jnp.dot(sel1_ref[...], prev3, preferred_element_type=F32)
            ov2 = jnp.dot(sel2_ref[...], prev3, preferred_element_type=F32)
            pos = lax.broadcasted_iota(jnp.int32, gate.shape, 0) % seq
            g1 = jnp.where(pos == 0, ov1, g_scr[pad - 1:pad - 1 + tm, :])
            g2 = jnp.where(pos < 2, ov2, g_scr[pad - 2:pad - 2 + tm, :])
        conv = cb_ref[...] + g2 * cw_ref[0:1, :] + g1 * cw_ref[1:2, :] + gate * cw_ref[2:3, :]
        act_scr[...] = (_silu(conv) * up).astype(BF16)

    def down():
        return jnp.dot(act_scr[...], weight(wd_ref, None if seq is None else wd16_ref),
                       preferred_element_type=F32)

    @pl.when(f == 0)
    def _():
        if seq is None:
            @pl.when(i == 0)
            def _():
                carry_scr[...] = jnp.zeros(carry_scr.shape, F32)
        out_ref[...] = x2_ref[...]
        gate_up()

    @pl.when((f > 0) & (f < nf))
    def _():
        contrib = down()
        gate_up()
        out_ref[...] += contrib

    @pl.when(f == nf)
    def _():
        res = out_ref[...] + down()
        out_ref[...] = _rms(res, gfin_ref[...]) if final_norm else res


def _ffn_prev_selectors(m, seq, nprev):
    sel1 = np.zeros((m, 3 * nprev), np.float32)
    sel2 = np.zeros((m, 3 * nprev), np.float32)
    nstate = FFN_CONV - 1
    for b in range(m // seq):
        for piece in range(3):
            base = piece * nprev + b * nstate
            sel1[b * seq, base + 1] = 1.0
            sel2[b * seq, base + 0] = 1.0
            sel2[b * seq + 1, base + 1] = 1.0
    return jnp.asarray(sel1, BF16), jnp.asarray(sel2, BF16)


def _ffn(hf, x2, wg, wu, wd, cw, cb, gfin, tm, tf, final_norm, seq=None, prev=None):
    m = hf.shape[0]
    nf = D_FF // tf
    cur = lambda f: jnp.minimum(f, nf - 1)
    last = lambda f: jnp.maximum(f - 1, 0)
    in_specs = [
        pl.BlockSpec((tm, D_MODEL), lambda i, f: (i, 0)),
        pl.BlockSpec((tm, D_MODEL), lambda i, f: (i, 0), pipeline_mode=pl.Buffered(1)),
        pl.BlockSpec((D_MODEL, tf), lambda i, f: (0, cur(f))),
        pl.BlockSpec((D_MODEL, tf), lambda i, f: (0, cur(f))),
        pl.BlockSpec((tf, D_MODEL), lambda i, f: (last(f), 0)),
        pl.BlockSpec((FFN_CONV, tf), lambda i, f: (0, cur(f))),
        pl.BlockSpec((1, tf), lambda i, f: (0, cur(f))),
        pl.BlockSpec((1, D_MODEL), lambda i, f: (0, 0)),
    ]
    args = [hf, x2, wg, wu, wd, cw, cb[None, :], gfin[None, :]]
    scratch = [pltpu.VMEM((tm + SUBLANES, tf), F32), pltpu.VMEM((tm, tf), BF16)]
    if seq is None:
        gl_shape = jax.ShapeDtypeStruct((m // tm * SUBLANES, D_FF), F32)
        gl_spec = pl.BlockSpec((SUBLANES, tf), lambda i, f: (i, cur(f)))
        scratch.append(pltpu.VMEM((nf, SUBLANES, tf), F32))
    else:
        assert m == tm, "sample FFN handles all sequences in one row tile"
        nprev = prev.shape[0]
        sel1, sel2 = _ffn_prev_selectors(m, seq, nprev)
        in_specs += [pl.BlockSpec((nprev, tf), lambda i, f: (0, cur(f))),
                     pl.BlockSpec(sel1.shape, lambda i, f: (0, 0)),
                     pl.BlockSpec(sel2.shape, lambda i, f: (0, 0))]
        args += [prev, sel1, sel2]
        gl_shape = jax.ShapeDtypeStruct((m, D_FF), F32)
        gl_spec = pl.BlockSpec((tm, tf), lambda i, f: (i, cur(f)))
    out_specs = [pl.BlockSpec((tm, D_MODEL), lambda i, f: (i, 0)), gl_spec]
    out_shape = [jax.ShapeDtypeStruct((m, D_MODEL), F32), gl_shape]
    if seq is not None:
        out_specs += [in_specs[2], in_specs[3], in_specs[4]]
        out_shape += [jax.ShapeDtypeStruct(wg.shape, BF16), jax.ShapeDtypeStruct(wu.shape, BF16),
                      jax.ShapeDtypeStruct(wd.shape, BF16)]
    return pl.pallas_call(
        functools.partial(_ffn_kernel, tm=tm, seq=seq, final_norm=final_norm),
        grid=(m // tm, nf + 1),
        in_specs=in_specs,
        out_specs=tuple(out_specs),
        out_shape=tuple(out_shape),
        scratch_shapes=scratch,
        compiler_params=_cparams(("arbitrary", "arbitrary")),
        name="ffn_sample" if seq is not None else "ffn_prompt",
    )(*args)


def _layer(x, w, lam_init, final_norm, gfin, *, batch, seq, state=None):
    m = x.shape[0]
    lamv = jnp.stack([w["lambda_q1"], w["lambda_k1"], w["lambda_q2"], w["lambda_k2"]])
    tm = min(512, m)
    w16 = {}
    q, k, v, z, xs, bc, dt, *extra = _in_proj(x, w["norm_mix_g"][None, :], w["w_in_main"], w["w_in_dt"],
                                              tm=min(1024, m))
    if extra:
        w16["w_in_main"] = extra[0]

    if state is None:
        slopes = jnp.broadcast_to(
            jnp.asarray(2.0 ** (-8.0 * np.arange(1, ATT_HEADS + 1) / ATT_HEADS), F32)[:, None, None],
            (ATT_HEADS, 1, LANES))
        o = _attn_prompt(q, k, v, slopes, lamv, w["attn_subln_g"][None, :], lam_init, tile=256)
        y, s_new = _ssd(xs, bc, dt, z, w["conv_w"], w["conv_b"], w["dt_bias"], w["a_log"],
                        w["d_skip"], w["ssm_norm_g"], lc=CHUNK, nchunk=4)
        conv_new = jnp.concatenate([xs[m - (SSM_CONV - 1):], bc[m - (SSM_CONV - 1):]], axis=-1)[None]
        s_new = s_new[None]
    else:
        cache_k, cache_v, conv_prev, ssm_prev, ffn_prev = state
        past = cache_k.shape[1]
        o = _attn_sample(q, k, v, cache_k.reshape(batch, past * ATT_HEADS, V_DIM),
                         cache_v.reshape(batch, past * ATT_HEADS, V_DIM),
                         lamv, w["attn_subln_g"][None, :], lam_init, seq)
        prev8 = jnp.pad(conv_prev, ((0, 0), (SUBLANES - (SSM_CONV - 1), 0), (0, 0)))
        y, s_new = _ssd(xs, bc, dt, z, w["conv_w"], w["conv_b"], w["dt_bias"], w["a_log"],
                        w["d_skip"], w["ssm_norm_g"], lc=seq, nchunk=1, conv_prev=prev8,
                        h0=ssm_prev.reshape(batch, D_SSM, SSM_STATE))
        conv_new = jnp.concatenate([xs.reshape(batch, seq, D_SSM)[:, seq - (SSM_CONV - 1):],
                                    bc.reshape(batch, seq, BC_DIM)[:, seq - (SSM_CONV - 1):]], axis=-1)

    x2, hf, *extra = _out_proj(o, y, x, w["w_out"], w["norm_ffn_g"][None, :], tm=tm)
    if extra:
        w16["w_out"] = extra[0]

    ffn_args = (hf, x2, w["w_gate"], w["w_up"], w["w_down"], w["ffn_conv_w"], w["ffn_conv_b"], gfin)
    if state is None:
        x3, gl = _ffn(*ffn_args, tm=min(1024, m), tf=512, final_norm=final_norm)
        ffn_new = gl[None, gl.shape[0] - (FFN_CONV - 1):]
    else:
        x3, gl, w16["w_gate"], w16["w_up"], w16["w_down"] = _ffn(
            *ffn_args, tm=tm, tf=256, final_norm=final_norm, seq=seq,
            prev=ffn_prev.reshape(batch * (FFN_CONV - 1), D_FF))
        ffn_new = gl.reshape(batch, seq, D_FF)[:, seq - (FFN_CONV - 1):]
    k_new = k.reshape(batch, seq, ATT_HEADS, 2 * QK_DIM)
    v_new = v.reshape(batch, seq, ATT_HEADS, V_DIM)
    s_new = s_new.reshape(batch, SSM_HEADS, SSM_HEAD_DIM, SSM_STATE)
    return (x3, k_new, v_new, conv_new, s_new, ffn_new), w16


def kernel(x_prompt, x_sample, cache_k, cache_v, state_ssm_conv, state_ssm, state_ffn_conv, norm_mix_g, w_in, lambda_q1, lambda_k1, lambda_q2, lambda_k2, attn_subln_g, conv_w, conv_b, dt_bias, a_log, d_skip, ssm_norm_g, w_out, norm_ffn_g, w_gate, w_up, ffn_conv_w, ffn_conv_b, w_down, norm_final_g):
    depth = w_in.shape[0]
    pb, pl_len, _ = x_prompt.shape
    sb, sl_len, _ = x_sample.shape
    assert pb == 1 and pl_len % CHUNK == 0
    xp = x_prompt.reshape(pb * pl_len, D_MODEL)
    xs = x_sample.reshape(sb * sl_len, D_MODEL)
    n_main = 2 * D_ATT + D_ATT + D_SSM + CONV_DIM
    outs_p, outs_s = [], []
    for layer in range(depth):
        lam_init = 0.8 - 0.6 * math.exp(-0.3 * layer)
        w_in_t = jnp.swapaxes(w_in[layer], 0, 1)
        w = dict(
            norm_mix_g=norm_mix_g[layer],
            w_in_main=w_in_t,
            w_in_dt=jnp.pad(w_in_t[n_main:].astype(BF16), ((0, LANES - SSM_HEADS), (0, 0))),
            lambda_q1=lambda_q1[layer], lambda_k1=lambda_k1[layer],
            lambda_q2=lambda_q2[layer], lambda_k2=lambda_k2[layer],
            attn_subln_g=attn_subln_g[layer], conv_w=conv_w[layer], conv_b=conv_b[layer],
            dt_bias=dt_bias[layer], a_log=a_log[layer], d_skip=d_skip[layer],
            ssm_norm_g=ssm_norm_g[layer], w_out=w_out[layer],
            norm_ffn_g=norm_ffn_g[layer], w_gate=w_gate[layer],
            w_up=w_up[layer], ffn_conv_w=ffn_conv_w[layer],
            ffn_conv_b=ffn_conv_b[layer], w_down=w_down[layer],
        )
        last = layer == depth - 1
        (xs, *new_s), w16 = _layer(xs, w, lam_init, last, norm_final_g, batch=sb, seq=sl_len,
                                   state=(cache_k[layer], cache_v[layer], state_ssm_conv[layer],
                                          state_ssm[layer], state_ffn_conv[layer]))
        (xp, *new_p), _ = _layer(xp, {**w, **w16}, lam_init, last, norm_final_g, batch=pb, seq=pl_len)
        outs_p.append(new_p)
        outs_s.append(new_s)
    stack = lambda outs, idx: jnp.stack([o[idx] for o in outs])
    return (xp.reshape(pb, pl_len, D_MODEL), xs.reshape(sb, sl_len, D_MODEL),
            *[stack(outs_p, idx) for idx in range(5)],
            *[stack(outs_s, idx) for idx in range(5)])
```

```python
import functools
import math

import jax
import jax.numpy as jnp
import numpy as np
from jax import lax
from jax.experimental import pallas as pl
from jax.experimental.pallas import tpu as pltpu

F32 = jnp.float32
BF16 = jnp.bfloat16

D_MODEL = 2048
CHUNK = 64
ATT_HEADS = 8
QK_DIM = 64
V_DIM = 128
D_ATT = ATT_HEADS * V_DIM
SSM_HEADS = 16
SSM_HEAD_DIM = 64
D_SSM = SSM_HEADS * SSM_HEAD_DIM
SSM_GROUPS = 2
SSM_STATE = 128
SSM_CONV = 4
BC_DIM = 2 * SSM_GROUPS * SSM_STATE
CONV_DIM = D_SSM + BC_DIM
D_FF = 5632
FFN_CONV = 3
EPS = 1e-6
LOG2E = math.log2(math.e)
Q_SCALE = QK_DIM ** -0.5 * LOG2E
LANES = 128
SUBLANES = 8
VMEM_LIMIT = 56 * 1024 * 1024

_NT = (((1,), (1,)), ((), ()))
_TN = (((0,), (0,)), ((), ()))


def _cparams(sem):
    return pltpu.CompilerParams(dimension_semantics=sem, vmem_limit_bytes=VMEM_LIMIT)


def _silu(x):
    return x * jax.nn.sigmoid(x)


def _rms(x, g):
    return x * lax.rsqrt(jnp.mean(x * x, axis=-1, keepdims=True) + EPS) * g


_PROJ_TN = 512
_PROJ_SEGS = ((0, 2), (2, 4), (4, 6), (6, 8), (8, 10), (10, 11))


def _in_proj_kernel(x_ref, g_ref, w_ref, wdt_ref,
                    q_ref, k_ref, v_ref, z_ref, xs_ref, bc_ref, dt_ref, *rest):
    w16_ref, h_scr = rest if len(rest) == 2 else (None, rest[0])
    j = pl.program_id(1)

    @pl.when(j == 0)
    def _():
        hb = _rms(x_ref[...], g_ref[...]).astype(BF16)
        h_scr[...] = hb
        dt_ref[...] = lax.dot_general(hb, wdt_ref[...], _NT, preferred_element_type=F32)

    outs = (q_ref, k_ref, v_ref, z_ref, xs_ref, bc_ref)
    for (lo, hi), ref in zip(_PROJ_SEGS, outs):
        @pl.when((j >= lo) & (j < hi))
        def _(ref=ref):
            w = w_ref[...]
            if w16_ref is not None:
                w = w.astype(BF16)
                w16_ref[...] = w
            res = lax.dot_general(h_scr[...], w, _NT, preferred_element_type=F32)
            val = res * Q_SCALE if ref is q_ref else res
            ref[...] = val.astype(ref.dtype)


def _in_proj(x, g, w_main, w_dt, tm):
    m = x.shape[0]
    tn = _PROJ_TN
    nj = w_main.shape[0] // tn

    def seg_spec(lo, hi):
        return pl.BlockSpec((tm, tn), lambda i, j: (i, jnp.clip(j - lo, 0, hi - lo - 1)))

    out_shape = (
        jax.ShapeDtypeStruct((m, D_ATT), BF16),
        jax.ShapeDtypeStruct((m, D_ATT), F32),
        jax.ShapeDtypeStruct((m, D_ATT), F32),
        jax.ShapeDtypeStruct((m, D_SSM), F32),
        jax.ShapeDtypeStruct((m, D_SSM), F32),
        jax.ShapeDtypeStruct((m, BC_DIM), F32),
        jax.ShapeDtypeStruct((m, LANES), F32),
    )
    out_specs = tuple(seg_spec(lo, hi) for lo, hi in _PROJ_SEGS) + (
        pl.BlockSpec((tm, LANES), lambda i, j: (i, 0)),)
    if w_main.dtype != BF16:
        assert m == tm, "the bf16 weight copy is written once per tile: needs a single row tile"
        out_shape += (jax.ShapeDtypeStruct((nj * tn, D_MODEL), BF16),)
        out_specs += (pl.BlockSpec((tn, D_MODEL), lambda i, j: (j, 0)),)
    return pl.pallas_call(
        _in_proj_kernel,
        grid=(m // tm, nj),
        in_specs=[
            pl.BlockSpec((tm, D_MODEL), lambda i, j: (i, 0)),
            pl.BlockSpec((1, D_MODEL), lambda i, j: (0, 0)),
            pl.BlockSpec((tn, D_MODEL), lambda i, j: (j, 0)),
            pl.BlockSpec((LANES, D_MODEL), lambda i, j: (0, 0)),
        ],
        out_specs=out_specs,
        out_shape=out_shape,
        scratch_shapes=[pltpu.VMEM((tm, D_MODEL), BF16)],
        compiler_params=_cparams(("arbitrary", "arbitrary")),
        name="in_proj",
    )(x, g, w_main, w_dt)


def _lambda_value(lamv_ref, lam_init):
    lv = lamv_ref[...]
    s1 = jnp.sum(lv[0:1] * lv[1:2], axis=-1, keepdims=True)
    s2 = jnp.sum(lv[2:3] * lv[3:4], axis=-1, keepdims=True)
    return jnp.exp(s1) - jnp.exp(s2) + lam_init


def _split_maps(q):
    lane = lax.broadcasted_iota(jnp.int32, q.shape, 1)
    zero = jnp.zeros_like(q)
    return jnp.where(lane < QK_DIM, q, zero), jnp.where(lane >= QK_DIM, q, zero)


_VT_CHUNK = 512
_VT_PAD = 16
_UNROLL_SHIFT = 2
_UNROLL = 1 << _UNROLL_SHIFT
_HEADS_PER_STEP = 2


def _attn_prompt_kernel(q_ref, k_ref, v_ref, slope_ref, lamv_ref, g_ref, dmask_ref, dneg_ref, o_ref,
                        k_scr, vt_scr, acc_scr, t0_scr, t1_scr, dbias_scr, stat_scr,
                        *, tile, lam_init):
    length = k_ref.shape[0]
    width = 2 * tile
    nheads = k_scr.shape[0]
    lam = _lambda_value(lamv_ref, lam_init)

    lane = lax.broadcasted_iota(jnp.int32, (length, LANES), 1)
    koff = lax.broadcasted_iota(jnp.int32, (length, LANES), 0) % tile
    koff = jnp.where(lane < 3, koff, 0).astype(F32).astype(BF16)
    extra = (lax.broadcasted_iota(jnp.int32, (_VT_PAD, tile), 0) == 0).astype(F32).astype(BF16)
    brow = lax.broadcasted_iota(jnp.int32, (LANES, width), 0)
    per = _VT_CHUNK // tile
    ws, q_biases = [], []
    for hh in range(nheads):
        cols = slice(hh * V_DIM, (hh + 1) * V_DIM)
        k_scr[hh, :, 0:V_DIM] = k_ref[:, cols].astype(BF16)
        k_scr[hh, :, V_DIM:V_DIM + LANES] = koff
        for c in range(length // _VT_CHUNK):
            vt = v_ref[c * _VT_CHUNK:(c + 1) * _VT_CHUNK, cols].T.astype(BF16)
            for s in range(per):
                vt_scr[hh, c * per + s, 0:V_DIM, :] = vt[:, s * tile:(s + 1) * tile]
                vt_scr[hh, c * per + s, V_DIM:V_DIM + _VT_PAD, :] = extra
        w = slope_ref[hh][:, :1] * LOG2E
        w_hi, w_mid, w_lo = (piece.astype(F32) for piece in _split3(w))
        q_bias = jnp.where(brow == 0, w_hi, jnp.where(brow == 1, w_mid, jnp.where(brow == 2, w_lo, 0.0)))
        ws.append(w)
        q_biases.append(q_bias.astype(BF16))
        dbias_scr[hh] = w * dmask_ref[...] + dneg_ref[...]

    def q_pair(qp, _):
        _attn_prompt_pair(qp, ws, lam, q_biases, q_ref, g_ref, o_ref,
                          k_scr, vt_scr, acc_scr, (t0_scr, t1_scr),
                          dbias_scr, stat_scr, tile=tile, lam_init=lam_init)
        return 0

    lax.fori_loop(0, length // (2 * tile), q_pair, 0)


def _attn_prompt_pair(qp, ws, lam, q_biases, q_ref, g_ref, o_ref,
                      k_scr, vt_scr, acc_scr, t_slots, dbias_scr, stat_scr, *, tile, lam_init):
    width = 2 * tile
    nheads = len(ws)
    streams = [(hh, u) for hh in range(nheads) for u in range(2)]
    nfull = 2 * qp
    qis = [nfull + u for _, u in streams]
    rows = [pl.ds(pl.multiple_of(qi * tile, tile), tile) for qi in qis]
    q_augs = []
    for s, (hh, _) in enumerate(streams):
        q1, q2 = _split_maps(q_ref[rows[s], hh * V_DIM:(hh + 1) * V_DIM])
        q_t = jnp.concatenate([q1, q2], axis=0).astype(F32).T.astype(BF16)
        q_augs.append(jnp.concatenate([q_t, q_biases[hh]], axis=0))

    def scores(s, j):
        k0 = pl.multiple_of(jnp.minimum(j, qis[s]) * tile, tile)
        return jnp.dot(k_scr[streams[s][0], pl.ds(k0, tile), :], q_augs[s], preferred_element_type=F32)

    TMAX = 0

    def step(s, j, slot, m_old):
        hh = streams[s][0]
        off = -ws[hh] * jnp.asarray((qis[s] - j) * tile, F32)
        m_new = jnp.maximum(m_old, stat_scr[s, TMAX:TMAX + 1, :] + off)
        t_next = scores(s, j + 1)
        t_slots[1 - slot][s] = t_next
        stat_scr[s, TMAX:TMAX + 1, :] = jnp.max(t_next, axis=0, keepdims=True)
        p = jnp.exp2(t_slots[slot][s] + (off - m_new)).astype(BF16)
        pv = jnp.dot(vt_scr[hh, j], p, preferred_element_type=F32)
        acc_scr[s] = jnp.exp2(m_old - m_new) * acc_scr[s] + pv
        return m_new

    def run(first, ntiles_per_iter, niter, carry):
        def body(i, carry):
            for k in range(ntiles_per_iter):
                carry = tuple(step(s, first + ntiles_per_iter * i + k, k % 2, carry[s])
                              for s in range(len(streams)))
            return carry
        return lax.fori_loop(0, niter, body, carry)

    carry = []
    for s in range(len(streams)):
        acc_scr[s] = jnp.zeros(acc_scr.shape[1:], F32)
        t_first = scores(s, 0)
        t_slots[0][s] = t_first
        stat_scr[s, TMAX:TMAX + 1, :] = jnp.max(t_first, axis=0, keepdims=True)
        carry.append(jnp.full((1, width), -0.5 * float(jnp.finfo(F32).max), F32))
    n_main = lax.shift_right_logical(nfull, _UNROLL_SHIFT)
    done = n_main * _UNROLL
    carry = run(0, _UNROLL, n_main, tuple(carry))
    carry = run(done, 2, lax.shift_right_logical(nfull - done, 1), carry)

    for s, (hh, u) in enumerate(streams):
        m_old = carry[s]
        diag_slot = 0
        if u == 1:
            m_old = step(s, nfull, 0, m_old)
            diag_slot = 1
        t = t_slots[diag_slot][s] + dbias_scr[hh]
        m_new = jnp.maximum(m_old, jnp.max(t, axis=0, keepdims=True))
        p = jnp.exp2(t - m_new).astype(BF16)
        acc = (jnp.exp2(m_old - m_new) * acc_scr[s]
               + jnp.dot(vt_scr[hh, qis[s]], p, preferred_element_type=F32))

        on = acc[0:V_DIM] / acc[V_DIM:V_DIM + 1]
        o_t = on[:, :tile] - lam * on[:, tile:]
        o_t = o_t * lax.rsqrt(jnp.mean(o_t * o_t, axis=0, keepdims=True) + EPS)
        o_ref[rows[s], hh * V_DIM:(hh + 1) * V_DIM] = (
            o_t.T * g_ref[...] * (1.0 - lam_init)).astype(o_ref.dtype)


def _attn_prompt(q, k, v, slopes, lamv, g, lam_init, tile):
    length = q.shape[0]
    key = np.arange(tile)[:, None]
    qry = np.tile(np.arange(tile), 2)[None, :]
    dmask = jnp.asarray((qry - np.abs(qry - key)) - key, F32)
    dneg = jnp.asarray(np.where(key // CHUNK <= qry // CHUNK, 0.0, -np.inf), F32)
    nh = _HEADS_PER_STEP
    diag_spec = pl.BlockSpec((tile, 2 * tile), lambda h: (0, 0), pipeline_mode=pl.Buffered(1))
    head_spec = pl.BlockSpec((length, nh * V_DIM), lambda h: (0, h), pipeline_mode=pl.Buffered(1))
    kv_spec = pl.BlockSpec((length, nh * V_DIM), lambda h: (0, h), pipeline_mode=pl.Buffered(1))
    return pl.pallas_call(
        functools.partial(_attn_prompt_kernel, tile=tile, lam_init=lam_init),
        grid=(ATT_HEADS // nh,),
        in_specs=[
            kv_spec, kv_spec, kv_spec,
            pl.BlockSpec((nh, 1, LANES), lambda h: (h, 0, 0)),
            pl.BlockSpec((4, QK_DIM), lambda h: (0, 0)),
            pl.BlockSpec((1, V_DIM), lambda h: (0, 0)),
            diag_spec, diag_spec,
        ],
        out_specs=head_spec,
        out_shape=jax.ShapeDtypeStruct((length, D_ATT), BF16),
        scratch_shapes=[
            pltpu.VMEM((nh, length, V_DIM + LANES), BF16),
            pltpu.VMEM((nh, length // tile, V_DIM + _VT_PAD, tile), BF16),
            pltpu.VMEM((2 * nh, V_DIM + _VT_PAD, 2 * tile), F32),
            pltpu.VMEM((2 * nh, tile, 2 * tile), F32),
            pltpu.VMEM((2 * nh, tile, 2 * tile), F32),
            pltpu.VMEM((nh, tile, 2 * tile), F32),
            pltpu.VMEM((2 * nh, SUBLANES, 2 * tile), F32),
        ],
        compiler_params=_cparams(("arbitrary",)),
        name="attn_prompt",
    )(q, k, v, slopes, lamv, g, dmask, dneg)


def _attn_sample_kernel(q_ref, kn_ref, vn_ref, ck_ref, cv_ref, lamv_ref, g_ref, bias_c_ref, bias_n_ref,
                        o_ref, *, seq, past, lam_init):
    lam = _lambda_value(lamv_ref, lam_init)
    rows = 2 * seq
    sc, sn = [], []
    for h in range(ATT_HEADS):
        sl = slice(h * V_DIM, (h + 1) * V_DIM)
        q1, q2 = _split_maps(q_ref[:, sl])
        qq = jnp.concatenate([q1, q2], axis=0)
        kc = ck_ref[0, pl.ds(h, past, stride=ATT_HEADS), :].astype(BF16)
        sc.append(lax.dot_general(qq, kc, _NT, preferred_element_type=F32))
        sn.append(lax.dot_general(qq, kn_ref[:, sl].astype(BF16), _NT, preferred_element_type=F32))
    sc = jnp.concatenate(sc, axis=0) - bias_c_ref[...]
    sn = jnp.concatenate(sn, axis=0) - bias_n_ref[...]
    m = jnp.maximum(jnp.max(sc, axis=-1, keepdims=True), jnp.max(sn, axis=-1, keepdims=True))
    pc = jnp.exp2(sc - m)
    pn = jnp.exp2(sn - m)
    inv_l = 1.0 / (jnp.sum(pc, axis=-1, keepdims=True) + jnp.sum(pn, axis=-1, keepdims=True))
    pc = pc.astype(BF16)
    pn = pn.astype(BF16)
    for h in range(ATT_HEADS):
        sl = slice(h * V_DIM, (h + 1) * V_DIM)
        hr = slice(h * rows, (h + 1) * rows)
        vc = cv_ref[0, pl.ds(h, past, stride=ATT_HEADS), :].astype(BF16)
        acc = (jnp.dot(pc[hr], vc, preferred_element_type=F32)
               + jnp.dot(pn[hr], vn_ref[:, sl].astype(BF16), preferred_element_type=F32))
        on = acc * inv_l[hr]
        o = on[:seq] - lam * on[seq:]
        o_ref[:, sl] = (_rms(o, g_ref[...]) * (1.0 - lam_init)).astype(o_ref.dtype)


def _attn_sample(q, k_new, v_new, cache_k, cache_v, lamv, g, lam_init, seq):
    nb, past = cache_k.shape[0], cache_k.shape[1] // ATT_HEADS
    slope = np.repeat(2.0 ** (-8.0 * np.arange(1, ATT_HEADS + 1) / ATT_HEADS), 2 * seq)[:, None] * LOG2E
    qpos = np.tile(np.arange(seq), 2 * ATT_HEADS)[:, None]
    bias_c = jnp.asarray(slope * (qpos + past - np.arange(past)[None, :]), F32)
    bias_n = jnp.asarray(slope * np.abs(qpos - np.arange(seq)[None, :]), F32)
    row_spec = pl.BlockSpec((seq, D_ATT), lambda b: (b, 0))
    cache_spec = pl.BlockSpec((1, past * ATT_HEADS, V_DIM), lambda b: (b, 0, 0))

    def const(arr):
        return pl.BlockSpec(arr.shape, lambda b: (0, 0), pipeline_mode=pl.Buffered(1))

    return pl.pallas_call(
        functools.partial(_attn_sample_kernel, seq=seq, past=past, lam_init=lam_init),
        grid=(nb,),
        in_specs=[row_spec, row_spec, row_spec, cache_spec, cache_spec,
                  pl.BlockSpec((4, QK_DIM), lambda b: (0, 0)),
                  pl.BlockSpec((1, V_DIM), lambda b: (0, 0)),
                  const(bias_c), const(bias_n)],
        out_specs=row_spec,
        out_shape=jax.ShapeDtypeStruct((nb * seq, D_ATT), BF16),
        compiler_params=_cparams(("arbitrary",)),
        name="attn_sample",
    )(q, k_new, v_new, cache_k, cache_v, lamv, g, bias_c, bias_n)


def _split3(x):
    hi = x.astype(BF16)
    r = x - hi.astype(F32)
    mid = r.astype(BF16)
    lo = (r - mid.astype(F32)).astype(BF16)
    return hi, mid, lo


def _ssd_chunk(xs, bm, cm, dt, z, s_ref, a_heads, dskip, gn, e3, es3, t3, ones_k, lc):
    seg_w = SSM_HEADS * lc
    half = D_SSM // SSM_GROUPS
    d3 = jnp.concatenate(_split3(dt), axis=1)
    dt_x = jnp.dot(d3, e3, preferred_element_type=F32)
    la3 = jnp.concatenate(_split3(dt * a_heads), axis=0)
    ac3 = jnp.concatenate(_split3(jnp.dot(t3, la3, preferred_element_type=F32)), axis=1)
    acol = jnp.dot(ac3, e3, preferred_element_type=F32)
    acol_s = acol if seg_w == D_SSM else jnp.dot(ac3, es3, preferred_element_type=F32)
    t_idx = lax.broadcasted_iota(jnp.int32, (lc, seg_w), 0)
    s_idx = lax.broadcasted_iota(jnp.int32, (lc, seg_w), 1) % lc
    arow = jnp.sum(jnp.where(t_idx == s_idx, acol_s, 0.0), axis=0, keepdims=True)
    decay = jnp.where(t_idx >= s_idx, jnp.exp(acol_s - arow), 0.0)

    cb16 = cm.astype(BF16)
    bb16 = bm.astype(BF16)
    hpg = SSM_HEADS // SSM_GROUPS
    cbs = []
    for g in range(SSM_GROUPS):
        gs = slice(g * SSM_STATE, (g + 1) * SSM_STATE)
        b_rep = jnp.concatenate([bb16[:, gs]] * hpg, axis=0)
        cbs.append(lax.dot_general(cb16[:, gs], b_rep, _NT, preferred_element_type=F32))
    mmat = (jnp.concatenate(cbs, axis=1) * decay).astype(BF16)

    xdt = xs * dt_x
    xdt16 = xdt.astype(BF16)
    hk = (2 * LANES) // lc
    wd = hk * SSM_HEAD_DIM
    blk = (lax.broadcasted_iota(jnp.int32, (hk * lc, wd), 0) // lc
           == lax.broadcasted_iota(jnp.int32, (hk * lc, wd), 1) // SSM_HEAD_DIM)
    parts = []
    for i in range(SSM_HEADS // hk):
        xd = xdt16[:, i * wd:(i + 1) * wd]
        bd = jnp.where(blk, jnp.concatenate([xd] * hk, axis=0), jnp.zeros((), BF16))
        parts.append(jnp.dot(mmat[:, i * hk * lc:(i + 1) * hk * lc], bd, preferred_element_type=F32))
    y_intra = jnp.concatenate(parts, axis=1) if len(parts) > 1 else parts[0]

    yi = []
    for g in range(SSM_GROUPS):
        sg = s_ref[g * half:(g + 1) * half, :].astype(BF16)
        yi.append(lax.dot_general(cb16[:, g * SSM_STATE:(g + 1) * SSM_STATE], sg, _NT,
                                  preferred_element_type=F32))
    y_inter = jnp.exp(acol) * jnp.concatenate(yi, axis=1)

    alast = acol[lc - 1:lc, :]
    dec_end = jnp.exp(alast - acol)
    xd_end = (xdt * dec_end).astype(BF16)
    krow = lax.broadcasted_iota(jnp.int32, (ones_k.shape[0], D_SSM), 0)
    a_hi, a_mid, a_lo = (piece.astype(F32) for piece in _split3(alast))
    pieces = jnp.where(krow == 0, a_hi, jnp.where(krow == 1, a_mid, jnp.where(krow == 2, a_lo, 0.0)))
    acl = lax.dot_general(pieces.astype(BF16), ones_k, _TN, preferred_element_type=F32)
    for g in range(SSM_GROUPS):
        rows = slice(g * half, (g + 1) * half)
        upd = lax.dot_general(xd_end[:, rows], bb16[:, g * SSM_STATE:(g + 1) * SSM_STATE], _TN,
                              preferred_element_type=F32)
        s_ref[rows, :] = jnp.exp(acl[rows, :]) * s_ref[rows, :] + upd

    y = (y_intra + y_inter + dskip * xs) * _silu(z)
    outs = []
    for g in range(SSM_GROUPS):
        cs = slice(g * half, (g + 1) * half)
        outs.append(_rms(y[:, cs], gn[:, cs]))
    return jnp.concatenate(outs, axis=1)


def _ssd_kernel(*refs, lc, nchunk, has_state):
    if has_state:
        (xs_ref, bc_ref, dt_ref, z_ref, prev_ref, h0_ref, cw_ref, cbias_ref, dtb_ref, alog_ref,
         dskip_ref, gn_ref, e3_ref, es3_ref, t3_ref, ones_ref,
         y_ref, s_ref, scr_x, scr_bc) = refs
    else:
        (xs_ref, bc_ref, dt_ref, z_ref, cw_ref, cbias_ref, dtb_ref, alog_ref,
         dskip_ref, gn_ref, e3_ref, es3_ref, t3_ref, ones_ref,
         y_ref, s_ref, scr_x, scr_bc) = refs
    rows = lc * nchunk
    pad = SUBLANES
    if has_state:
        s2d = s_ref.at[0]
        scr_x[0:pad, :] = prev_ref[0, :, 0:D_SSM]
        scr_bc[0:pad, :] = prev_ref[0, :, D_SSM:CONV_DIM]
        s2d[...] = h0_ref[0]
    else:
        s2d = s_ref

        @pl.when(pl.program_id(0) == 0)
        def _():
            scr_x[0:pad, :] = jnp.zeros((pad, D_SSM), F32)
            scr_bc[0:pad, :] = jnp.zeros((pad, BC_DIM), F32)
            s_ref[...] = jnp.zeros(s_ref.shape, F32)

    scr_x[pad:pad + rows, :] = xs_ref[...]
    scr_bc[pad:pad + rows, :] = bc_ref[...]
    xc = cbias_ref[:, 0:D_SSM]
    bcc = cbias_ref[:, D_SSM:CONV_DIM]
    for tap in range(SSM_CONV):
        off = pad - (SSM_CONV - 1) + tap
        xc = xc + scr_x[off:off + rows, :] * cw_ref[tap:tap + 1, 0:D_SSM]
        bcc = bcc + scr_bc[off:off + rows, :] * cw_ref[tap:tap + 1, D_SSM:CONV_DIM]
    if not has_state:
        scr_x[0:pad, :] = scr_x[rows:rows + pad, :]
        scr_bc[0:pad, :] = scr_bc[rows:rows + pad, :]
    xs_act = _silu(xc)
    bc_act = _silu(bcc)
    dt_in = dt_ref[...] + dtb_ref[...]
    dt = jnp.maximum(dt_in, 0.0) + jnp.log1p(jnp.exp(-jnp.abs(dt_in)))
    a_heads = -jnp.exp(alog_ref[...])
    nbm = SSM_GROUPS * SSM_STATE
    for c in range(nchunk):
        rs = slice(c * lc, (c + 1) * lc)
        y = _ssd_chunk(xs_act[rs], bc_act[rs, 0:nbm], bc_act[rs, nbm:2 * nbm], dt[rs],
                       z_ref[rs, :], s2d, a_heads, dskip_ref[...], gn_ref[...],
                       e3_ref[...], es3_ref[...], t3_ref[...], ones_ref[...], lc)
        y_ref[rs, :] = y.astype(y_ref.dtype)


def _ssd_constants(lc):
    seg_w = SSM_HEADS * lc
    head_of_lane = np.arange(D_SSM) // SSM_HEAD_DIM
    e = (np.arange(LANES)[:, None] == head_of_lane[None, :]).astype(np.float32)
    es = (np.arange(LANES)[:, None] == (np.arange(seg_w) // lc)[None, :]).astype(np.float32)
    tri = np.tril(np.ones((lc, lc), np.float32))
    return (jnp.asarray(np.concatenate([e] * 3, axis=0), BF16),
            jnp.asarray(np.concatenate([es] * 3, axis=0), BF16),
            jnp.asarray(np.concatenate([tri] * 3, axis=1), BF16),
            jnp.ones((2 * SUBLANES, SSM_STATE), BF16))


def _ssd(xs, bc, dt, z, conv_w, conv_b, dt_bias, a_log, d_skip, gn, lc, nchunk,
         conv_prev=None, h0=None):
    m = xs.shape[0]
    rows = lc * nchunk
    has_state = h0 is not None
    e3, es3, t3, ones_k = _ssd_constants(lc)
    dtb = jnp.zeros((1, LANES), F32).at[0, :SSM_HEADS].set(dt_bias)
    alog = jnp.zeros((1, LANES), F32).at[0, :SSM_HEADS].set(a_log)
    dskip_x = jnp.repeat(d_skip, SSM_HEAD_DIM)[None, :]

    def const(shape):
        return pl.BlockSpec(shape, lambda i: (0,) * len(shape))

    def rowblk(width):
        return pl.BlockSpec((rows, width), lambda i: (i, 0))

    in_specs = [rowblk(D_SSM), rowblk(BC_DIM), rowblk(LANES), rowblk(D_SSM)]
    args = [xs, bc, dt, z]
    if has_state:
        nb = h0.shape[0]
        in_specs += [pl.BlockSpec((1, SUBLANES, CONV_DIM), lambda i: (i, 0, 0)),
                     pl.BlockSpec((1, D_SSM, SSM_STATE), lambda i: (i, 0, 0))]
        args += [conv_prev, h0]
        s_shape = jax.ShapeDtypeStruct((nb, D_SSM, SSM_STATE), F32)
        s_spec = pl.BlockSpec((1, D_SSM, SSM_STATE), lambda i: (i, 0, 0))
    else:
        s_shape = jax.ShapeDtypeStruct((D_SSM, SSM_STATE), F32)
        s_spec = const((D_SSM, SSM_STATE))
    in_specs += [const((SSM_CONV, CONV_DIM)), const((1, CONV_DIM)), const((1, LANES)),
                 const((1, LANES)), const((1, D_SSM)), const((1, D_SSM)),
                 const(e3.shape), const(es3.shape), const(t3.shape), const(ones_k.shape)]
    args += [conv_w, conv_b[None, :], dtb, alog, dskip_x, gn[None, :], e3, es3, t3, ones_k]
    return pl.pallas_call(
        functools.partial(_ssd_kernel, lc=lc, nchunk=nchunk, has_state=has_state),
        grid=(m // rows,),
        in_specs=in_specs,
        out_specs=(rowblk(D_SSM), s_spec),
        out_shape=(jax.ShapeDtypeStruct((m, D_SSM), BF16), s_shape),
        scratch_shapes=[pltpu.VMEM((rows + SUBLANES, D_SSM), F32),
                        pltpu.VMEM((rows + SUBLANES, BC_DIM), F32)],
        compiler_params=_cparams(("arbitrary",)),
        name="ssd_sample" if has_state else "ssd_prompt",
    )(*args)


def _out_proj_kernel(o_ref, y_ref, x_ref, w_ref, g_ref, x2_ref, hf_ref, w16_ref=None):
    if w16_ref is not None:
        w16_ref[...] = w_ref[...].astype(BF16)
        w_ref = w16_ref
    x2 = (x_ref[...]
          + jnp.dot(o_ref[...], w_ref[0:D_ATT, :], preferred_element_type=F32)
          + jnp.dot(y_ref[...], w_ref[D_ATT:D_ATT + D_SSM, :], preferred_element_type=F32))
    x2_ref[...] = x2
    hf_ref[...] = _rms(x2, g_ref[...]).astype(hf_ref.dtype)


def _out_proj(o, y, x, w, g, tm):
    m = x.shape[0]
    w_spec = pl.BlockSpec((D_ATT + D_SSM, D_MODEL), lambda i: (0, 0), pipeline_mode=pl.Buffered(1))
    out_specs = (pl.BlockSpec((tm, D_MODEL), lambda i: (i, 0)),
                 pl.BlockSpec((tm, D_MODEL), lambda i: (i, 0)))
    out_shape = (jax.ShapeDtypeStruct((m, D_MODEL), F32),
                 jax.ShapeDtypeStruct((m, D_MODEL), BF16))
    if w.dtype != BF16:
        assert m == tm, "the bf16 weight copy is written once: needs a single row tile"
        out_specs += (pl.BlockSpec((D_ATT + D_SSM, D_MODEL), lambda i: (0, 0)),)
        out_shape += (jax.ShapeDtypeStruct(w.shape, BF16),)
    return pl.pallas_call(
        _out_proj_kernel,
        grid=(m // tm,),
        in_specs=[
            pl.BlockSpec((tm, D_ATT), lambda i: (i, 0)),
            pl.BlockSpec((tm, D_SSM), lambda i: (i, 0)),
            pl.BlockSpec((tm, D_MODEL), lambda i: (i, 0)),
            w_spec,
            pl.BlockSpec((1, D_MODEL), lambda i: (0, 0)),
        ],
        out_specs=out_specs,
        out_shape=out_shape,
        compiler_params=_cparams(("arbitrary",)),
        name="out_proj",
    )(o, y, x, w, g)


def _ffn_kernel(*refs, tm, seq, final_norm):
    if seq is None:
        (hf_ref, x2_ref, wg_ref, wu_ref, wd_ref, cw_ref, cb_ref, gfin_ref,
         out_ref, gl_ref, g_scr, act_scr, carry_scr) = refs
    else:
        (hf_ref, x2_ref, wg_ref, wu_ref, wd_ref, cw_ref, cb_ref, gfin_ref, prev_ref, sel1_ref, sel2_ref,
         out_ref, gl_ref, wg16_ref, wu16_ref, wd16_ref, g_scr, act_scr) = refs
    i = pl.program_id(0)
    f = pl.program_id(1)
    nf = pl.num_programs(1) - 1
    pad = SUBLANES

    def weight(w_ref, w16_ref):
        if seq is None:
            return w_ref[...]
        w16 = w_ref[...].astype(BF16)
        w16_ref[...] = w16
        return w16

    def gate_up():
        hf = hf_ref[...]
        gate = jnp.dot(hf, weight(wg_ref, None if seq is None else wg16_ref), preferred_element_type=F32)
        up = jnp.dot(hf, weight(wu_ref, None if seq is None else wu16_ref), preferred_element_type=F32)
        g_scr[pad:pad + tm, :] = gate
        if seq is None:
            g_scr[0:pad, :] = carry_scr[f]
            carry_scr[f] = g_scr[tm:tm + pad, :]
            gl_ref[...] = g_scr[tm:tm + pad, :]
            g1 = g_scr[pad - 1:pad - 1 + tm, :]
            g2 = g_scr[pad - 2:pad - 2 + tm, :]
        else:
            g_scr[0:pad, :] = jnp.zeros((pad, gate.shape[1]), F32)
            gl_ref[...] = gate
            prev3 = jnp.concatenate(_split3(prev_ref[...]), axis=0)
            ov1 = jnp.dot(sel1_ref[...], prev3, preferred_element_type=F32)
            ov2 = jnp.dot(sel2_ref[...], prev3, preferred_element_type=F32)
            pos = lax.broadcasted_iota(jnp.int32, gate.shape, 0) % seq
            g1 = jnp.where(pos == 0, ov1, g_scr[pad - 1:pad - 1 + tm, :])
            g2 = jnp.where(pos < 2, ov2, g_scr[pad - 2:pad - 2 + tm, :])
        conv = cb_ref[...] + g2 * cw_ref[0:1, :] + g1 * cw_ref[1:2, :] + gate * cw_ref[2:3, :]
        act_scr[...] = (_silu(conv) * up).astype(BF16)

    def down():
        return jnp.dot(act_scr[...], weight(wd_ref, None if seq is None else wd16_ref),
                       preferred_element_type=F32)

    @pl.when(f == 0)
    def _():
        if seq is None:
            @pl.when(i == 0)
            def _():
                carry_scr[...] = jnp.zeros(carry_scr.shape, F32)
        out_ref[...] = x2_ref[...]
        gate_up()

    @pl.when((f > 0) & (f < nf))
    def _():
        contrib = down()
        gate_up()
        out_ref[...] += contrib

    @pl.when(f == nf)
    def _():
        res = out_ref[...] + down()
        out_ref[...] = _rms(res, gfin_ref[...]) if final_norm else res


def _ffn_prev_selectors(m, seq, nprev):
    sel1 = np.zeros((m, 3 * nprev), np.float32)
    sel2 = np.zeros((m, 3 * nprev), np.float32)
    nstate = FFN_CONV - 1
    for b in range(m // seq):
        for piece in range(3):
            base = piece * nprev + b * nstate
            sel1[b * seq, base + 1] = 1.0
            sel2[b * seq, base + 0] = 1.0
            sel2[b * seq + 1, base + 1] = 1.0
    return jnp.asarray(sel1, BF16), jnp.asarray(sel2, BF16)


def _ffn(hf, x2, wg, wu, wd, cw, cb, gfin, tm, tf, final_norm, seq=None, prev=None):
    m = hf.shape[0]
    nf = D_FF // tf
    cur = lambda f: jnp.minimum(f, nf - 1)
    last = lambda f: jnp.maximum(f - 1, 0)
    in_specs = [
        pl.BlockSpec((tm, D_MODEL), lambda i, f: (i, 0)),
        pl.BlockSpec((tm, D_MODEL), lambda i, f: (i, 0), pipeline_mode=pl.Buffered(1)),
        pl.BlockSpec((D_MODEL, tf), lambda i, f: (0, cur(f))),
        pl.BlockSpec((D_MODEL, tf), lambda i, f: (0, cur(f))),
        pl.BlockSpec((tf, D_MODEL), lambda i, f: (last(f), 0)),
        pl.BlockSpec((FFN_CONV, tf), lambda i, f: (0, cur(f))),
        pl.BlockSpec((1, tf), lambda i, f: (0, cur(f))),
        pl.BlockSpec((1, D_MODEL), lambda i, f: (0, 0)),
    ]
    args = [hf, x2, wg, wu, wd, cw, cb[None, :], gfin[None, :]]
    scratch = [pltpu.VMEM((tm + SUBLANES, tf), F32), pltpu.VMEM((tm, tf), BF16)]
    if seq is None:
        gl_shape = jax.ShapeDtypeStruct((m // tm * SUBLANES, D_FF), F32)
        gl_spec = pl.BlockSpec((SUBLANES, tf), lambda i, f: (i, cur(f)))
        scratch.append(pltpu.VMEM((nf, SUBLANES, tf), F32))
    else:
        assert m == tm, "sample FFN handles all sequences in one row tile"
        nprev = prev.shape[0]
        sel1, sel2 = _ffn_prev_selectors(m, seq, nprev)
        in_specs += [pl.BlockSpec((nprev, tf), lambda i, f: (0, cur(f))),
                     pl.BlockSpec(sel1.shape, lambda i, f: (0, 0)),
                     pl.BlockSpec(sel2.shape, lambda i, f: (0, 0))]
        args += [prev, sel1, sel2]
        gl_shape = jax.ShapeDtypeStruct((m, D_FF), F32)
        gl_spec = pl.BlockSpec((tm, tf), lambda i, f: (i, cur(f)))
    out_specs = [pl.BlockSpec((tm, D_MODEL), lambda i, f: (i, 0)), gl_spec]
    out_shape = [jax.ShapeDtypeStruct((m, D_MODEL), F32), gl_shape]
    if seq is not None:
        out_specs += [in_specs[2], in_specs[3], in_specs[4]]
        out_shape += [jax.ShapeDtypeStruct(wg.shape, BF16), jax.ShapeDtypeStruct(wu.shape, BF16),
                      jax.ShapeDtypeStruct(wd.shape, BF16)]
    return pl.pallas_call(
        functools.partial(_ffn_kernel, tm=tm, seq=seq, final_norm=final_norm),
        grid=(m // tm, nf + 1),
        in_specs=in_specs,
        out_specs=tuple(out_specs),
        out_shape=tuple(out_shape),
        scratch_shapes=scratch,
        compiler_params=_cparams(("arbitrary", "arbitrary")),
        name="ffn_sample" if seq is not None else "ffn_prompt",
    )(*args)


def _layer(x, w, lam_init, final_norm, gfin, *, batch, seq, state=None):
    m = x.shape[0]
    lamv = jnp.stack([w["lambda_q1"], w["lambda_k1"], w["lambda_q2"], w["lambda_k2"]])
    tm = min(512, m)
    w16 = {}
    q, k, v, z, xs, bc, dt, *extra = _in_proj(x, w["norm_mix_g"][None, :], w["w_in_main"], w["w_in_dt"],
                                              tm=min(1024, m))
    if extra:
        w16["w_in_main"] = extra[0]

    if state is None:
        slopes = jnp.broadcast_to(
            jnp.asarray(2.0 ** (-8.0 * np.arange(1, ATT_HEADS + 1) / ATT_HEADS), F32)[:, None, None],
            (ATT_HEADS, 1, LANES))
        o = _attn_prompt(q, k, v, slopes, lamv, w["attn_subln_g"][None, :], lam_init, tile=256)
        y, s_new = _ssd(xs, bc, dt, z, w["conv_w"], w["conv_b"], w["dt_bias"], w["a_log"],
                        w["d_skip"], w["ssm_norm_g"], lc=CHUNK, nchunk=4)
        conv_new = jnp.concatenate([xs[m - (SSM_CONV - 1):], bc[m - (SSM_CONV - 1):]], axis=-1)[None]
        s_new = s_new[None]
    else:
        cache_k, cache_v, conv_prev, ssm_prev, ffn_prev = state
        past = cache_k.shape[1]
        assert past % CHUNK == 0 and seq <= CHUNK
        o = _attn_sample(q, k, v, cache_k.reshape(batch, past * ATT_HEADS, V_DIM),
                         cache_v.reshape(batch, past * ATT_HEADS, V_DIM),
                         lamv, w["attn_subln_g"][None, :], lam_init, seq)
        prev8 = jnp.pad(conv_prev, ((0, 0), (SUBLANES - (SSM_CONV - 1), 0), (0, 0)))
        y, s_new = _ssd(xs, bc, dt, z, w["conv_w"], w["conv_b"], w["dt_bias"], w["a_log"],
                        w["d_skip"], w["ssm_norm_g"], lc=seq, nchunk=1, conv_prev=prev8,
                        h0=ssm_prev.reshape(batch, D_SSM, SSM_STATE))
        conv_new = jnp.concatenate([xs.reshape(batch, seq, D_SSM)[:, seq - (SSM_CONV - 1):],
                                    bc.reshape(batch, seq, BC_DIM)[:, seq - (SSM_CONV - 1):]], axis=-1)

    x2, hf, *extra = _out_proj(o, y, x, w["w_out"], w["norm_ffn_g"][None, :], tm=tm)
    if extra:
        w16["w_out"] = extra[0]

    ffn_args = (hf, x2, w["w_gate"], w["w_up"], w["w_down"], w["ffn_conv_w"], w["ffn_conv_b"], gfin)
    if state is None:
        x3, gl = _ffn(*ffn_args, tm=min(1024, m), tf=512, final_norm=final_norm)
        ffn_new = gl[None, gl.shape[0] - (FFN_CONV - 1):]
    else:
        x3, gl, w16["w_gate"], w16["w_up"], w16["w_down"] = _ffn(
            *ffn_args, tm=tm, tf=256, final_norm=final_norm, seq=seq,
            prev=ffn_prev.reshape(batch * (FFN_CONV - 1), D_FF))
        ffn_new = gl.reshape(batch, seq, D_FF)[:, seq - (FFN_CONV - 1):]
    k_new = k.reshape(batch, seq, ATT_HEADS, 2 * QK_DIM)
    v_new = v.reshape(batch, seq, ATT_HEADS, V_DIM)
    s_new = s_new.reshape(batch, SSM_HEADS, SSM_HEAD_DIM, SSM_STATE)
    return (x3, k_new, v_new, conv_new, s_new, ffn_new), w16


def kernel(x_prompt, x_sample, cache_k, cache_v, state_ssm_conv, state_ssm, state_ffn_conv, norm_mix_g, w_in, lambda_q1, lambda_k1, lambda_q2, lambda_k2, attn_subln_g, conv_w, conv_b, dt_bias, a_log, d_skip, ssm_norm_g, w_out, norm_ffn_g, w_gate, w_up, ffn_conv_w, ffn_conv_b, w_down, norm_final_g):
    depth = w_in.shape[0]
    pb, pl_len, _ = x_prompt.shape
    sb, sl_len, _ = x_sample.shape
    assert pb == 1 and pl_len % CHUNK == 0
    xp = x_prompt.reshape(pb * pl_len, D_MODEL)
    xs = x_sample.reshape(sb * sl_len, D_MODEL)
    n_main = 2 * D_ATT + D_ATT + D_SSM + CONV_DIM
    outs_p, outs_s = [], []
    for layer in range(depth):
        lam_init = 0.8 - 0.6 * math.exp(-0.3 * layer)
        w_in_t = jnp.swapaxes(w_in[layer], 0, 1)
        w = dict(
            norm_mix_g=norm_mix_g[layer],
            w_in_main=w_in_t,
            w_in_dt=jnp.pad(w_in_t[n_main:].astype(BF16), ((0, LANES - SSM_HEADS), (0, 0))),
            lambda_q1=lambda_q1[layer], lambda_k1=lambda_k1[layer],
            lambda_q2=lambda_q2[layer], lambda_k2=lambda_k2[layer],
            attn_subln_g=attn_subln_g[layer], conv_w=conv_w[layer], conv_b=conv_b[layer],
            dt_bias=dt_bias[layer], a_log=a_log[layer], d_skip=d_skip[layer],
            ssm_norm_g=ssm_norm_g[layer], w_out=w_out[layer],
            norm_ffn_g=norm_ffn_g[layer], w_gate=w_gate[layer],
            w_up=w_up[layer], ffn_conv_w=ffn_conv_w[layer],
            ffn_conv_b=ffn_conv_b[layer], w_down=w_down[layer],
        )
        last = layer == depth - 1
        (xs, *new_s), w16 = _layer(xs, w, lam_init, last, norm_final_g, batch=sb, seq=sl_len,
                                   state=(cache_k[layer], cache_v[layer], state_ssm_conv[layer],
                                          state_ssm[layer], state_ffn_conv[layer]))
        (xp, *new_p), _ = _layer(xp, {**w, **w16}, lam_init, last, norm_final_g, batch=pb, seq=pl_len)
        outs_p.append(new_p)
        outs_s.append(new_s)
    stack = lambda outs, idx: jnp.stack([o[idx] for o in outs])
    return (xp.reshape(pb, pl_len, D_MODEL), xs.reshape(sb, sl_len, D_MODEL),
            *[stack(outs_p, idx) for idx in range(5)],
            *[stack(outs_s, idx) for idx in range(5)])
```

```python
import functools
import math

import jax
import jax.numpy as jnp
import numpy as np
from jax import lax
from jax.experimental import pallas as pl
from jax.experimental.pallas import tpu as pltpu

F32 = jnp.float32
BF16 = jnp.bfloat16

D_MODEL = 2048
CHUNK = 64
ATT_HEADS = 8
QK_DIM = 64
V_DIM = 128
D_ATT = ATT_HEADS * V_DIM
SSM_HEADS = 16
SSM_HEAD_DIM = 64
D_SSM = SSM_HEADS * SSM_HEAD_DIM
SSM_GROUPS = 2
SSM_STATE = 128
SSM_CONV = 4
BC_DIM = 2 * SSM_GROUPS * SSM_STATE
CONV_DIM = D_SSM + BC_DIM
D_FF = 5632
FFN_CONV = 3
EPS = 1e-6
LOG2E = math.log2(math.e)
Q_SCALE = QK_DIM ** -0.5 * LOG2E
LANES = 128
SUBLANES = 8
VMEM_LIMIT = 56 * 1024 * 1024

_NT = (((1,), (1,)), ((), ()))
_TN = (((0,), (0,)), ((), ()))


def _cparams(sem):
    return pltpu.CompilerParams(dimension_semantics=sem, vmem_limit_bytes=VMEM_LIMIT)


def _silu(x):
    return x * jax.nn.sigmoid(x)


def _rms(x, g):
    return x * lax.rsqrt(jnp.mean(x * x, axis=-1, keepdims=True) + EPS) * g


_PROJ_TN = 512
_PROJ_SEGS = ((0, 2), (2, 4), (4, 6), (6, 8), (8, 10), (10, 11))


def _in_proj_kernel(x_ref, g_ref, w_ref, wdt_ref,
                    q_ref, k_ref, v_ref, z_ref, xs_ref, bc_ref, dt_ref, *rest):
    w16_ref, h_scr = rest if len(rest) == 2 else (None, rest[0])
    j = pl.program_id(1)

    @pl.when(j == 0)
    def _():
        hb = _rms(x_ref[...], g_ref[...]).astype(BF16)
        h_scr[...] = hb
        dt_ref[...] = lax.dot_general(hb, wdt_ref[...], _NT, preferred_element_type=F32)

    outs = (q_ref, k_ref, v_ref, z_ref, xs_ref, bc_ref)
    for (lo, hi), ref in zip(_PROJ_SEGS, outs):
        @pl.when((j >= lo) & (j < hi))
        def _(ref=ref):
            w = w_ref[...]
            if w16_ref is not None:
                w = w.astype(BF16)
                w16_ref[...] = w
            res = lax.dot_general(h_scr[...], w, _NT, preferred_element_type=F32)
            val = res * Q_SCALE if ref is q_ref else res
            ref[...] = val.astype(ref.dtype)


def _in_proj(x, g, w_main, w_dt, tm):
    m = x.shape[0]
    tn = _PROJ_TN
    nj = w_main.shape[0] // tn

    def seg_spec(lo, hi):
        return pl.BlockSpec((tm, tn), lambda i, j: (i, jnp.clip(j - lo, 0, hi - lo - 1)))

    out_shape = (
        jax.ShapeDtypeStruct((m, D_ATT), BF16),
        jax.ShapeDtypeStruct((m, D_ATT), F32),
        jax.ShapeDtypeStruct((m, D_ATT), F32),
        jax.ShapeDtypeStruct((m, D_SSM), F32),
        jax.ShapeDtypeStruct((m, D_SSM), F32),
        jax.ShapeDtypeStruct((m, BC_DIM), F32),
        jax.ShapeDtypeStruct((m, LANES), F32),
    )
    out_specs = tuple(seg_spec(lo, hi) for lo, hi in _PROJ_SEGS) + (
        pl.BlockSpec((tm, LANES), lambda i, j: (i, 0)),)
    if w_main.dtype != BF16:
        assert m == tm, "the bf16 weight copy is written once per tile: needs a single row tile"
        out_shape += (jax.ShapeDtypeStruct((nj * tn, D_MODEL), BF16),)
        out_specs += (pl.BlockSpec((tn, D_MODEL), lambda i, j: (j, 0)),)
    return pl.pallas_call(
        _in_proj_kernel,
        grid=(m // tm, nj),
        in_specs=[
            pl.BlockSpec((tm, D_MODEL), lambda i, j: (i, 0)),
            pl.BlockSpec((1, D_MODEL), lambda i, j: (0, 0)),
            pl.BlockSpec((tn, D_MODEL), lambda i, j: (j, 0)),
            pl.BlockSpec((LANES, D_MODEL), lambda i, j: (0, 0)),
        ],
        out_specs=out_specs,
        out_shape=out_shape,
        scratch_shapes=[pltpu.VMEM((tm, D_MODEL), BF16)],
        compiler_params=_cparams(("arbitrary", "arbitrary")),
        name="in_proj",
    )(x, g, w_main, w_dt)


def _lambda_value(lamv_ref, lam_init):
    lv = lamv_ref[...]
    s1 = jnp.sum(lv[0:1] * lv[1:2], axis=-1, keepdims=True)
    s2 = jnp.sum(lv[2:3] * lv[3:4], axis=-1, keepdims=True)
    return jnp.exp(s1) - jnp.exp(s2) + lam_init


def _split_maps(q):
    lane = lax.broadcasted_iota(jnp.int32, q.shape, 1)
    zero = jnp.zeros_like(q)
    return jnp.where(lane < QK_DIM, q, zero), jnp.where(lane >= QK_DIM, q, zero)


_VT_CHUNK = 512
_VT_PAD = 16
_UNROLL_SHIFT = 2
_UNROLL = 1 << _UNROLL_SHIFT
_HEADS_PER_STEP = 2
_QTILES_PER_GROUP = 4


def _attn_prompt_kernel(q_ref, k_ref, v_ref, slope_ref, lamv_ref, g_ref, dmask_ref, dneg_ref, o_ref,
                        k_scr, vt_scr, acc_scr, t0_scr, t1_scr, dbias_scr, stat_scr,
                        *, tile, lam_init):
    length = k_ref.shape[0]
    width = 2 * tile
    nheads = k_scr.shape[0]
    lam = _lambda_value(lamv_ref, lam_init)

    lane = lax.broadcasted_iota(jnp.int32, (length, LANES), 1)
    koff = lax.broadcasted_iota(jnp.int32, (length, LANES), 0) % tile
    koff = jnp.where(lane < 3, koff, 0).astype(F32).astype(BF16)
    extra = (lax.broadcasted_iota(jnp.int32, (_VT_PAD, tile), 0) == 0).astype(F32).astype(BF16)
    brow = lax.broadcasted_iota(jnp.int32, (LANES, width), 0)
    per = _VT_CHUNK // tile
    ws, q_biases = [], []
    for hh in range(nheads):
        cols = slice(hh * V_DIM, (hh + 1) * V_DIM)
        k_scr[hh, :, 0:V_DIM] = k_ref[:, cols].astype(BF16)
        k_scr[hh, :, V_DIM:V_DIM + LANES] = koff
        for c in range(length // _VT_CHUNK):
            vt = v_ref[c * _VT_CHUNK:(c + 1) * _VT_CHUNK, cols].T.astype(BF16)
            for s in range(per):
                vt_scr[hh, c * per + s, 0:V_DIM, :] = vt[:, s * tile:(s + 1) * tile]
                vt_scr[hh, c * per + s, V_DIM:V_DIM + _VT_PAD, :] = extra
        w = slope_ref[hh][:, :1] * LOG2E
        w_hi, w_mid, w_lo = (piece.astype(F32) for piece in _split3(w))
        q_bias = jnp.where(brow == 0, w_hi, jnp.where(brow == 1, w_mid, jnp.where(brow == 2, w_lo, 0.0)))
        ws.append(w)
        q_biases.append(q_bias.astype(BF16))
        dbias_scr[hh] = w * dmask_ref[...] + dneg_ref[...]

    def q_pair(qp, _):
        _attn_prompt_pair(qp, ws, lam, q_biases, q_ref, g_ref, o_ref,
                          k_scr, vt_scr, acc_scr, (t0_scr, t1_scr),
                          dbias_scr, stat_scr, tile=tile, lam_init=lam_init)
        return 0

    lax.fori_loop(0, length // (_QTILES_PER_GROUP * tile), q_pair, 0)


def _attn_prompt_pair(qp, ws, lam, q_biases, q_ref, g_ref, o_ref,
                      k_scr, vt_scr, acc_scr, t_slots, dbias_scr, stat_scr, *, tile, lam_init):
    width = 2 * tile
    nheads = len(ws)
    group = _QTILES_PER_GROUP
    streams = [(hh, u) for hh in range(nheads) for u in range(group)]
    nfull = group * qp
    qis = [nfull + u for _, u in streams]
    rows = [pl.ds(pl.multiple_of(qi * tile, tile), tile) for qi in qis]
    q_augs = []
    for s, (hh, _) in enumerate(streams):
        q1, q2 = _split_maps(q_ref[rows[s], hh * V_DIM:(hh + 1) * V_DIM])
        q_t = jnp.concatenate([q1, q2], axis=0).astype(F32).T.astype(BF16)
        q_augs.append(jnp.concatenate([q_t, q_biases[hh]], axis=0))

    def scores(s, j):
        k0 = pl.multiple_of(jnp.minimum(j, qis[s]) * tile, tile)
        return jnp.dot(k_scr[streams[s][0], pl.ds(k0, tile), :], q_augs[s], preferred_element_type=F32)

    TMAX = 0

    def step(s, j, slot, m_old):
        hh = streams[s][0]
        off = -ws[hh] * jnp.asarray((qis[s] - j) * tile, F32)
        m_new = jnp.maximum(m_old, stat_scr[s, TMAX:TMAX + 1, :] + off)
        t_next = scores(s, j + 1)
        t_slots[1 - slot][s] = t_next
        stat_scr[s, TMAX:TMAX + 1, :] = jnp.max(t_next, axis=0, keepdims=True)
        p = jnp.exp2(t_slots[slot][s] + (off - m_new)).astype(BF16)
        pv = jnp.dot(vt_scr[hh, j], p, preferred_element_type=F32)
        acc_scr[s] = jnp.exp2(m_old - m_new) * acc_scr[s] + pv
        return m_new

    def run(first, ntiles_per_iter, niter, carry):
        def body(i, carry):
            for k in range(ntiles_per_iter):
                carry = tuple(step(s, first + ntiles_per_iter * i + k, k % 2, carry[s])
                              for s in range(len(streams)))
            return carry
        return lax.fori_loop(0, niter, body, carry)

    carry = []
    for s in range(len(streams)):
        acc_scr[s] = jnp.zeros(acc_scr.shape[1:], F32)
        t_first = scores(s, 0)
        t_slots[0][s] = t_first
        stat_scr[s, TMAX:TMAX + 1, :] = jnp.max(t_first, axis=0, keepdims=True)
        carry.append(jnp.full((1, width), -0.5 * float(jnp.finfo(F32).max), F32))
    n_main = lax.shift_right_logical(nfull, _UNROLL_SHIFT)
    done = n_main * _UNROLL
    carry = run(0, _UNROLL, n_main, tuple(carry))
    carry = run(done, 2, lax.shift_right_logical(nfull - done, 1), carry)

    for s, (hh, u) in enumerate(streams):
        m_old = carry[s]
        diag_slot = 0
        for e in range(u):
            m_old = step(s, nfull + e, diag_slot, m_old)
            diag_slot = 1 - diag_slot
        t = t_slots[diag_slot][s] + dbias_scr[hh]
        m_new = jnp.maximum(m_old, jnp.max(t, axis=0, keepdims=True))
        p = jnp.exp2(t - m_new).astype(BF16)
        acc = (jnp.exp2(m_old - m_new) * acc_scr[s]
               + jnp.dot(vt_scr[hh, qis[s]], p, preferred_element_type=F32))

        on = acc[0:V_DIM] / acc[V_DIM:V_DIM + 1]
        o_t = on[:, :tile] - lam * on[:, tile:]
        o_t = o_t * lax.rsqrt(jnp.mean(o_t * o_t, axis=0, keepdims=True) + EPS)
        o_ref[rows[s], hh * V_DIM:(hh + 1) * V_DIM] = (
            o_t.T * g_ref[...] * (1.0 - lam_init)).astype(o_ref.dtype)


def _attn_prompt(q, k, v, slopes, lamv, g, lam_init, tile):
    length = q.shape[0]
    key = np.arange(tile)[:, None]
    qry = np.tile(np.arange(tile), 2)[None, :]
    dmask = jnp.asarray((qry - np.abs(qry - key)) - key, F32)
    dneg = jnp.asarray(np.where(key // CHUNK <= qry // CHUNK, 0.0, -np.inf), F32)
    nh = _HEADS_PER_STEP
    nstream = nh * _QTILES_PER_GROUP
    diag_spec = pl.BlockSpec((tile, 2 * tile), lambda h: (0, 0), pipeline_mode=pl.Buffered(1))
    head_spec = pl.BlockSpec((length, nh * V_DIM), lambda h: (0, h), pipeline_mode=pl.Buffered(1))
    kv_spec = pl.BlockSpec((length, nh * V_DIM), lambda h: (0, h), pipeline_mode=pl.Buffered(1))
    return pl.pallas_call(
        functools.partial(_attn_prompt_kernel, tile=tile, lam_init=lam_init),
        grid=(ATT_HEADS // nh,),
        in_specs=[
            kv_spec, kv_spec, kv_spec,
            pl.BlockSpec((nh, 1, LANES), lambda h: (h, 0, 0)),
            pl.BlockSpec((4, QK_DIM), lambda h: (0, 0)),
            pl.BlockSpec((1, V_DIM), lambda h: (0, 0)),
            diag_spec, diag_spec,
        ],
        out_specs=head_spec,
        out_shape=jax.ShapeDtypeStruct((length, D_ATT), BF16),
        scratch_shapes=[
            pltpu.VMEM((nh, length, V_DIM + LANES), BF16),
            pltpu.VMEM((nh, length // tile, V_DIM + _VT_PAD, tile), BF16),
            pltpu.VMEM((nstream, V_DIM + _VT_PAD, 2 * tile), F32),
            pltpu.VMEM((nstream, tile, 2 * tile), F32),
            pltpu.VMEM((nstream, tile, 2 * tile), F32),
            pltpu.VMEM((nh, tile, 2 * tile), F32),
            pltpu.VMEM((nstream, SUBLANES, 2 * tile), F32),
        ],
        compiler_params=_cparams(("arbitrary",)),
        name="attn_prompt",
    )(q, k, v, slopes, lamv, g, dmask, dneg)


_SAMPLE_SEQS_PER_STEP = 2


def _attn_sample_kernel(q_ref, kn_ref, vn_ref, ck_ref, cv_ref, lamv_ref, g_ref, bias_c_ref, bias_n_ref,
                        o_ref, *, seq, past, lam_init):
    lam = _lambda_value(lamv_ref, lam_init)
    rows = 2 * seq
    for bi in range(ck_ref.shape[0]):
        br = slice(bi * seq, (bi + 1) * seq)
        sc, sn = [], []
        for h in range(ATT_HEADS):
            sl = slice(h * V_DIM, (h + 1) * V_DIM)
            q1, q2 = _split_maps(q_ref[br, sl])
            qq = jnp.concatenate([q1, q2], axis=0)
            kc = ck_ref[bi, pl.ds(h, past, stride=ATT_HEADS), :].astype(BF16)
            sc.append(lax.dot_general(qq, kc, _NT, preferred_element_type=F32))
            sn.append(lax.dot_general(qq, kn_ref[br, sl].astype(BF16), _NT, preferred_element_type=F32))
        sc = jnp.concatenate(sc, axis=0) - bias_c_ref[...]
        sn = jnp.concatenate(sn, axis=0) - bias_n_ref[...]
        m = jnp.maximum(jnp.max(sc, axis=-1, keepdims=True), jnp.max(sn, axis=-1, keepdims=True))
        pc = jnp.exp2(sc - m)
        pn = jnp.exp2(sn - m)
        inv_l = 1.0 / (jnp.sum(pc, axis=-1, keepdims=True) + jnp.sum(pn, axis=-1, keepdims=True))
        pc = pc.astype(BF16)
        pn = pn.astype(BF16)
        for h in range(ATT_HEADS):
            sl = slice(h * V_DIM, (h + 1) * V_DIM)
            hr = slice(h * rows, (h + 1) * rows)
            vc = cv_ref[bi, pl.ds(h, past, stride=ATT_HEADS), :].astype(BF16)
            acc = (jnp.dot(pc[hr], vc, preferred_element_type=F32)
                   + jnp.dot(pn[hr], vn_ref[br, sl].astype(BF16), preferred_element_type=F32))
            on = acc * inv_l[hr]
            o = on[:seq] - lam * on[seq:]
            o_ref[br, sl] = (_rms(o, g_ref[...]) * (1.0 - lam_init)).astype(o_ref.dtype)


def _attn_sample(q, k_new, v_new, cache_k, cache_v, lamv, g, lam_init, seq):
    nb, past = cache_k.shape[0], cache_k.shape[1] // ATT_HEADS
    slope = np.repeat(2.0 ** (-8.0 * np.arange(1, ATT_HEADS + 1) / ATT_HEADS), 2 * seq)[:, None] * LOG2E
    qpos = np.tile(np.arange(seq), 2 * ATT_HEADS)[:, None]
    bias_c = jnp.asarray(slope * (qpos + past - np.arange(past)[None, :]), F32)
    bias_n = jnp.asarray(slope * np.abs(qpos - np.arange(seq)[None, :]), F32)
    per = _SAMPLE_SEQS_PER_STEP
    row_spec = pl.BlockSpec((per * seq, D_ATT), lambda b: (b, 0))
    cache_spec = pl.BlockSpec((per, past * ATT_HEADS, V_DIM), lambda b: (b, 0, 0))

    def const(arr):
        return pl.BlockSpec(arr.shape, lambda b: (0, 0), pipeline_mode=pl.Buffered(1))

    return pl.pallas_call(
        functools.partial(_attn_sample_kernel, seq=seq, past=past, lam_init=lam_init),
        grid=(nb // per,),
        in_specs=[row_spec, row_spec, row_spec, cache_spec, cache_spec,
                  pl.BlockSpec((4, QK_DIM), lambda b: (0, 0)),
                  pl.BlockSpec((1, V_DIM), lambda b: (0, 0)),
                  const(bias_c), const(bias_n)],
        out_specs=row_spec,
        out_shape=jax.ShapeDtypeStruct((nb * seq, D_ATT), BF16),
        compiler_params=_cparams(("arbitrary",)),
        name="attn_sample",
    )(q, k_new, v_new, cache_k, cache_v, lamv, g, bias_c, bias_n)


def _split3(x):
    hi = x.astype(BF16)
    r = x - hi.astype(F32)
    mid = r.astype(BF16)
    lo = (r - mid.astype(F32)).astype(BF16)
    return hi, mid, lo


def _ssd_chunk(xs, bm, cm, dt, z, s_ref, a_heads, dskip, gn, e3, es3, t3, ones_k, lc):
    seg_w = SSM_HEADS * lc
    half = D_SSM // SSM_GROUPS
    d3 = jnp.concatenate(_split3(dt), axis=1)
    dt_x = jnp.dot(d3, e3, preferred_element_type=F32)
    la3 = jnp.concatenate(_split3(dt * a_heads), axis=0)
    ac3 = jnp.concatenate(_split3(jnp.dot(t3, la3, preferred_element_type=F32)), axis=1)
    acol = jnp.dot(ac3, e3, preferred_element_type=F32)
    acol_s = acol if seg_w == D_SSM else jnp.dot(ac3, es3, preferred_element_type=F32)
    t_idx = lax.broadcasted_iota(jnp.int32, (lc, seg_w), 0)
    s_idx = lax.broadcasted_iota(jnp.int32, (lc, seg_w), 1) % lc
    arow = jnp.sum(jnp.where(t_idx == s_idx, acol_s, 0.0), axis=0, keepdims=True)
    decay = jnp.where(t_idx >= s_idx, jnp.exp(acol_s - arow), 0.0)

    cb16 = cm.astype(BF16)
    bb16 = bm.astype(BF16)
    hpg = SSM_HEADS // SSM_GROUPS
    cbs = []
    for g in range(SSM_GROUPS):
        gs = slice(g * SSM_STATE, (g + 1) * SSM_STATE)
        b_rep = jnp.concatenate([bb16[:, gs]] * hpg, axis=0)
        cbs.append(lax.dot_general(cb16[:, gs], b_rep, _NT, preferred_element_type=F32))
    mmat = (jnp.concatenate(cbs, axis=1) * decay).astype(BF16)

    xdt = xs * dt_x
    xdt16 = xdt.astype(BF16)
    hk = (2 * LANES) // lc
    wd = hk * SSM_HEAD_DIM
    blk = (lax.broadcasted_iota(jnp.int32, (hk * lc, wd), 0) // lc
           == lax.broadcasted_iota(jnp.int32, (hk * lc, wd), 1) // SSM_HEAD_DIM)
    parts = []
    for i in range(SSM_HEADS // hk):
        xd = xdt16[:, i * wd:(i + 1) * wd]
        bd = jnp.where(blk, jnp.concatenate([xd] * hk, axis=0), jnp.zeros((), BF16))
        parts.append(jnp.dot(mmat[:, i * hk * lc:(i + 1) * hk * lc], bd, preferred_element_type=F32))
    y_intra = jnp.concatenate(parts, axis=1) if len(parts) > 1 else parts[0]

    yi = []
    for g in range(SSM_GROUPS):
        sg = s_ref[g * half:(g + 1) * half, :].astype(BF16)
        yi.append(lax.dot_general(cb16[:, g * SSM_STATE:(g + 1) * SSM_STATE], sg, _NT,
                                  preferred_element_type=F32))
    y_inter = jnp.exp(acol) * jnp.concatenate(yi, axis=1)

    alast = acol[lc - 1:lc, :]
    dec_end = jnp.exp(alast - acol)
    xd_end = (xdt * dec_end).astype(BF16)
    krow = lax.broadcasted_iota(jnp.int32, (ones_k.shape[0], D_SSM), 0)
    a_hi, a_mid, a_lo = (piece.astype(F32) for piece in _split3(alast))
    pieces = jnp.where(krow == 0, a_hi, jnp.where(krow == 1, a_mid, jnp.where(krow == 2, a_lo, 0.0)))
    acl = lax.dot_general(pieces.astype(BF16), ones_k, _TN, preferred_element_type=F32)
    for g in range(SSM_GROUPS):
        rows = slice(g * half, (g + 1) * half)
        upd = lax.dot_general(xd_end[:, rows], bb16[:, g * SSM_STATE:(g + 1) * SSM_STATE], _TN,
                              preferred_element_type=F32)
        s_ref[rows, :] = jnp.exp(acl[rows, :]) * s_ref[rows, :] + upd

    y = (y_intra + y_inter + dskip * xs) * _silu(z)
    outs = []
    for g in range(SSM_GROUPS):
        cs = slice(g * half, (g + 1) * half)
        outs.append(_rms(y[:, cs], gn[:, cs]))
    return jnp.concatenate(outs, axis=1)


def _ssd_kernel(*refs, lc, nchunk, has_state):
    if has_state:
        (xs_ref, bc_ref, dt_ref, z_ref, prev_ref, h0_ref, cw_ref, cbias_ref, dtb_ref, alog_ref,
         dskip_ref, gn_ref, e3_ref, es3_ref, t3_ref, ones_ref,
         y_ref, s_ref, scr_x, scr_bc) = refs
    else:
        (xs_ref, bc_ref, dt_ref, z_ref, cw_ref, cbias_ref, dtb_ref, alog_ref,
         dskip_ref, gn_ref, e3_ref, es3_ref, t3_ref, ones_ref,
         y_ref, s_ref, scr_x, scr_bc) = refs
    rows = lc * nchunk
    pad = SUBLANES

    def conv_silu(scr_x, scr_bc, nrows):
        xc = cbias_ref[:, 0:D_SSM]
        bcc = cbias_ref[:, D_SSM:CONV_DIM]
        for tap in range(SSM_CONV):
            off = pad - (SSM_CONV - 1) + tap
            xc = xc + scr_x[off:off + nrows, :] * cw_ref[tap:tap + 1, 0:D_SSM]
            bcc = bcc + scr_bc[off:off + nrows, :] * cw_ref[tap:tap + 1, D_SSM:CONV_DIM]
        return _silu(xc), _silu(bcc)

    dt_in = dt_ref[...] + dtb_ref[...]
    dt = jnp.maximum(dt_in, 0.0) + jnp.log1p(jnp.exp(-jnp.abs(dt_in)))
    a_heads = -jnp.exp(alog_ref[...])
    nbm = SSM_GROUPS * SSM_STATE

    def chunk(c, xs_c, bc_c, s2d):
        rs = slice(c * lc, (c + 1) * lc)
        y = _ssd_chunk(xs_c, bc_c[:, 0:nbm], bc_c[:, nbm:2 * nbm], dt[rs],
                       z_ref[rs, :], s2d, a_heads, dskip_ref[...], gn_ref[...],
                       e3_ref[...], es3_ref[...], t3_ref[...], ones_ref[...], lc)
        y_ref[rs, :] = y.astype(y_ref.dtype)

    if has_state:
        for c in range(nchunk):
            rs = slice(c * lc, (c + 1) * lc)
            scr_x[c, 0:pad, :] = prev_ref[c, :, 0:D_SSM]
            scr_bc[c, 0:pad, :] = prev_ref[c, :, D_SSM:CONV_DIM]
            scr_x[c, pad:pad + lc, :] = xs_ref[rs, :]
            scr_bc[c, pad:pad + lc, :] = bc_ref[rs, :]
            s_ref[c] = h0_ref[c]
            xs_act, bc_act = conv_silu(scr_x.at[c], scr_bc.at[c], lc)
            chunk(c, xs_act, bc_act, s_ref.at[c])
    else:
        @pl.when(pl.program_id(0) == 0)
        def _():
            scr_x[0:pad, :] = jnp.zeros((pad, D_SSM), F32)
            scr_bc[0:pad, :] = jnp.zeros((pad, BC_DIM), F32)
            s_ref[...] = jnp.zeros(s_ref.shape, F32)

        scr_x[pad:pad + rows, :] = xs_ref[...]
        scr_bc[pad:pad + rows, :] = bc_ref[...]
        xs_act, bc_act = conv_silu(scr_x, scr_bc, rows)
        scr_x[0:pad, :] = scr_x[rows:rows + pad, :]
        scr_bc[0:pad, :] = scr_bc[rows:rows + pad, :]
        for c in range(nchunk):
            rs = slice(c * lc, (c + 1) * lc)
            chunk(c, xs_act[rs], bc_act[rs], s_ref)


def _ssd_constants(lc):
    seg_w = SSM_HEADS * lc
    head_of_lane = np.arange(D_SSM) // SSM_HEAD_DIM
    e = (np.arange(LANES)[:, None] == head_of_lane[None, :]).astype(np.float32)
    es = (np.arange(LANES)[:, None] == (np.arange(seg_w) // lc)[None, :]).astype(np.float32)
    tri = np.tril(np.ones((lc, lc), np.float32))
    return (jnp.asarray(np.concatenate([e] * 3, axis=0), BF16),
            jnp.asarray(np.concatenate([es] * 3, axis=0), BF16),
            jnp.asarray(np.concatenate([tri] * 3, axis=1), BF16),
            jnp.ones((2 * SUBLANES, SSM_STATE), BF16))


def _ssd(xs, bc, dt, z, conv_w, conv_b, dt_bias, a_log, d_skip, gn, lc, nchunk,
         conv_prev=None, h0=None):
    m = xs.shape[0]
    rows = lc * nchunk
    has_state = h0 is not None
    e3, es3, t3, ones_k = _ssd_constants(lc)
    dtb = jnp.zeros((1, LANES), F32).at[0, :SSM_HEADS].set(dt_bias)
    alog = jnp.zeros((1, LANES), F32).at[0, :SSM_HEADS].set(a_log)
    dskip_x = jnp.repeat(d_skip, SSM_HEAD_DIM)[None, :]

    def const(shape):
        return pl.BlockSpec(shape, lambda i: (0,) * len(shape))

    def rowblk(width):
        return pl.BlockSpec((rows, width), lambda i: (i, 0))

    in_specs = [rowblk(D_SSM), rowblk(BC_DIM), rowblk(LANES), rowblk(D_SSM)]
    args = [xs, bc, dt, z]
    if has_state:
        nb = h0.shape[0]
        in_specs += [pl.BlockSpec((nchunk, SUBLANES, CONV_DIM), lambda i: (i, 0, 0)),
                     pl.BlockSpec((nchunk, D_SSM, SSM_STATE), lambda i: (i, 0, 0))]
        args += [conv_prev, h0]
        s_shape = jax.ShapeDtypeStruct((nb, D_SSM, SSM_STATE), F32)
        s_spec = pl.BlockSpec((nchunk, D_SSM, SSM_STATE), lambda i: (i, 0, 0))
        scratch = [pltpu.VMEM((nchunk, lc + SUBLANES, D_SSM), F32),
                   pltpu.VMEM((nchunk, lc + SUBLANES, BC_DIM), F32)]
    else:
        s_shape = jax.ShapeDtypeStruct((D_SSM, SSM_STATE), F32)
        s_spec = const((D_SSM, SSM_STATE))
        scratch = [pltpu.VMEM((rows + SUBLANES, D_SSM), F32),
                   pltpu.VMEM((rows + SUBLANES, BC_DIM), F32)]
    in_specs += [const((SSM_CONV, CONV_DIM)), const((1, CONV_DIM)), const((1, LANES)),
                 const((1, LANES)), const((1, D_SSM)), const((1, D_SSM)),
                 const(e3.shape), const(es3.shape), const(t3.shape), const(ones_k.shape)]
    args += [conv_w, conv_b[None, :], dtb, alog, dskip_x, gn[None, :], e3, es3, t3, ones_k]
    return pl.pallas_call(
        functools.partial(_ssd_kernel, lc=lc, nchunk=nchunk, has_state=has_state),
        grid=(m // rows,),
        in_specs=in_specs,
        out_specs=(rowblk(D_SSM), s_spec),
        out_shape=(jax.ShapeDtypeStruct((m, D_SSM), BF16), s_shape),
        scratch_shapes=scratch,
        compiler_params=_cparams(("arbitrary",)),
        name="ssd_sample" if has_state else "ssd_prompt",
    )(*args)


def _out_proj_kernel(o_ref, y_ref, x_ref, w_ref, g_ref, x2_ref, hf_ref, w16_ref=None):
    if w16_ref is not None:
        w16_ref[...] = w_ref[...].astype(BF16)
        w_ref = w16_ref
    x2 = (x_ref[...]
          + jnp.dot(o_ref[...], w_ref[0:D_ATT, :], preferred_element_type=F32)
          + jnp.dot(y_ref[...], w_ref[D_ATT:D_ATT + D_SSM, :], preferred_element_type=F32))
    x2_ref[...] = x2
    hf_ref[...] = _rms(x2, g_ref[...]).astype(hf_ref.dtype)


def _out_proj(o, y, x, w, g, tm):
    m = x.shape[0]
    w_spec = pl.BlockSpec((D_ATT + D_SSM, D_MODEL), lambda i: (0, 0), pipeline_mode=pl.Buffered(1))
    out_specs = (pl.BlockSpec((tm, D_MODEL), lambda i: (i, 0)),
                 pl.BlockSpec((tm, D_MODEL), lambda i: (i, 0)))
    out_shape = (jax.ShapeDtypeStruct((m, D_MODEL), F32),
                 jax.ShapeDtypeStruct((m, D_MODEL), BF16))
    if w.dtype != BF16:
        assert m == tm, "the bf16 weight copy is written once: needs a single row tile"
        out_specs += (pl.BlockSpec((D_ATT + D_SSM, D_MODEL), lambda i: (0, 0)),)
        out_shape += (jax.ShapeDtypeStruct(w.shape, BF16),)
    return pl.pallas_call(
        _out_proj_kernel,
        grid=(m // tm,),
        in_specs=[
            pl.BlockSpec((tm, D_ATT), lambda i: (i, 0)),
            pl.BlockSpec((tm, D_SSM), lambda i: (i, 0)),
            pl.BlockSpec((tm, D_MODEL), lambda i: (i, 0)),
            w_spec,
            pl.BlockSpec((1, D_MODEL), lambda i: (0, 0)),
        ],
        out_specs=out_specs,
        out_shape=out_shape,
        compiler_params=_cparams(("arbitrary",)),
        name="out_proj",
    )(o, y, x, w, g)


def _ffn_kernel(*refs, tm, seq, final_norm):
    if seq is None:
        (hf_ref, x2_ref, wg_ref, wu_ref, wd_ref, cw_ref, cb_ref, gfin_ref,
         out_ref, gl_ref, g_scr, act_scr, carry_scr) = refs
    else:
        (hf_ref, x2_ref, wg_ref, wu_ref, wd_ref, cw_ref, cb_ref, gfin_ref, prev_ref, sel1_ref, sel2_ref,
         out_ref, gl_ref, wg16_ref, wu16_ref, wd16_ref, g_scr, act_scr) = refs
    i = pl.program_id(0)
    f = pl.program_id(1)
    nf = pl.num_programs(1) - 1
    pad = SUBLANES

    def weight(w_ref, w16_ref):
        if seq is None:
            return w_ref[...]
        w16 = w_ref[...].astype(BF16)
        w16_ref[...] = w16
        return w16

    def gate_up():
        hf = hf_ref[...]
        gate = jnp.dot(hf, weight(wg_ref, None if seq is None else wg16_ref), preferred_element_type=F32)
        up = jnp.dot(hf, weight(wu_ref, None if seq is None else wu16_ref), preferred_element_type=F32)
        g_scr[pad:pad + tm, :] = gate
        if seq is None:
            g_scr[0:pad, :] = carry_scr[f]
            carry_scr[f] = g_scr[tm:tm + pad, :]
            gl_ref[...] = g_scr[tm:tm + pad, :]
            g1 = g_scr[pad - 1:pad - 1 + tm, :]
            g2 = g_scr[pad - 2:pad - 2 + tm, :]
        else:
            g_scr[0:pad, :] = jnp.zeros((pad, gate.shape[1]), F32)
            gl_ref[...] = gate
            prev3 = jnp.concatenate(_split3(prev_ref[...]), axis=0)
            ov1 = jnp.dot(sel1_ref[...], prev3, preferred_element_type=F32)
            ov2 = jnp.dot(sel2_ref[...], prev3, preferred_element_type=F32)
            pos = lax.broadcasted_iota(jnp.int32, gate.shape, 0) % seq
            g1 = jnp.where(pos == 0, ov1, g_scr[pad - 1:pad - 1 + tm, :])
            g2 = jnp.where(pos < 2, ov2, g_scr[pad - 2:pad - 2 + tm, :])
        conv = cb_ref[...] + g2 * cw_ref[0:1, :] + g1 * cw_ref[1:2, :] + gate * cw_ref[2:3, :]
        act_scr[...] = (_silu(conv) * up).astype(BF16)

    def down():
        return jnp.dot(act_scr[...], weight(wd_ref, None if seq is None else wd16_ref),
                       preferred_element_type=F32)

    @pl.when(f == 0)
    def _():
        if seq is None:
            @pl.when(i == 0)
            def _():
                carry_scr[...] = jnp.zeros(carry_scr.shape, F32)
        out_ref[...] = x2_ref[...]
        gate_up()

    @pl.when((f > 0) & (f < nf))
    def _():
        contrib = down()
        gate_up()
        out_ref[...] += contrib

    @pl.when(f == nf)
    def _():
        res = out_ref[...] + down()
        out_ref[...] = _rms(res, gfin_ref[...]) if final_norm else res


def _ffn_prev_selectors(m, seq, nprev):
    sel1 = np.zeros((m, 3 * nprev), np.float32)
    sel2 = np.zeros((m, 3 * nprev), np.float32)
    nstate = FFN_CONV - 1
    for b in range(m // seq):
        for piece in range(3):
            base = piece * nprev + b * nstate
            sel1[b * seq, base + 1] = 1.0
            sel2[b * seq, base + 0] = 1.0
            sel2[b * seq + 1, base + 1] = 1.0
    return jnp.asarray(sel1, BF16), jnp.asarray(sel2, BF16)


def _ffn(hf, x2, wg, wu, wd, cw, cb, gfin, tm, tf, final_norm, seq=None, prev=None):
    m = hf.shape[0]
    nf = D_FF // tf
    cur = lambda f: jnp.minimum(f, nf - 1)
    last = lambda f: jnp.maximum(f - 1, 0)
    in_specs = [
        pl.BlockSpec((tm, D_MODEL), lambda i, f: (i, 0)),
        pl.BlockSpec((tm, D_MODEL), lambda i, f: (i, 0), pipeline_mode=pl.Buffered(1)),
        pl.BlockSpec((D_MODEL, tf), lambda i, f: (0, cur(f))),
        pl.BlockSpec((D_MODEL, tf), lambda i, f: (0, cur(f))),
        pl.BlockSpec((tf, D_MODEL), lambda i, f: (last(f), 0)),
        pl.BlockSpec((FFN_CONV, tf), lambda i, f: (0, cur(f))),
        pl.BlockSpec((1, tf), lambda i, f: (0, cur(f))),
        pl.BlockSpec((1, D_MODEL), lambda i, f: (0, 0)),
    ]
    args = [hf, x2, wg, wu, wd, cw, cb[None, :], gfin[None, :]]
    scratch = [pltpu.VMEM((tm + SUBLANES, tf), F32), pltpu.VMEM((tm, tf), BF16)]
    if seq is None:
        gl_shape = jax.ShapeDtypeStruct((m // tm * SUBLANES, D_FF), F32)
        gl_spec = pl.BlockSpec((SUBLANES, tf), lambda i, f: (i, cur(f)))
        scratch.append(pltpu.VMEM((nf, SUBLANES, tf), F32))
    else:
        assert m == tm, "sample FFN handles all sequences in one row tile"
        nprev = prev.shape[0]
        sel1, sel2 = _ffn_prev_selectors(m, seq, nprev)
        in_specs += [pl.BlockSpec((nprev, tf), lambda i, f: (0, cur(f))),
                     pl.BlockSpec(sel1.shape, lambda i, f: (0, 0)),
                     pl.BlockSpec(sel2.shape, lambda i, f: (0, 0))]
        args += [prev, sel1, sel2]
        gl_shape = jax.ShapeDtypeStruct((m, D_FF), F32)
        gl_spec = pl.BlockSpec((tm, tf), lambda i, f: (i, cur(f)))
    out_specs = [pl.BlockSpec((tm, D_MODEL), lambda i, f: (i, 0)), gl_spec]
    out_shape = [jax.ShapeDtypeStruct((m, D_MODEL), F32), gl_shape]
    if seq is not None:
        out_specs += [in_specs[2], in_specs[3], in_specs[4]]
        out_shape += [jax.ShapeDtypeStruct(wg.shape, BF16), jax.ShapeDtypeStruct(wu.shape, BF16),
                      jax.ShapeDtypeStruct(wd.shape, BF16)]
    return pl.pallas_call(
        functools.partial(_ffn_kernel, tm=tm, seq=seq, final_norm=final_norm),
        grid=(m // tm, nf + 1),
        in_specs=in_specs,
        out_specs=tuple(out_specs),
        out_shape=tuple(out_shape),
        scratch_shapes=scratch,
        compiler_params=_cparams(("arbitrary", "arbitrary")),
        name="ffn_sample" if seq is not None else "ffn_prompt",
    )(*args)


def _layer(x, w, lam_init, final_norm, gfin, *, batch, seq, state=None):
    m = x.shape[0]
    lamv = jnp.stack([w["lambda_q1"], w["lambda_k1"], w["lambda_q2"], w["lambda_k2"]])
    tm = min(512, m)
    w16 = {}
    q, k, v, z, xs, bc, dt, *extra = _in_proj(x, w["norm_mix_g"][None, :], w["w_in_main"], w["w_in_dt"],
                                              tm=min(1024, m))
    if extra:
        w16["w_in_main"] = extra[0]

    if state is None:
        slopes = jnp.broadcast_to(
            jnp.asarray(2.0 ** (-8.0 * np.arange(1, ATT_HEADS + 1) / ATT_HEADS), F32)[:, None, None],
            (ATT_HEADS, 1, LANES))
        o = _attn_prompt(q, k, v, slopes, lamv, w["attn_subln_g"][None, :], lam_init, tile=256)
        y, s_new = _ssd(xs, bc, dt, z, w["conv_w"], w["conv_b"], w["dt_bias"], w["a_log"],
                        w["d_skip"], w["ssm_norm_g"], lc=CHUNK, nchunk=4)
        conv_new = jnp.concatenate([xs[m - (SSM_CONV - 1):], bc[m - (SSM_CONV - 1):]], axis=-1)[None]
        s_new = s_new[None]
    else:
        cache_k, cache_v, conv_prev, ssm_prev, ffn_prev = state
        past = cache_k.shape[1]
        assert past % CHUNK == 0 and seq <= CHUNK
        o = _attn_sample(q, k, v, cache_k.reshape(batch, past * ATT_HEADS, V_DIM),
                         cache_v.reshape(batch, past * ATT_HEADS, V_DIM),
                         lamv, w["attn_subln_g"][None, :], lam_init, seq)
        prev8 = jnp.pad(conv_prev, ((0, 0), (SUBLANES - (SSM_CONV - 1), 0), (0, 0)))
        y, s_new = _ssd(xs, bc, dt, z, w["conv_w"], w["conv_b"], w["dt_bias"], w["a_log"],
                        w["d_skip"], w["ssm_norm_g"], lc=seq, nchunk=_SAMPLE_SEQS_PER_STEP, conv_prev=prev8,
                        h0=ssm_prev.reshape(batch, D_SSM, SSM_STATE))
        conv_new = jnp.concatenate([xs.reshape(batch, seq, D_SSM)[:, seq - (SSM_CONV - 1):],
                                    bc.reshape(batch, seq, BC_DIM)[:, seq - (SSM_CONV - 1):]], axis=-1)

    x2, hf, *extra = _out_proj(o, y, x, w["w_out"], w["norm_ffn_g"][None, :], tm=tm)
    if extra:
        w16["w_out"] = extra[0]

    ffn_args = (hf, x2, w["w_gate"], w["w_up"], w["w_down"], w["ffn_conv_w"], w["ffn_conv_b"], gfin)
    if state is None:
        x3, gl = _ffn(*ffn_args, tm=min(1024, m), tf=512, final_norm=final_norm)
        ffn_new = gl[None, gl.shape[0] - (FFN_CONV - 1):]
    else:
        x3, gl, w16["w_gate"], w16["w_up"], w16["w_down"] = _ffn(
            *ffn_args, tm=tm, tf=256, final_norm=final_norm, seq=seq,
            prev=ffn_prev.reshape(batch * (FFN_CONV - 1), D_FF))
        ffn_new = gl.reshape(batch, seq, D_FF)[:, seq - (FFN_CONV - 1):]
    k_new = k.reshape(batch, seq, ATT_HEADS, 2 * QK_DIM)
    v_new = v.reshape(batch, seq, ATT_HEADS, V_DIM)
    s_new = s_new.reshape(batch, SSM_HEADS, SSM_HEAD_DIM, SSM_STATE)
    return (x3, k_new, v_new, conv_new, s_new, ffn_new), w16


def kernel(x_prompt, x_sample, cache_k, cache_v, state_ssm_conv, state_ssm, state_ffn_conv, norm_mix_g, w_in, lambda_q1, lambda_k1, lambda_q2, lambda_k2, attn_subln_g, conv_w, conv_b, dt_bias, a_log, d_skip, ssm_norm_g, w_out, norm_ffn_g, w_gate, w_up, ffn_conv_w, ffn_conv_b, w_down, norm_final_g):
    depth = w_in.shape[0]
    pb, pl_len, _ = x_prompt.shape
    sb, sl_len, _ = x_sample.shape
    assert pb == 1 and pl_len % CHUNK == 0
    xp = x_prompt.reshape(pb * pl_len, D_MODEL)
    xs = x_sample.reshape(sb * sl_len, D_MODEL)
    n_main = 2 * D_ATT + D_ATT + D_SSM + CONV_DIM
    outs_p, outs_s = [], []
    for layer in range(depth):
        lam_init = 0.8 - 0.6 * math.exp(-0.3 * layer)
        w_in_t = jnp.swapaxes(w_in[layer], 0, 1)
        w = dict(
            norm_mix_g=norm_mix_g[layer],
            w_in_main=w_in_t,
            w_in_dt=jnp.pad(w_in_t[n_main:].astype(BF16), ((0, LANES - SSM_HEADS), (0, 0))),
            lambda_q1=lambda_q1[layer], lambda_k1=lambda_k1[layer],
            lambda_q2=lambda_q2[layer], lambda_k2=lambda_k2[layer],
            attn_subln_g=attn_subln_g[layer], conv_w=conv_w[layer], conv_b=conv_b[layer],
            dt_bias=dt_bias[layer], a_log=a_log[layer], d_skip=d_skip[layer],
            ssm_norm_g=ssm_norm_g[layer], w_out=w_out[layer],
            norm_ffn_g=norm_ffn_g[layer], w_gate=w_gate[layer],
            w_up=w_up[layer], ffn_conv_w=ffn_conv_w[layer],
            ffn_conv_b=ffn_conv_b[layer], w_down=w_down[layer],
        )
        last = layer == depth - 1
        (xs, *new_s), w16 = _layer(xs, w, lam_init, last, norm_final_g, batch=sb, seq=sl_len,
                                   state=(cache_k[layer], cache_v[layer], state_ssm_conv[layer],
                                          state_ssm[layer], state_ffn_conv[layer]))
        (xp, *new_p), _ = _layer(xp, {**w, **w16}, lam_init, last, norm_final_g, batch=pb, seq=pl_len)
        outs_p.append(new_p)
        outs_s.append(new_s)
    stack = lambda outs, idx: jnp.stack([o[idx] for o in outs])
    return (xp.reshape(pb, pl_len, D_MODEL), xs.reshape(sb, sl_len, D_MODEL),
            *[stack(outs_p, idx) for idx in range(5)],
            *[stack(outs_s, idx) for idx in range(5)])
```

```python
import functools
import math

import jax
import jax.numpy as jnp
import numpy as np
from jax import lax
from jax.experimental import pallas as pl
from jax.experimental.pallas import tpu as pltpu

F32 = jnp.float32
BF16 = jnp.bfloat16

D_MODEL = 2048
CHUNK = 64
ATT_HEADS = 8
QK_DIM = 64
V_DIM = 128
D_ATT = ATT_HEADS * V_DIM
SSM_HEADS = 16
SSM_HEAD_DIM = 64
D_SSM = SSM_HEADS * SSM_HEAD_DIM
SSM_GROUPS = 2
SSM_STATE = 128
SSM_CONV = 4
BC_DIM = 2 * SSM_GROUPS * SSM_STATE
CONV_DIM = D_SSM + BC_DIM
D_FF = 5632
FFN_CONV = 3
EPS = 1e-6
LOG2E = math.log2(math.e)
Q_SCALE = QK_DIM ** -0.5 * LOG2E
LANES = 128
SUBLANES = 8
VMEM_LIMIT = 56 * 1024 * 1024

_NT = (((1,), (1,)), ((), ()))
_TN = (((0,), (0,)), ((), ()))


def _cparams(sem):
    return pltpu.CompilerParams(dimension_semantics=sem, vmem_limit_bytes=VMEM_LIMIT)


def _silu(x):
    return x * jax.nn.sigmoid(x)


def _rms(x, g):
    return x * lax.rsqrt(jnp.mean(x * x, axis=-1, keepdims=True) + EPS) * g


_PROJ_TN = 512
_PROJ_SEGS = ((0, 2), (2, 4), (4, 6), (6, 8), (8, 10), (10, 11))


def _in_proj_kernel(x_ref, g_ref, w_ref, wdt_ref,
                    q_ref, k_ref, v_ref, z_ref, xs_ref, bc_ref, dt_ref, *rest):
    w16_ref, h_scr = rest if len(rest) == 2 else (None, rest[0])
    j = pl.program_id(1)

    @pl.when(j == 0)
    def _():
        hb = _rms(x_ref[...], g_ref[...]).astype(BF16)
        h_scr[...] = hb
        dt_ref[...] = lax.dot_general(hb, wdt_ref[...], _NT, preferred_element_type=F32)

    outs = (q_ref, k_ref, v_ref, z_ref, xs_ref, bc_ref)
    for (lo, hi), ref in zip(_PROJ_SEGS, outs):
        @pl.when((j >= lo) & (j < hi))
        def _(ref=ref):
            w = w_ref[...]
            if w16_ref is not None:
                w = w.astype(BF16)
                w16_ref[...] = w
            res = lax.dot_general(h_scr[...], w, _NT, preferred_element_type=F32)
            val = res * Q_SCALE if ref is q_ref else res
            ref[...] = val.astype(ref.dtype)


def _in_proj(x, g, w_main, w_dt, tm):
    m = x.shape[0]
    tn = _PROJ_TN
    nj = w_main.shape[0] // tn

    def seg_spec(lo, hi):
        return pl.BlockSpec((tm, tn), lambda i, j: (i, jnp.clip(j - lo, 0, hi - lo - 1)))

    out_shape = (
        jax.ShapeDtypeStruct((m, D_ATT), BF16),
        jax.ShapeDtypeStruct((m, D_ATT), F32),
        jax.ShapeDtypeStruct((m, D_ATT), F32),
        jax.ShapeDtypeStruct((m, D_SSM), F32),
        jax.ShapeDtypeStruct((m, D_SSM), F32),
        jax.ShapeDtypeStruct((m, BC_DIM), F32),
        jax.ShapeDtypeStruct((m, LANES), F32),
    )
    out_specs = tuple(seg_spec(lo, hi) for lo, hi in _PROJ_SEGS) + (
        pl.BlockSpec((tm, LANES), lambda i, j: (i, 0)),)
    if w_main.dtype != BF16:
        assert m == tm, "the bf16 weight copy is written once per tile: needs a single row tile"
        out_shape += (jax.ShapeDtypeStruct((nj * tn, D_MODEL), BF16),)
        out_specs += (pl.BlockSpec((tn, D_MODEL), lambda i, j: (j, 0)),)
    return pl.pallas_call(
        _in_proj_kernel,
        grid=(m // tm, nj),
        in_specs=[
            pl.BlockSpec((tm, D_MODEL), lambda i, j: (i, 0)),
            pl.BlockSpec((1, D_MODEL), lambda i, j: (0, 0)),
            pl.BlockSpec((tn, D_MODEL), lambda i, j: (j, 0)),
            pl.BlockSpec((LANES, D_MODEL), lambda i, j: (0, 0)),
        ],
        out_specs=out_specs,
        out_shape=out_shape,
        scratch_shapes=[pltpu.VMEM((tm, D_MODEL), BF16)],
        compiler_params=_cparams(("arbitrary", "arbitrary")),
        name="in_proj",
    )(x, g, w_main, w_dt)


def _lambda_value(lamv_ref, lam_init):
    lv = lamv_ref[...]
    s1 = jnp.sum(lv[0:1] * lv[1:2], axis=-1, keepdims=True)
    s2 = jnp.sum(lv[2:3] * lv[3:4], axis=-1, keepdims=True)
    return jnp.exp(s1) - jnp.exp(s2) + lam_init


def _split_maps(q):
    lane = lax.broadcasted_iota(jnp.int32, q.shape, 1)
    zero = jnp.zeros_like(q)
    return jnp.where(lane < QK_DIM, q, zero), jnp.where(lane >= QK_DIM, q, zero)


_VT_CHUNK = 512
_VT_PAD = 16
_UNROLL_SHIFT = 2
_UNROLL = 1 << _UNROLL_SHIFT
_HEADS_PER_STEP = 2
_QTILES_PER_GROUP = 4


def _attn_prompt_kernel(q_ref, k_ref, v_ref, slope_ref, lamv_ref, g_ref, dmask_ref, dneg_ref, o_ref,
                        k_scr, vt_scr, acc_scr, t0_scr, t1_scr, dbias_scr, stat_scr,
                        *, tile, lam_init):
    length = k_ref.shape[0]
    width = 2 * tile
    nheads = k_scr.shape[0]
    lam = _lambda_value(lamv_ref, lam_init)

    lane = lax.broadcasted_iota(jnp.int32, (length, LANES), 1)
    koff = lax.broadcasted_iota(jnp.int32, (length, LANES), 0) % tile
    koff = jnp.where(lane < 3, koff, 0).astype(F32).astype(BF16)
    extra = (lax.broadcasted_iota(jnp.int32, (_VT_PAD, tile), 0) == 0).astype(F32).astype(BF16)
    brow = lax.broadcasted_iota(jnp.int32, (LANES, width), 0)
    per = _VT_CHUNK // tile
    ws, q_biases = [], []
    for hh in range(nheads):
        cols = slice(hh * V_DIM, (hh + 1) * V_DIM)
        k_scr[hh, :, 0:V_DIM] = k_ref[:, cols].astype(BF16)
        k_scr[hh, :, V_DIM:V_DIM + LANES] = koff
        for c in range(length // _VT_CHUNK):
            vt = v_ref[c * _VT_CHUNK:(c + 1) * _VT_CHUNK, cols].T.astype(BF16)
            for s in range(per):
                vt_scr[hh, c * per + s, 0:V_DIM, :] = vt[:, s * tile:(s + 1) * tile]
                vt_scr[hh, c * per + s, V_DIM:V_DIM + _VT_PAD, :] = extra
        w = slope_ref[hh][:, :1] * LOG2E
        w_hi, w_mid, w_lo = (piece.astype(F32) for piece in _split3(w))
        q_bias = jnp.where(brow == 0, w_hi, jnp.where(brow == 1, w_mid, jnp.where(brow == 2, w_lo, 0.0)))
        ws.append(w)
        q_biases.append(q_bias.astype(BF16))
        dbias_scr[hh] = w * dmask_ref[...] + dneg_ref[...]

    def q_pair(qp, _):
        _attn_prompt_pair(qp, ws, lam, q_biases, q_ref, g_ref, o_ref,
                          k_scr, vt_scr, acc_scr, (t0_scr, t1_scr),
                          dbias_scr, stat_scr, tile=tile, lam_init=lam_init)
        return 0

    lax.fori_loop(0, length // (_QTILES_PER_GROUP * tile), q_pair, 0)


def _attn_prompt_pair(qp, ws, lam, q_biases, q_ref, g_ref, o_ref,
                      k_scr, vt_scr, acc_scr, t_slots, dbias_scr, stat_scr, *, tile, lam_init):
    width = 2 * tile
    nheads = len(ws)
    group = _QTILES_PER_GROUP
    streams = [(hh, u) for hh in range(nheads) for u in range(group)]
    nfull = group * qp
    qis = [nfull + u for _, u in streams]
    rows = [pl.ds(pl.multiple_of(qi * tile, tile), tile) for qi in qis]
    q_augs = []
    for s, (hh, _) in enumerate(streams):
        q1, q2 = _split_maps(q_ref[rows[s], hh * V_DIM:(hh + 1) * V_DIM])
        q_t = jnp.concatenate([q1, q2], axis=0).astype(F32).T.astype(BF16)
        q_augs.append(jnp.concatenate([q_t, q_biases[hh]], axis=0))

    def scores(s, j):
        k0 = pl.multiple_of(jnp.minimum(j, qis[s]) * tile, tile)
        return jnp.dot(k_scr[streams[s][0], pl.ds(k0, tile), :], q_augs[s], preferred_element_type=F32)

    TMAX = 0

    def step(s, j, slot, m_old):
        hh = streams[s][0]
        off = -ws[hh] * jnp.asarray((qis[s] - j) * tile, F32)
        m_new = jnp.maximum(m_old, stat_scr[s, TMAX:TMAX + 1, :] + off)
        t_next = scores(s, j + 1)
        t_slots[1 - slot][s] = t_next
        stat_scr[s, TMAX:TMAX + 1, :] = jnp.max(t_next, axis=0, keepdims=True)
        p = jnp.exp2(t_slots[slot][s] + (off - m_new)).astype(BF16)
        pv = jnp.dot(vt_scr[hh, j], p, preferred_element_type=F32)
        acc_scr[s] = jnp.exp2(m_old - m_new) * acc_scr[s] + pv
        return m_new

    def run(first, ntiles_per_iter, niter, carry):
        def body(i, carry):
            for k in range(ntiles_per_iter):
                carry = tuple(step(s, first + ntiles_per_iter * i + k, k % 2, carry[s])
                              for s in range(len(streams)))
            return carry
        return lax.fori_loop(0, niter, body, carry)

    carry = []
    for s in range(len(streams)):
        acc_scr[s] = jnp.zeros(acc_scr.shape[1:], F32)
        t_first = scores(s, 0)
        t_slots[0][s] = t_first
        stat_scr[s, TMAX:TMAX + 1, :] = jnp.max(t_first, axis=0, keepdims=True)
        carry.append(jnp.full((1, width), -0.5 * float(jnp.finfo(F32).max), F32))
    n_main = lax.shift_right_logical(nfull, _UNROLL_SHIFT)
    done = n_main * _UNROLL
    carry = run(0, _UNROLL, n_main, tuple(carry))
    carry = run(done, 2, lax.shift_right_logical(nfull - done, 1), carry)

    for s, (hh, u) in enumerate(streams):
        m_old = carry[s]
        diag_slot = 0
        for e in range(u):
            m_old = step(s, nfull + e, diag_slot, m_old)
            diag_slot = 1 - diag_slot
        t = t_slots[diag_slot][s] + dbias_scr[hh]
        m_new = jnp.maximum(m_old, jnp.max(t, axis=0, keepdims=True))
        p = jnp.exp2(t - m_new).astype(BF16)
        acc = (jnp.exp2(m_old - m_new) * acc_scr[s]
               + jnp.dot(vt_scr[hh, qis[s]], p, preferred_element_type=F32))

        on = acc[0:V_DIM] / acc[V_DIM:V_DIM + 1]
        o_t = on[:, :tile] - lam * on[:, tile:]
        o_t = o_t * lax.rsqrt(jnp.mean(o_t * o_t, axis=0, keepdims=True) + EPS)
        o_ref[rows[s], hh * V_DIM:(hh + 1) * V_DIM] = (
            o_t.T * g_ref[...] * (1.0 - lam_init)).astype(o_ref.dtype)


def _attn_prompt(q, k, v, slopes, lamv, g, lam_init, tile):
    length = q.shape[0]
    key = np.arange(tile)[:, None]
    qry = np.tile(np.arange(tile), 2)[None, :]
    dmask = jnp.asarray((qry - np.abs(qry - key)) - key, F32)
    dneg = jnp.asarray(np.where(key // CHUNK <= qry // CHUNK, 0.0, -np.inf), F32)
    nh = _HEADS_PER_STEP
    nstream = nh * _QTILES_PER_GROUP
    diag_spec = pl.BlockSpec((tile, 2 * tile), lambda h: (0, 0), pipeline_mode=pl.Buffered(1))
    head_spec = pl.BlockSpec((length, nh * V_DIM), lambda h: (0, h), pipeline_mode=pl.Buffered(1))
    kv_spec = pl.BlockSpec((length, nh * V_DIM), lambda h: (0, h), pipeline_mode=pl.Buffered(1))
    return pl.pallas_call(
        functools.partial(_attn_prompt_kernel, tile=tile, lam_init=lam_init),
        grid=(ATT_HEADS // nh,),
        in_specs=[
            kv_spec, kv_spec, kv_spec,
            pl.BlockSpec((nh, 1, LANES), lambda h: (h, 0, 0)),
            pl.BlockSpec((4, QK_DIM), lambda h: (0, 0)),
            pl.BlockSpec((1, V_DIM), lambda h: (0, 0)),
            diag_spec, diag_spec,
        ],
        out_specs=head_spec,
        out_shape=jax.ShapeDtypeStruct((length, D_ATT), BF16),
        scratch_shapes=[
            pltpu.VMEM((nh, length, V_DIM + LANES), BF16),
            pltpu.VMEM((nh, length // tile, V_DIM + _VT_PAD, tile), BF16),
            pltpu.VMEM((nstream, V_DIM + _VT_PAD, 2 * tile), F32),
            pltpu.VMEM((nstream, tile, 2 * tile), F32),
            pltpu.VMEM((nstream, tile, 2 * tile), F32),
            pltpu.VMEM((nh, tile, 2 * tile), F32),
            pltpu.VMEM((nstream, SUBLANES, 2 * tile), F32),
        ],
        compiler_params=_cparams(("arbitrary",)),
        name="attn_prompt",
    )(q, k, v, slopes, lamv, g, dmask, dneg)


_SAMPLE_SEQS_PER_STEP = 2


def _attn_sample_kernel(q_ref, kn_ref, vn_ref, ck_ref, cv_ref, lamv_ref, g_ref, bias_c_ref, bias_n_ref,
                        o_ref, *, seq, past, lam_init):
    lam = _lambda_value(lamv_ref, lam_init)
    rows = 2 * seq
    for bi in range(ck_ref.shape[0]):
        br = slice(bi * seq, (bi + 1) * seq)
        sc, sn = [], []
        for h in range(ATT_HEADS):
            sl = slice(h * V_DIM, (h + 1) * V_DIM)
            q1, q2 = _split_maps(q_ref[br, sl])
            qq = jnp.concatenate([q1, q2], axis=0)
            kc = ck_ref[bi, pl.ds(h, past, stride=ATT_HEADS), :].astype(BF16)
            sc.append(lax.dot_general(qq, kc, _NT, preferred_element_type=F32))
            sn.append(lax.dot_general(qq, kn_ref[br, sl].astype(BF16), _NT, preferred_element_type=F32))
        sc = jnp.concatenate(sc, axis=0) - bias_c_ref[...]
        sn = jnp.concatenate(sn, axis=0) - bias_n_ref[...]
        m = jnp.maximum(jnp.max(sc, axis=-1, keepdims=True), jnp.max(sn, axis=-1, keepdims=True))
        pc = jnp.exp2(sc - m)
        pn = jnp.exp2(sn - m)
        inv_l = 1.0 / (jnp.sum(pc, axis=-1, keepdims=True) + jnp.sum(pn, axis=-1, keepdims=True))
        pc = pc.astype(BF16)
        pn = pn.astype(BF16)
        for h in range(ATT_HEADS):
            sl = slice(h * V_DIM, (h + 1) * V_DIM)
            hr = slice(h * rows, (h + 1) * rows)
            vc = cv_ref[bi, pl.ds(h, past, stride=ATT_HEADS), :].astype(BF16)
            acc = (jnp.dot(pc[hr], vc, preferred_element_type=F32)
                   + jnp.dot(pn[hr], vn_ref[br, sl].astype(BF16), preferred_element_type=F32))
            on = acc * inv_l[hr]
            o = on[:seq] - lam * on[seq:]
            o_ref[br, sl] = (_rms(o, g_ref[...]) * (1.0 - lam_init)).astype(o_ref.dtype)


def _attn_sample(q, k_new, v_new, cache_k, cache_v, lamv, g, lam_init, seq):
    nb, past = cache_k.shape[0], cache_k.shape[1] // ATT_HEADS
    slope = np.repeat(2.0 ** (-8.0 * np.arange(1, ATT_HEADS + 1) / ATT_HEADS), 2 * seq)[:, None] * LOG2E
    qpos = np.tile(np.arange(seq), 2 * ATT_HEADS)[:, None]
    bias_c = jnp.asarray(slope * (qpos + past - np.arange(past)[None, :]), F32)
    bias_n = jnp.asarray(slope * np.abs(qpos - np.arange(seq)[None, :]), F32)
    per = _SAMPLE_SEQS_PER_STEP
    row_spec = pl.BlockSpec((per * seq, D_ATT), lambda b: (b, 0))
    cache_spec = pl.BlockSpec((per, past * ATT_HEADS, V_DIM), lambda b: (b, 0, 0))

    def const(arr):
        return pl.BlockSpec(arr.shape, lambda b: (0, 0), pipeline_mode=pl.Buffered(1))

    return pl.pallas_call(
        functools.partial(_attn_sample_kernel, seq=seq, past=past, lam_init=lam_init),
        grid=(nb // per,),
        in_specs=[row_spec, row_spec, row_spec, cache_spec, cache_spec,
                  pl.BlockSpec((4, QK_DIM), lambda b: (0, 0)),
                  pl.BlockSpec((1, V_DIM), lambda b: (0, 0)),
                  const(bias_c), const(bias_n)],
        out_specs=row_spec,
        out_shape=jax.ShapeDtypeStruct((nb * seq, D_ATT), BF16),
        compiler_params=_cparams(("arbitrary",)),
        name="attn_sample",
    )(q, k_new, v_new, cache_k, cache_v, lamv, g, bias_c, bias_n)


def _split3(x):
    hi = x.astype(BF16)
    r = x - hi.astype(F32)
    mid = r.astype(BF16)
    lo = (r - mid.astype(F32)).astype(BF16)
    return hi, mid, lo


def _ssd_chunk(xs, bm, cm, dt, z, s_ref, a_heads, dskip, gn, e3, es3, t3, ones_k, lc):
    seg_w = SSM_HEADS * lc
    half = D_SSM // SSM_GROUPS
    d3 = jnp.concatenate(_split3(dt), axis=1)
    dt_x = jnp.dot(d3, e3, preferred_element_type=F32)
    la3 = jnp.concatenate(_split3(dt * a_heads), axis=0)
    ac3 = jnp.concatenate(_split3(jnp.dot(t3, la3, preferred_element_type=F32)), axis=1)
    acol = jnp.dot(ac3, e3, preferred_element_type=F32)
    acol_s = acol if seg_w == D_SSM else jnp.dot(ac3, es3, preferred_element_type=F32)
    t_idx = lax.broadcasted_iota(jnp.int32, (lc, seg_w), 0)
    s_idx = lax.broadcasted_iota(jnp.int32, (lc, seg_w), 1) % lc
    arow = jnp.sum(jnp.where(t_idx == s_idx, acol_s, 0.0), axis=0, keepdims=True)
    decay = jnp.where(t_idx >= s_idx, jnp.exp(acol_s - arow), 0.0)

    cb16 = cm.astype(BF16)
    bb16 = bm.astype(BF16)
    hpg = SSM_HEADS // SSM_GROUPS
    cbs = []
    for g in range(SSM_GROUPS):
        gs = slice(g * SSM_STATE, (g + 1) * SSM_STATE)
        b_rep = jnp.concatenate([bb16[:, gs]] * hpg, axis=0)
        cbs.append(lax.dot_general(cb16[:, gs], b_rep, _NT, preferred_element_type=F32))
    mmat = (jnp.concatenate(cbs, axis=1) * decay).astype(BF16)

    xdt = xs * dt_x
    xdt16 = xdt.astype(BF16)
    hk = (2 * LANES) // lc
    wd = hk * SSM_HEAD_DIM
    blk = (lax.broadcasted_iota(jnp.int32, (hk * lc, wd), 0) // lc
           == lax.broadcasted_iota(jnp.int32, (hk * lc, wd), 1) // SSM_HEAD_DIM)
    parts = []
    for i in range(SSM_HEADS // hk):
        xd = xdt16[:, i * wd:(i + 1) * wd]
        bd = jnp.where(blk, jnp.concatenate([xd] * hk, axis=0), jnp.zeros((), BF16))
        parts.append(jnp.dot(mmat[:, i * hk * lc:(i + 1) * hk * lc], bd, preferred_element_type=F32))
    y_intra = jnp.concatenate(parts, axis=1) if len(parts) > 1 else parts[0]

    yi = []
    for g in range(SSM_GROUPS):
        sg = s_ref[g * half:(g + 1) * half, :].astype(BF16)
        yi.append(lax.dot_general(cb16[:, g * SSM_STATE:(g + 1) * SSM_STATE], sg, _NT,
                                  preferred_element_type=F32))
    y_inter = jnp.exp(acol) * jnp.concatenate(yi, axis=1)

    alast = acol[lc - 1:lc, :]
    dec_end = jnp.exp(alast - acol)
    xd_end = (xdt * dec_end).astype(BF16)
    krow = lax.broadcasted_iota(jnp.int32, (ones_k.shape[0], D_SSM), 0)
    a_hi, a_mid, a_lo = (piece.astype(F32) for piece in _split3(alast))
    pieces = jnp.where(krow == 0, a_hi, jnp.where(krow == 1, a_mid, jnp.where(krow == 2, a_lo, 0.0)))
    acl = lax.dot_general(pieces.astype(BF16), ones_k, _TN, preferred_element_type=F32)
    for g in range(SSM_GROUPS):
        rows = slice(g * half, (g + 1) * half)
        upd = lax.dot_general(xd_end[:, rows], bb16[:, g * SSM_STATE:(g + 1) * SSM_STATE], _TN,
                              preferred_element_type=F32)
        s_ref[rows, :] = jnp.exp(acl[rows, :]) * s_ref[rows, :] + upd

    y = (y_intra + y_inter + dskip * xs) * _silu(z)
    outs = []
    for g in range(SSM_GROUPS):
        cs = slice(g * half, (g + 1) * half)
        outs.append(_rms(y[:, cs], gn[:, cs]))
    return jnp.concatenate(outs, axis=1)


def _ssd_kernel(*refs, lc, nchunk, has_state):
    if has_state:
        (xs_ref, bc_ref, dt_ref, z_ref, prev_ref, h0_ref, cw_ref, cbias_ref, dtb_ref, alog_ref,
         dskip_ref, gn_ref, e3_ref, es3_ref, t3_ref, ones_ref,
         y_ref, s_ref, scr_x, scr_bc) = refs
    else:
        (xs_ref, bc_ref, dt_ref, z_ref, cw_ref, cbias_ref, dtb_ref, alog_ref,
         dskip_ref, gn_ref, e3_ref, es3_ref, t3_ref, ones_ref,
         y_ref, s_ref, scr_x, scr_bc) = refs
    rows = lc * nchunk
    pad = SUBLANES

    def conv_silu(scr_x, scr_bc, nrows):
        xc = cbias_ref[:, 0:D_SSM]
        bcc = cbias_ref[:, D_SSM:CONV_DIM]
        for tap in range(SSM_CONV):
            off = pad - (SSM_CONV - 1) + tap
            xc = xc + scr_x[off:off + nrows, :] * cw_ref[tap:tap + 1, 0:D_SSM]
            bcc = bcc + scr_bc[off:off + nrows, :] * cw_ref[tap:tap + 1, D_SSM:CONV_DIM]
        return _silu(xc), _silu(bcc)

    dt_in = dt_ref[...] + dtb_ref[...]
    dt = jnp.maximum(dt_in, 0.0) + jnp.log1p(jnp.exp(-jnp.abs(dt_in)))
    a_heads = -jnp.exp(alog_ref[...])
    nbm = SSM_GROUPS * SSM_STATE

    def chunk(c, xs_c, bc_c, s2d):
        rs = slice(c * lc, (c + 1) * lc)
        y = _ssd_chunk(xs_c, bc_c[:, 0:nbm], bc_c[:, nbm:2 * nbm], dt[rs],
                       z_ref[rs, :], s2d, a_heads, dskip_ref[...], gn_ref[...],
                       e3_ref[...], es3_ref[...], t3_ref[...], ones_ref[...], lc)
        y_ref[rs, :] = y.astype(y_ref.dtype)

    if has_state:
        for c in range(nchunk):
            rs = slice(c * lc, (c + 1) * lc)
            scr_x[c, 0:pad, :] = prev_ref[c, :, 0:D_SSM]
            scr_bc[c, 0:pad, :] = prev_ref[c, :, D_SSM:CONV_DIM]
            scr_x[c, pad:pad + lc, :] = xs_ref[rs, :]
            scr_bc[c, pad:pad + lc, :] = bc_ref[rs, :]
            s_ref[c] = h0_ref[c]
            xs_act, bc_act = conv_silu(scr_x.at[c], scr_bc.at[c], lc)
            chunk(c, xs_act, bc_act, s_ref.at[c])
    else:
        @pl.when(pl.program_id(0) == 0)
        def _():
            scr_x[0:pad, :] = jnp.zeros((pad, D_SSM), F32)
            scr_bc[0:pad, :] = jnp.zeros((pad, BC_DIM), F32)
            s_ref[...] = jnp.zeros(s_ref.shape, F32)

        scr_x[pad:pad + rows, :] = xs_ref[...]
        scr_bc[pad:pad + rows, :] = bc_ref[...]
        xs_act, bc_act = conv_silu(scr_x, scr_bc, rows)
        scr_x[0:pad, :] = scr_x[rows:rows + pad, :]
        scr_bc[0:pad, :] = scr_bc[rows:rows + pad, :]
        for c in range(nchunk):
            rs = slice(c * lc, (c + 1) * lc)
            chunk(c, xs_act[rs], bc_act[rs], s_ref)


def _ssd_constants(lc):
    seg_w = SSM_HEADS * lc
    head_of_lane = np.arange(D_SSM) // SSM_HEAD_DIM
    e = (np.arange(LANES)[:, None] == head_of_lane[None, :]).astype(np.float32)
    es = (np.arange(LANES)[:, None] == (np.arange(seg_w) // lc)[None, :]).astype(np.float32)
    tri = np.tril(np.ones((lc, lc), np.float32))
    return (jnp.asarray(np.concatenate([e] * 3, axis=0), BF16),
            jnp.asarray(np.concatenate([es] * 3, axis=0), BF16),
            jnp.asarray(np.concatenate([tri] * 3, axis=1), BF16),
            jnp.ones((2 * SUBLANES, SSM_STATE), BF16))


def _ssd(xs, bc, dt, z, conv_w, conv_b, dt_bias, a_log, d_skip, gn, lc, nchunk,
         conv_prev=None, h0=None):
    m = xs.shape[0]
    rows = lc * nchunk
    has_state = h0 is not None
    e3, es3, t3, ones_k = _ssd_constants(lc)
    dtb = jnp.zeros((1, LANES), F32).at[0, :SSM_HEADS].set(dt_bias)
    alog = jnp.zeros((1, LANES), F32).at[0, :SSM_HEADS].set(a_log)
    dskip_x = jnp.repeat(d_skip, SSM_HEAD_DIM)[None, :]

    def const(shape):
        return pl.BlockSpec(shape, lambda i: (0,) * len(shape))

    def rowblk(width):
        return pl.BlockSpec((rows, width), lambda i: (i, 0))

    in_specs = [rowblk(D_SSM), rowblk(BC_DIM), rowblk(LANES), rowblk(D_SSM)]
    args = [xs, bc, dt, z]
    if has_state:
        nb = h0.shape[0]
        in_specs += [pl.BlockSpec((nchunk, SUBLANES, CONV_DIM), lambda i: (i, 0, 0)),
                     pl.BlockSpec((nchunk, D_SSM, SSM_STATE), lambda i: (i, 0, 0))]
        args += [conv_prev, h0]
        s_shape = jax.ShapeDtypeStruct((nb, D_SSM, SSM_STATE), F32)
        s_spec = pl.BlockSpec((nchunk, D_SSM, SSM_STATE), lambda i: (i, 0, 0))
        scratch = [pltpu.VMEM((nchunk, lc + SUBLANES, D_SSM), F32),
                   pltpu.VMEM((nchunk, lc + SUBLANES, BC_DIM), F32)]
    else:
        s_shape = jax.ShapeDtypeStruct((D_SSM, SSM_STATE), F32)
        s_spec = const((D_SSM, SSM_STATE))
        scratch = [pltpu.VMEM((rows + SUBLANES, D_SSM), F32),
                   pltpu.VMEM((rows + SUBLANES, BC_DIM), F32)]
    in_specs += [const((SSM_CONV, CONV_DIM)), const((1, CONV_DIM)), const((1, LANES)),
                 const((1, LANES)), const((1, D_SSM)), const((1, D_SSM)),
                 const(e3.shape), const(es3.shape), const(t3.shape), const(ones_k.shape)]
    args += [conv_w, conv_b[None, :], dtb, alog, dskip_x, gn[None, :], e3, es3, t3, ones_k]
    return pl.pallas_call(
        functools.partial(_ssd_kernel, lc=lc, nchunk=nchunk, has_state=has_state),
        grid=(m // rows,),
        in_specs=in_specs,
        out_specs=(rowblk(D_SSM), s_spec),
        out_shape=(jax.ShapeDtypeStruct((m, D_SSM), BF16), s_shape),
        scratch_shapes=scratch,
        compiler_params=_cparams(("arbitrary",)),
        name="ssd_sample" if has_state else "ssd_prompt",
    )(*args)


def _out_proj_kernel(o_ref, y_ref, x_ref, w_ref, g_ref, x2_ref, hf_ref, w16_ref=None):
    if w16_ref is not None:
        w16_ref[...] = w_ref[...].astype(BF16)
        w_ref = w16_ref
    x2 = (x_ref[...]
          + jnp.dot(o_ref[...], w_ref[0:D_ATT, :], preferred_element_type=F32)
          + jnp.dot(y_ref[...], w_ref[D_ATT:D_ATT + D_SSM, :], preferred_element_type=F32))
    x2_ref[...] = x2
    hf_ref[...] = _rms(x2, g_ref[...]).astype(hf_ref.dtype)


def _out_proj(o, y, x, w, g, tm):
    m = x.shape[0]
    w_spec = pl.BlockSpec((D_ATT + D_SSM, D_MODEL), lambda i: (0, 0), pipeline_mode=pl.Buffered(1))
    out_specs = (pl.BlockSpec((tm, D_MODEL), lambda i: (i, 0)),
                 pl.BlockSpec((tm, D_MODEL), lambda i: (i, 0)))
    out_shape = (jax.ShapeDtypeStruct((m, D_MODEL), F32),
                 jax.ShapeDtypeStruct((m, D_MODEL), BF16))
    if w.dtype != BF16:
        assert m == tm, "the bf16 weight copy is written once: needs a single row tile"
        out_specs += (pl.BlockSpec((D_ATT + D_SSM, D_MODEL), lambda i: (0, 0)),)
        out_shape += (jax.ShapeDtypeStruct(w.shape, BF16),)
    return pl.pallas_call(
        _out_proj_kernel,
        grid=(m // tm,),
        in_specs=[
            pl.BlockSpec((tm, D_ATT), lambda i: (i, 0)),
            pl.BlockSpec((tm, D_SSM), lambda i: (i, 0)),
            pl.BlockSpec((tm, D_MODEL), lambda i: (i, 0)),
            w_spec,
            pl.BlockSpec((1, D_MODEL), lambda i: (0, 0)),
        ],
        out_specs=out_specs,
        out_shape=out_shape,
        compiler_params=_cparams(("arbitrary",)),
        name="out_proj",
    )(o, y, x, w, g)


def _ffn_kernel(*refs, tm, seq, final_norm):
    if seq is None:
        (hf_ref, x2_ref, wg_ref, wu_ref, wd_ref, cw_ref, cb_ref, gfin_ref,
         out_ref, gl_ref, g_scr, act_scr, carry_scr) = refs
    else:
        (hf_ref, x2_ref, wg_ref, wu_ref, wd_ref, cw_ref, cb_ref, gfin_ref, prev_ref, sel1_ref, sel2_ref,
         out_ref, gl_ref, wg16_ref, wu16_ref, wd16_ref, g_scr, act_scr) = refs
    i = pl.program_id(0)
    f = pl.program_id(1)
    nf = pl.num_programs(1) - 1
    pad = SUBLANES

    def weight(w_ref, w16_ref):
        if seq is None:
            return w_ref[...]
        w16 = w_ref[...].astype(BF16)
        w16_ref[...] = w16
        return w16

    def gate_up():
        hf = hf_ref[...]
        gate = jnp.dot(hf, weight(wg_ref, None if seq is None else wg16_ref), preferred_element_type=F32)
        up = jnp.dot(hf, weight(wu_ref, None if seq is None else wu16_ref), preferred_element_type=F32)
        g_scr[pad:pad + tm, :] = gate
        if seq is None:
            g_scr[0:pad, :] = carry_scr[f]
            carry_scr[f] = g_scr[tm:tm + pad, :]
            gl_ref[...] = g_scr[tm:tm + pad, :]
            g1 = g_scr[pad - 1:pad - 1 + tm, :]
            g2 = g_scr[pad - 2:pad - 2 + tm, :]
        else:
            g_scr[0:pad, :] = jnp.zeros((pad, gate.shape[1]), F32)
            gl_ref[...] = gate
            prev3 = jnp.concatenate(_split3(prev_ref[...]), axis=0)
            ov1 = jnp.dot(sel1_ref[...], prev3, preferred_element_type=F32)
            ov2 = jnp.dot(sel2_ref[...], prev3, preferred_element_type=F32)
            pos = lax.broadcasted_iota(jnp.int32, gate.shape, 0) % seq
            g1 = jnp.where(pos == 0, ov1, g_scr[pad - 1:pad - 1 + tm, :])
            g2 = jnp.where(pos < 2, ov2, g_scr[pad - 2:pad - 2 + tm, :])
        conv = cb_ref[...] + g2 * cw_ref[0:1, :] + g1 * cw_ref[1:2, :] + gate * cw_ref[2:3, :]
        act_scr[...] = (_silu(conv) * up).astype(BF16)

    def down():
        return jnp.dot(act_scr[...], weight(wd_ref, None if seq is None else wd16_ref),
                       preferred_element_type=F32)

    @pl.when(f == 0)
    def _():
        if seq is None:
            @pl.when(i == 0)
            def _():
                carry_scr[...] = jnp.zeros(carry_scr.shape, F32)
        out_ref[...] = x2_ref[...]
        gate_up()

    @pl.when((f > 0) & (f < nf))
    def _():
        contrib = down()
        gate_up()
        out_ref[...] += contrib

    @pl.when(f == nf)
    def _():
        res = out_ref[...] + down()
        out_ref[...] = _rms(res, gfin_ref[...]) if final_norm else res


def _ffn_prev_selectors(m, seq, nprev):
    sel1 = np.zeros((m, 3 * nprev), np.float32)
    sel2 = np.zeros((m, 3 * nprev), np.float32)
    nstate = FFN_CONV - 1
    for b in range(m // seq):
        for piece in range(3):
            base = piece * nprev + b * nstate
            sel1[b * seq, base + 1] = 1.0
            sel2[b * seq, base + 0] = 1.0
            sel2[b * seq + 1, base + 1] = 1.0
    return jnp.asarray(sel1, BF16), jnp.asarray(sel2, BF16)


def _ffn(hf, x2, wg, wu, wd, cw, cb, gfin, tm, tf, final_norm, seq=None, prev=None):
    m = hf.shape[0]
    nf = D_FF // tf
    cur = lambda f: jnp.minimum(f, nf - 1)
    last = lambda f: jnp.maximum(f - 1, 0)
    in_specs = [
        pl.BlockSpec((tm, D_MODEL), lambda i, f: (i, 0)),
        pl.BlockSpec((tm, D_MODEL), lambda i, f: (i, 0), pipeline_mode=pl.Buffered(1)),
        pl.BlockSpec((D_MODEL, tf), lambda i, f: (0, cur(f))),
        pl.BlockSpec((D_MODEL, tf), lambda i, f: (0, cur(f))),
        pl.BlockSpec((tf, D_MODEL), lambda i, f: (last(f), 0)),
        pl.BlockSpec((FFN_CONV, tf), lambda i, f: (0, cur(f))),
        pl.BlockSpec((1, tf), lambda i, f: (0, cur(f))),
        pl.BlockSpec((1, D_MODEL), lambda i, f: (0, 0)),
    ]
    args = [hf, x2, wg, wu, wd, cw, cb[None, :], gfin[None, :]]
    scratch = [pltpu.VMEM((tm + SUBLANES, tf), F32), pltpu.VMEM((tm, tf), BF16)]
    if seq is None:
        gl_shape = jax.ShapeDtypeStruct((m // tm * SUBLANES, D_FF), F32)
        gl_spec = pl.BlockSpec((SUBLANES, tf), lambda i, f: (i, cur(f)))
        scratch.append(pltpu.VMEM((nf, SUBLANES, tf), F32))
    else:
        assert m == tm, "sample FFN handles all sequences in one row tile"
        nprev = prev.shape[0]
        sel1, sel2 = _ffn_prev_selectors(m, seq, nprev)
        in_specs += [pl.BlockSpec((nprev, tf), lambda i, f: (0, cur(f))),
                     pl.BlockSpec(sel1.shape, lambda i, f: (0, 0)),
                     pl.BlockSpec(sel2.shape, lambda i, f: (0, 0))]
        args += [prev, sel1, sel2]
        gl_shape = jax.ShapeDtypeStruct((m, D_FF), F32)
        gl_spec = pl.BlockSpec((tm, tf), lambda i, f: (i, cur(f)))
    out_specs = [pl.BlockSpec((tm, D_MODEL), lambda i, f: (i, 0)), gl_spec]
    out_shape = [jax.ShapeDtypeStruct((m, D_MODEL), F32), gl_shape]
    if seq is not None:
        out_specs += [in_specs[2], in_specs[3], in_specs[4]]
        out_shape += [jax.ShapeDtypeStruct(wg.shape, BF16), jax.ShapeDtypeStruct(wu.shape, BF16),
                      jax.ShapeDtypeStruct(wd.shape, BF16)]
    return pl.pallas_call(
        functools.partial(_ffn_kernel, tm=tm, seq=seq, final_norm=final_norm),
        grid=(m // tm, nf + 1),
        in_specs=in_specs,
        out_specs=tuple(out_specs),
        out_shape=tuple(out_shape),
        scratch_shapes=scratch,
        compiler_params=_cparams(("arbitrary", "arbitrary")),
        name="ffn_sample" if seq is not None else "ffn_prompt",
    )(*args)


def _tiling(m, has_state):
    return dict(
        proj_tm=min(1024, m),
        attn_tile=256,
        ssd_chunks=2 * _SAMPLE_SEQS_PER_STEP if has_state else 8,
        out_tm=min(512, m),
        ffn_tm=min(512, m) if has_state else min(1024, m),
        ffn_tf=256 if has_state else 512,
    )


def _layer(x, w, lam_init, final_norm, gfin, *, batch, seq, state=None):
    m = x.shape[0]
    lamv = jnp.stack([w["lambda_q1"], w["lambda_k1"], w["lambda_q2"], w["lambda_k2"]])
    tiles = _tiling(m, state is not None)
    w16 = {}
    q, k, v, z, xs, bc, dt, *extra = _in_proj(x, w["norm_mix_g"][None, :], w["w_in_main"], w["w_in_dt"],
                                              tm=tiles["proj_tm"])
    if extra:
        w16["w_in_main"] = extra[0]

    if state is None:
        slopes = jnp.broadcast_to(
            jnp.asarray(2.0 ** (-8.0 * np.arange(1, ATT_HEADS + 1) / ATT_HEADS), F32)[:, None, None],
            (ATT_HEADS, 1, LANES))
        o = _attn_prompt(q, k, v, slopes, lamv, w["attn_subln_g"][None, :], lam_init,
                         tile=tiles["attn_tile"])
        y, s_new = _ssd(xs, bc, dt, z, w["conv_w"], w["conv_b"], w["dt_bias"], w["a_log"],
                        w["d_skip"], w["ssm_norm_g"], lc=CHUNK, nchunk=tiles["ssd_chunks"])
        conv_new = jnp.concatenate([xs[m - (SSM_CONV - 1):], bc[m - (SSM_CONV - 1):]], axis=-1)[None]
        s_new = s_new[None]
    else:
        cache_k, cache_v, conv_prev, ssm_prev, ffn_prev = state
        past = cache_k.shape[1]
        assert past % CHUNK == 0 and seq <= CHUNK
        o = _attn_sample(q, k, v, cache_k.reshape(batch, past * ATT_HEADS, V_DIM),
                         cache_v.reshape(batch, past * ATT_HEADS, V_DIM),
                         lamv, w["attn_subln_g"][None, :], lam_init, seq)
        prev8 = jnp.pad(conv_prev, ((0, 0), (SUBLANES - (SSM_CONV - 1), 0), (0, 0)))
        y, s_new = _ssd(xs, bc, dt, z, w["conv_w"], w["conv_b"], w["dt_bias"], w["a_log"],
                        w["d_skip"], w["ssm_norm_g"], lc=seq, nchunk=tiles["ssd_chunks"],
                        conv_prev=prev8, h0=ssm_prev.reshape(batch, D_SSM, SSM_STATE))
        conv_new = jnp.concatenate([xs.reshape(batch, seq, D_SSM)[:, seq - (SSM_CONV - 1):],
                                    bc.reshape(batch, seq, BC_DIM)[:, seq - (SSM_CONV - 1):]], axis=-1)

    x2, hf, *extra = _out_proj(o, y, x, w["w_out"], w["norm_ffn_g"][None, :], tm=tiles["out_tm"])
    if extra:
        w16["w_out"] = extra[0]

    ffn_args = (hf, x2, w["w_gate"], w["w_up"], w["w_down"], w["ffn_conv_w"], w["ffn_conv_b"], gfin)
    ffn_tiles = dict(tm=tiles["ffn_tm"], tf=tiles["ffn_tf"], final_norm=final_norm)
    if state is None:
        x3, gl = _ffn(*ffn_args, **ffn_tiles)
        ffn_new = gl[None, gl.shape[0] - (FFN_CONV - 1):]
    else:
        x3, gl, w16["w_gate"], w16["w_up"], w16["w_down"] = _ffn(
            *ffn_args, **ffn_tiles, seq=seq, prev=ffn_prev.reshape(batch * (FFN_CONV - 1), D_FF))
        ffn_new = gl.reshape(batch, seq, D_FF)[:, seq - (FFN_CONV - 1):]
    k_new = k.reshape(batch, seq, ATT_HEADS, 2 * QK_DIM)
    v_new = v.reshape(batch, seq, ATT_HEADS, V_DIM)
    s_new = s_new.reshape(batch, SSM_HEADS, SSM_HEAD_DIM, SSM_STATE)
    return (x3, k_new, v_new, conv_new, s_new, ffn_new), w16


def kernel(x_prompt, x_sample, cache_k, cache_v, state_ssm_conv, state_ssm, state_ffn_conv, norm_mix_g, w_in, lambda_q1, lambda_k1, lambda_q2, lambda_k2, attn_subln_g, conv_w, conv_b, dt_bias, a_log, d_skip, ssm_norm_g, w_out, norm_ffn_g, w_gate, w_up, ffn_conv_w, ffn_conv_b, w_down, norm_final_g):
    depth = w_in.shape[0]
    pb, pl_len, _ = x_prompt.shape
    sb, sl_len, _ = x_sample.shape
    assert pb == 1 and pl_len % CHUNK == 0
    xp = x_prompt.reshape(pb * pl_len, D_MODEL)
    xs = x_sample.reshape(sb * sl_len, D_MODEL)
    n_main = 2 * D_ATT + D_ATT + D_SSM + CONV_DIM
    outs_p, outs_s = [], []
    for layer in range(depth):
        lam_init = 0.8 - 0.6 * math.exp(-0.3 * layer)
        w_in_t = jnp.swapaxes(w_in[layer], 0, 1)
        w = dict(
            norm_mix_g=norm_mix_g[layer],
            w_in_main=w_in_t,
            w_in_dt=jnp.pad(w_in_t[n_main:].astype(BF16), ((0, LANES - SSM_HEADS), (0, 0))),
            lambda_q1=lambda_q1[layer], lambda_k1=lambda_k1[layer],
            lambda_q2=lambda_q2[layer], lambda_k2=lambda_k2[layer],
            attn_subln_g=attn_subln_g[layer], conv_w=conv_w[layer], conv_b=conv_b[layer],
            dt_bias=dt_bias[layer], a_log=a_log[layer], d_skip=d_skip[layer],
            ssm_norm_g=ssm_norm_g[layer], w_out=w_out[layer],
            norm_ffn_g=norm_ffn_g[layer], w_gate=w_gate[layer],
            w_up=w_up[layer], ffn_conv_w=ffn_conv_w[layer],
            ffn_conv_b=ffn_conv_b[layer], w_down=w_down[layer],
        )
        last = layer == depth - 1
        (xs, *new_s), w16 = _layer(xs, w, lam_init, last, norm_final_g, batch=sb, seq=sl_len,
                                   state=(cache_k[layer], cache_v[layer], state_ssm_conv[layer],
                                          state_ssm[layer], state_ffn_conv[layer]))
        (xp, *new_p), _ = _layer(xp, {**w, **w16}, lam_init, last, norm_final_g, batch=pb, seq=pl_len)
        outs_p.append(new_p)
        outs_s.append(new_s)
    stack = lambda outs, idx: jnp.stack([o[idx] for o in outs])
    return (xp.reshape(pb, pl_len, D_MODEL), xs.reshape(sb, sl_len, D_MODEL),
            *[stack(outs_p, idx) for idx in range(5)],
            *[stack(outs_s, idx) for idx in range(5)])
```

```python
import functools
import math

import jax
import jax.numpy as jnp
import numpy as np
from jax import lax
from jax.experimental import pallas as pl
from jax.experimental.pallas import tpu as pltpu

F32 = jnp.float32
BF16 = jnp.bfloat16

D_MODEL = 2048
CHUNK = 64
ATT_HEADS = 8
QK_DIM = 64
V_DIM = 128
D_ATT = ATT_HEADS * V_DIM
SSM_HEADS = 16
SSM_HEAD_DIM = 64
D_SSM = SSM_HEADS * SSM_HEAD_DIM
SSM_GROUPS = 2
SSM_STATE = 128
SSM_CONV = 4
BC_DIM = 2 * SSM_GROUPS * SSM_STATE
CONV_DIM = D_SSM + BC_DIM
D_FF = 5632
FFN_CONV = 3
EPS = 1e-6
LOG2E = math.log2(math.e)
Q_SCALE = QK_DIM ** -0.5 * LOG2E
LANES = 128
SUBLANES = 8
VMEM_LIMIT = 58 * 1024 * 1024

_NT = (((1,), (1,)), ((), ()))
_TN = (((0,), (0,)), ((), ()))


def _cparams(sem):
    return pltpu.CompilerParams(dimension_semantics=sem, vmem_limit_bytes=VMEM_LIMIT)


def _silu(x):
    return x * jax.nn.sigmoid(x)


def _rms(x, g):
    return x * lax.rsqrt(jnp.mean(x * x, axis=-1, keepdims=True) + EPS) * g


_PROJ_TN = 512
_PROJ_SEGS = ((0, 2), (2, 4), (4, 6), (6, 8), (8, 10), (10, 11))


def _in_proj_kernel(x_ref, g_ref, w_ref, wdt_ref,
                    q_ref, k_ref, v_ref, z_ref, xs_ref, bc_ref, dt_ref, *rest):
    w16_ref, h_scr = rest if len(rest) == 2 else (None, rest[0])
    j = pl.program_id(1)

    @pl.when(j == 0)
    def _():
        hb = _rms(x_ref[...], g_ref[...]).astype(BF16)
        h_scr[...] = hb
        dt_ref[...] = lax.dot_general(hb, wdt_ref[...], _NT, preferred_element_type=F32)

    outs = (q_ref, k_ref, v_ref, z_ref, xs_ref, bc_ref)
    for (lo, hi), ref in zip(_PROJ_SEGS, outs):
        @pl.when((j >= lo) & (j < hi))
        def _(ref=ref):
            w = w_ref[...]
            if w16_ref is not None:
                w = w.astype(BF16)
                w16_ref[...] = w
            res = lax.dot_general(h_scr[...], w, _NT, preferred_element_type=F32)
            val = res * Q_SCALE if ref is q_ref else res
            ref[...] = val.astype(ref.dtype)


def _in_proj(x, g, w_main, w_dt, tm):
    m = x.shape[0]
    tn = _PROJ_TN
    nj = w_main.shape[0] // tn

    def seg_spec(lo, hi):
        return pl.BlockSpec((tm, tn), lambda i, j: (i, jnp.clip(j - lo, 0, hi - lo - 1)))

    out_shape = (
        jax.ShapeDtypeStruct((m, D_ATT), BF16),
        jax.ShapeDtypeStruct((m, D_ATT), F32),
        jax.ShapeDtypeStruct((m, D_ATT), F32),
        jax.ShapeDtypeStruct((m, D_SSM), F32),
        jax.ShapeDtypeStruct((m, D_SSM), F32),
        jax.ShapeDtypeStruct((m, BC_DIM), F32),
        jax.ShapeDtypeStruct((m, LANES), F32),
    )
    out_specs = tuple(seg_spec(lo, hi) for lo, hi in _PROJ_SEGS) + (
        pl.BlockSpec((tm, LANES), lambda i, j: (i, 0)),)
    if w_main.dtype != BF16:
        assert m == tm, "the bf16 weight copy is written once per tile: needs a single row tile"
        out_shape += (jax.ShapeDtypeStruct((nj * tn, D_MODEL), BF16),)
        out_specs += (pl.BlockSpec((tn, D_MODEL), lambda i, j: (j, 0)),)
    return pl.pallas_call(
        _in_proj_kernel,
        grid=(m // tm, nj),
        in_specs=[
            pl.BlockSpec((tm, D_MODEL), lambda i, j: (i, 0)),
            pl.BlockSpec((1, D_MODEL), lambda i, j: (0, 0)),
            pl.BlockSpec((tn, D_MODEL), lambda i, j: (j, 0)),
            pl.BlockSpec((LANES, D_MODEL), lambda i, j: (0, 0)),
        ],
        out_specs=out_specs,
        out_shape=out_shape,
        scratch_shapes=[pltpu.VMEM((tm, D_MODEL), BF16)],
        compiler_params=_cparams(("arbitrary", "arbitrary")),
        name="in_proj",
    )(x, g, w_main, w_dt)


def _lambda_value(lamv_ref, lam_init):
    lv = lamv_ref[...]
    s1 = jnp.sum(lv[0:1] * lv[1:2], axis=-1, keepdims=True)
    s2 = jnp.sum(lv[2:3] * lv[3:4], axis=-1, keepdims=True)
    return jnp.exp(s1) - jnp.exp(s2) + lam_init


def _split_maps(q):
    lane = lax.broadcasted_iota(jnp.int32, q.shape, 1)
    zero = jnp.zeros_like(q)
    return jnp.where(lane < QK_DIM, q, zero), jnp.where(lane >= QK_DIM, q, zero)


_VT_CHUNK = 512
_VT_PAD = 16
_UNROLL_SHIFT = 2
_UNROLL = 1 << _UNROLL_SHIFT
_HEADS_PER_STEP = 2
_QTILES_PER_GROUP = 4


def _attn_prompt_kernel(q_ref, k_ref, v_ref, slope_ref, lamv_ref, g_ref, dmask_ref, dneg_ref, o_ref,
                        k_scr, vt_scr, acc_scr, t0_scr, t1_scr, dbias_scr, stat_scr,
                        *, tile, lam_init):
    length = k_ref.shape[0]
    width = 2 * tile
    nheads = k_scr.shape[0]
    lam = _lambda_value(lamv_ref, lam_init)

    lane = lax.broadcasted_iota(jnp.int32, (length, LANES), 1)
    koff = lax.broadcasted_iota(jnp.int32, (length, LANES), 0) % tile
    koff = jnp.where(lane < 3, koff, 0).astype(F32).astype(BF16)
    extra = (lax.broadcasted_iota(jnp.int32, (_VT_PAD, tile), 0) == 0).astype(F32).astype(BF16)
    brow = lax.broadcasted_iota(jnp.int32, (LANES, width), 0)
    per = _VT_CHUNK // tile
    ws, q_biases = [], []
    for hh in range(nheads):
        cols = slice(hh * V_DIM, (hh + 1) * V_DIM)
        k_scr[hh, :, 0:V_DIM] = k_ref[:, cols].astype(BF16)
        k_scr[hh, :, V_DIM:V_DIM + LANES] = koff
        for c in range(length // _VT_CHUNK):
            vt = v_ref[c * _VT_CHUNK:(c + 1) * _VT_CHUNK, cols].T.astype(BF16)
            for s in range(per):
                vt_scr[hh, c * per + s, 0:V_DIM, :] = vt[:, s * tile:(s + 1) * tile]
                vt_scr[hh, c * per + s, V_DIM:V_DIM + _VT_PAD, :] = extra
        w = slope_ref[hh][:, :1] * LOG2E
        w_hi, w_mid, w_lo = (piece.astype(F32) for piece in _split3(w))
        q_bias = jnp.where(brow == 0, w_hi, jnp.where(brow == 1, w_mid, jnp.where(brow == 2, w_lo, 0.0)))
        ws.append(w)
        q_biases.append(q_bias.astype(BF16))
        dbias_scr[hh] = w * dmask_ref[...] + dneg_ref[...]

    def q_pair(qp, _):
        _attn_prompt_pair(qp, ws, lam, q_biases, q_ref, g_ref, o_ref,
                          k_scr, vt_scr, acc_scr, (t0_scr, t1_scr),
                          dbias_scr, stat_scr, tile=tile, lam_init=lam_init)
        return 0

    lax.fori_loop(0, length // (_QTILES_PER_GROUP * tile), q_pair, 0)


def _attn_prompt_pair(qp, ws, lam, q_biases, q_ref, g_ref, o_ref,
                      k_scr, vt_scr, acc_scr, t_slots, dbias_scr, stat_scr, *, tile, lam_init):
    width = 2 * tile
    nheads = len(ws)
    group = _QTILES_PER_GROUP
    streams = [(hh, u) for hh in range(nheads) for u in range(group)]
    nfull = group * qp
    qis = [nfull + u for _, u in streams]
    rows = [pl.ds(pl.multiple_of(qi * tile, tile), tile) for qi in qis]
    q_augs = []
    for s, (hh, _) in enumerate(streams):
        q1, q2 = _split_maps(q_ref[rows[s], hh * V_DIM:(hh + 1) * V_DIM])
        q_t = jnp.concatenate([q1, q2], axis=0).astype(F32).T.astype(BF16)
        q_augs.append(jnp.concatenate([q_t, q_biases[hh]], axis=0))

    def scores(s, j):
        k0 = pl.multiple_of(jnp.minimum(j, qis[s]) * tile, tile)
        return jnp.dot(k_scr[streams[s][0], pl.ds(k0, tile), :], q_augs[s], preferred_element_type=F32)

    TMAX = 0

    def step(s, j, slot, m_old):
        hh = streams[s][0]
        off = -ws[hh] * jnp.asarray((qis[s] - j) * tile, F32)
        m_new = jnp.maximum(m_old, stat_scr[s, TMAX:TMAX + 1, :] + off)
        t_next = scores(s, j + 1)
        t_slots[1 - slot][s] = t_next
        stat_scr[s, TMAX:TMAX + 1, :] = jnp.max(t_next, axis=0, keepdims=True)
        p = jnp.exp2(t_slots[slot][s] + (off - m_new)).astype(BF16)
        pv = jnp.dot(vt_scr[hh, j], p, preferred_element_type=F32)
        acc_scr[s] = jnp.exp2(m_old - m_new) * acc_scr[s] + pv
        return m_new

    def run(first, ntiles_per_iter, niter, carry):
        def body(i, carry):
            for k in range(ntiles_per_iter):
                carry = tuple(step(s, first + ntiles_per_iter * i + k, k % 2, carry[s])
                              for s in range(len(streams)))
            return carry
        return lax.fori_loop(0, niter, body, carry)

    carry = []
    for s in range(len(streams)):
        acc_scr[s] = jnp.zeros(acc_scr.shape[1:], F32)
        t_first = scores(s, 0)
        t_slots[0][s] = t_first
        stat_scr[s, TMAX:TMAX + 1, :] = jnp.max(t_first, axis=0, keepdims=True)
        carry.append(jnp.full((1, width), -0.5 * float(jnp.finfo(F32).max), F32))
    n_main = lax.shift_right_logical(nfull, _UNROLL_SHIFT)
    done = n_main * _UNROLL
    carry = run(0, _UNROLL, n_main, tuple(carry))
    carry = run(done, 2, lax.shift_right_logical(nfull - done, 1), carry)

    m_run = list(carry)
    diag_slot = [0] * len(streams)
    for e in range(group - 1):
        for s, (hh, u) in enumerate(streams):
            if u > e:
                m_run[s] = step(s, nfull + e, diag_slot[s], m_run[s])
                diag_slot[s] = 1 - diag_slot[s]
    for s, (hh, u) in enumerate(streams):
        t = t_slots[diag_slot[s]][s] + dbias_scr[hh]
        m_new = jnp.maximum(m_run[s], jnp.max(t, axis=0, keepdims=True))
        p = jnp.exp2(t - m_new).astype(BF16)
        acc_scr[s] = (jnp.exp2(m_run[s] - m_new) * acc_scr[s]
                      + jnp.dot(vt_scr[hh, qis[s]], p, preferred_element_type=F32))
    for s, (hh, u) in enumerate(streams):
        acc = acc_scr[s]
        on = acc[0:V_DIM] / acc[V_DIM:V_DIM + 1]
        o_t = on[:, :tile] - lam * on[:, tile:]
        o_t = o_t * lax.rsqrt(jnp.mean(o_t * o_t, axis=0, keepdims=True) + EPS)
        o_ref[rows[s], hh * V_DIM:(hh + 1) * V_DIM] = (
            o_t.T * g_ref[...] * (1.0 - lam_init)).astype(o_ref.dtype)


def _attn_prompt(q, k, v, slopes, lamv, g, lam_init, tile):
    length = q.shape[0]
    key = np.arange(tile)[:, None]
    qry = np.tile(np.arange(tile), 2)[None, :]
    dmask = jnp.asarray((qry - np.abs(qry - key)) - key, F32)
    dneg = jnp.asarray(np.where(key // CHUNK <= qry // CHUNK, 0.0, -np.inf), F32)
    nh = _HEADS_PER_STEP
    nstream = nh * _QTILES_PER_GROUP
    diag_spec = pl.BlockSpec((tile, 2 * tile), lambda h: (0, 0), pipeline_mode=pl.Buffered(1))
    head_spec = pl.BlockSpec((length, nh * V_DIM), lambda h: (0, h), pipeline_mode=pl.Buffered(1))
    kv_spec = pl.BlockSpec((length, nh * V_DIM), lambda h: (0, h), pipeline_mode=pl.Buffered(1))
    return pl.pallas_call(
        functools.partial(_attn_prompt_kernel, tile=tile, lam_init=lam_init),
        grid=(ATT_HEADS // nh,),
        in_specs=[
            kv_spec, kv_spec, kv_spec,
            pl.BlockSpec((nh, 1, LANES), lambda h: (h, 0, 0)),
            pl.BlockSpec((4, QK_DIM), lambda h: (0, 0)),
            pl.BlockSpec((1, V_DIM), lambda h: (0, 0)),
            diag_spec, diag_spec,
        ],
        out_specs=head_spec,
        out_shape=jax.ShapeDtypeStruct((length, D_ATT), BF16),
        scratch_shapes=[
            pltpu.VMEM((nh, length, V_DIM + LANES), BF16),
            pltpu.VMEM((nh, length // tile, V_DIM + _VT_PAD, tile), BF16),
            pltpu.VMEM((nstream, V_DIM + _VT_PAD, 2 * tile), F32),
            pltpu.VMEM((nstream, tile, 2 * tile), F32),
            pltpu.VMEM((nstream, tile, 2 * tile), F32),
            pltpu.VMEM((nh, tile, 2 * tile), F32),
            pltpu.VMEM((nstream, SUBLANES, 2 * tile), F32),
        ],
        compiler_params=_cparams(("arbitrary",)),
        name="attn_prompt",
    )(q, k, v, slopes, lamv, g, dmask, dneg)


_SAMPLE_SEQS_PER_STEP = 2


def _attn_sample_kernel(q_ref, kn_ref, vn_ref, ck_ref, cv_ref, lamv_ref, g_ref, bias_c_ref, bias_n_ref,
                        o_ref, *, seq, past, lam_init):
    lam = _lambda_value(lamv_ref, lam_init)
    rows = 2 * seq
    for bi in range(ck_ref.shape[0]):
        br = slice(bi * seq, (bi + 1) * seq)
        sc, sn = [], []
        for h in range(ATT_HEADS):
            sl = slice(h * V_DIM, (h + 1) * V_DIM)
            q1, q2 = _split_maps(q_ref[br, sl])
            qq = jnp.concatenate([q1, q2], axis=0)
            kc = ck_ref[bi, pl.ds(h, past, stride=ATT_HEADS), :].astype(BF16)
            sc.append(lax.dot_general(qq, kc, _NT, preferred_element_type=F32))
            sn.append(lax.dot_general(qq, kn_ref[br, sl].astype(BF16), _NT, preferred_element_type=F32))
        sc = jnp.concatenate(sc, axis=0) - bias_c_ref[...]
        sn = jnp.concatenate(sn, axis=0) - bias_n_ref[...]
        m = jnp.maximum(jnp.max(sc, axis=-1, keepdims=True), jnp.max(sn, axis=-1, keepdims=True))
        pc = jnp.exp2(sc - m)
        pn = jnp.exp2(sn - m)
        inv_l = 1.0 / (jnp.sum(pc, axis=-1, keepdims=True) + jnp.sum(pn, axis=-1, keepdims=True))
        pc = pc.astype(BF16)
        pn = pn.astype(BF16)
        for h in range(ATT_HEADS):
            sl = slice(h * V_DIM, (h + 1) * V_DIM)
            hr = slice(h * rows, (h + 1) * rows)
            vc = cv_ref[bi, pl.ds(h, past, stride=ATT_HEADS), :].astype(BF16)
            acc = (jnp.dot(pc[hr], vc, preferred_element_type=F32)
                   + jnp.dot(pn[hr], vn_ref[br, sl].astype(BF16), preferred_element_type=F32))
            on = acc * inv_l[hr]
            o = on[:seq] - lam * on[seq:]
            o_ref[br, sl] = (_rms(o, g_ref[...]) * (1.0 - lam_init)).astype(o_ref.dtype)


def _attn_sample(q, k_new, v_new, cache_k, cache_v, lamv, g, lam_init, seq):
    nb, past = cache_k.shape[0], cache_k.shape[1] // ATT_HEADS
    slope = np.repeat(2.0 ** (-8.0 * np.arange(1, ATT_HEADS + 1) / ATT_HEADS), 2 * seq)[:, None] * LOG2E
    qpos = np.tile(np.arange(seq), 2 * ATT_HEADS)[:, None]
    bias_c = jnp.asarray(slope * (qpos + past - np.arange(past)[None, :]), F32)
    bias_n = jnp.asarray(slope * np.abs(qpos - np.arange(seq)[None, :]), F32)
    per = _SAMPLE_SEQS_PER_STEP
    row_spec = pl.BlockSpec((per * seq, D_ATT), lambda b: (b, 0))
    cache_spec = pl.BlockSpec((per, past * ATT_HEADS, V_DIM), lambda b: (b, 0, 0))

    def const(arr):
        return pl.BlockSpec(arr.shape, lambda b: (0, 0), pipeline_mode=pl.Buffered(1))

    return pl.pallas_call(
        functools.partial(_attn_sample_kernel, seq=seq, past=past, lam_init=lam_init),
        grid=(nb // per,),
        in_specs=[row_spec, row_spec, row_spec, cache_spec, cache_spec,
                  pl.BlockSpec((4, QK_DIM), lambda b: (0, 0)),
                  pl.BlockSpec((1, V_DIM), lambda b: (0, 0)),
                  const(bias_c), const(bias_n)],
        out_specs=row_spec,
        out_shape=jax.ShapeDtypeStruct((nb * seq, D_ATT), BF16),
        compiler_params=_cparams(("arbitrary",)),
        name="attn_sample",
    )(q, k_new, v_new, cache_k, cache_v, lamv, g, bias_c, bias_n)


def _split3(x):
    hi = x.astype(BF16)
    r = x - hi.astype(F32)
    mid = r.astype(BF16)
    lo = (r - mid.astype(F32)).astype(BF16)
    return hi, mid, lo


def _ssd_chunk(xs, bm, cm, dt, z, s_ref, a_heads, dskip, gn, e3, es3, t3, ones_k, lc):
    seg_w = SSM_HEADS * lc
    half = D_SSM // SSM_GROUPS
    d3 = jnp.concatenate(_split3(dt), axis=1)
    dt_x = jnp.dot(d3, e3, preferred_element_type=F32)
    la3 = jnp.concatenate(_split3(dt * a_heads), axis=0)
    ac3 = jnp.concatenate(_split3(jnp.dot(t3, la3, preferred_element_type=F32)), axis=1)
    acol = jnp.dot(ac3, e3, preferred_element_type=F32)
    acol_s = acol if seg_w == D_SSM else jnp.dot(ac3, es3, preferred_element_type=F32)
    t_idx = lax.broadcasted_iota(jnp.int32, (lc, seg_w), 0)
    s_idx = lax.broadcasted_iota(jnp.int32, (lc, seg_w), 1) % lc
    arow = jnp.sum(jnp.where(t_idx == s_idx, acol_s, 0.0), axis=0, keepdims=True)
    decay = jnp.where(t_idx >= s_idx, jnp.exp(acol_s - arow), 0.0)

    cb16 = cm.astype(BF16)
    bb16 = bm.astype(BF16)
    hpg = SSM_HEADS // SSM_GROUPS
    cbs = []
    for g in range(SSM_GROUPS):
        gs = slice(g * SSM_STATE, (g + 1) * SSM_STATE)
        b_rep = jnp.concatenate([bb16[:, gs]] * hpg, axis=0)
        cbs.append(lax.dot_general(cb16[:, gs], b_rep, _NT, preferred_element_type=F32))
    mmat = (jnp.concatenate(cbs, axis=1) * decay).astype(BF16)

    xdt = xs * dt_x
    xdt16 = xdt.astype(BF16)
    hk = (2 * LANES) // lc
    wd = hk * SSM_HEAD_DIM
    blk = (lax.broadcasted_iota(jnp.int32, (hk * lc, wd), 0) // lc
           == lax.broadcasted_iota(jnp.int32, (hk * lc, wd), 1) // SSM_HEAD_DIM)
    parts = []
    for i in range(SSM_HEADS // hk):
        xd = xdt16[:, i * wd:(i + 1) * wd]
        bd = jnp.where(blk, jnp.concatenate([xd] * hk, axis=0), jnp.zeros((), BF16))
        parts.append(jnp.dot(mmat[:, i * hk * lc:(i + 1) * hk * lc], bd, preferred_element_type=F32))
    y_intra = jnp.concatenate(parts, axis=1) if len(parts) > 1 else parts[0]

    yi = []
    for g in range(SSM_GROUPS):
        sg = s_ref[g * half:(g + 1) * half, :].astype(BF16)
        yi.append(lax.dot_general(cb16[:, g * SSM_STATE:(g + 1) * SSM_STATE], sg, _NT,
                                  preferred_element_type=F32))
    y_inter = jnp.exp(acol) * jnp.concatenate(yi, axis=1)

    alast = acol[lc - 1:lc, :]
    dec_end = jnp.exp(alast - acol)
    xd_end = (xdt * dec_end).astype(BF16)
    krow = lax.broadcasted_iota(jnp.int32, (ones_k.shape[0], D_SSM), 0)
    a_hi, a_mid, a_lo = (piece.astype(F32) for piece in _split3(alast))
    pieces = jnp.where(krow == 0, a_hi, jnp.where(krow == 1, a_mid, jnp.where(krow == 2, a_lo, 0.0)))
    acl = lax.dot_general(pieces.astype(BF16), ones_k, _TN, preferred_element_type=F32)
    for g in range(SSM_GROUPS):
        rows = slice(g * half, (g + 1) * half)
        upd = lax.dot_general(xd_end[:, rows], bb16[:, g * SSM_STATE:(g + 1) * SSM_STATE], _TN,
                              preferred_element_type=F32)
        s_ref[rows, :] = jnp.exp(acl[rows, :]) * s_ref[rows, :] + upd

    y = (y_intra + y_inter + dskip * xs) * _silu(z)
    outs = []
    for g in range(SSM_GROUPS):
        cs = slice(g * half, (g + 1) * half)
        outs.append(_rms(y[:, cs], gn[:, cs]))
    return jnp.concatenate(outs, axis=1)


def _ssd_kernel(*refs, lc, nchunk, has_state):
    if has_state:
        (xs_ref, bc_ref, dt_ref, z_ref, prev_ref, h0_ref, cw_ref, cbias_ref, dtb_ref, alog_ref,
         dskip_ref, gn_ref, e3_ref, es3_ref, t3_ref, ones_ref,
         y_ref, s_ref, scr_x, scr_bc) = refs
    else:
        (xs_ref, bc_ref, dt_ref, z_ref, cw_ref, cbias_ref, dtb_ref, alog_ref,
         dskip_ref, gn_ref, e3_ref, es3_ref, t3_ref, ones_ref,
         y_ref, s_ref, scr_x, scr_bc) = refs
    rows = lc * nchunk
    pad = SUBLANES

    def conv_silu(scr_x, scr_bc, nrows):
        xc = cbias_ref[:, 0:D_SSM]
        bcc = cbias_ref[:, D_SSM:CONV_DIM]
        for tap in range(SSM_CONV):
            off = pad - (SSM_CONV - 1) + tap
            xc = xc + scr_x[off:off + nrows, :] * cw_ref[tap:tap + 1, 0:D_SSM]
            bcc = bcc + scr_bc[off:off + nrows, :] * cw_ref[tap:tap + 1, D_SSM:CONV_DIM]
        return _silu(xc), _silu(bcc)

    dt_in = dt_ref[...] + dtb_ref[...]
    dt = jnp.maximum(dt_in, 0.0) + jnp.log1p(jnp.exp(-jnp.abs(dt_in)))
    a_heads = -jnp.exp(alog_ref[...])
    nbm = SSM_GROUPS * SSM_STATE

    def chunk(c, xs_c, bc_c, s2d):
        rs = slice(c * lc, (c + 1) * lc)
        y = _ssd_chunk(xs_c, bc_c[:, 0:nbm], bc_c[:, nbm:2 * nbm], dt[rs],
                       z_ref[rs, :], s2d, a_heads, dskip_ref[...], gn_ref[...],
                       e3_ref[...], es3_ref[...], t3_ref[...], ones_ref[...], lc)
        y_ref[rs, :] = y.astype(y_ref.dtype)

    if has_state:
        for c in range(nchunk):
            rs = slice(c * lc, (c + 1) * lc)
            scr_x[c, 0:pad, :] = prev_ref[c, :, 0:D_SSM]
            scr_bc[c, 0:pad, :] = prev_ref[c, :, D_SSM:CONV_DIM]
            scr_x[c, pad:pad + lc, :] = xs_ref[rs, :]
            scr_bc[c, pad:pad + lc, :] = bc_ref[rs, :]
            s_ref[c] = h0_ref[c]
            xs_act, bc_act = conv_silu(scr_x.at[c], scr_bc.at[c], lc)
            chunk(c, xs_act, bc_act, s_ref.at[c])
    else:
        @pl.when(pl.program_id(0) == 0)
        def _():
            scr_x[0:pad, :] = jnp.zeros((pad, D_SSM), F32)
            scr_bc[0:pad, :] = jnp.zeros((pad, BC_DIM), F32)
            s_ref[...] = jnp.zeros(s_ref.shape, F32)

        scr_x[pad:pad + rows, :] = xs_ref[...]
        scr_bc[pad:pad + rows, :] = bc_ref[...]
        xs_act, bc_act = conv_silu(scr_x, scr_bc, rows)
        scr_x[0:pad, :] = scr_x[rows:rows + pad, :]
        scr_bc[0:pad, :] = scr_bc[rows:rows + pad, :]
        for c in range(nchunk):
            rs = slice(c * lc, (c + 1) * lc)
            chunk(c, xs_act[rs], bc_act[rs], s_ref)


def _ssd_constants(lc):
    seg_w = SSM_HEADS * lc
    head_of_lane = np.arange(D_SSM) // SSM_HEAD_DIM
    e = (np.arange(LANES)[:, None] == head_of_lane[None, :]).astype(np.float32)
    es = (np.arange(LANES)[:, None] == (np.arange(seg_w) // lc)[None, :]).astype(np.float32)
    tri = np.tril(np.ones((lc, lc), np.float32))
    return (jnp.asarray(np.concatenate([e] * 3, axis=0), BF16),
            jnp.asarray(np.concatenate([es] * 3, axis=0), BF16),
            jnp.asarray(np.concatenate([tri] * 3, axis=1), BF16),
            jnp.ones((2 * SUBLANES, SSM_STATE), BF16))


def _ssd(xs, bc, dt, z, conv_w, conv_b, dt_bias, a_log, d_skip, gn, lc, nchunk,
         conv_prev=None, h0=None):
    m = xs.shape[0]
    rows = lc * nchunk
    has_state = h0 is not None
    e3, es3, t3, ones_k = _ssd_constants(lc)
    dtb = jnp.zeros((1, LANES), F32).at[0, :SSM_HEADS].set(dt_bias)
    alog = jnp.zeros((1, LANES), F32).at[0, :SSM_HEADS].set(a_log)
    dskip_x = jnp.repeat(d_skip, SSM_HEAD_DIM)[None, :]

    def const(shape):
        return pl.BlockSpec(shape, lambda i: (0,) * len(shape))

    def rowblk(width):
        return pl.BlockSpec((rows, width), lambda i: (i, 0))

    in_specs = [rowblk(D_SSM), rowblk(BC_DIM), rowblk(LANES), rowblk(D_SSM)]
    args = [xs, bc, dt, z]
    if has_state:
        nb = h0.shape[0]
        in_specs += [pl.BlockSpec((nchunk, SUBLANES, CONV_DIM), lambda i: (i, 0, 0)),
                     pl.BlockSpec((nchunk, D_SSM, SSM_STATE), lambda i: (i, 0, 0))]
        args += [conv_prev, h0]
        s_shape = jax.ShapeDtypeStruct((nb, D_SSM, SSM_STATE), F32)
        s_spec = pl.BlockSpec((nchunk, D_SSM, SSM_STATE), lambda i: (i, 0, 0))
        scratch = [pltpu.VMEM((nchunk, lc + SUBLANES, D_SSM), F32),
                   pltpu.VMEM((nchunk, lc + SUBLANES, BC_DIM), F32)]
    else:
        s_shape = jax.ShapeDtypeStruct((D_SSM, SSM_STATE), F32)
        s_spec = const((D_SSM, SSM_STATE))
        scratch = [pltpu.VMEM((rows + SUBLANES, D_SSM), F32),
                   pltpu.VMEM((rows + SUBLANES, BC_DIM), F32)]
    in_specs += [const((SSM_CONV, CONV_DIM)), const((1, CONV_DIM)), const((1, LANES)),
                 const((1, LANES)), const((1, D_SSM)), const((1, D_SSM)),
                 const(e3.shape), const(es3.shape), const(t3.shape), const(ones_k.shape)]
    args += [conv_w, conv_b[None, :], dtb, alog, dskip_x, gn[None, :], e3, es3, t3, ones_k]
    return pl.pallas_call(
        functools.partial(_ssd_kernel, lc=lc, nchunk=nchunk, has_state=has_state),
        grid=(m // rows,),
        in_specs=in_specs,
        out_specs=(rowblk(D_SSM), s_spec),
        out_shape=(jax.ShapeDtypeStruct((m, D_SSM), BF16), s_shape),
        scratch_shapes=scratch,
        compiler_params=_cparams(("arbitrary",)),
        name="ssd_sample" if has_state else "ssd_prompt",
    )(*args)


def _out_proj_kernel(o_ref, y_ref, x_ref, w_ref, g_ref, x2_ref, hf_ref, w16_ref=None):
    if w16_ref is not None:
        w16_ref[...] = w_ref[...].astype(BF16)
        w_ref = w16_ref
    x2 = (x_ref[...]
          + jnp.dot(o_ref[...], w_ref[0:D_ATT, :], preferred_element_type=F32)
          + jnp.dot(y_ref[...], w_ref[D_ATT:D_ATT + D_SSM, :], preferred_element_type=F32))
    x2_ref[...] = x2
    hf_ref[...] = _rms(x2, g_ref[...]).astype(hf_ref.dtype)


def _out_proj(o, y, x, w, g, tm):
    m = x.shape[0]
    w_spec = pl.BlockSpec((D_ATT + D_SSM, D_MODEL), lambda i: (0, 0), pipeline_mode=pl.Buffered(1))
    out_specs = (pl.BlockSpec((tm, D_MODEL), lambda i: (i, 0)),
                 pl.BlockSpec((tm, D_MODEL), lambda i: (i, 0)))
    out_shape = (jax.ShapeDtypeStruct((m, D_MODEL), F32),
                 jax.ShapeDtypeStruct((m, D_MODEL), BF16))
    if w.dtype != BF16:
        assert m == tm, "the bf16 weight copy is written once: needs a single row tile"
        out_specs += (pl.BlockSpec((D_ATT + D_SSM, D_MODEL), lambda i: (0, 0)),)
        out_shape += (jax.ShapeDtypeStruct(w.shape, BF16),)
    return pl.pallas_call(
        _out_proj_kernel,
        grid=(m // tm,),
        in_specs=[
            pl.BlockSpec((tm, D_ATT), lambda i: (i, 0)),
            pl.BlockSpec((tm, D_SSM), lambda i: (i, 0)),
            pl.BlockSpec((tm, D_MODEL), lambda i: (i, 0)),
            w_spec,
            pl.BlockSpec((1, D_MODEL), lambda i: (0, 0)),
        ],
        out_specs=out_specs,
        out_shape=out_shape,
        compiler_params=_cparams(("arbitrary",)),
        name="out_proj",
    )(o, y, x, w, g)


def _ffn_kernel(*refs, tm, seq, final_norm):
    if seq is None:
        (hf_ref, x2_ref, wg_ref, wu_ref, wd_ref, cw_ref, cb_ref, gfin_ref,
         out_ref, gl_ref, g_scr, act_scr, carry_scr) = refs
    else:
        (hf_ref, x2_ref, wg_ref, wu_ref, wd_ref, cw_ref, cb_ref, gfin_ref, prev_ref, sel1_ref, sel2_ref,
         out_ref, gl_ref, wg16_ref, wu16_ref, wd16_ref, g_scr, act_scr) = refs
    i = pl.program_id(0)
    f = pl.program_id(1)
    nf = pl.num_programs(1) - 1
    pad = SUBLANES

    def weight(w_ref, w16_ref):
        if seq is None:
            return w_ref[...]
        w16 = w_ref[...].astype(BF16)
        w16_ref[...] = w16
        return w16

    def gate_up():
        hf = hf_ref[...]
        gate = jnp.dot(hf, weight(wg_ref, None if seq is None else wg16_ref), preferred_element_type=F32)
        up = jnp.dot(hf, weight(wu_ref, None if seq is None else wu16_ref), preferred_element_type=F32)
        g_scr[pad:pad + tm, :] = gate
        if seq is None:
            g_scr[0:pad, :] = carry_scr[f]
            carry_scr[f] = g_scr[tm:tm + pad, :]
            gl_ref[...] = g_scr[tm:tm + pad, :]
            g1 = g_scr[pad - 1:pad - 1 + tm, :]
            g2 = g_scr[pad - 2:pad - 2 + tm, :]
        else:
            g_scr[0:pad, :] = jnp.zeros((pad, gate.shape[1]), F32)
            gl_ref[...] = gate
            prev3 = jnp.concatenate(_split3(prev_ref[...]), axis=0)
            ov1 = jnp.dot(sel1_ref[...], prev3, preferred_element_type=F32)
            ov2 = jnp.dot(sel2_ref[...], prev3, preferred_element_type=F32)
            pos = lax.broadcasted_iota(jnp.int32, gate.shape, 0) % seq
            g1 = jnp.where(pos == 0, ov1, g_scr[pad - 1:pad - 1 + tm, :])
            g2 = jnp.where(pos < 2, ov2, g_scr[pad - 2:pad - 2 + tm, :])
        conv = cb_ref[...] + g2 * cw_ref[0:1, :] + g1 * cw_ref[1:2, :] + gate * cw_ref[2:3, :]
        act_scr[...] = (_silu(conv) * up).astype(BF16)

    def down():
        return jnp.dot(act_scr[...], weight(wd_ref, None if seq is None else wd16_ref),
                       preferred_element_type=F32)

    @pl.when(f == 0)
    def _():
        if seq is None:
            @pl.when(i == 0)
            def _():
                carry_scr[...] = jnp.zeros(carry_scr.shape, F32)
        out_ref[...] = x2_ref[...]
        gate_up()

    @pl.when((f > 0) & (f < nf))
    def _():
        contrib = down()
        gate_up()
        out_ref[...] += contrib

    @pl.when(f == nf)
    def _():
        res = out_ref[...] + down()
        out_ref[...] = _rms(res, gfin_ref[...]) if final_norm else res


def _ffn_prev_selectors(m, seq, nprev):
    sel1 = np.zeros((m, 3 * nprev), np.float32)
    sel2 = np.zeros((m, 3 * nprev), np.float32)
    nstate = FFN_CONV - 1
    for b in range(m // seq):
        for piece in range(3):
            base = piece * nprev + b * nstate
            sel1[b * seq, base + 1] = 1.0
            sel2[b * seq, base + 0] = 1.0
            sel2[b * seq + 1, base + 1] = 1.0
    return jnp.asarray(sel1, BF16), jnp.asarray(sel2, BF16)


def _ffn(hf, x2, wg, wu, wd, cw, cb, gfin, tm, tf, final_norm, seq=None, prev=None):
    m = hf.shape[0]
    nf = D_FF // tf
    cur = lambda f: jnp.minimum(f, nf - 1)
    last = lambda f: jnp.maximum(f - 1, 0)
    in_specs = [
        pl.BlockSpec((tm, D_MODEL), lambda i, f: (i, 0)),
        pl.BlockSpec((tm, D_MODEL), lambda i, f: (i, 0), pipeline_mode=pl.Buffered(1)),
        pl.BlockSpec((D_MODEL, tf), lambda i, f: (0, cur(f))),
        pl.BlockSpec((D_MODEL, tf), lambda i, f: (0, cur(f))),
        pl.BlockSpec((tf, D_MODEL), lambda i, f: (last(f), 0)),
        pl.BlockSpec((FFN_CONV, tf), lambda i, f: (0, cur(f))),
        pl.BlockSpec((1, tf), lambda i, f: (0, cur(f))),
        pl.BlockSpec((1, D_MODEL), lambda i, f: (0, 0)),
    ]
    args = [hf, x2, wg, wu, wd, cw, cb[None, :], gfin[None, :]]
    scratch = [pltpu.VMEM((tm + SUBLANES, tf), F32), pltpu.VMEM((tm, tf), BF16)]
    if seq is None:
        gl_shape = jax.ShapeDtypeStruct((m // tm * SUBLANES, D_FF), F32)
        gl_spec = pl.BlockSpec((SUBLANES, tf), lambda i, f: (i, cur(f)))
        scratch.append(pltpu.VMEM((nf, SUBLANES, tf), F32))
    else:
        assert m == tm, "sample FFN handles all sequences in one row tile"
        nprev = prev.shape[0]
        sel1, sel2 = _ffn_prev_selectors(m, seq, nprev)
        in_specs += [pl.BlockSpec((nprev, tf), lambda i, f: (0, cur(f))),
                     pl.BlockSpec(sel1.shape, lambda i, f: (0, 0)),
                     pl.BlockSpec(sel2.shape, lambda i, f: (0, 0))]
        args += [prev, sel1, sel2]
        gl_shape = jax.ShapeDtypeStruct((m, D_FF), F32)
        gl_spec = pl.BlockSpec((tm, tf), lambda i, f: (i, cur(f)))
    out_specs = [pl.BlockSpec((tm, D_MODEL), lambda i, f: (i, 0)), gl_spec]
    out_shape = [jax.ShapeDtypeStruct((m, D_MODEL), F32), gl_shape]
    if seq is not None:
        out_specs += [in_specs[2], in_specs[3], in_specs[4]]
        out_shape += [jax.ShapeDtypeStruct(wg.shape, BF16), jax.ShapeDtypeStruct(wu.shape, BF16),
                      jax.ShapeDtypeStruct(wd.shape, BF16)]
    return pl.pallas_call(
        functools.partial(_ffn_kernel, tm=tm, seq=seq, final_norm=final_norm),
        grid=(m // tm, nf + 1),
        in_specs=in_specs,
        out_specs=tuple(out_specs),
        out_shape=tuple(out_shape),
        scratch_shapes=scratch,
        compiler_params=_cparams(("arbitrary", "arbitrary")),
        name="ffn_sample" if seq is not None else "ffn_prompt",
    )(*args)


def _tiling(m, has_state):
    return dict(
        proj_tm=min(1024, m),
        attn_tile=256,
        ssd_chunks=2 * _SAMPLE_SEQS_PER_STEP if has_state else 8,
        out_tm=min(512, m),
        ffn_tm=min(512, m) if has_state else min(1024, m),
        ffn_tf=256 if has_state else 512,
    )


def _layer(x, w, lam_init, final_norm, gfin, *, batch, seq, state=None):
    m = x.shape[0]
    lamv = jnp.stack([w["lambda_q1"], w["lambda_k1"], w["lambda_q2"], w["lambda_k2"]])
    tiles = _tiling(m, state is not None)
    w16 = {}
    q, k, v, z, xs, bc, dt, *extra = _in_proj(x, w["norm_mix_g"][None, :], w["w_in_main"], w["w_in_dt"],
                                              tm=tiles["proj_tm"])
    if extra:
        w16["w_in_main"] = extra[0]

    if state is None:
        slopes = jnp.broadcast_to(
            jnp.asarray(2.0 ** (-8.0 * np.arange(1, ATT_HEADS + 1) / ATT_HEADS), F32)[:, None, None],
            (ATT_HEADS, 1, LANES))
        o = _attn_prompt(q, k, v, slopes, lamv, w["attn_subln_g"][None, :], lam_init,
                         tile=tiles["attn_tile"])
        y, s_new = _ssd(xs, bc, dt, z, w["conv_w"], w["conv_b"], w["dt_bias"], w["a_log"],
                        w["d_skip"], w["ssm_norm_g"], lc=CHUNK, nchunk=tiles["ssd_chunks"])
        conv_new = jnp.concatenate([xs[m - (SSM_CONV - 1):], bc[m - (SSM_CONV - 1):]], axis=-1)[None]
        s_new = s_new[None]
    else:
        cache_k, cache_v, conv_prev, ssm_prev, ffn_prev = state
        past = cache_k.shape[1]
        assert past % CHUNK == 0 and seq <= CHUNK
        o = _attn_sample(q, k, v, cache_k.reshape(batch, past * ATT_HEADS, V_DIM),
                         cache_v.reshape(batch, past * ATT_HEADS, V_DIM),
                         lamv, w["attn_subln_g"][None, :], lam_init, seq)
        prev8 = jnp.pad(conv_prev, ((0, 0), (SUBLANES - (SSM_CONV - 1), 0), (0, 0)))
        y, s_new = _ssd(xs, bc, dt, z, w["conv_w"], w["conv_b"], w["dt_bias"], w["a_log"],
                        w["d_skip"], w["ssm_norm_g"], lc=seq, nchunk=tiles["ssd_chunks"],
                        conv_prev=prev8, h0=ssm_prev.reshape(batch, D_SSM, SSM_STATE))
        conv_new = jnp.concatenate([xs.reshape(batch, seq, D_SSM)[:, seq - (SSM_CONV - 1):],
                                    bc.reshape(batch, seq, BC_DIM)[:, seq - (SSM_CONV - 1):]], axis=-1)

    x2, hf, *extra = _out_proj(o, y, x, w["w_out"], w["norm_ffn_g"][None, :], tm=tiles["out_tm"])
    if extra:
        w16["w_out"] = extra[0]

    ffn_args = (hf, x2, w["w_gate"], w["w_up"], w["w_down"], w["ffn_conv_w"], w["ffn_conv_b"], gfin)
    ffn_tiles = dict(tm=tiles["ffn_tm"], tf=tiles["ffn_tf"], final_norm=final_norm)
    if state is None:
        x3, gl = _ffn(*ffn_args, **ffn_tiles)
        ffn_new = gl[None, gl.shape[0] - (FFN_CONV - 1):]
    else:
        x3, gl, w16["w_gate"], w16["w_up"], w16["w_down"] = _ffn(
            *ffn_args, **ffn_tiles, seq=seq, prev=ffn_prev.reshape(batch * (FFN_CONV - 1), D_FF))
        ffn_new = gl.reshape(batch, seq, D_FF)[:, seq - (FFN_CONV - 1):]
    k_new = k.reshape(batch, seq, ATT_HEADS, 2 * QK_DIM)
    v_new = v.reshape(batch, seq, ATT_HEADS, V_DIM)
    s_new = s_new.reshape(batch, SSM_HEADS, SSM_HEAD_DIM, SSM_STATE)
    return (x3, k_new, v_new, conv_new, s_new, ffn_new), w16


def kernel(x_prompt, x_sample, cache_k, cache_v, state_ssm_conv, state_ssm, state_ffn_conv, norm_mix_g, w_in, lambda_q1, lambda_k1, lambda_q2, lambda_k2, attn_subln_g, conv_w, conv_b, dt_bias, a_log, d_skip, ssm_norm_g, w_out, norm_ffn_g, w_gate, w_up, ffn_conv_w, ffn_conv_b, w_down, norm_final_g):
    depth = w_in.shape[0]
    pb, pl_len, _ = x_prompt.shape
    sb, sl_len, _ = x_sample.shape
    assert pb == 1 and pl_len % CHUNK == 0
    xp = x_prompt.reshape(pb * pl_len, D_MODEL)
    xs = x_sample.reshape(sb * sl_len, D_MODEL)
    n_main = 2 * D_ATT + D_ATT + D_SSM + CONV_DIM
    outs_p, outs_s = [], []
    for layer in range(depth):
        lam_init = 0.8 - 0.6 * math.exp(-0.3 * layer)
        w_in_t = jnp.swapaxes(w_in[layer], 0, 1)
        w = dict(
            norm_mix_g=norm_mix_g[layer],
            w_in_main=w_in_t,
            w_in_dt=jnp.pad(w_in_t[n_main:].astype(BF16), ((0, LANES - SSM_HEADS), (0, 0))),
            lambda_q1=lambda_q1[layer], lambda_k1=lambda_k1[layer],
            lambda_q2=lambda_q2[layer], lambda_k2=lambda_k2[layer],
            attn_subln_g=attn_subln_g[layer], conv_w=conv_w[layer], conv_b=conv_b[layer],
            dt_bias=dt_bias[layer], a_log=a_log[layer], d_skip=d_skip[layer],
            ssm_norm_g=ssm_norm_g[layer], w_out=w_out[layer],
            norm_ffn_g=norm_ffn_g[layer], w_gate=w_gate[layer],
            w_up=w_up[layer], ffn_conv_w=ffn_conv_w[layer],
            ffn_conv_b=ffn_conv_b[layer], w_down=w_down[layer],
        )
        last = layer == depth - 1
        (xs, *new_s), w16 = _layer(xs, w, lam_init, last, norm_final_g, batch=sb, seq=sl_len,
                                   state=(cache_k[layer], cache_v[layer], state_ssm_conv[layer],
                                          state_ssm[layer], state_ffn_conv[layer]))
        (xp, *new_p), _ = _layer(xp, {**w, **w16}, lam_init, last, norm_final_g, batch=pb, seq=pl_len)
        outs_p.append(new_p)
        outs_s.append(new_s)
    stack = lambda outs, idx: jnp.stack([o[idx] for o in outs])
    return (xp.reshape(pb, pl_len, D_MODEL), xs.reshape(sb, sl_len, D_MODEL),
            *[stack(outs_p, idx) for idx in range(5)],
            *[stack(outs_s, idx) for idx in range(5)])
```

```python
import functools
import math

import jax
import jax.numpy as jnp
import numpy as np
from jax import lax
from jax.experimental import pallas as pl
from jax.experimental.pallas import tpu as pltpu

F32 = jnp.float32
BF16 = jnp.bfloat16

D_MODEL = 2048
CHUNK = 64
ATT_HEADS = 8
QK_DIM = 64
V_DIM = 128
D_ATT = ATT_HEADS * V_DIM
SSM_HEADS = 16
SSM_HEAD_DIM = 64
D_SSM = SSM_HEADS * SSM_HEAD_DIM
SSM_GROUPS = 2
SSM_STATE = 128
SSM_CONV = 4
BC_DIM = 2 * SSM_GROUPS * SSM_STATE
CONV_DIM = D_SSM + BC_DIM
D_FF = 5632
FFN_CONV = 3
EPS = 1e-6
LOG2E = math.log2(math.e)
Q_SCALE = QK_DIM ** -0.5 * LOG2E
LANES = 128
SUBLANES = 8
VMEM_LIMIT = 58 * 1024 * 1024

_NT = (((1,), (1,)), ((), ()))
_TN = (((0,), (0,)), ((), ()))


def _cparams(sem):
    return pltpu.CompilerParams(dimension_semantics=sem, vmem_limit_bytes=VMEM_LIMIT)


def _silu(x):
    return x * jax.nn.sigmoid(x)


def _rms(x, g):
    return x * lax.rsqrt(jnp.mean(x * x, axis=-1, keepdims=True) + EPS) * g


_PROJ_TN = 512
_PROJ_SEGS = ((0, 2), (2, 4), (4, 6), (6, 8), (8, 10), (10, 11))


def _in_proj_kernel(x_ref, g_ref, w_ref, wdt_ref,
                    q_ref, k_ref, v_ref, z_ref, xs_ref, bc_ref, dt_ref, *rest):
    w16_ref, h_scr = rest if len(rest) == 2 else (None, rest[0])
    j = pl.program_id(1)

    @pl.when(j == 0)
    def _():
        hb = _rms(x_ref[...], g_ref[...]).astype(BF16)
        h_scr[...] = hb
        dt_ref[...] = lax.dot_general(hb, wdt_ref[...], _NT, preferred_element_type=F32)

    outs = (q_ref, k_ref, v_ref, z_ref, xs_ref, bc_ref)
    for (lo, hi), ref in zip(_PROJ_SEGS, outs):
        @pl.when((j >= lo) & (j < hi))
        def _(ref=ref):
            w = w_ref[...]
            if w16_ref is not None:
                w = w.astype(BF16)
                w16_ref[...] = w
            res = lax.dot_general(h_scr[...], w, _NT, preferred_element_type=F32)
            val = res * Q_SCALE if ref is q_ref else res
            ref[...] = val.astype(ref.dtype)


def _in_proj(x, g, w_main, w_dt, tm):
    m = x.shape[0]
    tn = _PROJ_TN
    nj = w_main.shape[0] // tn

    def seg_spec(lo, hi):
        return pl.BlockSpec((tm, tn), lambda i, j: (i, jnp.clip(j - lo, 0, hi - lo - 1)))

    out_shape = (
        jax.ShapeDtypeStruct((m, D_ATT), BF16),
        jax.ShapeDtypeStruct((m, D_ATT), F32),
        jax.ShapeDtypeStruct((m, D_ATT), F32),
        jax.ShapeDtypeStruct((m, D_SSM), F32),
        jax.ShapeDtypeStruct((m, D_SSM), F32),
        jax.ShapeDtypeStruct((m, BC_DIM), F32),
        jax.ShapeDtypeStruct((m, LANES), F32),
    )
    out_specs = tuple(seg_spec(lo, hi) for lo, hi in _PROJ_SEGS) + (
        pl.BlockSpec((tm, LANES), lambda i, j: (i, 0)),)
    if w_main.dtype != BF16:
        assert m == tm, "the bf16 weight copy is written once per tile: needs a single row tile"
        out_shape += (jax.ShapeDtypeStruct((nj * tn, D_MODEL), BF16),)
        out_specs += (pl.BlockSpec((tn, D_MODEL), lambda i, j: (j, 0)),)
    return pl.pallas_call(
        _in_proj_kernel,
        grid=(m // tm, nj),
        in_specs=[
            pl.BlockSpec((tm, D_MODEL), lambda i, j: (i, 0)),
            pl.BlockSpec((1, D_MODEL), lambda i, j: (0, 0)),
            pl.BlockSpec((tn, D_MODEL), lambda i, j: (j, 0)),
            pl.BlockSpec((LANES, D_MODEL), lambda i, j: (0, 0)),
        ],
        out_specs=out_specs,
        out_shape=out_shape,
        scratch_shapes=[pltpu.VMEM((tm, D_MODEL), BF16)],
        compiler_params=_cparams(("arbitrary", "arbitrary")),
        name="in_proj",
    )(x, g, w_main, w_dt)


def _lambda_value(lamv_ref, lam_init):
    lv = lamv_ref[...]
    s1 = jnp.sum(lv[0:1] * lv[1:2], axis=-1, keepdims=True)
    s2 = jnp.sum(lv[2:3] * lv[3:4], axis=-1, keepdims=True)
    return jnp.exp(s1) - jnp.exp(s2) + lam_init


def _split_maps(q):
    lane = lax.broadcasted_iota(jnp.int32, q.shape, 1)
    zero = jnp.zeros_like(q)
    return jnp.where(lane < QK_DIM, q, zero), jnp.where(lane >= QK_DIM, q, zero)


_VT_CHUNK = 512
_VT_PAD = 16
_UNROLL_SHIFT = 2
_UNROLL = 1 << _UNROLL_SHIFT
_HEADS_PER_STEP = 2
_QTILES_PER_GROUP = 4


def _attn_prompt_kernel(q_ref, k_ref, v_ref, slope_ref, lamv_ref, g_ref, dmask_ref, dneg_ref, o_ref,
                        k_scr, vt_scr, acc_scr, t0_scr, t1_scr, dbias_scr, stat_scr,
                        *, tile, lam_init):
    length = k_ref.shape[0]
    width = 2 * tile
    nheads = k_scr.shape[0]
    lam = _lambda_value(lamv_ref, lam_init)

    lane = lax.broadcasted_iota(jnp.int32, (length, LANES), 1)
    koff = lax.broadcasted_iota(jnp.int32, (length, LANES), 0) % tile
    koff = jnp.where(lane < 3, koff, 0).astype(F32).astype(BF16)
    extra = (lax.broadcasted_iota(jnp.int32, (_VT_PAD, tile), 0) == 0).astype(F32).astype(BF16)
    brow = lax.broadcasted_iota(jnp.int32, (LANES, width), 0)
    per = _VT_CHUNK // tile
    ws, q_biases = [], []
    for hh in range(nheads):
        cols = slice(hh * V_DIM, (hh + 1) * V_DIM)
        k_scr[hh, :, 0:V_DIM] = k_ref[:, cols].astype(BF16)
        k_scr[hh, :, V_DIM:V_DIM + LANES] = koff
        for c in range(length // _VT_CHUNK):
            vt = v_ref[c * _VT_CHUNK:(c + 1) * _VT_CHUNK, cols].T.astype(BF16)
            for s in range(per):
                vt_scr[hh, c * per + s, 0:V_DIM, :] = vt[:, s * tile:(s + 1) * tile]
                vt_scr[hh, c * per + s, V_DIM:V_DIM + _VT_PAD, :] = extra
        w = slope_ref[hh][:, :1] * LOG2E
        w_hi, w_mid, w_lo = (piece.astype(F32) for piece in _split3(w))
        q_bias = jnp.where(brow == 0, w_hi, jnp.where(brow == 1, w_mid, jnp.where(brow == 2, w_lo, 0.0)))
        ws.append(w)
        q_biases.append(q_bias.astype(BF16))
        dbias_scr[hh] = w * dmask_ref[...] + dneg_ref[...]

    def q_pair(qp, _):
        _attn_prompt_pair(qp, ws, lam, q_biases, q_ref, g_ref, o_ref,
                          k_scr, vt_scr, acc_scr, (t0_scr, t1_scr),
                          dbias_scr, stat_scr, tile=tile, lam_init=lam_init)
        return 0

    lax.fori_loop(0, length // (_QTILES_PER_GROUP * tile), q_pair, 0)


def _attn_prompt_pair(qp, ws, lam, q_biases, q_ref, g_ref, o_ref,
                      k_scr, vt_scr, acc_scr, t_slots, dbias_scr, stat_scr, *, tile, lam_init):
    width = 2 * tile
    nheads = len(ws)
    group = _QTILES_PER_GROUP
    streams = [(hh, u) for hh in range(nheads) for u in range(group)]
    nfull = group * qp
    qis = [nfull + u for _, u in streams]
    rows = [pl.ds(pl.multiple_of(qi * tile, tile), tile) for qi in qis]
    q_augs = []
    for s, (hh, _) in enumerate(streams):
        q1, q2 = _split_maps(q_ref[rows[s], hh * V_DIM:(hh + 1) * V_DIM])
        q_t = jnp.concatenate([q1, q2], axis=0).astype(F32).T.astype(BF16)
        q_augs.append(jnp.concatenate([q_t, q_biases[hh]], axis=0))

    def scores(s, j):
        k0 = pl.multiple_of(jnp.minimum(j, qis[s]) * tile, tile)
        return jnp.dot(k_scr[streams[s][0], pl.ds(k0, tile), :], q_augs[s], preferred_element_type=F32)

    TMAX = 0

    def step(s, j, slot, m_old):
        hh = streams[s][0]
        off = -ws[hh] * jnp.asarray((qis[s] - j) * tile, F32)
        m_new = jnp.maximum(m_old, stat_scr[s, TMAX:TMAX + 1, :] + off)
        t_next = scores(s, j + 1)
        t_slots[1 - slot][s] = t_next
        stat_scr[s, TMAX:TMAX + 1, :] = jnp.max(t_next, axis=0, keepdims=True)
        p = jnp.exp2(t_slots[slot][s] + (off - m_new)).astype(BF16)
        pv = jnp.dot(vt_scr[hh, j], p, preferred_element_type=F32)
        acc_scr[s] = jnp.exp2(m_old - m_new) * acc_scr[s] + pv
        return m_new

    def run(first, ntiles_per_iter, niter, carry):
        def body(i, carry):
            for k in range(ntiles_per_iter):
                carry = tuple(step(s, first + ntiles_per_iter * i + k, k % 2, carry[s])
                              for s in range(len(streams)))
            return carry
        return lax.fori_loop(0, niter, body, carry)

    carry = []
    for s in range(len(streams)):
        acc_scr[s] = jnp.zeros(acc_scr.shape[1:], F32)
        t_first = scores(s, 0)
        t_slots[0][s] = t_first
        stat_scr[s, TMAX:TMAX + 1, :] = jnp.max(t_first, axis=0, keepdims=True)
        carry.append(jnp.full((1, width), -0.5 * float(jnp.finfo(F32).max), F32))
    n_main = lax.shift_right_logical(nfull, _UNROLL_SHIFT)
    done = n_main * _UNROLL
    carry = run(0, _UNROLL, n_main, tuple(carry))
    carry = run(done, 2, lax.shift_right_logical(nfull - done, 1), carry)

    m_run = list(carry)
    diag_slot = [0] * len(streams)
    for e in range(group - 1):
        for s, (hh, u) in enumerate(streams):
            if u > e:
                m_run[s] = step(s, nfull + e, diag_slot[s], m_run[s])
                diag_slot[s] = 1 - diag_slot[s]
    for s, (hh, u) in enumerate(streams):
        t = t_slots[diag_slot[s]][s] + dbias_scr[hh]
        m_new = jnp.maximum(m_run[s], jnp.max(t, axis=0, keepdims=True))
        p = jnp.exp2(t - m_new).astype(BF16)
        acc_scr[s] = (jnp.exp2(m_run[s] - m_new) * acc_scr[s]
                      + jnp.dot(vt_scr[hh, qis[s]], p, preferred_element_type=F32))
    for s, (hh, u) in enumerate(streams):
        acc = acc_scr[s]
        on = acc[0:V_DIM] / acc[V_DIM:V_DIM + 1]
        o_t = on[:, :tile] - lam * on[:, tile:]
        o_t = o_t * lax.rsqrt(jnp.mean(o_t * o_t, axis=0, keepdims=True) + EPS)
        o_ref[rows[s], hh * V_DIM:(hh + 1) * V_DIM] = (
            o_t.T * g_ref[...] * (1.0 - lam_init)).astype(o_ref.dtype)


def _attn_prompt(q, k, v, slopes, lamv, g, lam_init, tile):
    length = q.shape[0]
    key = np.arange(tile)[:, None]
    qry = np.tile(np.arange(tile), 2)[None, :]
    dmask = jnp.asarray((qry - np.abs(qry - key)) - key, F32)
    dneg = jnp.asarray(np.where(key // CHUNK <= qry // CHUNK, 0.0, -np.inf), F32)
    nh = _HEADS_PER_STEP
    nstream = nh * _QTILES_PER_GROUP
    diag_spec = pl.BlockSpec((tile, 2 * tile), lambda h: (0, 0), pipeline_mode=pl.Buffered(1))
    head_spec = pl.BlockSpec((length, nh * V_DIM), lambda h: (0, h), pipeline_mode=pl.Buffered(1))
    kv_spec = pl.BlockSpec((length, nh * V_DIM), lambda h: (0, h), pipeline_mode=pl.Buffered(1))
    return pl.pallas_call(
        functools.partial(_attn_prompt_kernel, tile=tile, lam_init=lam_init),
        grid=(ATT_HEADS // nh,),
        in_specs=[
            kv_spec, kv_spec, kv_spec,
            pl.BlockSpec((nh, 1, LANES), lambda h: (h, 0, 0)),
            pl.BlockSpec((4, QK_DIM), lambda h: (0, 0)),
            pl.BlockSpec((1, V_DIM), lambda h: (0, 0)),
            diag_spec, diag_spec,
        ],
        out_specs=head_spec,
        out_shape=jax.ShapeDtypeStruct((length, D_ATT), BF16),
        scratch_shapes=[
            pltpu.VMEM((nh, length, V_DIM + LANES), BF16),
            pltpu.VMEM((nh, length // tile, V_DIM + _VT_PAD, tile), BF16),
            pltpu.VMEM((nstream, V_DIM + _VT_PAD, 2 * tile), F32),
            pltpu.VMEM((nstream, tile, 2 * tile), F32),
            pltpu.VMEM((nstream, tile, 2 * tile), F32),
            pltpu.VMEM((nh, tile, 2 * tile), F32),
            pltpu.VMEM((nstream, SUBLANES, 2 * tile), F32),
        ],
        compiler_params=_cparams(("arbitrary",)),
        name="attn_prompt",
    )(q, k, v, slopes, lamv, g, dmask, dneg)


_SAMPLE_SEQS_PER_STEP = 2


def _attn_sample_kernel(q_ref, kn_ref, vn_ref, ck_ref, cv_ref, lamv_ref, g_ref, bias_c_ref, bias_n_ref,
                        o_ref, *, seq, past, lam_init):
    lam = _lambda_value(lamv_ref, lam_init)
    rows = 2 * seq
    for bi in range(ck_ref.shape[0]):
        br = slice(bi * seq, (bi + 1) * seq)
        sc, sn = [], []
        for h in range(ATT_HEADS):
            sl = slice(h * V_DIM, (h + 1) * V_DIM)
            q1, q2 = _split_maps(q_ref[br, sl])
            qq = jnp.concatenate([q1, q2], axis=0)
            kc = ck_ref[bi, pl.ds(h, past, stride=ATT_HEADS), :].astype(BF16)
            sc.append(lax.dot_general(qq, kc, _NT, preferred_element_type=F32))
            sn.append(lax.dot_general(qq, kn_ref[br, sl].astype(BF16), _NT, preferred_element_type=F32))
        sc = jnp.concatenate(sc, axis=0) - bias_c_ref[...]
        sn = jnp.concatenate(sn, axis=0) - bias_n_ref[...]
        m = jnp.maximum(jnp.max(sc, axis=-1, keepdims=True), jnp.max(sn, axis=-1, keepdims=True))
        pc = jnp.exp2(sc - m)
        pn = jnp.exp2(sn - m)
        inv_l = 1.0 / (jnp.sum(pc, axis=-1, keepdims=True) + jnp.sum(pn, axis=-1, keepdims=True))
        pc = pc.astype(BF16)
        pn = pn.astype(BF16)
        for h in range(ATT_HEADS):
            sl = slice(h * V_DIM, (h + 1) * V_DIM)
            hr = slice(h * rows, (h + 1) * rows)
            vc = cv_ref[bi, pl.ds(h, past, stride=ATT_HEADS), :].astype(BF16)
            acc = (jnp.dot(pc[hr], vc, preferred_element_type=F32)
                   + jnp.dot(pn[hr], vn_ref[br, sl].astype(BF16), preferred_element_type=F32))
            on = acc * inv_l[hr]
            o = on[:seq] - lam * on[seq:]
            o_ref[br, sl] = (_rms(o, g_ref[...]) * (1.0 - lam_init)).astype(o_ref.dtype)


def _attn_sample(q, k_new, v_new, cache_k, cache_v, lamv, g, lam_init, seq):
    nb, past = cache_k.shape[0], cache_k.shape[1] // ATT_HEADS
    slope = np.repeat(2.0 ** (-8.0 * np.arange(1, ATT_HEADS + 1) / ATT_HEADS), 2 * seq)[:, None] * LOG2E
    qpos = np.tile(np.arange(seq), 2 * ATT_HEADS)[:, None]
    bias_c = jnp.asarray(slope * (qpos + past - np.arange(past)[None, :]), F32)
    bias_n = jnp.asarray(slope * np.abs(qpos - np.arange(seq)[None, :]), F32)
    per = _SAMPLE_SEQS_PER_STEP
    row_spec = pl.BlockSpec((per * seq, D_ATT), lambda b: (b, 0))
    cache_spec = pl.BlockSpec((per, past * ATT_HEADS, V_DIM), lambda b: (b, 0, 0))

    def const(arr):
        return pl.BlockSpec(arr.shape, lambda b: (0, 0), pipeline_mode=pl.Buffered(1))

    return pl.pallas_call(
        functools.partial(_attn_sample_kernel, seq=seq, past=past, lam_init=lam_init),
        grid=(nb // per,),
        in_specs=[row_spec, row_spec, row_spec, cache_spec, cache_spec,
                  pl.BlockSpec((4, QK_DIM), lambda b: (0, 0)),
                  pl.BlockSpec((1, V_DIM), lambda b: (0, 0)),
                  const(bias_c), const(bias_n)],
        out_specs=row_spec,
        out_shape=jax.ShapeDtypeStruct((nb * seq, D_ATT), BF16),
        compiler_params=_cparams(("arbitrary",)),
        name="attn_sample",
    )(q, k_new, v_new, cache_k, cache_v, lamv, g, bias_c, bias_n)


def _split3(x):
    hi = x.astype(BF16)
    r = x - hi.astype(F32)
    mid = r.astype(BF16)
    lo = (r - mid.astype(F32)).astype(BF16)
    return hi, mid, lo


def _ssd_chunk(xs, bm, cm, dt, z, s_ref, a_heads, dskip, gn, e3, es3, t3, ones_k, lc):
    seg_w = SSM_HEADS * lc
    half = D_SSM // SSM_GROUPS
    d3 = jnp.concatenate(_split3(dt), axis=1)
    dt_x = jnp.dot(d3, e3, preferred_element_type=F32)
    la3 = jnp.concatenate(_split3(dt * a_heads), axis=0)
    ac3 = jnp.concatenate(_split3(jnp.dot(t3, la3, preferred_element_type=F32)), axis=1)
    acol = jnp.dot(ac3, e3, preferred_element_type=F32)
    acol_s = acol if seg_w == D_SSM else jnp.dot(ac3, es3, preferred_element_type=F32)
    t_idx = lax.broadcasted_iota(jnp.int32, (lc, seg_w), 0)
    s_idx = lax.broadcasted_iota(jnp.int32, (lc, seg_w), 1) % lc
    arow = jnp.sum(jnp.where(t_idx == s_idx, acol_s, 0.0), axis=0, keepdims=True)
    decay = jnp.where(t_idx >= s_idx, jnp.exp(acol_s - arow), 0.0)

    cb16 = cm.astype(BF16)
    bb16 = bm.astype(BF16)
    hpg = SSM_HEADS // SSM_GROUPS
    cbs = []
    for g in range(SSM_GROUPS):
        gs = slice(g * SSM_STATE, (g + 1) * SSM_STATE)
        b_rep = jnp.concatenate([bb16[:, gs]] * hpg, axis=0)
        cbs.append(lax.dot_general(cb16[:, gs], b_rep, _NT, preferred_element_type=F32))
    mmat = (jnp.concatenate(cbs, axis=1) * decay).astype(BF16)

    xdt = xs * dt_x
    xdt16 = xdt.astype(BF16)
    hk = (2 * LANES) // lc
    wd = hk * SSM_HEAD_DIM
    blk = (lax.broadcasted_iota(jnp.int32, (hk * lc, wd), 0) // lc
           == lax.broadcasted_iota(jnp.int32, (hk * lc, wd), 1) // SSM_HEAD_DIM)
    parts = []
    for i in range(SSM_HEADS // hk):
        xd = xdt16[:, i * wd:(i + 1) * wd]
        bd = jnp.where(blk, jnp.concatenate([xd] * hk, axis=0), jnp.zeros((), BF16))
        parts.append(jnp.dot(mmat[:, i * hk * lc:(i + 1) * hk * lc], bd, preferred_element_type=F32))
    y_intra = jnp.concatenate(parts, axis=1) if len(parts) > 1 else parts[0]

    yi = []
    for g in range(SSM_GROUPS):
        sg = s_ref[g * half:(g + 1) * half, :].astype(BF16)
        yi.append(lax.dot_general(cb16[:, g * SSM_STATE:(g + 1) * SSM_STATE], sg, _NT,
                                  preferred_element_type=F32))
    y_inter = jnp.exp(acol) * jnp.concatenate(yi, axis=1)

    alast = acol[lc - 1:lc, :]
    dec_end = jnp.exp(alast - acol)
    xd_end = (xdt * dec_end).astype(BF16)
    krow = lax.broadcasted_iota(jnp.int32, (ones_k.shape[0], D_SSM), 0)
    a_hi, a_mid, a_lo = (piece.astype(F32) for piece in _split3(alast))
    pieces = jnp.where(krow == 0, a_hi, jnp.where(krow == 1, a_mid, jnp.where(krow == 2, a_lo, 0.0)))
    acl = lax.dot_general(pieces.astype(BF16), ones_k, _TN, preferred_element_type=F32)
    for g in range(SSM_GROUPS):
        rows = slice(g * half, (g + 1) * half)
        upd = lax.dot_general(xd_end[:, rows], bb16[:, g * SSM_STATE:(g + 1) * SSM_STATE], _TN,
                              preferred_element_type=F32)
        s_ref[rows, :] = jnp.exp(acl[rows, :]) * s_ref[rows, :] + upd

    y = (y_intra + y_inter + dskip * xs) * _silu(z)
    outs = []
    for g in range(SSM_GROUPS):
        cs = slice(g * half, (g + 1) * half)
        outs.append(_rms(y[:, cs], gn[:, cs]))
    return jnp.concatenate(outs, axis=1)


def _ssd_kernel(*refs, lc, nchunk, has_state):
    if has_state:
        (xs_ref, bc_ref, dt_ref, z_ref, prev_ref, h0_ref, cw_ref, cbias_ref, dtb_ref, alog_ref,
         dskip_ref, gn_ref, e3_ref, es3_ref, t3_ref, ones_ref,
         y_ref, s_ref, scr_x, scr_bc) = refs
    else:
        (xs_ref, bc_ref, dt_ref, z_ref, cw_ref, cbias_ref, dtb_ref, alog_ref,
         dskip_ref, gn_ref, e3_ref, es3_ref, t3_ref, ones_ref,
         y_ref, s_ref, scr_x, scr_bc) = refs
    rows = lc * nchunk
    pad = SUBLANES

    def conv_silu(scr_x, scr_bc, nrows):
        xc = cbias_ref[:, 0:D_SSM]
        bcc = cbias_ref[:, D_SSM:CONV_DIM]
        for tap in range(SSM_CONV):
            off = pad - (SSM_CONV - 1) + tap
            xc = xc + scr_x[off:off + nrows, :] * cw_ref[tap:tap + 1, 0:D_SSM]
            bcc = bcc + scr_bc[off:off + nrows, :] * cw_ref[tap:tap + 1, D_SSM:CONV_DIM]
        return _silu(xc), _silu(bcc)

    dt_in = dt_ref[...] + dtb_ref[...]
    dt = jnp.maximum(dt_in, 0.0) + jnp.log1p(jnp.exp(-jnp.abs(dt_in)))
    a_heads = -jnp.exp(alog_ref[...])
    nbm = SSM_GROUPS * SSM_STATE

    def chunk(c, xs_c, bc_c, s2d):
        rs = slice(c * lc, (c + 1) * lc)
        y = _ssd_chunk(xs_c, bc_c[:, 0:nbm], bc_c[:, nbm:2 * nbm], dt[rs],
                       z_ref[rs, :], s2d, a_heads, dskip_ref[...], gn_ref[...],
                       e3_ref[...], es3_ref[...], t3_ref[...], ones_ref[...], lc)
        y_ref[rs, :] = y.astype(y_ref.dtype)

    if has_state:
        for c in range(nchunk):
            rs = slice(c * lc, (c + 1) * lc)
            scr_x[c, 0:pad, :] = prev_ref[c, :, 0:D_SSM]
            scr_bc[c, 0:pad, :] = prev_ref[c, :, D_SSM:CONV_DIM]
            scr_x[c, pad:pad + lc, :] = xs_ref[rs, :]
            scr_bc[c, pad:pad + lc, :] = bc_ref[rs, :]
            s_ref[c] = h0_ref[c]
            xs_act, bc_act = conv_silu(scr_x.at[c], scr_bc.at[c], lc)
            chunk(c, xs_act, bc_act, s_ref.at[c])
    else:
        @pl.when(pl.program_id(0) == 0)
        def _():
            scr_x[0:pad, :] = jnp.zeros((pad, D_SSM), F32)
            scr_bc[0:pad, :] = jnp.zeros((pad, BC_DIM), F32)
            s_ref[...] = jnp.zeros(s_ref.shape, F32)

        scr_x[pad:pad + rows, :] = xs_ref[...]
        scr_bc[pad:pad + rows, :] = bc_ref[...]
        xs_act, bc_act = conv_silu(scr_x, scr_bc, rows)
        scr_x[0:pad, :] = scr_x[rows:rows + pad, :]
        scr_bc[0:pad, :] = scr_bc[rows:rows + pad, :]
        for c in range(nchunk):
            rs = slice(c * lc, (c + 1) * lc)
            chunk(c, xs_act[rs], bc_act[rs], s_ref)


def _ssd_constants(lc):
    seg_w = SSM_HEADS * lc
    head_of_lane = np.arange(D_SSM) // SSM_HEAD_DIM
    e = (np.arange(LANES)[:, None] == head_of_lane[None, :]).astype(np.float32)
    es = (np.arange(LANES)[:, None] == (np.arange(seg_w) // lc)[None, :]).astype(np.float32)
    tri = np.tril(np.ones((lc, lc), np.float32))
    return (jnp.asarray(np.concatenate([e] * 3, axis=0), BF16),
            jnp.asarray(np.concatenate([es] * 3, axis=0), BF16),
            jnp.asarray(np.concatenate([tri] * 3, axis=1), BF16),
            jnp.ones((2 * SUBLANES, SSM_STATE), BF16))


def _ssd(xs, bc, dt, z, conv_w, conv_b, dt_bias, a_log, d_skip, gn, lc, nchunk,
         conv_prev=None, h0=None):
    m = xs.shape[0]
    rows = lc * nchunk
    has_state = h0 is not None
    e3, es3, t3, ones_k = _ssd_constants(lc)
    dtb = jnp.zeros((1, LANES), F32).at[0, :SSM_HEADS].set(dt_bias)
    alog = jnp.zeros((1, LANES), F32).at[0, :SSM_HEADS].set(a_log)
    dskip_x = jnp.repeat(d_skip, SSM_HEAD_DIM)[None, :]

    def const(shape):
        return pl.BlockSpec(shape, lambda i: (0,) * len(shape))

    def rowblk(width):
        return pl.BlockSpec((rows, width), lambda i: (i, 0))

    in_specs = [rowblk(D_SSM), rowblk(BC_DIM), rowblk(LANES), rowblk(D_SSM)]
    args = [xs, bc, dt, z]
    if has_state:
        nb = h0.shape[0]
        in_specs += [pl.BlockSpec((nchunk, SUBLANES, CONV_DIM), lambda i: (i, 0, 0)),
                     pl.BlockSpec((nchunk, D_SSM, SSM_STATE), lambda i: (i, 0, 0))]
        args += [conv_prev, h0]
        s_shape = jax.ShapeDtypeStruct((nb, D_SSM, SSM_STATE), F32)
        s_spec = pl.BlockSpec((nchunk, D_SSM, SSM_STATE), lambda i: (i, 0, 0))
        scratch = [pltpu.VMEM((nchunk, lc + SUBLANES, D_SSM), F32),
                   pltpu.VMEM((nchunk, lc + SUBLANES, BC_DIM), F32)]
    else:
        s_shape = jax.ShapeDtypeStruct((D_SSM, SSM_STATE), F32)
        s_spec = const((D_SSM, SSM_STATE))
        scratch = [pltpu.VMEM((rows + SUBLANES, D_SSM), F32),
                   pltpu.VMEM((rows + SUBLANES, BC_DIM), F32)]
    in_specs += [const((SSM_CONV, CONV_DIM)), const((1, CONV_DIM)), const((1, LANES)),
                 const((1, LANES)), const((1, D_SSM)), const((1, D_SSM)),
                 const(e3.shape), const(es3.shape), const(t3.shape), const(ones_k.shape)]
    args += [conv_w, conv_b[None, :], dtb, alog, dskip_x, gn[None, :], e3, es3, t3, ones_k]
    return pl.pallas_call(
        functools.partial(_ssd_kernel, lc=lc, nchunk=nchunk, has_state=has_state),
        grid=(m // rows,),
        in_specs=in_specs,
        out_specs=(rowblk(D_SSM), s_spec),
        out_shape=(jax.ShapeDtypeStruct((m, D_SSM), BF16), s_shape),
        scratch_shapes=scratch,
        compiler_params=_cparams(("arbitrary",)),
        name="ssd_sample" if has_state else "ssd_prompt",
    )(*args)


def _out_proj_kernel(o_ref, y_ref, x_ref, w_ref, g_ref, x2_ref, hf_ref, w16_ref=None):
    if w16_ref is not None:
        w16_ref[...] = w_ref[...].astype(BF16)
        w_ref = w16_ref
    x2 = (x_ref[...]
          + jnp.dot(o_ref[...], w_ref[0:D_ATT, :], preferred_element_type=F32)
          + jnp.dot(y_ref[...], w_ref[D_ATT:D_ATT + D_SSM, :], preferred_element_type=F32))
    x2_ref[...] = x2
    hf_ref[...] = _rms(x2, g_ref[...]).astype(hf_ref.dtype)


def _out_proj(o, y, x, w, g, tm):
    m = x.shape[0]
    w_spec = pl.BlockSpec((D_ATT + D_SSM, D_MODEL), lambda i: (0, 0), pipeline_mode=pl.Buffered(1))
    out_specs = (pl.BlockSpec((tm, D_MODEL), lambda i: (i, 0)),
                 pl.BlockSpec((tm, D_MODEL), lambda i: (i, 0)))
    out_shape = (jax.ShapeDtypeStruct((m, D_MODEL), F32),
                 jax.ShapeDtypeStruct((m, D_MODEL), BF16))
    if w.dtype != BF16:
        assert m == tm, "the bf16 weight copy is written once: needs a single row tile"
        out_specs += (pl.BlockSpec((D_ATT + D_SSM, D_MODEL), lambda i: (0, 0)),)
        out_shape += (jax.ShapeDtypeStruct(w.shape, BF16),)
    return pl.pallas_call(
        _out_proj_kernel,
        grid=(m // tm,),
        in_specs=[
            pl.BlockSpec((tm, D_ATT), lambda i: (i, 0)),
            pl.BlockSpec((tm, D_SSM), lambda i: (i, 0)),
            pl.BlockSpec((tm, D_MODEL), lambda i: (i, 0)),
            w_spec,
            pl.BlockSpec((1, D_MODEL), lambda i: (0, 0)),
        ],
        out_specs=out_specs,
        out_shape=out_shape,
        compiler_params=_cparams(("arbitrary",)),
        name="out_proj",
    )(o, y, x, w, g)


_RESIDUAL_PIECES = 8


def _ffn_kernel(*refs, tm, seq, final_norm):
    if seq is None:
        (hf_ref, x2_ref, wg_ref, wu_ref, wd_ref, cw_ref, cb_ref, gfin_ref,
         out_ref, gl_ref, g_scr, act_scr, carry_scr) = refs
    else:
        (hf_ref, x2_ref, wg_ref, wu_ref, wd_ref, cw_ref, cb_ref, gfin_ref, prev_ref, sel1_ref, sel2_ref,
         out_ref, gl_ref, wg16_ref, wu16_ref, wd16_ref, g_scr, act_scr) = refs
    i = pl.program_id(0)
    f = pl.program_id(1)
    nf = pl.num_programs(1) - 1
    pad = SUBLANES
    res_rows = tm // _RESIDUAL_PIECES

    def weight(w_ref, w16_ref):
        if seq is None:
            return w_ref[...]
        w16 = w_ref[...].astype(BF16)
        w16_ref[...] = w16
        return w16

    def gate_up():
        hf = hf_ref[...]
        gate = jnp.dot(hf, weight(wg_ref, None if seq is None else wg16_ref), preferred_element_type=F32)
        up = jnp.dot(hf, weight(wu_ref, None if seq is None else wu16_ref), preferred_element_type=F32)
        g_scr[pad:pad + tm, :] = gate
        if seq is None:
            g_scr[0:pad, :] = carry_scr[f]
            carry_scr[f] = g_scr[tm:tm + pad, :]
            gl_ref[...] = g_scr[tm:tm + pad, :]
            g1 = g_scr[pad - 1:pad - 1 + tm, :]
            g2 = g_scr[pad - 2:pad - 2 + tm, :]
        else:
            g_scr[0:pad, :] = jnp.zeros((pad, gate.shape[1]), F32)
            gl_ref[...] = gate
            prev3 = jnp.concatenate(_split3(prev_ref[...]), axis=0)
            ov1 = jnp.dot(sel1_ref[...], prev3, preferred_element_type=F32)
            ov2 = jnp.dot(sel2_ref[...], prev3, preferred_element_type=F32)
            pos = lax.broadcasted_iota(jnp.int32, gate.shape, 0) % seq
            g1 = jnp.where(pos == 0, ov1, g_scr[pad - 1:pad - 1 + tm, :])
            g2 = jnp.where(pos < 2, ov2, g_scr[pad - 2:pad - 2 + tm, :])
        conv = cb_ref[...] + g2 * cw_ref[0:1, :] + g1 * cw_ref[1:2, :] + gate * cw_ref[2:3, :]
        act_scr[...] = (_silu(conv) * up).astype(BF16)

    def down():
        return jnp.dot(act_scr[...], weight(wd_ref, None if seq is None else wd16_ref),
                       preferred_element_type=F32)

    @pl.when(f == 0)
    def _():
        if seq is None:
            @pl.when(i == 0)
            def _():
                carry_scr[...] = jnp.zeros(carry_scr.shape, F32)
        out_ref[...] = jnp.zeros(out_ref.shape, F32)
        out_ref[0:res_rows, :] = x2_ref[...]
        gate_up()

    @pl.when((f > 0) & (f < nf))
    def _():
        contrib = down()
        gate_up()
        out_ref[...] += contrib
        r0 = pl.multiple_of(jnp.minimum(f, _RESIDUAL_PIECES - 1) * res_rows, res_rows)
        out_ref[pl.ds(r0, res_rows), :] += jnp.where(f < _RESIDUAL_PIECES, x2_ref[...], 0.0)

    @pl.when(f == nf)
    def _():
        res = out_ref[...] + down()
        out_ref[...] = _rms(res, gfin_ref[...]) if final_norm else res


def _ffn_prev_selectors(m, seq, nprev):
    sel1 = np.zeros((m, 3 * nprev), np.float32)
    sel2 = np.zeros((m, 3 * nprev), np.float32)
    nstate = FFN_CONV - 1
    for b in range(m // seq):
        for piece in range(3):
            base = piece * nprev + b * nstate
            sel1[b * seq, base + 1] = 1.0
            sel2[b * seq, base + 0] = 1.0
            sel2[b * seq + 1, base + 1] = 1.0
    return jnp.asarray(sel1, BF16), jnp.asarray(sel2, BF16)


def _ffn(hf, x2, wg, wu, wd, cw, cb, gfin, tm, tf, final_norm, seq=None, prev=None):
    m = hf.shape[0]
    nf = D_FF // tf
    assert nf >= _RESIDUAL_PIECES and tm % (_RESIDUAL_PIECES * SUBLANES) == 0
    cur = lambda f: jnp.minimum(f, nf - 1)
    last = lambda f: jnp.maximum(f - 1, 0)
    in_specs = [
        pl.BlockSpec((tm, D_MODEL), lambda i, f: (i, 0)),
        pl.BlockSpec((tm // _RESIDUAL_PIECES, D_MODEL),
                     lambda i, f: (i * _RESIDUAL_PIECES + jnp.minimum(f, _RESIDUAL_PIECES - 1), 0)),
        pl.BlockSpec((D_MODEL, tf), lambda i, f: (0, cur(f))),
        pl.BlockSpec((D_MODEL, tf), lambda i, f: (0, cur(f))),
        pl.BlockSpec((tf, D_MODEL), lambda i, f: (last(f), 0)),
        pl.BlockSpec((FFN_CONV, tf), lambda i, f: (0, cur(f))),
        pl.BlockSpec((1, tf), lambda i, f: (0, cur(f))),
        pl.BlockSpec((1, D_MODEL), lambda i, f: (0, 0)),
    ]
    args = [hf, x2, wg, wu, wd, cw, cb[None, :], gfin[None, :]]
    scratch = [pltpu.VMEM((tm + SUBLANES, tf), F32), pltpu.VMEM((tm, tf), BF16)]
    if seq is None:
        gl_shape = jax.ShapeDtypeStruct((m // tm * SUBLANES, D_FF), F32)
        gl_spec = pl.BlockSpec((SUBLANES, tf), lambda i, f: (i, cur(f)))
        scratch.append(pltpu.VMEM((nf, SUBLANES, tf), F32))
    else:
        assert m == tm, "sample FFN handles all sequences in one row tile"
        nprev = prev.shape[0]
        sel1, sel2 = _ffn_prev_selectors(m, seq, nprev)
        in_specs += [pl.BlockSpec((nprev, tf), lambda i, f: (0, cur(f))),
                     pl.BlockSpec(sel1.shape, lambda i, f: (0, 0)),
                     pl.BlockSpec(sel2.shape, lambda i, f: (0, 0))]
        args += [prev, sel1, sel2]
        gl_shape = jax.ShapeDtypeStruct((m, D_FF), F32)
        gl_spec = pl.BlockSpec((tm, tf), lambda i, f: (i, cur(f)))
    out_specs = [pl.BlockSpec((tm, D_MODEL), lambda i, f: (i, 0)), gl_spec]
    out_shape = [jax.ShapeDtypeStruct((m, D_MODEL), F32), gl_shape]
    if seq is not None:
        out_specs += [in_specs[2], in_specs[3], in_specs[4]]
        out_shape += [jax.ShapeDtypeStruct(wg.shape, BF16), jax.ShapeDtypeStruct(wu.shape, BF16),
                      jax.ShapeDtypeStruct(wd.shape, BF16)]
    return pl.pallas_call(
        functools.partial(_ffn_kernel, tm=tm, seq=seq, final_norm=final_norm),
        grid=(m // tm, nf + 1),
        in_specs=in_specs,
        out_specs=tuple(out_specs),
        out_shape=tuple(out_shape),
        scratch_shapes=scratch,
        compiler_params=_cparams(("arbitrary", "arbitrary")),
        name="ffn_sample" if seq is not None else "ffn_prompt",
    )(*args)


def _tiling(m, has_state):
    return dict(
        proj_tm=min(1024, m),
        attn_tile=256,
        ssd_chunks=2 * _SAMPLE_SEQS_PER_STEP if has_state else 8,
        out_tm=min(512, m),
        ffn_tm=min(512, m) if has_state else min(1024, m),
        ffn_tf=256 if has_state else 512,
    )


def _layer(x, w, lam_init, final_norm, gfin, *, batch, seq, state=None):
    m = x.shape[0]
    lamv = jnp.stack([w["lambda_q1"], w["lambda_k1"], w["lambda_q2"], w["lambda_k2"]])
    tiles = _tiling(m, state is not None)
    w16 = {}
    q, k, v, z, xs, bc, dt, *extra = _in_proj(x, w["norm_mix_g"][None, :], w["w_in_main"], w["w_in_dt"],
                                              tm=tiles["proj_tm"])
    if extra:
        w16["w_in_main"] = extra[0]

    if state is None:
        slopes = jnp.broadcast_to(
            jnp.asarray(2.0 ** (-8.0 * np.arange(1, ATT_HEADS + 1) / ATT_HEADS), F32)[:, None, None],
            (ATT_HEADS, 1, LANES))
        o = _attn_prompt(q, k, v, slopes, lamv, w["attn_subln_g"][None, :], lam_init,
                         tile=tiles["attn_tile"])
        y, s_new = _ssd(xs, bc, dt, z, w["conv_w"], w["conv_b"], w["dt_bias"], w["a_log"],
                        w["d_skip"], w["ssm_norm_g"], lc=CHUNK, nchunk=tiles["ssd_chunks"])
        conv_new = jnp.concatenate([xs[m - (SSM_CONV - 1):], bc[m - (SSM_CONV - 1):]], axis=-1)[None]
        s_new = s_new[None]
    else:
        cache_k, cache_v, conv_prev, ssm_prev, ffn_prev = state
        past = cache_k.shape[1]
        assert past % CHUNK == 0 and seq <= CHUNK
        o = _attn_sample(q, k, v, cache_k.reshape(batch, past * ATT_HEADS, V_DIM),
                         cache_v.reshape(batch, past * ATT_HEADS, V_DIM),
                         lamv, w["attn_subln_g"][None, :], lam_init, seq)
        prev8 = jnp.pad(conv_prev, ((0, 0), (SUBLANES - (SSM_CONV - 1), 0), (0, 0)))
        y, s_new = _ssd(xs, bc, dt, z, w["conv_w"], w["conv_b"], w["dt_bias"], w["a_log"],
                        w["d_skip"], w["ssm_norm_g"], lc=seq, nchunk=tiles["ssd_chunks"],
                        conv_prev=prev8, h0=ssm_prev.reshape(batch, D_SSM, SSM_STATE))
        conv_new = jnp.concatenate([xs.reshape(batch, seq, D_SSM)[:, seq - (SSM_CONV - 1):],
                                    bc.reshape(batch, seq, BC_DIM)[:, seq - (SSM_CONV - 1):]], axis=-1)

    x2, hf, *extra = _out_proj(o, y, x, w["w_out"], w["norm_ffn_g"][None, :], tm=tiles["out_tm"])
    if extra:
        w16["w_out"] = extra[0]

    ffn_args = (hf, x2, w["w_gate"], w["w_up"], w["w_down"], w["ffn_conv_w"], w["ffn_conv_b"], gfin)
    ffn_tiles = dict(tm=tiles["ffn_tm"], tf=tiles["ffn_tf"], final_norm=final_norm)
    if state is None:
        x3, gl = _ffn(*ffn_args, **ffn_tiles)
        ffn_new = gl[None, gl.shape[0] - (FFN_CONV - 1):]
    else:
        x3, gl, w16["w_gate"], w16["w_up"], w16["w_down"] = _ffn(
            *ffn_args, **ffn_tiles, seq=seq, prev=ffn_prev.reshape(batch * (FFN_CONV - 1), D_FF))
        ffn_new = gl.reshape(batch, seq, D_FF)[:, seq - (FFN_CONV - 1):]
    k_new = k.reshape(batch, seq, ATT_HEADS, 2 * QK_DIM)
    v_new = v.reshape(batch, seq, ATT_HEADS, V_DIM)
    s_new = s_new.reshape(batch, SSM_HEADS, SSM_HEAD_DIM, SSM_STATE)
    return (x3, k_new, v_new, conv_new, s_new, ffn_new), w16


def kernel(x_prompt, x_sample, cache_k, cache_v, state_ssm_conv, state_ssm, state_ffn_conv, norm_mix_g, w_in, lambda_q1, lambda_k1, lambda_q2, lambda_k2, attn_subln_g, conv_w, conv_b, dt_bias, a_log, d_skip, ssm_norm_g, w_out, norm_ffn_g, w_gate, w_up, ffn_conv_w, ffn_conv_b, w_down, norm_final_g):
    depth = w_in.shape[0]
    pb, pl_len, _ = x_prompt.shape
    sb, sl_len, _ = x_sample.shape
    assert pb == 1 and pl_len % CHUNK == 0
    xp = x_prompt.reshape(pb * pl_len, D_MODEL)
    xs = x_sample.reshape(sb * sl_len, D_MODEL)
    n_main = 2 * D_ATT + D_ATT + D_SSM + CONV_DIM
    outs_p, outs_s = [], []
    for layer in range(depth):
        lam_init = 0.8 - 0.6 * math.exp(-0.3 * layer)
        w_in_t = jnp.swapaxes(w_in[layer], 0, 1)
        w = dict(
            norm_mix_g=norm_mix_g[layer],
            w_in_main=w_in_t,
            w_in_dt=jnp.pad(w_in_t[n_main:].astype(BF16), ((0, LANES - SSM_HEADS), (0, 0))),
            lambda_q1=lambda_q1[layer], lambda_k1=lambda_k1[layer],
            lambda_q2=lambda_q2[layer], lambda_k2=lambda_k2[layer],
            attn_subln_g=attn_subln_g[layer], conv_w=conv_w[layer], conv_b=conv_b[layer],
            dt_bias=dt_bias[layer], a_log=a_log[layer], d_skip=d_skip[layer],
            ssm_norm_g=ssm_norm_g[layer], w_out=w_out[layer],
            norm_ffn_g=norm_ffn_g[layer], w_gate=w_gate[layer],
            w_up=w_up[layer], ffn_conv_w=ffn_conv_w[layer],
            ffn_conv_b=ffn_conv_b[layer], w_down=w_down[layer],
        )
        last = layer == depth - 1
        (xs, *new_s), w16 = _layer(xs, w, lam_init, last, norm_final_g, batch=sb, seq=sl_len,
                                   state=(cache_k[layer], cache_v[layer], state_ssm_conv[layer],
                                          state_ssm[layer], state_ffn_conv[layer]))
        (xp, *new_p), _ = _layer(xp, {**w, **w16}, lam_init, last, norm_final_g, batch=pb, seq=pl_len)
        outs_p.append(new_p)
        outs_s.append(new_s)
    stack = lambda outs, idx: jnp.stack([o[idx] for o in outs])
    return (xp.reshape(pb, pl_len, D_MODEL), xs.reshape(sb, sl_len, D_MODEL),
            *[stack(outs_p, idx) for idx in range(5)],
            *[stack(outs_s, idx) for idx in range(5)])
```

```python
import functools
import math

import jax
import jax.numpy as jnp
import numpy as np
from jax import lax
from jax.experimental import pallas as pl
from jax.experimental.pallas import tpu as pltpu

F32 = jnp.float32
BF16 = jnp.bfloat16

D_MODEL = 2048
CHUNK = 64
ATT_HEADS = 8
QK_DIM = 64
V_DIM = 128
D_ATT = ATT_HEADS * V_DIM
SSM_HEADS = 16
SSM_HEAD_DIM = 64
D_SSM = SSM_HEADS * SSM_HEAD_DIM
SSM_GROUPS = 2
SSM_STATE = 128
SSM_CONV = 4
BC_DIM = 2 * SSM_GROUPS * SSM_STATE
CONV_DIM = D_SSM + BC_DIM
D_FF = 5632
FFN_CONV = 3
EPS = 1e-6
LOG2E = math.log2(math.e)
Q_SCALE = QK_DIM ** -0.5 * LOG2E
LANES = 128
SUBLANES = 8
VMEM_LIMIT = 58 * 1024 * 1024

_NT = (((1,), (1,)), ((), ()))
_TN = (((0,), (0,)), ((), ()))


def _cparams(sem):
    return pltpu.CompilerParams(dimension_semantics=sem, vmem_limit_bytes=VMEM_LIMIT)


def _silu(x):
    return x * jax.nn.sigmoid(x)


def _rms(x, g):
    return x * lax.rsqrt(jnp.mean(x * x, axis=-1, keepdims=True) + EPS) * g


_PROJ_TN = 512
_PROJ_SEGS = ((0, 2), (2, 4), (4, 6), (6, 8), (8, 10), (10, 11))


def _in_proj_kernel(x_ref, g_ref, w_ref, wdt_ref,
                    q_ref, k_ref, v_ref, z_ref, xs_ref, bc_ref, dt_ref, *rest):
    w16_ref, k16_ref, v16_ref = None, None, None
    if len(rest) == 2:
        w16_ref, h_scr = rest
    else:
        k16_ref, v16_ref, h_scr = rest
    j = pl.program_id(1)

    @pl.when(j == 0)
    def _():
        hb = _rms(x_ref[...], g_ref[...]).astype(BF16)
        h_scr[...] = hb
        dt_ref[...] = lax.dot_general(hb, wdt_ref[...], _NT, preferred_element_type=F32)

    outs = (q_ref, k_ref, v_ref, z_ref, xs_ref, bc_ref)
    for (lo, hi), ref in zip(_PROJ_SEGS, outs):
        @pl.when((j >= lo) & (j < hi))
        def _(ref=ref):
            w = w_ref[...]
            if w16_ref is not None:
                w = w.astype(BF16)
                w16_ref[...] = w
            res = lax.dot_general(h_scr[...], w, _NT, preferred_element_type=F32)
            val = res * Q_SCALE if ref is q_ref else res
            ref[...] = val.astype(ref.dtype)
            copy_ref = k16_ref if ref is k_ref else v16_ref if ref is v_ref else None
            if copy_ref is not None:
                copy_ref[...] = res.astype(BF16)


def _in_proj(x, g, w_main, w_dt, tm):
    m = x.shape[0]
    tn = _PROJ_TN
    nj = w_main.shape[0] // tn

    def seg_spec(lo, hi):
        return pl.BlockSpec((tm, tn), lambda i, j: (i, jnp.clip(j - lo, 0, hi - lo - 1)))

    out_shape = (
        jax.ShapeDtypeStruct((m, D_ATT), BF16),
        jax.ShapeDtypeStruct((m, D_ATT), F32),
        jax.ShapeDtypeStruct((m, D_ATT), F32),
        jax.ShapeDtypeStruct((m, D_SSM), F32),
        jax.ShapeDtypeStruct((m, D_SSM), F32),
        jax.ShapeDtypeStruct((m, BC_DIM), F32),
        jax.ShapeDtypeStruct((m, LANES), F32),
    )
    out_specs = tuple(seg_spec(lo, hi) for lo, hi in _PROJ_SEGS) + (
        pl.BlockSpec((tm, LANES), lambda i, j: (i, 0)),)
    if w_main.dtype != BF16:
        assert m == tm, "the bf16 weight copy is written once per tile: needs a single row tile"
        out_shape += (jax.ShapeDtypeStruct((nj * tn, D_MODEL), BF16),)
        out_specs += (pl.BlockSpec((tn, D_MODEL), lambda i, j: (j, 0)),)
    else:
        out_shape += (jax.ShapeDtypeStruct((m, D_ATT), BF16),) * 2
        out_specs += (seg_spec(*_PROJ_SEGS[1]), seg_spec(*_PROJ_SEGS[2]))
    return pl.pallas_call(
        _in_proj_kernel,
        grid=(m // tm, nj),
        in_specs=[
            pl.BlockSpec((tm, D_MODEL), lambda i, j: (i, 0)),
            pl.BlockSpec((1, D_MODEL), lambda i, j: (0, 0)),
            pl.BlockSpec((tn, D_MODEL), lambda i, j: (j, 0)),
            pl.BlockSpec((LANES, D_MODEL), lambda i, j: (0, 0)),
        ],
        out_specs=out_specs,
        out_shape=out_shape,
        scratch_shapes=[pltpu.VMEM((tm, D_MODEL), BF16)],
        compiler_params=_cparams(("arbitrary", "arbitrary")),
        name="in_proj",
    )(x, g, w_main, w_dt)


def _lambda_value(lamv_ref, lam_init):
    lv = lamv_ref[...]
    s1 = jnp.sum(lv[0:1] * lv[1:2], axis=-1, keepdims=True)
    s2 = jnp.sum(lv[2:3] * lv[3:4], axis=-1, keepdims=True)
    return jnp.exp(s1) - jnp.exp(s2) + lam_init


def _split_maps(q):
    lane = lax.broadcasted_iota(jnp.int32, q.shape, 1)
    zero = jnp.zeros_like(q)
    return jnp.where(lane < QK_DIM, q, zero), jnp.where(lane >= QK_DIM, q, zero)


_VT_CHUNK = 512
_VT_PAD = 16
_UNROLL_SHIFT = 2
_UNROLL = 1 << _UNROLL_SHIFT
_HEADS_PER_STEP = 2
_QTILES_PER_GROUP = 4


def _attn_prompt_kernel(q_ref, k_ref, v_ref, slope_ref, lamv_ref, g_ref, dmask_ref, dneg_ref, o_ref,
                        k_scr, vt_scr, acc_scr, t0_scr, t1_scr, dbias_scr, stat_scr,
                        *, tile, lam_init):
    length = k_ref.shape[0]
    width = 2 * tile
    nheads = k_scr.shape[0]
    lam = _lambda_value(lamv_ref, lam_init)

    lane = lax.broadcasted_iota(jnp.int32, (length, LANES), 1)
    koff = lax.broadcasted_iota(jnp.int32, (length, LANES), 0) % tile
    koff = jnp.where(lane < 3, koff, 0).astype(F32).astype(BF16)
    extra = (lax.broadcasted_iota(jnp.int32, (_VT_PAD, tile), 0) == 0).astype(F32).astype(BF16)
    brow = lax.broadcasted_iota(jnp.int32, (LANES, width), 0)
    per = _VT_CHUNK // tile
    ws, q_biases = [], []
    for hh in range(nheads):
        cols = slice(hh * V_DIM, (hh + 1) * V_DIM)
        k_scr[hh, :, 0:V_DIM] = k_ref[:, cols]
        k_scr[hh, :, V_DIM:V_DIM + LANES] = koff
        for c in range(length // _VT_CHUNK):
            vt = v_ref[c * _VT_CHUNK:(c + 1) * _VT_CHUNK, cols].astype(F32).T.astype(BF16)
            for s in range(per):
                vt_scr[hh, c * per + s, 0:V_DIM, :] = vt[:, s * tile:(s + 1) * tile]
                vt_scr[hh, c * per + s, V_DIM:V_DIM + _VT_PAD, :] = extra
        w = slope_ref[hh][:, :1] * LOG2E
        w_hi, w_mid, w_lo = (piece.astype(F32) for piece in _split3(w))
        q_bias = jnp.where(brow == 0, w_hi, jnp.where(brow == 1, w_mid, jnp.where(brow == 2, w_lo, 0.0)))
        ws.append(w)
        q_biases.append(q_bias.astype(BF16))
        dbias_scr[hh] = w * dmask_ref[...] + dneg_ref[...]

    def q_pair(qp, _):
        _attn_prompt_pair(qp, ws, lam, q_biases, q_ref, g_ref, o_ref,
                          k_scr, vt_scr, acc_scr, (t0_scr, t1_scr),
                          dbias_scr, stat_scr, tile=tile, lam_init=lam_init)
        return 0

    lax.fori_loop(0, length // (_QTILES_PER_GROUP * tile), q_pair, 0)


def _attn_prompt_pair(qp, ws, lam, q_biases, q_ref, g_ref, o_ref,
                      k_scr, vt_scr, acc_scr, t_slots, dbias_scr, stat_scr, *, tile, lam_init):
    width = 2 * tile
    nheads = len(ws)
    group = _QTILES_PER_GROUP
    streams = [(hh, u) for hh in range(nheads) for u in range(group)]
    nfull = group * qp
    qis = [nfull + u for _, u in streams]
    rows = [pl.ds(pl.multiple_of(qi * tile, tile), tile) for qi in qis]
    q_augs = []
    for s, (hh, _) in enumerate(streams):
        q1, q2 = _split_maps(q_ref[rows[s], hh * V_DIM:(hh + 1) * V_DIM])
        q_t = jnp.concatenate([q1, q2], axis=0).astype(F32).T.astype(BF16)
        q_augs.append(jnp.concatenate([q_t, q_biases[hh]], axis=0))

    def scores(s, j):
        k0 = pl.multiple_of(jnp.minimum(j, qis[s]) * tile, tile)
        return jnp.dot(k_scr[streams[s][0], pl.ds(k0, tile), :], q_augs[s], preferred_element_type=F32)

    TMAX = 0

    def step(s, j, slot, m_old):
        hh = streams[s][0]
        off = -ws[hh] * jnp.asarray((qis[s] - j) * tile, F32)
        m_new = jnp.maximum(m_old, stat_scr[s, TMAX:TMAX + 1, :] + off)
        t_next = scores(s, j + 1)
        t_slots[1 - slot][s] = t_next
        stat_scr[s, TMAX:TMAX + 1, :] = jnp.max(t_next, axis=0, keepdims=True)
        p = jnp.exp2(t_slots[slot][s] + (off - m_new)).astype(BF16)
        pv = jnp.dot(vt_scr[hh, j], p, preferred_element_type=F32)
        acc_scr[s] = jnp.exp2(m_old - m_new) * acc_scr[s] + pv
        return m_new

    def run(first, ntiles_per_iter, niter, carry):
        def body(i, carry):
            for k in range(ntiles_per_iter):
                carry = tuple(step(s, first + ntiles_per_iter * i + k, k % 2, carry[s])
                              for s in range(len(streams)))
            return carry
        return lax.fori_loop(0, niter, body, carry)

    carry = []
    for s in range(len(streams)):
        acc_scr[s] = jnp.zeros(acc_scr.shape[1:], F32)
        t_first = scores(s, 0)
        t_slots[0][s] = t_first
        stat_scr[s, TMAX:TMAX + 1, :] = jnp.max(t_first, axis=0, keepdims=True)
        carry.append(jnp.full((1, width), -0.5 * float(jnp.finfo(F32).max), F32))
    n_main = lax.shift_right_logical(nfull, _UNROLL_SHIFT)
    done = n_main * _UNROLL
    carry = run(0, _UNROLL, n_main, tuple(carry))
    carry = run(done, 2, lax.shift_right_logical(nfull - done, 1), carry)

    m_run = list(carry)
    diag_slot = [0] * len(streams)
    for e in range(group - 1):
        for s, (hh, u) in enumerate(streams):
            if u > e:
                m_run[s] = step(s, nfull + e, diag_slot[s], m_run[s])
                diag_slot[s] = 1 - diag_slot[s]
    for s, (hh, u) in enumerate(streams):
        t = t_slots[diag_slot[s]][s] + dbias_scr[hh]
        m_new = jnp.maximum(m_run[s], jnp.max(t, axis=0, keepdims=True))
        p = jnp.exp2(t - m_new).astype(BF16)
        acc_scr[s] = (jnp.exp2(m_run[s] - m_new) * acc_scr[s]
                      + jnp.dot(vt_scr[hh, qis[s]], p, preferred_element_type=F32))
    for s, (hh, u) in enumerate(streams):
        acc = acc_scr[s]
        on = acc[0:V_DIM] / acc[V_DIM:V_DIM + 1]
        o_t = on[:, :tile] - lam * on[:, tile:]
        o_t = o_t * lax.rsqrt(jnp.mean(o_t * o_t, axis=0, keepdims=True) + EPS)
        o_ref[rows[s], hh * V_DIM:(hh + 1) * V_DIM] = (
            o_t.T * g_ref[...] * (1.0 - lam_init)).astype(o_ref.dtype)


def _attn_prompt(q, k, v, slopes, lamv, g, lam_init, tile):
    length = q.shape[0]
    key = np.arange(tile)[:, None]
    qry = np.tile(np.arange(tile), 2)[None, :]
    dmask = jnp.asarray((qry - np.abs(qry - key)) - key, F32)
    dneg = jnp.asarray(np.where(key // CHUNK <= qry // CHUNK, 0.0, -np.inf), F32)
    nh = _HEADS_PER_STEP
    nstream = nh * _QTILES_PER_GROUP
    diag_spec = pl.BlockSpec((tile, 2 * tile), lambda h: (0, 0), pipeline_mode=pl.Buffered(1))
    head_spec = pl.BlockSpec((length, nh * V_DIM), lambda h: (0, h), pipeline_mode=pl.Buffered(1))
    kv_spec = pl.BlockSpec((length, nh * V_DIM), lambda h: (0, h))
    return pl.pallas_call(
        functools.partial(_attn_prompt_kernel, tile=tile, lam_init=lam_init),
        grid=(ATT_HEADS // nh,),
        in_specs=[
            head_spec, kv_spec, kv_spec,
            pl.BlockSpec((nh, 1, LANES), lambda h: (h, 0, 0)),
            pl.BlockSpec((4, QK_DIM), lambda h: (0, 0)),
            pl.BlockSpec((1, V_DIM), lambda h: (0, 0)),
            diag_spec, diag_spec,
        ],
        out_specs=head_spec,
        out_shape=jax.ShapeDtypeStruct((length, D_ATT), BF16),
        scratch_shapes=[
            pltpu.VMEM((nh, length, V_DIM + LANES), BF16),
            pltpu.VMEM((nh, length // tile, V_DIM + _VT_PAD, tile), BF16),
            pltpu.VMEM((nstream, V_DIM + _VT_PAD, 2 * tile), F32),
            pltpu.VMEM((nstream, tile, 2 * tile), F32),
            pltpu.VMEM((nstream, tile, 2 * tile), F32),
            pltpu.VMEM((nh, tile, 2 * tile), F32),
            pltpu.VMEM((nstream, SUBLANES, 2 * tile), F32),
        ],
        compiler_params=_cparams(("arbitrary",)),
        name="attn_prompt",
    )(q, k, v, slopes, lamv, g, dmask, dneg)


_SAMPLE_SEQS_PER_STEP = 2


def _attn_sample_kernel(q_ref, kn_ref, vn_ref, ck_ref, cv_ref, lamv_ref, g_ref, bias_c_ref, bias_n_ref,
                        o_ref, *, seq, past, lam_init):
    lam = _lambda_value(lamv_ref, lam_init)
    rows = 2 * seq
    for bi in range(ck_ref.shape[0]):
        br = slice(bi * seq, (bi + 1) * seq)
        sc, sn = [], []
        for h in range(ATT_HEADS):
            sl = slice(h * V_DIM, (h + 1) * V_DIM)
            q1, q2 = _split_maps(q_ref[br, sl])
            qq = jnp.concatenate([q1, q2], axis=0)
            kc = ck_ref[bi, pl.ds(h, past, stride=ATT_HEADS), :].astype(BF16)
            sc.append(lax.dot_general(qq, kc, _NT, preferred_element_type=F32))
            sn.append(lax.dot_general(qq, kn_ref[br, sl].astype(BF16), _NT, preferred_element_type=F32))
        sc = jnp.concatenate(sc, axis=0) - bias_c_ref[...]
        sn = jnp.concatenate(sn, axis=0) - bias_n_ref[...]
        m = jnp.maximum(jnp.max(sc, axis=-1, keepdims=True), jnp.max(sn, axis=-1, keepdims=True))
        pc = jnp.exp2(sc - m)
        pn = jnp.exp2(sn - m)
        inv_l = 1.0 / (jnp.sum(pc, axis=-1, keepdims=True) + jnp.sum(pn, axis=-1, keepdims=True))
        pc = pc.astype(BF16)
        pn = pn.astype(BF16)
        for h in range(ATT_HEADS):
            sl = slice(h * V_DIM, (h + 1) * V_DIM)
            hr = slice(h * rows, (h + 1) * rows)
            vc = cv_ref[bi, pl.ds(h, past, stride=ATT_HEADS), :].astype(BF16)
            acc = (jnp.dot(pc[hr], vc, preferred_element_type=F32)
                   + jnp.dot(pn[hr], vn_ref[br, sl].astype(BF16), preferred_element_type=F32))
            on = acc * inv_l[hr]
            o = on[:seq] - lam * on[seq:]
            o_ref[br, sl] = (_rms(o, g_ref[...]) * (1.0 - lam_init)).astype(o_ref.dtype)


def _attn_sample(q, k_new, v_new, cache_k, cache_v, lamv, g, lam_init, seq):
    nb, past = cache_k.shape[0], cache_k.shape[1] // ATT_HEADS
    slope = np.repeat(2.0 ** (-8.0 * np.arange(1, ATT_HEADS + 1) / ATT_HEADS), 2 * seq)[:, None] * LOG2E
    qpos = np.tile(np.arange(seq), 2 * ATT_HEADS)[:, None]
    bias_c = jnp.asarray(slope * (qpos + past - np.arange(past)[None, :]), F32)
    bias_n = jnp.asarray(slope * np.abs(qpos - np.arange(seq)[None, :]), F32)
    per = _SAMPLE_SEQS_PER_STEP
    row_spec = pl.BlockSpec((per * seq, D_ATT), lambda b: (b, 0))
    cache_spec = pl.BlockSpec((per, past * ATT_HEADS, V_DIM), lambda b: (b, 0, 0))

    def const(arr):
        return pl.BlockSpec(arr.shape, lambda b: (0, 0), pipeline_mode=pl.Buffered(1))

    return pl.pallas_call(
        functools.partial(_attn_sample_kernel, seq=seq, past=past, lam_init=lam_init),
        grid=(nb // per,),
        in_specs=[row_spec, row_spec, row_spec, cache_spec, cache_spec,
                  pl.BlockSpec((4, QK_DIM), lambda b: (0, 0)),
                  pl.BlockSpec((1, V_DIM), lambda b: (0, 0)),
                  const(bias_c), const(bias_n)],
        out_specs=row_spec,
        out_shape=jax.ShapeDtypeStruct((nb * seq, D_ATT), BF16),
        compiler_params=_cparams(("arbitrary",)),
        name="attn_sample",
    )(q, k_new, v_new, cache_k, cache_v, lamv, g, bias_c, bias_n)


def _split3(x):
    hi = x.astype(BF16)
    r = x - hi.astype(F32)
    mid = r.astype(BF16)
    lo = (r - mid.astype(F32)).astype(BF16)
    return hi, mid, lo


def _ssd_chunk(xs, bm, cm, dt, z, s_ref, a_heads, dskip, gn, e3, es3, t3, ones_k, lc):
    seg_w = SSM_HEADS * lc
    half = D_SSM // SSM_GROUPS
    d3 = jnp.concatenate(_split3(dt), axis=1)
    dt_x = jnp.dot(d3, e3, preferred_element_type=F32)
    la3 = jnp.concatenate(_split3(dt * a_heads), axis=0)
    ac3 = jnp.concatenate(_split3(jnp.dot(t3, la3, preferred_element_type=F32)), axis=1)
    acol = jnp.dot(ac3, e3, preferred_element_type=F32)
    acol_s = acol if seg_w == D_SSM else jnp.dot(ac3, es3, preferred_element_type=F32)
    t_idx = lax.broadcasted_iota(jnp.int32, (lc, seg_w), 0)
    s_idx = lax.broadcasted_iota(jnp.int32, (lc, seg_w), 1) % lc
    arow = jnp.sum(jnp.where(t_idx == s_idx, acol_s, 0.0), axis=0, keepdims=True)
    decay = jnp.where(t_idx >= s_idx, jnp.exp(acol_s - arow), 0.0)

    cb16 = cm.astype(BF16)
    bb16 = bm.astype(BF16)
    hpg = SSM_HEADS // SSM_GROUPS
    cbs = []
    for g in range(SSM_GROUPS):
        gs = slice(g * SSM_STATE, (g + 1) * SSM_STATE)
        b_rep = jnp.concatenate([bb16[:, gs]] * hpg, axis=0)
        cbs.append(lax.dot_general(cb16[:, gs], b_rep, _NT, preferred_element_type=F32))
    mmat = (jnp.concatenate(cbs, axis=1) * decay).astype(BF16)

    xdt = xs * dt_x
    xdt16 = xdt.astype(BF16)
    hk = (2 * LANES) // lc
    wd = hk * SSM_HEAD_DIM
    blk = (lax.broadcasted_iota(jnp.int32, (hk * lc, wd), 0) // lc
           == lax.broadcasted_iota(jnp.int32, (hk * lc, wd), 1) // SSM_HEAD_DIM)
    parts = []
    for i in range(SSM_HEADS // hk):
        xd = xdt16[:, i * wd:(i + 1) * wd]
        bd = jnp.where(blk, jnp.concatenate([xd] * hk, axis=0), jnp.zeros((), BF16))
        parts.append(jnp.dot(mmat[:, i * hk * lc:(i + 1) * hk * lc], bd, preferred_element_type=F32))
    y_intra = jnp.concatenate(parts, axis=1) if len(parts) > 1 else parts[0]

    yi = []
    for g in range(SSM_GROUPS):
        sg = s_ref[g * half:(g + 1) * half, :].astype(BF16)
        yi.append(lax.dot_general(cb16[:, g * SSM_STATE:(g + 1) * SSM_STATE], sg, _NT,
                                  preferred_element_type=F32))
    y_inter = jnp.exp(acol) * jnp.concatenate(yi, axis=1)

    alast = acol[lc - 1:lc, :]
    dec_end = jnp.exp(alast - acol)
    xd_end = (xdt * dec_end).astype(BF16)
    krow = lax.broadcasted_iota(jnp.int32, (ones_k.shape[0], D_SSM), 0)
    a_hi, a_mid, a_lo = (piece.astype(F32) for piece in _split3(alast))
    pieces = jnp.where(krow == 0, a_hi, jnp.where(krow == 1, a_mid, jnp.where(krow == 2, a_lo, 0.0)))
    acl = lax.dot_general(pieces.astype(BF16), ones_k, _TN, preferred_element_type=F32)
    for g in range(SSM_GROUPS):
        rows = slice(g * half, (g + 1) * half)
        upd = lax.dot_general(xd_end[:, rows], bb16[:, g * SSM_STATE:(g + 1) * SSM_STATE], _TN,
                              preferred_element_type=F32)
        s_ref[rows, :] = jnp.exp(acl[rows, :]) * s_ref[rows, :] + upd

    y = (y_intra + y_inter + dskip * xs) * _silu(z)
    outs = []
    for g in range(SSM_GROUPS):
        cs = slice(g * half, (g + 1) * half)
        outs.append(_rms(y[:, cs], gn[:, cs]))
    return jnp.concatenate(outs, axis=1)


def _ssd_kernel(*refs, lc, nchunk, has_state):
    if has_state:
        (xs_ref, bc_ref, dt_ref, z_ref, prev_ref, h0_ref, cw_ref, cbias_ref, dtb_ref, alog_ref,
         dskip_ref, gn_ref, e3_ref, es3_ref, t3_ref, ones_ref,
         y_ref, s_ref, scr_x, scr_bc) = refs
    else:
        (xs_ref, bc_ref, dt_ref, z_ref, cw_ref, cbias_ref, dtb_ref, alog_ref,
         dskip_ref, gn_ref, e3_ref, es3_ref, t3_ref, ones_ref,
         y_ref, s_ref, scr_x, scr_bc) = refs
    rows = lc * nchunk
    pad = SUBLANES

    def conv_silu(scr_x, scr_bc, nrows):
        xc = cbias_ref[:, 0:D_SSM]
        bcc = cbias_ref[:, D_SSM:CONV_DIM]
        for tap in range(SSM_CONV):
            off = pad - (SSM_CONV - 1) + tap
            xc = xc + scr_x[off:off + nrows, :] * cw_ref[tap:tap + 1, 0:D_SSM]
            bcc = bcc + scr_bc[off:off + nrows, :] * cw_ref[tap:tap + 1, D_SSM:CONV_DIM]
        return _silu(xc), _silu(bcc)

    dt_in = dt_ref[...] + dtb_ref[...]
    dt = jnp.maximum(dt_in, 0.0) + jnp.log1p(jnp.exp(-jnp.abs(dt_in)))
    a_heads = -jnp.exp(alog_ref[...])
    nbm = SSM_GROUPS * SSM_STATE

    def chunk(c, xs_c, bc_c, s2d):
        rs = slice(c * lc, (c + 1) * lc)
        y = _ssd_chunk(xs_c, bc_c[:, 0:nbm], bc_c[:, nbm:2 * nbm], dt[rs],
                       z_ref[rs, :], s2d, a_heads, dskip_ref[...], gn_ref[...],
                       e3_ref[...], es3_ref[...], t3_ref[...], ones_ref[...], lc)
        y_ref[rs, :] = y.astype(y_ref.dtype)

    if has_state:
        for c in range(nchunk):
            rs = slice(c * lc, (c + 1) * lc)
            scr_x[c, 0:pad, :] = prev_ref[c, :, 0:D_SSM]
            scr_bc[c, 0:pad, :] = prev_ref[c, :, D_SSM:CONV_DIM]
            scr_x[c, pad:pad + lc, :] = xs_ref[rs, :]
            scr_bc[c, pad:pad + lc, :] = bc_ref[rs, :]
            s_ref[c] = h0_ref[c]
            xs_act, bc_act = conv_silu(scr_x.at[c], scr_bc.at[c], lc)
            chunk(c, xs_act, bc_act, s_ref.at[c])
    else:
        @pl.when(pl.program_id(0) == 0)
        def _():
            scr_x[0:pad, :] = jnp.zeros((pad, D_SSM), F32)
            scr_bc[0:pad, :] = jnp.zeros((pad, BC_DIM), F32)
            s_ref[...] = jnp.zeros(s_ref.shape, F32)

        scr_x[pad:pad + rows, :] = xs_ref[...]
        scr_bc[pad:pad + rows, :] = bc_ref[...]
        xs_act, bc_act = conv_silu(scr_x, scr_bc, rows)
        scr_x[0:pad, :] = scr_x[rows:rows + pad, :]
        scr_bc[0:pad, :] = scr_bc[rows:rows + pad, :]
        for c in range(nchunk):
            rs = slice(c * lc, (c + 1) * lc)
            chunk(c, xs_act[rs], bc_act[rs], s_ref)


def _ssd_constants(lc):
    seg_w = SSM_HEADS * lc
    head_of_lane = np.arange(D_SSM) // SSM_HEAD_DIM
    e = (np.arange(LANES)[:, None] == head_of_lane[None, :]).astype(np.float32)
    es = (np.arange(LANES)[:, None] == (np.arange(seg_w) // lc)[None, :]).astype(np.float32)
    tri = np.tril(np.ones((lc, lc), np.float32))
    return (jnp.asarray(np.concatenate([e] * 3, axis=0), BF16),
            jnp.asarray(np.concatenate([es] * 3, axis=0), BF16),
            jnp.asarray(np.concatenate([tri] * 3, axis=1), BF16),
            jnp.ones((2 * SUBLANES, SSM_STATE), BF16))


def _ssd(xs, bc, dt, z, conv_w, conv_b, dt_bias, a_log, d_skip, gn, lc, nchunk,
         conv_prev=None, h0=None):
    m = xs.shape[0]
    rows = lc * nchunk
    has_state = h0 is not None
    e3, es3, t3, ones_k = _ssd_constants(lc)
    dtb = jnp.zeros((1, LANES), F32).at[0, :SSM_HEADS].set(dt_bias)
    alog = jnp.zeros((1, LANES), F32).at[0, :SSM_HEADS].set(a_log)
    dskip_x = jnp.repeat(d_skip, SSM_HEAD_DIM)[None, :]

    def const(shape):
        return pl.BlockSpec(shape, lambda i: (0,) * len(shape))

    def rowblk(width):
        return pl.BlockSpec((rows, width), lambda i: (i, 0))

    in_specs = [rowblk(D_SSM), rowblk(BC_DIM), rowblk(LANES), rowblk(D_SSM)]
    args = [xs, bc, dt, z]
    if has_state:
        nb = h0.shape[0]
        in_specs += [pl.BlockSpec((nchunk, SUBLANES, CONV_DIM), lambda i: (i, 0, 0)),
                     pl.BlockSpec((nchunk, D_SSM, SSM_STATE), lambda i: (i, 0, 0))]
        args += [conv_prev, h0]
        s_shape = jax.ShapeDtypeStruct((nb, D_SSM, SSM_STATE), F32)
        s_spec = pl.BlockSpec((nchunk, D_SSM, SSM_STATE), lambda i: (i, 0, 0))
        scratch = [pltpu.VMEM((nchunk, lc + SUBLANES, D_SSM), F32),
                   pltpu.VMEM((nchunk, lc + SUBLANES, BC_DIM), F32)]
    else:
        s_shape = jax.ShapeDtypeStruct((D_SSM, SSM_STATE), F32)
        s_spec = const((D_SSM, SSM_STATE))
        scratch = [pltpu.VMEM((rows + SUBLANES, D_SSM), F32),
                   pltpu.VMEM((rows + SUBLANES, BC_DIM), F32)]
    in_specs += [const((SSM_CONV, CONV_DIM)), const((1, CONV_DIM)), const((1, LANES)),
                 const((1, LANES)), const((1, D_SSM)), const((1, D_SSM)),
                 const(e3.shape), const(es3.shape), const(t3.shape), const(ones_k.shape)]
    args += [conv_w, conv_b[None, :], dtb, alog, dskip_x, gn[None, :], e3, es3, t3, ones_k]
    return pl.pallas_call(
        functools.partial(_ssd_kernel, lc=lc, nchunk=nchunk, has_state=has_state),
        grid=(m // rows,),
        in_specs=in_specs,
        out_specs=(rowblk(D_SSM), s_spec),
        out_shape=(jax.ShapeDtypeStruct((m, D_SSM), BF16), s_shape),
        scratch_shapes=scratch,
        compiler_params=_cparams(("arbitrary",)),
        name="ssd_sample" if has_state else "ssd_prompt",
    )(*args)


def _out_proj_kernel(o_ref, y_ref, x_ref, w_ref, g_ref, x2_ref, hf_ref, w16_ref=None):
    if w16_ref is not None:
        w16_ref[...] = w_ref[...].astype(BF16)
        w_ref = w16_ref
    x2 = (x_ref[...]
          + jnp.dot(o_ref[...], w_ref[0:D_ATT, :], preferred_element_type=F32)
          + jnp.dot(y_ref[...], w_ref[D_ATT:D_ATT + D_SSM, :], preferred_element_type=F32))
    x2_ref[...] = x2
    hf_ref[...] = _rms(x2, g_ref[...]).astype(hf_ref.dtype)


def _out_proj(o, y, x, w, g, tm):
    m = x.shape[0]
    w_spec = pl.BlockSpec((D_ATT + D_SSM, D_MODEL), lambda i: (0, 0), pipeline_mode=pl.Buffered(1))
    out_specs = (pl.BlockSpec((tm, D_MODEL), lambda i: (i, 0)),
                 pl.BlockSpec((tm, D_MODEL), lambda i: (i, 0)))
    out_shape = (jax.ShapeDtypeStruct((m, D_MODEL), F32),
                 jax.ShapeDtypeStruct((m, D_MODEL), BF16))
    if w.dtype != BF16:
        assert m == tm, "the bf16 weight copy is written once: needs a single row tile"
        out_specs += (pl.BlockSpec((D_ATT + D_SSM, D_MODEL), lambda i: (0, 0)),)
        out_shape += (jax.ShapeDtypeStruct(w.shape, BF16),)
    return pl.pallas_call(
        _out_proj_kernel,
        grid=(m // tm,),
        in_specs=[
            pl.BlockSpec((tm, D_ATT), lambda i: (i, 0)),
            pl.BlockSpec((tm, D_SSM), lambda i: (i, 0)),
            pl.BlockSpec((tm, D_MODEL), lambda i: (i, 0)),
            w_spec,
            pl.BlockSpec((1, D_MODEL), lambda i: (0, 0)),
        ],
        out_specs=out_specs,
        out_shape=out_shape,
        compiler_params=_cparams(("arbitrary",)),
        name="out_proj",
    )(o, y, x, w, g)


_RESIDUAL_PIECES = 8


def _ffn_kernel(*refs, tm, seq, final_norm):
    if seq is None:
        (hf_ref, x2_ref, wg_ref, wu_ref, wd_ref, cw_ref, cb_ref, gfin_ref,
         out_ref, gl_ref, g_scr, act_scr, carry_scr) = refs
    else:
        (hf_ref, x2_ref, wg_ref, wu_ref, wd_ref, cw_ref, cb_ref, gfin_ref, prev_ref, sel1_ref, sel2_ref,
         out_ref, gl_ref, wg16_ref, wu16_ref, wd16_ref, g_scr, act_scr) = refs
    i = pl.program_id(0)
    f = pl.program_id(1)
    nf = pl.num_programs(1) - 1
    pad = SUBLANES
    res_rows = tm // _RESIDUAL_PIECES

    def weight(w_ref, w16_ref):
        if seq is None:
            return w_ref[...]
        w16 = w_ref[...].astype(BF16)
        w16_ref[...] = w16
        return w16

    def gate_up():
        hf = hf_ref[...]
        gate = jnp.dot(hf, weight(wg_ref, None if seq is None else wg16_ref), preferred_element_type=F32)
        up = jnp.dot(hf, weight(wu_ref, None if seq is None else wu16_ref), preferred_element_type=F32)
        g_scr[pad:pad + tm, :] = gate
        if seq is None:
            g_scr[0:pad, :] = carry_scr[f]
            carry_scr[f] = g_scr[tm:tm + pad, :]
            gl_ref[...] = g_scr[tm:tm + pad, :]
            g1 = g_scr[pad - 1:pad - 1 + tm, :]
            g2 = g_scr[pad - 2:pad - 2 + tm, :]
        else:
            g_scr[0:pad, :] = jnp.zeros((pad, gate.shape[1]), F32)
            gl_ref[...] = gate
            prev3 = jnp.concatenate(_split3(prev_ref[...]), axis=0)
            ov1 = jnp.dot(sel1_ref[...], prev3, preferred_element_type=F32)
            ov2 = jnp.dot(sel2_ref[...], prev3, preferred_element_type=F32)
            pos = lax.broadcasted_iota(jnp.int32, gate.shape, 0) % seq
            g1 = jnp.where(pos == 0, ov1, g_scr[pad - 1:pad - 1 + tm, :])
            g2 = jnp.where(pos < 2, ov2, g_scr[pad - 2:pad - 2 + tm, :])
        conv = cb_ref[...] + g2 * cw_ref[0:1, :] + g1 * cw_ref[1:2, :] + gate * cw_ref[2:3, :]
        act_scr[...] = (_silu(conv) * up).astype(BF16)

    def down():
        return jnp.dot(act_scr[...], weight(wd_ref, None if seq is None else wd16_ref),
                       preferred_element_type=F32)

    @pl.when(f == 0)
    def _():
        if seq is None:
            @pl.when(i == 0)
            def _():
                carry_scr[...] = jnp.zeros(carry_scr.shape, F32)
        out_ref[...] = jnp.zeros(out_ref.shape, F32)
        out_ref[0:res_rows, :] = x2_ref[...]
        gate_up()

    @pl.when((f > 0) & (f < nf))
    def _():
        contrib = down()
        gate_up()
        out_ref[...] += contrib
        r0 = pl.multiple_of(jnp.minimum(f, _RESIDUAL_PIECES - 1) * res_rows, res_rows)
        out_ref[pl.ds(r0, res_rows), :] += jnp.where(f < _RESIDUAL_PIECES, x2_ref[...], 0.0)

    @pl.when(f == nf)
    def _():
        res = out_ref[...] + down()
        out_ref[...] = _rms(res, gfin_ref[...]) if final_norm else res


def _ffn_prev_selectors(m, seq, nprev):
    sel1 = np.zeros((m, 3 * nprev), np.float32)
    sel2 = np.zeros((m, 3 * nprev), np.float32)
    nstate = FFN_CONV - 1
    for b in range(m // seq):
        for piece in range(3):
            base = piece * nprev + b * nstate
            sel1[b * seq, base + 1] = 1.0
            sel2[b * seq, base + 0] = 1.0
            sel2[b * seq + 1, base + 1] = 1.0
    return jnp.asarray(sel1, BF16), jnp.asarray(sel2, BF16)


def _ffn(hf, x2, wg, wu, wd, cw, cb, gfin, tm, tf, final_norm, seq=None, prev=None):
    m = hf.shape[0]
    nf = D_FF // tf
    assert nf >= _RESIDUAL_PIECES and tm % (_RESIDUAL_PIECES * SUBLANES) == 0
    cur = lambda f: jnp.minimum(f, nf - 1)
    last = lambda f: jnp.maximum(f - 1, 0)
    in_specs = [
        pl.BlockSpec((tm, D_MODEL), lambda i, f: (i, 0)),
        pl.BlockSpec((tm // _RESIDUAL_PIECES, D_MODEL),
                     lambda i, f: (i * _RESIDUAL_PIECES + jnp.minimum(f, _RESIDUAL_PIECES - 1), 0)),
        pl.BlockSpec((D_MODEL, tf), lambda i, f: (0, cur(f))),
        pl.BlockSpec((D_MODEL, tf), lambda i, f: (0, cur(f))),
        pl.BlockSpec((tf, D_MODEL), lambda i, f: (last(f), 0)),
        pl.BlockSpec((FFN_CONV, tf), lambda i, f: (0, cur(f))),
        pl.BlockSpec((1, tf), lambda i, f: (0, cur(f))),
        pl.BlockSpec((1, D_MODEL), lambda i, f: (0, 0)),
    ]
    args = [hf, x2, wg, wu, wd, cw, cb[None, :], gfin[None, :]]
    scratch = [pltpu.VMEM((tm + SUBLANES, tf), F32), pltpu.VMEM((tm, tf), BF16)]
    if seq is None:
        gl_shape = jax.ShapeDtypeStruct((m // tm * SUBLANES, D_FF), F32)
        gl_spec = pl.BlockSpec((SUBLANES, tf), lambda i, f: (i, cur(f)))
        scratch.append(pltpu.VMEM((nf, SUBLANES, tf), F32))
    else:
        assert m == tm, "sample FFN handles all sequences in one row tile"
        nprev = prev.shape[0]
        sel1, sel2 = _ffn_prev_selectors(m, seq, nprev)
        in_specs += [pl.BlockSpec((nprev, tf), lambda i, f: (0, cur(f))),
                     pl.BlockSpec(sel1.shape, lambda i, f: (0, 0)),
                     pl.BlockSpec(sel2.shape, lambda i, f: (0, 0))]
        args += [prev, sel1, sel2]
        gl_shape = jax.ShapeDtypeStruct((m, D_FF), F32)
        gl_spec = pl.BlockSpec((tm, tf), lambda i, f: (i, cur(f)))
    out_specs = [pl.BlockSpec((tm, D_MODEL), lambda i, f: (i, 0)), gl_spec]
    out_shape = [jax.ShapeDtypeStruct((m, D_MODEL), F32), gl_shape]
    if seq is not None:
        out_specs += [in_specs[2], in_specs[3], in_specs[4]]
        out_shape += [jax.ShapeDtypeStruct(wg.shape, BF16), jax.ShapeDtypeStruct(wu.shape, BF16),
                      jax.ShapeDtypeStruct(wd.shape, BF16)]
    return pl.pallas_call(
        functools.partial(_ffn_kernel, tm=tm, seq=seq, final_norm=final_norm),
        grid=(m // tm, nf + 1),
        in_specs=in_specs,
        out_specs=tuple(out_specs),
        out_shape=tuple(out_shape),
        scratch_shapes=scratch,
        compiler_params=_cparams(("arbitrary", "arbitrary")),
        name="ffn_sample" if seq is not None else "ffn_prompt",
    )(*args)


def _tiling(m, has_state):
    return dict(
        proj_tm=min(1024, m),
        attn_tile=256,
        ssd_chunks=2 * _SAMPLE_SEQS_PER_STEP if has_state else 8,
        out_tm=min(512, m),
        ffn_tm=min(512, m) if has_state else min(1024, m),
        ffn_tf=256 if has_state else 512,
    )


def _layer(x, w, lam_init, final_norm, gfin, *, batch, seq, state=None):
    m = x.shape[0]
    lamv = jnp.stack([w["lambda_q1"], w["lambda_k1"], w["lambda_q2"], w["lambda_k2"]])
    tiles = _tiling(m, state is not None)
    w16 = {}
    q, k, v, z, xs, bc, dt, *extra = _in_proj(x, w["norm_mix_g"][None, :], w["w_in_main"], w["w_in_dt"],
                                              tm=tiles["proj_tm"])
    if len(extra) == 1:
        w16["w_in_main"] = extra[0]

    if state is None:
        k16, v16 = extra if len(extra) == 2 else (k.astype(BF16), v.astype(BF16))
        slopes = jnp.broadcast_to(
            jnp.asarray(2.0 ** (-8.0 * np.arange(1, ATT_HEADS + 1) / ATT_HEADS), F32)[:, None, None],
            (ATT_HEADS, 1, LANES))
        o = _attn_prompt(q, k16, v16, slopes, lamv, w["attn_subln_g"][None, :], lam_init,
                         tile=tiles["attn_tile"])
        y, s_new = _ssd(xs, bc, dt, z, w["conv_w"], w["conv_b"], w["dt_bias"], w["a_log"],
                        w["d_skip"], w["ssm_norm_g"], lc=CHUNK, nchunk=tiles["ssd_chunks"])
        conv_new = jnp.concatenate([xs[m - (SSM_CONV - 1):], bc[m - (SSM_CONV - 1):]], axis=-1)[None]
        s_new = s_new[None]
    else:
        cache_k, cache_v, conv_prev, ssm_prev, ffn_prev = state
        past = cache_k.shape[1]
        assert past % CHUNK == 0 and seq <= CHUNK
        o = _attn_sample(q, k, v, cache_k.reshape(batch, past * ATT_HEADS, V_DIM),
                         cache_v.reshape(batch, past * ATT_HEADS, V_DIM),
                         lamv, w["attn_subln_g"][None, :], lam_init, seq)
        prev8 = jnp.pad(conv_prev, ((0, 0), (SUBLANES - (SSM_CONV - 1), 0), (0, 0)))
        y, s_new = _ssd(xs, bc, dt, z, w["conv_w"], w["conv_b"], w["dt_bias"], w["a_log"],
                        w["d_skip"], w["ssm_norm_g"], lc=seq, nchunk=tiles["ssd_chunks"],
                        conv_prev=prev8, h0=ssm_prev.reshape(batch, D_SSM, SSM_STATE))
        conv_new = jnp.concatenate([xs.reshape(batch, seq, D_SSM)[:, seq - (SSM_CONV - 1):],
                                    bc.reshape(batch, seq, BC_DIM)[:, seq - (SSM_CONV - 1):]], axis=-1)

    x2, hf, *extra = _out_proj(o, y, x, w["w_out"], w["norm_ffn_g"][None, :], tm=tiles["out_tm"])
    if extra:
        w16["w_out"] = extra[0]

    ffn_args = (hf, x2, w["w_gate"], w["w_up"], w["w_down"], w["ffn_conv_w"], w["ffn_conv_b"], gfin)
    ffn_tiles = dict(tm=tiles["ffn_tm"], tf=tiles["ffn_tf"], final_norm=final_norm)
    if state is None:
        x3, gl = _ffn(*ffn_args, **ffn_tiles)
        ffn_new = gl[None, gl.shape[0] - (FFN_CONV - 1):]
    else:
        x3, gl, w16["w_gate"], w16["w_up"], w16["w_down"] = _ffn(
            *ffn_args, **ffn_tiles, seq=seq, prev=ffn_prev.reshape(batch * (FFN_CONV - 1), D_FF))
        ffn_new = gl.reshape(batch, seq, D_FF)[:, seq - (FFN_CONV - 1):]
    k_new = k.reshape(batch, seq, ATT_HEADS, 2 * QK_DIM)
    v_new = v.reshape(batch, seq, ATT_HEADS, V_DIM)
    s_new = s_new.reshape(batch, SSM_HEADS, SSM_HEAD_DIM, SSM_STATE)
    return (x3, k_new, v_new, conv_new, s_new, ffn_new), w16


def kernel(x_prompt, x_sample, cache_k, cache_v, state_ssm_conv, state_ssm, state_ffn_conv, norm_mix_g, w_in, lambda_q1, lambda_k1, lambda_q2, lambda_k2, attn_subln_g, conv_w, conv_b, dt_bias, a_log, d_skip, ssm_norm_g, w_out, norm_ffn_g, w_gate, w_up, ffn_conv_w, ffn_conv_b, w_down, norm_final_g):
    depth = w_in.shape[0]
    pb, pl_len, _ = x_prompt.shape
    sb, sl_len, _ = x_sample.shape
    assert pb == 1 and pl_len % CHUNK == 0
    xp = x_prompt.reshape(pb * pl_len, D_MODEL)
    xs = x_sample.reshape(sb * sl_len, D_MODEL)
    n_main = 2 * D_ATT + D_ATT + D_SSM + CONV_DIM
    outs_p, outs_s = [], []
    for layer in range(depth):
        lam_init = 0.8 - 0.6 * math.exp(-0.3 * layer)
        w_in_t = jnp.swapaxes(w_in[layer], 0, 1)
        w = dict(
            norm_mix_g=norm_mix_g[layer],
            w_in_main=w_in_t,
            w_in_dt=jnp.pad(w_in_t[n_main:].astype(BF16), ((0, LANES - SSM_HEADS), (0, 0))),
            lambda_q1=lambda_q1[layer], lambda_k1=lambda_k1[layer],
            lambda_q2=lambda_q2[layer], lambda_k2=lambda_k2[layer],
            attn_subln_g=attn_subln_g[layer], conv_w=conv_w[layer], conv_b=conv_b[layer],
            dt_bias=dt_bias[layer], a_log=a_log[layer], d_skip=d_skip[layer],
            ssm_norm_g=ssm_norm_g[layer], w_out=w_out[layer],
            norm_ffn_g=norm_ffn_g[layer], w_gate=w_gate[layer],
            w_up=w_up[layer], ffn_conv_w=ffn_conv_w[layer],
            ffn_conv_b=ffn_conv_b[layer], w_down=w_down[layer],
        )
        last = layer == depth - 1
        (xs, *new_s), w16 = _layer(xs, w, lam_init, last, norm_final_g, batch=sb, seq=sl_len,
                                   state=(cache_k[layer], cache_v[layer], state_ssm_conv[layer],
                                          state_ssm[layer], state_ffn_conv[layer]))
        (xp, *new_p), _ = _layer(xp, {**w, **w16}, lam_init, last, norm_final_g, batch=pb, seq=pl_len)
        outs_p.append(new_p)
        outs_s.append(new_s)
    stack = lambda outs, idx: jnp.stack([o[idx] for o in outs])
    return (xp.reshape(pb, pl_len, D_MODEL), xs.reshape(sb, sl_len, D_MODEL),
            *[stack(outs_p, idx) for idx in range(5)],
            *[stack(outs_s, idx) for idx in range(5)])
```

```python
import functools
import math

import jax
import jax.numpy as jnp
import numpy as np
from jax import lax
from jax.experimental import pallas as pl
from jax.experimental.pallas import tpu as pltpu

F32 = jnp.float32
BF16 = jnp.bfloat16

D_MODEL = 2048
CHUNK = 64
ATT_HEADS = 8
QK_DIM = 64
V_DIM = 128
D_ATT = ATT_HEADS * V_DIM
SSM_HEADS = 16
SSM_HEAD_DIM = 64
D_SSM = SSM_HEADS * SSM_HEAD_DIM
SSM_GROUPS = 2
SSM_STATE = 128
SSM_CONV = 4
BC_DIM = 2 * SSM_GROUPS * SSM_STATE
CONV_DIM = D_SSM + BC_DIM
D_FF = 5632
FFN_CONV = 3
EPS = 1e-6
LOG2E = math.log2(math.e)
Q_SCALE = QK_DIM ** -0.5 * LOG2E
LANES = 128
SUBLANES = 8
VMEM_LIMIT = 58 * 1024 * 1024

_NT = (((1,), (1,)), ((), ()))
_TN = (((0,), (0,)), ((), ()))


def _cparams(sem):
    return pltpu.CompilerParams(dimension_semantics=sem, vmem_limit_bytes=VMEM_LIMIT)


def _silu(x):
    return x * jax.nn.sigmoid(x)


def _rms(x, g):
    return x * lax.rsqrt(jnp.mean(x * x, axis=-1, keepdims=True) + EPS) * g


_PROJ_TN = 512
_PROJ_SEGS = ((0, 2), (2, 4), (4, 6), (6, 8), (8, 10), (10, 11))


def _in_proj_kernel(x_ref, g_ref, w_ref, wdt_ref,
                    q_ref, k_ref, v_ref, z_ref, xs_ref, bc_ref, dt_ref, *rest):
    w16_ref, k16_ref, v16_ref = None, None, None
    if len(rest) == 2:
        w16_ref, h_scr = rest
    else:
        k16_ref, v16_ref, h_scr = rest
    j = pl.program_id(1)

    @pl.when(j == 0)
    def _():
        hb = _rms(x_ref[...], g_ref[...]).astype(BF16)
        h_scr[...] = hb
        dt_ref[...] = lax.dot_general(hb, wdt_ref[...], _NT, preferred_element_type=F32)

    outs = (q_ref, k_ref, v_ref, z_ref, xs_ref, bc_ref)
    for (lo, hi), ref in zip(_PROJ_SEGS, outs):
        @pl.when((j >= lo) & (j < hi))
        def _(ref=ref):
            w = w_ref[...]
            if w16_ref is not None:
                w = w.astype(BF16)
                w16_ref[...] = w
            res = lax.dot_general(h_scr[...], w, _NT, preferred_element_type=F32)
            val = res * Q_SCALE if ref is q_ref else res
            ref[...] = val.astype(ref.dtype)
            copy_ref = k16_ref if ref is k_ref else v16_ref if ref is v_ref else None
            if copy_ref is not None:
                copy_ref[...] = res.astype(BF16)


def _in_proj(x, g, w_main, w_dt, tm):
    m = x.shape[0]
    tn = _PROJ_TN
    nj = w_main.shape[0] // tn

    def seg_spec(lo, hi):
        return pl.BlockSpec((tm, tn), lambda i, j: (i, jnp.clip(j - lo, 0, hi - lo - 1)))

    out_shape = (
        jax.ShapeDtypeStruct((m, D_ATT), BF16),
        jax.ShapeDtypeStruct((m, D_ATT), F32),
        jax.ShapeDtypeStruct((m, D_ATT), F32),
        jax.ShapeDtypeStruct((m, D_SSM), F32),
        jax.ShapeDtypeStruct((m, D_SSM), F32),
        jax.ShapeDtypeStruct((m, BC_DIM), F32),
        jax.ShapeDtypeStruct((m, LANES), F32),
    )
    out_specs = tuple(seg_spec(lo, hi) for lo, hi in _PROJ_SEGS) + (
        pl.BlockSpec((tm, LANES), lambda i, j: (i, 0)),)
    if w_main.dtype != BF16:
        assert m == tm, "the bf16 weight copy is written once per tile: needs a single row tile"
        out_shape += (jax.ShapeDtypeStruct((nj * tn, D_MODEL), BF16),)
        out_specs += (pl.BlockSpec((tn, D_MODEL), lambda i, j: (j, 0)),)
    else:
        out_shape += (jax.ShapeDtypeStruct((m, D_ATT), BF16),) * 2
        out_specs += (seg_spec(*_PROJ_SEGS[1]), seg_spec(*_PROJ_SEGS[2]))
    return pl.pallas_call(
        _in_proj_kernel,
        grid=(m // tm, nj),
        in_specs=[
            pl.BlockSpec((tm, D_MODEL), lambda i, j: (i, 0)),
            pl.BlockSpec((1, D_MODEL), lambda i, j: (0, 0)),
            pl.BlockSpec((tn, D_MODEL), lambda i, j: (j, 0)),
            pl.BlockSpec((LANES, D_MODEL), lambda i, j: (0, 0)),
        ],
        out_specs=out_specs,
        out_shape=out_shape,
        scratch_shapes=[pltpu.VMEM((tm, D_MODEL), BF16)],
        compiler_params=_cparams(("arbitrary", "arbitrary")),
        name="in_proj",
    )(x, g, w_main, w_dt)


def _lambda_value(lamv_ref, lam_init):
    lv = lamv_ref[...]
    s1 = jnp.sum(lv[0:1] * lv[1:2], axis=-1, keepdims=True)
    s2 = jnp.sum(lv[2:3] * lv[3:4], axis=-1, keepdims=True)
    return jnp.exp(s1) - jnp.exp(s2) + lam_init


def _split_maps(q):
    lane = lax.broadcasted_iota(jnp.int32, q.shape, 1)
    zero = jnp.zeros_like(q)
    return jnp.where(lane < QK_DIM, q, zero), jnp.where(lane >= QK_DIM, q, zero)


_VT_CHUNK = 512
_VT_PAD = 16
_UNROLL_SHIFT = 2
_UNROLL = 1 << _UNROLL_SHIFT
_HEADS_PER_STEP = 2
_QTILES_PER_GROUP = 4


def _attn_prompt_kernel(q_ref, k_ref, v_ref, slope_ref, lamv_ref, g_ref, dmask_ref, dneg_ref, o_ref,
                        k_scr, vt_scr, acc_scr, t0_scr, t1_scr, dbias_scr, stat_scr,
                        *, tile, lam_init):
    length = k_ref.shape[0]
    width = 2 * tile
    nheads = k_scr.shape[0]
    lam = _lambda_value(lamv_ref, lam_init)

    lane = lax.broadcasted_iota(jnp.int32, (length, LANES), 1)
    koff = lax.broadcasted_iota(jnp.int32, (length, LANES), 0) % tile
    koff = jnp.where(lane < 3, koff, 0).astype(F32).astype(BF16)
    extra = (lax.broadcasted_iota(jnp.int32, (_VT_PAD, tile), 0) == 0).astype(F32).astype(BF16)
    brow = lax.broadcasted_iota(jnp.int32, (LANES, width), 0)
    per = _VT_CHUNK // tile
    ws, q_biases = [], []
    for hh in range(nheads):
        cols = slice(hh * V_DIM, (hh + 1) * V_DIM)
        k_scr[hh, :, 0:V_DIM] = k_ref[:, cols]
        k_scr[hh, :, V_DIM:V_DIM + LANES] = koff
        for c in range(length // _VT_CHUNK):
            vt = v_ref[c * _VT_CHUNK:(c + 1) * _VT_CHUNK, cols].astype(F32).T.astype(BF16)
            for s in range(per):
                vt_scr[hh, c * per + s, 0:V_DIM, :] = vt[:, s * tile:(s + 1) * tile]
                vt_scr[hh, c * per + s, V_DIM:V_DIM + _VT_PAD, :] = extra
        w = slope_ref[hh][:, :1] * LOG2E
        w_hi, w_mid, w_lo = (piece.astype(F32) for piece in _split3(w))
        q_bias = jnp.where(brow == 0, w_hi, jnp.where(brow == 1, w_mid, jnp.where(brow == 2, w_lo, 0.0)))
        ws.append(w)
        q_biases.append(q_bias.astype(BF16))
        dbias_scr[hh] = w * dmask_ref[...] + dneg_ref[...]

    def q_pair(qp, _):
        _attn_prompt_pair(qp, ws, lam, q_biases, q_ref, g_ref, o_ref,
                          k_scr, vt_scr, acc_scr, (t0_scr, t1_scr),
                          dbias_scr, stat_scr, tile=tile, lam_init=lam_init)
        return 0

    lax.fori_loop(0, length // (_QTILES_PER_GROUP * tile), q_pair, 0)


def _attn_prompt_pair(qp, ws, lam, q_biases, q_ref, g_ref, o_ref,
                      k_scr, vt_scr, acc_scr, t_slots, dbias_scr, stat_scr, *, tile, lam_init):
    width = 2 * tile
    nheads = len(ws)
    group = _QTILES_PER_GROUP
    streams = [(hh, u) for hh in range(nheads) for u in range(group)]
    nfull = group * qp
    qis = [nfull + u for _, u in streams]
    rows = [pl.ds(pl.multiple_of(qi * tile, tile), tile) for qi in qis]
    q_augs = []
    for s, (hh, _) in enumerate(streams):
        q1, q2 = _split_maps(q_ref[rows[s], hh * V_DIM:(hh + 1) * V_DIM])
        q_t = jnp.concatenate([q1, q2], axis=0).astype(F32).T.astype(BF16)
        q_augs.append(jnp.concatenate([q_t, q_biases[hh]], axis=0))

    def scores(s, j):
        k0 = pl.multiple_of(jnp.minimum(j, qis[s]) * tile, tile)
        return jnp.dot(k_scr[streams[s][0], pl.ds(k0, tile), :], q_augs[s], preferred_element_type=F32)

    TMAX = 0

    def step(s, j, slot, m_old):
        hh = streams[s][0]
        off = -ws[hh] * jnp.asarray((qis[s] - j) * tile, F32)
        m_new = jnp.maximum(m_old, stat_scr[s, TMAX:TMAX + 1, :] + off)
        t_next = scores(s, j + 1)
        t_slots[1 - slot][s] = t_next
        stat_scr[s, TMAX:TMAX + 1, :] = jnp.max(t_next, axis=0, keepdims=True)
        p = jnp.exp2(t_slots[slot][s] + (off - m_new)).astype(BF16)
        pv = jnp.dot(vt_scr[hh, j], p, preferred_element_type=F32)
        acc_scr[s] = jnp.exp2(m_old - m_new) * acc_scr[s] + pv
        return m_new

    def run(first, ntiles_per_iter, niter, carry):
        def body(i, carry):
            for k in range(ntiles_per_iter):
                carry = tuple(step(s, first + ntiles_per_iter * i + k, k % 2, carry[s])
                              for s in range(len(streams)))
            return carry
        return lax.fori_loop(0, niter, body, carry)

    carry = []
    for s in range(len(streams)):
        acc_scr[s] = jnp.zeros(acc_scr.shape[1:], F32)
        t_first = scores(s, 0)
        t_slots[0][s] = t_first
        stat_scr[s, TMAX:TMAX + 1, :] = jnp.max(t_first, axis=0, keepdims=True)
        carry.append(jnp.full((1, width), -0.5 * float(jnp.finfo(F32).max), F32))
    n_main = lax.shift_right_logical(nfull, _UNROLL_SHIFT)
    done = n_main * _UNROLL
    carry = run(0, _UNROLL, n_main, tuple(carry))
    carry = run(done, 2, lax.shift_right_logical(nfull - done, 1), carry)

    m_run = list(carry)
    diag_slot = [0] * len(streams)
    for e in range(group - 1):
        for s, (hh, u) in enumerate(streams):
            if u > e:
                m_run[s] = step(s, nfull + e, diag_slot[s], m_run[s])
                diag_slot[s] = 1 - diag_slot[s]
    for s, (hh, u) in enumerate(streams):
        t = t_slots[diag_slot[s]][s] + dbias_scr[hh]
        m_new = jnp.maximum(m_run[s], jnp.max(t, axis=0, keepdims=True))
        p = jnp.exp2(t - m_new).astype(BF16)
        acc_scr[s] = (jnp.exp2(m_run[s] - m_new) * acc_scr[s]
                      + jnp.dot(vt_scr[hh, qis[s]], p, preferred_element_type=F32))
    for s, (hh, u) in enumerate(streams):
        acc = acc_scr[s]
        on = acc[0:V_DIM] / acc[V_DIM:V_DIM + 1]
        o_t = on[:, :tile] - lam * on[:, tile:]
        o_t = o_t * lax.rsqrt(jnp.mean(o_t * o_t, axis=0, keepdims=True) + EPS)
        o_ref[rows[s], hh * V_DIM:(hh + 1) * V_DIM] = (
            o_t.T * g_ref[...] * (1.0 - lam_init)).astype(o_ref.dtype)


def _attn_prompt(q, k, v, slopes, lamv, g, lam_init, tile):
    length = q.shape[0]
    key = np.arange(tile)[:, None]
    qry = np.tile(np.arange(tile), 2)[None, :]
    dmask = jnp.asarray((qry - np.abs(qry - key)) - key, F32)
    dneg = jnp.asarray(np.where(key // CHUNK <= qry // CHUNK, 0.0, -np.inf), F32)
    nh = _HEADS_PER_STEP
    nstream = nh * _QTILES_PER_GROUP
    diag_spec = pl.BlockSpec((tile, 2 * tile), lambda h: (0, 0), pipeline_mode=pl.Buffered(1))
    head_spec = pl.BlockSpec((length, nh * V_DIM), lambda h: (0, h), pipeline_mode=pl.Buffered(1))
    kv_spec = pl.BlockSpec((length, nh * V_DIM), lambda h: (0, h))
    return pl.pallas_call(
        functools.partial(_attn_prompt_kernel, tile=tile, lam_init=lam_init),
        grid=(ATT_HEADS // nh,),
        in_specs=[
            head_spec, kv_spec, kv_spec,
            pl.BlockSpec((nh, 1, LANES), lambda h: (h, 0, 0)),
            pl.BlockSpec((4, QK_DIM), lambda h: (0, 0)),
            pl.BlockSpec((1, V_DIM), lambda h: (0, 0)),
            diag_spec, diag_spec,
        ],
        out_specs=head_spec,
        out_shape=jax.ShapeDtypeStruct((length, D_ATT), BF16),
        scratch_shapes=[
            pltpu.VMEM((nh, length, V_DIM + LANES), BF16),
            pltpu.VMEM((nh, length // tile, V_DIM + _VT_PAD, tile), BF16),
            pltpu.VMEM((nstream, V_DIM + _VT_PAD, 2 * tile), F32),
            pltpu.VMEM((nstream, tile, 2 * tile), F32),
            pltpu.VMEM((nstream, tile, 2 * tile), F32),
            pltpu.VMEM((nh, tile, 2 * tile), F32),
            pltpu.VMEM((nstream, SUBLANES, 2 * tile), F32),
        ],
        compiler_params=_cparams(("arbitrary",)),
        name="attn_prompt",
    )(q, k, v, slopes, lamv, g, dmask, dneg)


_SAMPLE_SEQS_PER_STEP = 2


def _attn_sample_kernel(q_ref, kn_ref, vn_ref, ck_ref, cv_ref, lamv_ref, g_ref, bias_c_ref, bias_n_ref,
                        o_ref, *, seq, past, lam_init):
    lam = _lambda_value(lamv_ref, lam_init)
    rows = 2 * seq
    for bi in range(ck_ref.shape[0]):
        br = slice(bi * seq, (bi + 1) * seq)
        sc, sn = [], []
        for h in range(ATT_HEADS):
            sl = slice(h * V_DIM, (h + 1) * V_DIM)
            q1, q2 = _split_maps(q_ref[br, sl])
            qq = jnp.concatenate([q1, q2], axis=0)
            kc = ck_ref[bi, pl.ds(h, past, stride=ATT_HEADS), :].astype(BF16)
            sc.append(lax.dot_general(qq, kc, _NT, preferred_element_type=F32))
            sn.append(lax.dot_general(qq, kn_ref[br, sl].astype(BF16), _NT, preferred_element_type=F32))
        sc = jnp.concatenate(sc, axis=0) - bias_c_ref[...]
        sn = jnp.concatenate(sn, axis=0) - bias_n_ref[...]
        m = jnp.maximum(jnp.max(sc, axis=-1, keepdims=True), jnp.max(sn, axis=-1, keepdims=True))
        pc = jnp.exp2(sc - m)
        pn = jnp.exp2(sn - m)
        inv_l = 1.0 / (jnp.sum(pc, axis=-1, keepdims=True) + jnp.sum(pn, axis=-1, keepdims=True))
        pc = pc.astype(BF16)
        pn = pn.astype(BF16)
        for h in range(ATT_HEADS):
            sl = slice(h * V_DIM, (h + 1) * V_DIM)
            hr = slice(h * rows, (h + 1) * rows)
            vc = cv_ref[bi, pl.ds(h, past, stride=ATT_HEADS), :].astype(BF16)
            acc = (jnp.dot(pc[hr], vc, preferred_element_type=F32)
                   + jnp.dot(pn[hr], vn_ref[br, sl].astype(BF16), preferred_element_type=F32))
            on = acc * inv_l[hr]
            o = on[:seq] - lam * on[seq:]
            o_ref[br, sl] = (_rms(o, g_ref[...]) * (1.0 - lam_init)).astype(o_ref.dtype)


def _attn_sample(q, k_new, v_new, cache_k, cache_v, lamv, g, lam_init, seq):
    nb, past = cache_k.shape[0], cache_k.shape[1] // ATT_HEADS
    slope = np.repeat(2.0 ** (-8.0 * np.arange(1, ATT_HEADS + 1) / ATT_HEADS), 2 * seq)[:, None] * LOG2E
    qpos = np.tile(np.arange(seq), 2 * ATT_HEADS)[:, None]
    bias_c = jnp.asarray(slope * (qpos + past - np.arange(past)[None, :]), F32)
    bias_n = jnp.asarray(slope * np.abs(qpos - np.arange(seq)[None, :]), F32)
    per = _SAMPLE_SEQS_PER_STEP
    row_spec = pl.BlockSpec((per * seq, D_ATT), lambda b: (b, 0))
    cache_spec = pl.BlockSpec((per, past * ATT_HEADS, V_DIM), lambda b: (b, 0, 0))

    def const(arr):
        return pl.BlockSpec(arr.shape, lambda b: (0, 0), pipeline_mode=pl.Buffered(1))

    return pl.pallas_call(
        functools.partial(_attn_sample_kernel, seq=seq, past=past, lam_init=lam_init),
        grid=(nb // per,),
        in_specs=[row_spec, row_spec, row_spec, cache_spec, cache_spec,
                  pl.BlockSpec((4, QK_DIM), lambda b: (0, 0)),
                  pl.BlockSpec((1, V_DIM), lambda b: (0, 0)),
                  const(bias_c), const(bias_n)],
        out_specs=row_spec,
        out_shape=jax.ShapeDtypeStruct((nb * seq, D_ATT), BF16),
        compiler_params=_cparams(("arbitrary",)),
        name="attn_sample",
    )(q, k_new, v_new, cache_k, cache_v, lamv, g, bias_c, bias_n)


def _split3(x):
    hi = x.astype(BF16)
    r = x - hi.astype(F32)
    mid = r.astype(BF16)
    lo = (r - mid.astype(F32)).astype(BF16)
    return hi, mid, lo


def _ssd_chunk(xs, bm, cm, dt, z, s_ref, a_heads, dskip, gn, e3, es3, t3, ones_k, lc):
    seg_w = SSM_HEADS * lc
    half = D_SSM // SSM_GROUPS
    d3 = jnp.concatenate(_split3(dt), axis=1)
    dt_x = jnp.dot(d3, e3, preferred_element_type=F32)
    la3 = jnp.concatenate(_split3(dt * a_heads), axis=0)
    ac3 = jnp.concatenate(_split3(jnp.dot(t3, la3, preferred_element_type=F32)), axis=1)
    acol = jnp.dot(ac3, e3, preferred_element_type=F32)
    acol_s = acol if seg_w == D_SSM else jnp.dot(ac3, es3, preferred_element_type=F32)
    t_idx = lax.broadcasted_iota(jnp.int32, (lc, seg_w), 0)
    s_idx = lax.broadcasted_iota(jnp.int32, (lc, seg_w), 1) % lc
    arow = jnp.sum(jnp.where(t_idx == s_idx, acol_s, 0.0), axis=0, keepdims=True)
    decay = jnp.where(t_idx >= s_idx, jnp.exp(acol_s - arow), 0.0)

    cb16 = cm.astype(BF16)
    bb16 = bm.astype(BF16)
    hpg = SSM_HEADS // SSM_GROUPS
    cbs = []
    for g in range(SSM_GROUPS):
        gs = slice(g * SSM_STATE, (g + 1) * SSM_STATE)
        b_rep = jnp.concatenate([bb16[:, gs]] * hpg, axis=0)
        cbs.append(lax.dot_general(cb16[:, gs], b_rep, _NT, preferred_element_type=F32))
    mmat = (jnp.concatenate(cbs, axis=1) * decay).astype(BF16)

    xdt = xs * dt_x
    xdt16 = xdt.astype(BF16)
    hk = (2 * LANES) // lc
    wd = hk * SSM_HEAD_DIM
    blk = (lax.broadcasted_iota(jnp.int32, (hk * lc, wd), 0) // lc
           == lax.broadcasted_iota(jnp.int32, (hk * lc, wd), 1) // SSM_HEAD_DIM)
    parts = []
    for i in range(SSM_HEADS // hk):
        xd = xdt16[:, i * wd:(i + 1) * wd]
        bd = jnp.where(blk, jnp.concatenate([xd] * hk, axis=0), jnp.zeros((), BF16))
        parts.append(jnp.dot(mmat[:, i * hk * lc:(i + 1) * hk * lc], bd, preferred_element_type=F32))
    y_intra = jnp.concatenate(parts, axis=1) if len(parts) > 1 else parts[0]

    yi = []
    for g in range(SSM_GROUPS):
        sg = s_ref[g * half:(g + 1) * half, :].astype(BF16)
        yi.append(lax.dot_general(cb16[:, g * SSM_STATE:(g + 1) * SSM_STATE], sg, _NT,
                                  preferred_element_type=F32))
    y_inter = jnp.exp(acol) * jnp.concatenate(yi, axis=1)

    alast = acol[lc - 1:lc, :]
    dec_end = jnp.exp(alast - acol)
    xd_end = (xdt * dec_end).astype(BF16)
    krow = lax.broadcasted_iota(jnp.int32, (ones_k.shape[0], D_SSM), 0)
    a_hi, a_mid, a_lo = (piece.astype(F32) for piece in _split3(alast))
    pieces = jnp.where(krow == 0, a_hi, jnp.where(krow == 1, a_mid, jnp.where(krow == 2, a_lo, 0.0)))
    acl = lax.dot_general(pieces.astype(BF16), ones_k, _TN, preferred_element_type=F32)
    for g in range(SSM_GROUPS):
        rows = slice(g * half, (g + 1) * half)
        upd = lax.dot_general(xd_end[:, rows], bb16[:, g * SSM_STATE:(g + 1) * SSM_STATE], _TN,
                              preferred_element_type=F32)
        s_ref[rows, :] = jnp.exp(acl[rows, :]) * s_ref[rows, :] + upd

    y = (y_intra + y_inter + dskip * xs) * _silu(z)
    outs = []
    for g in range(SSM_GROUPS):
        cs = slice(g * half, (g + 1) * half)
        outs.append(_rms(y[:, cs], gn[:, cs]))
    return jnp.concatenate(outs, axis=1)


def _ssd_kernel(*refs, lc, nchunk, has_state):
    if has_state:
        (xs_ref, bc_ref, dt_ref, z_ref, prev_ref, h0_ref, cw_ref, cbias_ref, dtb_ref, alog_ref,
         dskip_ref, gn_ref, e3_ref, es3_ref, t3_ref, ones_ref,
         y_ref, s_ref, scr_x, scr_bc) = refs
    else:
        (xs_ref, bc_ref, dt_ref, z_ref, cw_ref, cbias_ref, dtb_ref, alog_ref,
         dskip_ref, gn_ref, e3_ref, es3_ref, t3_ref, ones_ref,
         y_ref, s_ref, scr_x, scr_bc) = refs
    rows = lc * nchunk
    pad = SUBLANES

    def conv_silu(scr_x, scr_bc, nrows):
        xc = cbias_ref[:, 0:D_SSM]
        bcc = cbias_ref[:, D_SSM:CONV_DIM]
        for tap in range(SSM_CONV):
            off = pad - (SSM_CONV - 1) + tap
            xc = xc + scr_x[off:off + nrows, :] * cw_ref[tap:tap + 1, 0:D_SSM]
            bcc = bcc + scr_bc[off:off + nrows, :] * cw_ref[tap:tap + 1, D_SSM:CONV_DIM]
        return _silu(xc), _silu(bcc)

    dt_in = dt_ref[...] + dtb_ref[...]
    dt = jnp.maximum(dt_in, 0.0) + jnp.log1p(jnp.exp(-jnp.abs(dt_in)))
    a_heads = -jnp.exp(alog_ref[...])
    nbm = SSM_GROUPS * SSM_STATE

    def chunk(c, xs_c, bc_c, s2d):
        rs = slice(c * lc, (c + 1) * lc)
        y = _ssd_chunk(xs_c, bc_c[:, 0:nbm], bc_c[:, nbm:2 * nbm], dt[rs],
                       z_ref[rs, :], s2d, a_heads, dskip_ref[...], gn_ref[...],
                       e3_ref[...], es3_ref[...], t3_ref[...], ones_ref[...], lc)
        y_ref[rs, :] = y.astype(y_ref.dtype)

    if has_state:
        for c in range(nchunk):
            rs = slice(c * lc, (c + 1) * lc)
            scr_x[c, 0:pad, :] = prev_ref[c, :, 0:D_SSM]
            scr_bc[c, 0:pad, :] = prev_ref[c, :, D_SSM:CONV_DIM]
            scr_x[c, pad:pad + lc, :] = xs_ref[rs, :]
            scr_bc[c, pad:pad + lc, :] = bc_ref[rs, :]
            s_ref[c] = h0_ref[c]
            xs_act, bc_act = conv_silu(scr_x.at[c], scr_bc.at[c], lc)
            chunk(c, xs_act, bc_act, s_ref.at[c])
    else:
        @pl.when(pl.program_id(0) == 0)
        def _():
            scr_x[0:pad, :] = jnp.zeros((pad, D_SSM), F32)
            scr_bc[0:pad, :] = jnp.zeros((pad, BC_DIM), F32)
            s_ref[...] = jnp.zeros(s_ref.shape, F32)

        scr_x[pad:pad + rows, :] = xs_ref[...]
        scr_bc[pad:pad + rows, :] = bc_ref[...]
        xs_act, bc_act = conv_silu(scr_x, scr_bc, rows)
        scr_x[0:pad, :] = scr_x[rows:rows + pad, :]
        scr_bc[0:pad, :] = scr_bc[rows:rows + pad, :]
        for c in range(nchunk):
            rs = slice(c * lc, (c + 1) * lc)
            chunk(c, xs_act[rs], bc_act[rs], s_ref)


def _ssd_constants(lc):
    seg_w = SSM_HEADS * lc
    head_of_lane = np.arange(D_SSM) // SSM_HEAD_DIM
    e = (np.arange(LANES)[:, None] == head_of_lane[None, :]).astype(np.float32)
    es = (np.arange(LANES)[:, None] == (np.arange(seg_w) // lc)[None, :]).astype(np.float32)
    tri = np.tril(np.ones((lc, lc), np.float32))
    return (jnp.asarray(np.concatenate([e] * 3, axis=0), BF16),
            jnp.asarray(np.concatenate([es] * 3, axis=0), BF16),
            jnp.asarray(np.concatenate([tri] * 3, axis=1), BF16),
            jnp.ones((2 * SUBLANES, SSM_STATE), BF16))


def _ssd(xs, bc, dt, z, conv_w, conv_b, dt_bias, a_log, d_skip, gn, lc, nchunk,
         conv_prev=None, h0=None):
    m = xs.shape[0]
    rows = lc * nchunk
    has_state = h0 is not None
    e3, es3, t3, ones_k = _ssd_constants(lc)
    dtb = jnp.zeros((1, LANES), F32).at[0, :SSM_HEADS].set(dt_bias)
    alog = jnp.zeros((1, LANES), F32).at[0, :SSM_HEADS].set(a_log)
    dskip_x = jnp.repeat(d_skip, SSM_HEAD_DIM)[None, :]

    def const(shape):
        return pl.BlockSpec(shape, lambda i: (0,) * len(shape))

    def rowblk(width):
        return pl.BlockSpec((rows, width), lambda i: (i, 0))

    in_specs = [rowblk(D_SSM), rowblk(BC_DIM), rowblk(LANES), rowblk(D_SSM)]
    args = [xs, bc, dt, z]
    if has_state:
        nb = h0.shape[0]
        in_specs += [pl.BlockSpec((nchunk, SUBLANES, CONV_DIM), lambda i: (i, 0, 0)),
                     pl.BlockSpec((nchunk, D_SSM, SSM_STATE), lambda i: (i, 0, 0))]
        args += [conv_prev, h0]
        s_shape = jax.ShapeDtypeStruct((nb, D_SSM, SSM_STATE), F32)
        s_spec = pl.BlockSpec((nchunk, D_SSM, SSM_STATE), lambda i: (i, 0, 0))
        scratch = [pltpu.VMEM((nchunk, lc + SUBLANES, D_SSM), F32),
                   pltpu.VMEM((nchunk, lc + SUBLANES, BC_DIM), F32)]
    else:
        s_shape = jax.ShapeDtypeStruct((D_SSM, SSM_STATE), F32)
        s_spec = const((D_SSM, SSM_STATE))
        scratch = [pltpu.VMEM((rows + SUBLANES, D_SSM), F32),
                   pltpu.VMEM((rows + SUBLANES, BC_DIM), F32)]
    in_specs += [const((SSM_CONV, CONV_DIM)), const((1, CONV_DIM)), const((1, LANES)),
                 const((1, LANES)), const((1, D_SSM)), const((1, D_SSM)),
                 const(e3.shape), const(es3.shape), const(t3.shape), const(ones_k.shape)]
    args += [conv_w, conv_b[None, :], dtb, alog, dskip_x, gn[None, :], e3, es3, t3, ones_k]
    return pl.pallas_call(
        functools.partial(_ssd_kernel, lc=lc, nchunk=nchunk, has_state=has_state),
        grid=(m // rows,),
        in_specs=in_specs,
        out_specs=(rowblk(D_SSM), s_spec),
        out_shape=(jax.ShapeDtypeStruct((m, D_SSM), BF16), s_shape),
        scratch_shapes=scratch,
        compiler_params=_cparams(("arbitrary",)),
        name="ssd_sample" if has_state else "ssd_prompt",
    )(*args)


def _out_proj_kernel(o_ref, y_ref, x_ref, w_ref, g_ref, x2_ref, hf_ref, w16_ref=None):
    if w16_ref is not None:
        w16_ref[...] = w_ref[...].astype(BF16)
        w_ref = w16_ref
    x2 = (x_ref[...]
          + jnp.dot(o_ref[...], w_ref[0:D_ATT, :], preferred_element_type=F32)
          + jnp.dot(y_ref[...], w_ref[D_ATT:D_ATT + D_SSM, :], preferred_element_type=F32))
    x2_ref[...] = x2
    hf_ref[...] = _rms(x2, g_ref[...]).astype(hf_ref.dtype)


def _out_proj(o, y, x, w, g, tm):
    m = x.shape[0]
    w_spec = pl.BlockSpec((D_ATT + D_SSM, D_MODEL), lambda i: (0, 0), pipeline_mode=pl.Buffered(1))
    out_specs = (pl.BlockSpec((tm, D_MODEL), lambda i: (i, 0)),
                 pl.BlockSpec((tm, D_MODEL), lambda i: (i, 0)))
    out_shape = (jax.ShapeDtypeStruct((m, D_MODEL), F32),
                 jax.ShapeDtypeStruct((m, D_MODEL), BF16))
    if w.dtype != BF16:
        assert m == tm, "the bf16 weight copy is written once: needs a single row tile"
        out_specs += (pl.BlockSpec((D_ATT + D_SSM, D_MODEL), lambda i: (0, 0)),)
        out_shape += (jax.ShapeDtypeStruct(w.shape, BF16),)
    return pl.pallas_call(
        _out_proj_kernel,
        grid=(m // tm,),
        in_specs=[
            pl.BlockSpec((tm, D_ATT), lambda i: (i, 0)),
            pl.BlockSpec((tm, D_SSM), lambda i: (i, 0)),
            pl.BlockSpec((tm, D_MODEL), lambda i: (i, 0)),
            w_spec,
            pl.BlockSpec((1, D_MODEL), lambda i: (0, 0)),
        ],
        out_specs=out_specs,
        out_shape=out_shape,
        compiler_params=_cparams(("arbitrary",)),
        name="out_proj",
    )(o, y, x, w, g)


_RESIDUAL_PIECES = 8


def _ffn_kernel(*refs, tm, seq, final_norm):
    if seq is None:
        (hf_ref, x2_ref, wg_ref, wu_ref, wd_ref, cw_ref, cb_ref, gfin_ref,
         out_ref, gl_ref, g_scr, act_scr, carry_scr) = refs
    else:
        (hf_ref, x2_ref, wg_ref, wu_ref, wd_ref, cw_ref, cb_ref, gfin_ref, prev_ref, sel1_ref, sel2_ref,
         out_ref, gl_ref, wg16_ref, wu16_ref, wd16_ref, g_scr, act_scr) = refs
    i = pl.program_id(0)
    f = pl.program_id(1)
    nf = pl.num_programs(1) - 1
    pad = SUBLANES
    res_rows = tm // _RESIDUAL_PIECES

    def weight(w_ref, w16_ref):
        if seq is None:
            return w_ref[...]
        w16 = w_ref[...].astype(BF16)
        w16_ref[...] = w16
        return w16

    def gate_up():
        hf = hf_ref[...]
        gate = jnp.dot(hf, weight(wg_ref, None if seq is None else wg16_ref), preferred_element_type=F32)
        up = jnp.dot(hf, weight(wu_ref, None if seq is None else wu16_ref), preferred_element_type=F32)
        g_scr[pad:pad + tm, :] = gate
        if seq is None:
            g_scr[0:pad, :] = carry_scr[f]
            carry_scr[f] = g_scr[tm:tm + pad, :]
            gl_ref[...] = g_scr[tm:tm + pad, :]
            g1 = g_scr[pad - 1:pad - 1 + tm, :]
            g2 = g_scr[pad - 2:pad - 2 + tm, :]
        else:
            g_scr[0:pad, :] = jnp.zeros((pad, gate.shape[1]), F32)
            gl_ref[...] = gate
            prev3 = jnp.concatenate(_split3(prev_ref[...]), axis=0)
            ov1 = jnp.dot(sel1_ref[...], prev3, preferred_element_type=F32)
            ov2 = jnp.dot(sel2_ref[...], prev3, preferred_element_type=F32)
            pos = lax.broadcasted_iota(jnp.int32, gate.shape, 0) % seq
            g1 = jnp.where(pos == 0, ov1, g_scr[pad - 1:pad - 1 + tm, :])
            g2 = jnp.where(pos < 2, ov2, g_scr[pad - 2:pad - 2 + tm, :])
        conv = cb_ref[...] + g2 * cw_ref[0:1, :] + g1 * cw_ref[1:2, :] + gate * cw_ref[2:3, :]
        act_scr[...] = (_silu(conv) * up).astype(BF16)

    def down():
        return jnp.dot(act_scr[...], weight(wd_ref, None if seq is None else wd16_ref),
                       preferred_element_type=F32)

    @pl.when(f == 0)
    def _():
        if seq is None:
            @pl.when(i == 0)
            def _():
                carry_scr[...] = jnp.zeros(carry_scr.shape, F32)
        out_ref[...] = jnp.zeros(out_ref.shape, F32)
        out_ref[0:res_rows, :] = x2_ref[...]
        gate_up()

    @pl.when((f > 0) & (f < nf))
    def _():
        contrib = down()
        gate_up()
        out_ref[...] += contrib
        r0 = pl.multiple_of(jnp.minimum(f, _RESIDUAL_PIECES - 1) * res_rows, res_rows)
        out_ref[pl.ds(r0, res_rows), :] += jnp.where(f < _RESIDUAL_PIECES, x2_ref[...], 0.0)

    @pl.when(f == nf)
    def _():
        res = out_ref[...] + down()
        out_ref[...] = _rms(res, gfin_ref[...]) if final_norm else res


def _ffn_prev_selectors(m, seq, nprev):
    sel1 = np.zeros((m, 3 * nprev), np.float32)
    sel2 = np.zeros((m, 3 * nprev), np.float32)
    nstate = FFN_CONV - 1
    for b in range(m // seq):
        for piece in range(3):
            base = piece * nprev + b * nstate
            sel1[b * seq, base + 1] = 1.0
            sel2[b * seq, base + 0] = 1.0
            sel2[b * seq + 1, base + 1] = 1.0
    return jnp.asarray(sel1, BF16), jnp.asarray(sel2, BF16)


def _ffn(hf, x2, wg, wu, wd, cw, cb, gfin, tm, tf, final_norm, seq=None, prev=None):
    m = hf.shape[0]
    nf = D_FF // tf
    assert nf >= _RESIDUAL_PIECES and tm % (_RESIDUAL_PIECES * SUBLANES) == 0
    cur = lambda f: jnp.minimum(f, nf - 1)
    last = lambda f: jnp.maximum(f - 1, 0)
    in_specs = [
        pl.BlockSpec((tm, D_MODEL), lambda i, f: (i, 0)),
        pl.BlockSpec((tm // _RESIDUAL_PIECES, D_MODEL),
                     lambda i, f: (i * _RESIDUAL_PIECES + jnp.minimum(f, _RESIDUAL_PIECES - 1), 0)),
        pl.BlockSpec((D_MODEL, tf), lambda i, f: (0, cur(f))),
        pl.BlockSpec((D_MODEL, tf), lambda i, f: (0, cur(f))),
        pl.BlockSpec((tf, D_MODEL), lambda i, f: (last(f), 0)),
        pl.BlockSpec((FFN_CONV, tf), lambda i, f: (0, cur(f))),
        pl.BlockSpec((1, tf), lambda i, f: (0, cur(f))),
        pl.BlockSpec((1, D_MODEL), lambda i, f: (0, 0)),
    ]
    args = [hf, x2, wg, wu, wd, cw, cb[None, :], gfin[None, :]]
    scratch = [pltpu.VMEM((tm + SUBLANES, tf), F32), pltpu.VMEM((tm, tf), BF16)]
    if seq is None:
        gl_shape = jax.ShapeDtypeStruct((m // tm * SUBLANES, D_FF), F32)
        gl_spec = pl.BlockSpec((SUBLANES, tf), lambda i, f: (i, cur(f)))
        scratch.append(pltpu.VMEM((nf, SUBLANES, tf), F32))
    else:
        assert m == tm, "sample FFN handles all sequences in one row tile"
        nprev = prev.shape[0]
        sel1, sel2 = _ffn_prev_selectors(m, seq, nprev)
        in_specs += [pl.BlockSpec((nprev, tf), lambda i, f: (0, cur(f))),
                     pl.BlockSpec(sel1.shape, lambda i, f: (0, 0)),
                     pl.BlockSpec(sel2.shape, lambda i, f: (0, 0))]
        args += [prev, sel1, sel2]
        gl_shape = jax.ShapeDtypeStruct((m, D_FF), F32)
        gl_spec = pl.BlockSpec((tm, tf), lambda i, f: (i, cur(f)))
    out_specs = [pl.BlockSpec((tm, D_MODEL), lambda i, f: (i, 0)), gl_spec]
    out_shape = [jax.ShapeDtypeStruct((m, D_MODEL), F32), gl_shape]
    if seq is not None:
        out_specs += [in_specs[2], in_specs[3], in_specs[4]]
        out_shape += [jax.ShapeDtypeStruct(wg.shape, BF16), jax.ShapeDtypeStruct(wu.shape, BF16),
                      jax.ShapeDtypeStruct(wd.shape, BF16)]
    return pl.pallas_call(
        functools.partial(_ffn_kernel, tm=tm, seq=seq, final_norm=final_norm),
        grid=(m // tm, nf + 1),
        in_specs=in_specs,
        out_specs=tuple(out_specs),
        out_shape=tuple(out_shape),
        scratch_shapes=scratch,
        compiler_params=_cparams(("arbitrary", "arbitrary")),
        name="ffn_sample" if seq is not None else "ffn_prompt",
    )(*args)


def _tiling(m, has_state):
    return dict(
        proj_tm=min(1024, m),
        attn_tile=256,
        ssd_chunks=2 * _SAMPLE_SEQS_PER_STEP if has_state else 8,
        out_tm=min(512, m),
        ffn_tm=min(512, m) if has_state else min(1024, m),
        ffn_tf=512,
    )


def _layer(x, w, lam_init, final_norm, gfin, *, batch, seq, state=None):
    m = x.shape[0]
    lamv = jnp.stack([w["lambda_q1"], w["lambda_k1"], w["lambda_q2"], w["lambda_k2"]])
    tiles = _tiling(m, state is not None)
    w16 = {}
    q, k, v, z, xs, bc, dt, *extra = _in_proj(x, w["norm_mix_g"][None, :], w["w_in_main"], w["w_in_dt"],
                                              tm=tiles["proj_tm"])
    if len(extra) == 1:
        w16["w_in_main"] = extra[0]

    if state is None:
        k16, v16 = extra if len(extra) == 2 else (k.astype(BF16), v.astype(BF16))
        slopes = jnp.broadcast_to(
            jnp.asarray(2.0 ** (-8.0 * np.arange(1, ATT_HEADS + 1) / ATT_HEADS), F32)[:, None, None],
            (ATT_HEADS, 1, LANES))
        o = _attn_prompt(q, k16, v16, slopes, lamv, w["attn_subln_g"][None, :], lam_init,
                         tile=tiles["attn_tile"])
        y, s_new = _ssd(xs, bc, dt, z, w["conv_w"], w["conv_b"], w["dt_bias"], w["a_log"],
                        w["d_skip"], w["ssm_norm_g"], lc=CHUNK, nchunk=tiles["ssd_chunks"])
        conv_new = jnp.concatenate([xs[m - (SSM_CONV - 1):], bc[m - (SSM_CONV - 1):]], axis=-1)[None]
        s_new = s_new[None]
    else:
        cache_k, cache_v, conv_prev, ssm_prev, ffn_prev = state
        past = cache_k.shape[1]
        assert past % CHUNK == 0 and seq <= CHUNK
        o = _attn_sample(q, k, v, cache_k.reshape(batch, past * ATT_HEADS, V_DIM),
                         cache_v.reshape(batch, past * ATT_HEADS, V_DIM),
                         lamv, w["attn_subln_g"][None, :], lam_init, seq)
        prev8 = jnp.pad(conv_prev, ((0, 0), (SUBLANES - (SSM_CONV - 1), 0), (0, 0)))
        y, s_new = _ssd(xs, bc, dt, z, w["conv_w"], w["conv_b"], w["dt_bias"], w["a_log"],
                        w["d_skip"], w["ssm_norm_g"], lc=seq, nchunk=tiles["ssd_chunks"],
                        conv_prev=prev8, h0=ssm_prev.reshape(batch, D_SSM, SSM_STATE))
        conv_new = jnp.concatenate([xs.reshape(batch, seq, D_SSM)[:, seq - (SSM_CONV - 1):],
                                    bc.reshape(batch, seq, BC_DIM)[:, seq - (SSM_CONV - 1):]], axis=-1)

    x2, hf, *extra = _out_proj(o, y, x, w["w_out"], w["norm_ffn_g"][None, :], tm=tiles["out_tm"])
    if extra:
        w16["w_out"] = extra[0]

    ffn_args = (hf, x2, w["w_gate"], w["w_up"], w["w_down"], w["ffn_conv_w"], w["ffn_conv_b"], gfin)
    ffn_tiles = dict(tm=tiles["ffn_tm"], tf=tiles["ffn_tf"], final_norm=final_norm)
    if state is None:
        x3, gl = _ffn(*ffn_args, **ffn_tiles)
        ffn_new = gl[None, gl.shape[0] - (FFN_CONV - 1):]
    else:
        x3, gl, w16["w_gate"], w16["w_up"], w16["w_down"] = _ffn(
            *ffn_args, **ffn_tiles, seq=seq, prev=ffn_prev.reshape(batch * (FFN_CONV - 1), D_FF))
        ffn_new = gl.reshape(batch, seq, D_FF)[:, seq - (FFN_CONV - 1):]
    k_new = k.reshape(batch, seq, ATT_HEADS, 2 * QK_DIM)
    v_new = v.reshape(batch, seq, ATT_HEADS, V_DIM)
    s_new = s_new.reshape(batch, SSM_HEADS, SSM_HEAD_DIM, SSM_STATE)
    return (x3, k_new, v_new, conv_new, s_new, ffn_new), w16


def kernel(x_prompt, x_sample, cache_k, cache_v, state_ssm_conv, state_ssm, state_ffn_conv, norm_mix_g, w_in, lambda_q1, lambda_k1, lambda_q2, lambda_k2, attn_subln_g, conv_w, conv_b, dt_bias, a_log, d_skip, ssm_norm_g, w_out, norm_ffn_g, w_gate, w_up, ffn_conv_w, ffn_conv_b, w_down, norm_final_g):
    depth = w_in.shape[0]
    pb, pl_len, _ = x_prompt.shape
    sb, sl_len, _ = x_sample.shape
    assert pb == 1 and pl_len % CHUNK == 0
    xp = x_prompt.reshape(pb * pl_len, D_MODEL)
    xs = x_sample.reshape(sb * sl_len, D_MODEL)
    n_main = 2 * D_ATT + D_ATT + D_SSM + CONV_DIM
    outs_p, outs_s = [], []
    for layer in range(depth):
        lam_init = 0.8 - 0.6 * math.exp(-0.3 * layer)
        w_in_t = jnp.swapaxes(w_in[layer], 0, 1)
        w = dict(
            norm_mix_g=norm_mix_g[layer],
            w_in_main=w_in_t,
            w_in_dt=jnp.pad(w_in_t[n_main:].astype(BF16), ((0, LANES - SSM_HEADS), (0, 0))),
            lambda_q1=lambda_q1[layer], lambda_k1=lambda_k1[layer],
            lambda_q2=lambda_q2[layer], lambda_k2=lambda_k2[layer],
            attn_subln_g=attn_subln_g[layer], conv_w=conv_w[layer], conv_b=conv_b[layer],
            dt_bias=dt_bias[layer], a_log=a_log[layer], d_skip=d_skip[layer],
            ssm_norm_g=ssm_norm_g[layer], w_out=w_out[layer],
            norm_ffn_g=norm_ffn_g[layer], w_gate=w_gate[layer],
            w_up=w_up[layer], ffn_conv_w=ffn_conv_w[layer],
            ffn_conv_b=ffn_conv_b[layer], w_down=w_down[layer],
        )
        last = layer == depth - 1
        (xs, *new_s), w16 = _layer(xs, w, lam_init, last, norm_final_g, batch=sb, seq=sl_len,
                                   state=(cache_k[layer], cache_v[layer], state_ssm_conv[layer],
                                          state_ssm[layer], state_ffn_conv[layer]))
        (xp, *new_p), _ = _layer(xp, {**w, **w16}, lam_init, last, norm_final_g, batch=pb, seq=pl_len)
        outs_p.append(new_p)
        outs_s.append(new_s)
    stack = lambda outs, idx: jnp.stack([o[idx] for o in outs])
    return (xp.reshape(pb, pl_len, D_MODEL), xs.reshape(sb, sl_len, D_MODEL),
            *[stack(outs_p, idx) for idx in range(5)],
            *[stack(outs_s, idx) for idx in range(5)])
```

```python
import functools
import math

import jax
import jax.numpy as jnp
import numpy as np
from jax import lax
from jax.experimental import pallas as pl
from jax.experimental.pallas import tpu as pltpu

F32 = jnp.float32
BF16 = jnp.bfloat16

D_MODEL = 2048
CHUNK = 64
ATT_HEADS = 8
QK_DIM = 64
V_DIM = 128
D_ATT = ATT_HEADS * V_DIM
SSM_HEADS = 16
SSM_HEAD_DIM = 64
D_SSM = SSM_HEADS * SSM_HEAD_DIM
SSM_GROUPS = 2
SSM_STATE = 128
SSM_CONV = 4
BC_DIM = 2 * SSM_GROUPS * SSM_STATE
CONV_DIM = D_SSM + BC_DIM
D_FF = 5632
FFN_CONV = 3
EPS = 1e-6
LOG2E = math.log2(math.e)
Q_SCALE = QK_DIM ** -0.5 * LOG2E
LANES = 128
SUBLANES = 8
VMEM_LIMIT = 58 * 1024 * 1024

_NT = (((1,), (1,)), ((), ()))
_TN = (((0,), (0,)), ((), ()))


def _cparams(sem):
    return pltpu.CompilerParams(dimension_semantics=sem, vmem_limit_bytes=VMEM_LIMIT)


def _silu(x):
    return x * jax.nn.sigmoid(x)


def _rms(x, g):
    return x * lax.rsqrt(jnp.mean(x * x, axis=-1, keepdims=True) + EPS) * g


_PROJ_TN = 512
_PROJ_SEGS = ((0, 2), (2, 4), (4, 6), (6, 8), (8, 10), (10, 11))


def _in_proj_kernel(x_ref, g_ref, w_ref, wdt_ref,
                    q_ref, k_ref, v_ref, z_ref, xs_ref, bc_ref, dt_ref, *rest):
    w16_ref, k16_ref, v16_ref = None, None, None
    if len(rest) == 2:
        w16_ref, h_scr = rest
    else:
        k16_ref, v16_ref, h_scr = rest
    j = pl.program_id(1)

    @pl.when(j == 0)
    def _():
        hb = _rms(x_ref[...], g_ref[...]).astype(BF16)
        h_scr[...] = hb
        dt_ref[...] = lax.dot_general(hb, wdt_ref[...], _NT, preferred_element_type=F32)

    outs = (q_ref, k_ref, v_ref, z_ref, xs_ref, bc_ref)
    for (lo, hi), ref in zip(_PROJ_SEGS, outs):
        @pl.when((j >= lo) & (j < hi))
        def _(ref=ref):
            w = w_ref[...]
            if w16_ref is not None:
                w = w.astype(BF16)
                w16_ref[...] = w
            res = lax.dot_general(h_scr[...], w, _NT, preferred_element_type=F32)
            val = res * Q_SCALE if ref is q_ref else res
            ref[...] = val.astype(ref.dtype)
            copy_ref = k16_ref if ref is k_ref else v16_ref if ref is v_ref else None
            if copy_ref is not None:
                copy_ref[...] = res.astype(BF16)


def _in_proj(x, g, w_main, w_dt, tm):
    m = x.shape[0]
    tn = _PROJ_TN
    nj = w_main.shape[0] // tn

    def seg_spec(lo, hi):
        return pl.BlockSpec((tm, tn), lambda i, j: (i, jnp.clip(j - lo, 0, hi - lo - 1)))

    out_shape = (
        jax.ShapeDtypeStruct((m, D_ATT), BF16),
        jax.ShapeDtypeStruct((m, D_ATT), F32),
        jax.ShapeDtypeStruct((m, D_ATT), F32),
        jax.ShapeDtypeStruct((m, D_SSM), F32),
        jax.ShapeDtypeStruct((m, D_SSM), F32),
        jax.ShapeDtypeStruct((m, BC_DIM), F32),
        jax.ShapeDtypeStruct((m, LANES), F32),
    )
    out_specs = tuple(seg_spec(lo, hi) for lo, hi in _PROJ_SEGS) + (
        pl.BlockSpec((tm, LANES), lambda i, j: (i, 0)),)
    if w_main.dtype != BF16:
        assert m == tm, "the bf16 weight copy is written once per tile: needs a single row tile"
        out_shape += (jax.ShapeDtypeStruct((nj * tn, D_MODEL), BF16),)
        out_specs += (pl.BlockSpec((tn, D_MODEL), lambda i, j: (j, 0)),)
    else:
        out_shape += (jax.ShapeDtypeStruct((m, D_ATT), BF16),) * 2
        out_specs += (seg_spec(*_PROJ_SEGS[1]), seg_spec(*_PROJ_SEGS[2]))
    return pl.pallas_call(
        _in_proj_kernel,
        grid=(m // tm, nj),
        in_specs=[
            pl.BlockSpec((tm, D_MODEL), lambda i, j: (i, 0)),
            pl.BlockSpec((1, D_MODEL), lambda i, j: (0, 0)),
            pl.BlockSpec((tn, D_MODEL), lambda i, j: (j, 0)),
            pl.BlockSpec((LANES, D_MODEL), lambda i, j: (0, 0)),
        ],
        out_specs=out_specs,
        out_shape=out_shape,
        scratch_shapes=[pltpu.VMEM((tm, D_MODEL), BF16)],
        compiler_params=_cparams(("arbitrary", "arbitrary")),
        name="in_proj",
    )(x, g, w_main, w_dt)


def _lambda_value(lamv_ref, lam_init):
    lv = lamv_ref[...]
    s1 = jnp.sum(lv[0:1] * lv[1:2], axis=-1, keepdims=True)
    s2 = jnp.sum(lv[2:3] * lv[3:4], axis=-1, keepdims=True)
    return jnp.exp(s1) - jnp.exp(s2) + lam_init


def _split_maps(q):
    lane = lax.broadcasted_iota(jnp.int32, q.shape, 1)
    zero = jnp.zeros_like(q)
    return jnp.where(lane < QK_DIM, q, zero), jnp.where(lane >= QK_DIM, q, zero)


_VT_CHUNK = 512
_VT_PAD = 16
_UNROLL_SHIFT = 2
_UNROLL = 1 << _UNROLL_SHIFT
_HEADS_PER_STEP = 2
_QTILES_PER_GROUP = 4


def _attn_prompt_kernel(q_ref, k_ref, v_ref, slope_ref, lamv_ref, g_ref, dmask_ref, dneg_ref, o_ref,
                        k_scr, vt_scr, acc_scr, t0_scr, t1_scr, dbias_scr, stat_scr,
                        *, tile, lam_init):
    length = k_ref.shape[0]
    width = 2 * tile
    nheads = k_scr.shape[0]
    lam = _lambda_value(lamv_ref, lam_init)

    lane = lax.broadcasted_iota(jnp.int32, (length, LANES), 1)
    koff = lax.broadcasted_iota(jnp.int32, (length, LANES), 0) % tile
    koff = jnp.where(lane < 3, koff, 0).astype(F32).astype(BF16)
    extra = (lax.broadcasted_iota(jnp.int32, (_VT_PAD, tile), 0) == 0).astype(F32).astype(BF16)
    brow = lax.broadcasted_iota(jnp.int32, (LANES, width), 0)
    per = _VT_CHUNK // tile
    ws, q_biases = [], []
    for hh in range(nheads):
        cols = slice(hh * V_DIM, (hh + 1) * V_DIM)
        k_scr[hh, :, 0:V_DIM] = k_ref[:, cols]
        k_scr[hh, :, V_DIM:V_DIM + LANES] = koff
        for c in range(length // _VT_CHUNK):
            vt = v_ref[c * _VT_CHUNK:(c + 1) * _VT_CHUNK, cols].astype(F32).T.astype(BF16)
            for s in range(per):
                vt_scr[hh, c * per + s, 0:V_DIM, :] = vt[:, s * tile:(s + 1) * tile]
                vt_scr[hh, c * per + s, V_DIM:V_DIM + _VT_PAD, :] = extra
        w = slope_ref[hh][:, :1] * LOG2E
        w_hi, w_mid, w_lo = (piece.astype(F32) for piece in _split3(w))
        q_bias = jnp.where(brow == 0, w_hi, jnp.where(brow == 1, w_mid, jnp.where(brow == 2, w_lo, 0.0)))
        ws.append(w)
        q_biases.append(q_bias.astype(BF16))
        dbias_scr[hh] = w * dmask_ref[...] + dneg_ref[...]

    def q_pair(qp, _):
        _attn_prompt_pair(qp, ws, lam, q_biases, q_ref, g_ref, o_ref,
                          k_scr, vt_scr, acc_scr, (t0_scr, t1_scr),
                          dbias_scr, stat_scr, tile=tile, lam_init=lam_init)
        return 0

    lax.fori_loop(0, length // (_QTILES_PER_GROUP * tile), q_pair, 0)


def _attn_prompt_pair(qp, ws, lam, q_biases, q_ref, g_ref, o_ref,
                      k_scr, vt_scr, acc_scr, t_slots, dbias_scr, stat_scr, *, tile, lam_init):
    width = 2 * tile
    nheads = len(ws)
    group = _QTILES_PER_GROUP
    streams = [(hh, u) for hh in range(nheads) for u in range(group)]
    nfull = group * qp
    qis = [nfull + u for _, u in streams]
    rows = [pl.ds(pl.multiple_of(qi * tile, tile), tile) for qi in qis]
    q_augs = []
    for s, (hh, _) in enumerate(streams):
        q1, q2 = _split_maps(q_ref[rows[s], hh * V_DIM:(hh + 1) * V_DIM])
        q_t = jnp.concatenate([q1, q2], axis=0).astype(F32).T.astype(BF16)
        q_augs.append(jnp.concatenate([q_t, q_biases[hh]], axis=0))

    def scores(s, j):
        k0 = pl.multiple_of(jnp.minimum(j, qis[s]) * tile, tile)
        return jnp.dot(k_scr[streams[s][0], pl.ds(k0, tile), :], q_augs[s], preferred_element_type=F32)

    TMAX = 0

    def step(s, j, slot, m_old):
        hh = streams[s][0]
        off = -ws[hh] * jnp.asarray((qis[s] - j) * tile, F32)
        m_new = jnp.maximum(m_old, stat_scr[s, TMAX:TMAX + 1, :] + off)
        t_next = scores(s, j + 1)
        t_slots[1 - slot][s] = t_next
        stat_scr[s, TMAX:TMAX + 1, :] = jnp.max(t_next, axis=0, keepdims=True)
        p = jnp.exp2(t_slots[slot][s] + (off - m_new)).astype(BF16)
        pv = jnp.dot(vt_scr[hh, j], p, preferred_element_type=F32)
        acc_scr[s] = jnp.exp2(m_old - m_new) * acc_scr[s] + pv
        return m_new

    def run(first, ntiles_per_iter, niter, carry):
        def body(i, carry):
            for k in range(ntiles_per_iter):
                carry = tuple(step(s, first + ntiles_per_iter * i + k, k % 2, carry[s])
                              for s in range(len(streams)))
            return carry
        return lax.fori_loop(0, niter, body, carry)

    carry = []
    for s in range(len(streams)):
        acc_scr[s] = jnp.zeros(acc_scr.shape[1:], F32)
        t_first = scores(s, 0)
        t_slots[0][s] = t_first
        stat_scr[s, TMAX:TMAX + 1, :] = jnp.max(t_first, axis=0, keepdims=True)
        carry.append(jnp.full((1, width), -0.5 * float(jnp.finfo(F32).max), F32))
    n_main = lax.shift_right_logical(nfull, _UNROLL_SHIFT)
    done = n_main * _UNROLL
    carry = run(0, _UNROLL, n_main, tuple(carry))
    carry = run(done, 2, lax.shift_right_logical(nfull - done, 1), carry)

    m_run = list(carry)
    diag_slot = [0] * len(streams)
    for e in range(group - 1):
        for s, (hh, u) in enumerate(streams):
            if u > e:
                m_run[s] = step(s, nfull + e, diag_slot[s], m_run[s])
                diag_slot[s] = 1 - diag_slot[s]
    for s, (hh, u) in enumerate(streams):
        t = t_slots[diag_slot[s]][s] + dbias_scr[hh]
        m_new = jnp.maximum(m_run[s], jnp.max(t, axis=0, keepdims=True))
        p = jnp.exp2(t - m_new).astype(BF16)
        acc_scr[s] = (jnp.exp2(m_run[s] - m_new) * acc_scr[s]
                      + jnp.dot(vt_scr[hh, qis[s]], p, preferred_element_type=F32))
    for s, (hh, u) in enumerate(streams):
        acc = acc_scr[s]
        on = acc[0:V_DIM] / acc[V_DIM:V_DIM + 1]
        o_t = on[:, :tile] - lam * on[:, tile:]
        o_t = o_t * lax.rsqrt(jnp.mean(o_t * o_t, axis=0, keepdims=True) + EPS)
        o_ref[rows[s], hh * V_DIM:(hh + 1) * V_DIM] = (
            o_t.T * g_ref[...] * (1.0 - lam_init)).astype(o_ref.dtype)


def _attn_prompt(q, k, v, slopes, lamv, g, lam_init, tile):
    length = q.shape[0]
    key = np.arange(tile)[:, None]
    qry = np.tile(np.arange(tile), 2)[None, :]
    dmask = jnp.asarray((qry - np.abs(qry - key)) - key, F32)
    dneg = jnp.asarray(np.where(key // CHUNK <= qry // CHUNK, 0.0, -np.inf), F32)
    nh = _HEADS_PER_STEP
    nstream = nh * _QTILES_PER_GROUP
    diag_spec = pl.BlockSpec((tile, 2 * tile), lambda h: (0, 0), pipeline_mode=pl.Buffered(1))
    head_spec = pl.BlockSpec((length, nh * V_DIM), lambda h: (0, h), pipeline_mode=pl.Buffered(1))
    kv_spec = pl.BlockSpec((length, nh * V_DIM), lambda h: (0, h))
    return pl.pallas_call(
        functools.partial(_attn_prompt_kernel, tile=tile, lam_init=lam_init),
        grid=(ATT_HEADS // nh,),
        in_specs=[
            head_spec, kv_spec, kv_spec,
            pl.BlockSpec((nh, 1, LANES), lambda h: (h, 0, 0)),
            pl.BlockSpec((4, QK_DIM), lambda h: (0, 0)),
            pl.BlockSpec((1, V_DIM), lambda h: (0, 0)),
            diag_spec, diag_spec,
        ],
        out_specs=head_spec,
        out_shape=jax.ShapeDtypeStruct((length, D_ATT), BF16),
        scratch_shapes=[
            pltpu.VMEM((nh, length, V_DIM + LANES), BF16),
            pltpu.VMEM((nh, length // tile, V_DIM + _VT_PAD, tile), BF16),
            pltpu.VMEM((nstream, V_DIM + _VT_PAD, 2 * tile), F32),
            pltpu.VMEM((nstream, tile, 2 * tile), F32),
            pltpu.VMEM((nstream, tile, 2 * tile), F32),
            pltpu.VMEM((nh, tile, 2 * tile), F32),
            pltpu.VMEM((nstream, SUBLANES, 2 * tile), F32),
        ],
        compiler_params=_cparams(("arbitrary",)),
        name="attn_prompt",
    )(q, k, v, slopes, lamv, g, dmask, dneg)


_SAMPLE_SEQS_PER_STEP = 2


_SAMPLE_CACHE_SLOTS = 3


def _attn_sample_kernel(q_ref, kn_ref, vn_ref, ck_hbm, cv_hbm, lamv_ref, g_ref, bias_c_ref, bias_n_ref,
                        o_ref, ck_ring, cv_ring, sem, *, seq, past, lam_init):
    b = pl.program_id(0)
    nsteps = pl.num_programs(0)
    slots, per = ck_ring.shape[0], ck_ring.shape[1]

    def fetches(step, slot):
        src = pl.ds(step * per, per)
        return (pltpu.make_async_copy(ck_hbm.at[src], ck_ring.at[slot], sem.at[0, slot]),
                pltpu.make_async_copy(cv_hbm.at[src], cv_ring.at[slot], sem.at[1, slot]))

    @pl.when(b == 0)
    def _():
        for s in range(slots - 1):
            for copy in fetches(s, s):
                copy.start()

    ahead = b + slots - 1

    @pl.when(ahead < nsteps)
    def _():
        for copy in fetches(ahead, ahead % slots):
            copy.start()

    slot = b % slots
    for copy in fetches(b, slot):
        copy.wait()
    ck_ref, cv_ref = ck_ring.at[slot], cv_ring.at[slot]

    lam = _lambda_value(lamv_ref, lam_init)
    rows = 2 * seq
    for bi in range(ck_ref.shape[0]):
        br = slice(bi * seq, (bi + 1) * seq)
        sc, sn = [], []
        for h in range(ATT_HEADS):
            sl = slice(h * V_DIM, (h + 1) * V_DIM)
            q1, q2 = _split_maps(q_ref[br, sl])
            qq = jnp.concatenate([q1, q2], axis=0)
            kc = ck_ref[bi, pl.ds(h, past, stride=ATT_HEADS), :].astype(BF16)
            sc.append(lax.dot_general(qq, kc, _NT, preferred_element_type=F32))
            sn.append(lax.dot_general(qq, kn_ref[br, sl].astype(BF16), _NT, preferred_element_type=F32))
        sc = jnp.concatenate(sc, axis=0) - bias_c_ref[...]
        sn = jnp.concatenate(sn, axis=0) - bias_n_ref[...]
        m = jnp.maximum(jnp.max(sc, axis=-1, keepdims=True), jnp.max(sn, axis=-1, keepdims=True))
        pc = jnp.exp2(sc - m)
        pn = jnp.exp2(sn - m)
        inv_l = 1.0 / (jnp.sum(pc, axis=-1, keepdims=True) + jnp.sum(pn, axis=-1, keepdims=True))
        pc = pc.astype(BF16)
        pn = pn.astype(BF16)
        for h in range(ATT_HEADS):
            sl = slice(h * V_DIM, (h + 1) * V_DIM)
            hr = slice(h * rows, (h + 1) * rows)
            vc = cv_ref[bi, pl.ds(h, past, stride=ATT_HEADS), :].astype(BF16)
            acc = (jnp.dot(pc[hr], vc, preferred_element_type=F32)
                   + jnp.dot(pn[hr], vn_ref[br, sl].astype(BF16), preferred_element_type=F32))
            on = acc * inv_l[hr]
            o = on[:seq] - lam * on[seq:]
            o_ref[br, sl] = (_rms(o, g_ref[...]) * (1.0 - lam_init)).astype(o_ref.dtype)


def _attn_sample(q, k_new, v_new, cache_k, cache_v, lamv, g, lam_init, seq):
    nb, past = cache_k.shape[0], cache_k.shape[1] // ATT_HEADS
    slope = np.repeat(2.0 ** (-8.0 * np.arange(1, ATT_HEADS + 1) / ATT_HEADS), 2 * seq)[:, None] * LOG2E
    qpos = np.tile(np.arange(seq), 2 * ATT_HEADS)[:, None]
    bias_c = jnp.asarray(slope * (qpos + past - np.arange(past)[None, :]), F32)
    bias_n = jnp.asarray(slope * np.abs(qpos - np.arange(seq)[None, :]), F32)
    per = _SAMPLE_SEQS_PER_STEP
    row_spec = pl.BlockSpec((per * seq, D_ATT), lambda b: (b, 0))
    cache_spec = pl.BlockSpec(memory_space=pl.ANY)
    slots = _SAMPLE_CACHE_SLOTS
    assert nb % per == 0 and nb // per >= slots - 1

    def const(arr):
        return pl.BlockSpec(arr.shape, lambda b: (0, 0), pipeline_mode=pl.Buffered(1))

    return pl.pallas_call(
        functools.partial(_attn_sample_kernel, seq=seq, past=past, lam_init=lam_init),
        grid=(nb // per,),
        in_specs=[row_spec, row_spec, row_spec, cache_spec, cache_spec,
                  pl.BlockSpec((4, QK_DIM), lambda b: (0, 0)),
                  pl.BlockSpec((1, V_DIM), lambda b: (0, 0)),
                  const(bias_c), const(bias_n)],
        out_specs=row_spec,
        out_shape=jax.ShapeDtypeStruct((nb * seq, D_ATT), BF16),
        scratch_shapes=[
            pltpu.VMEM((slots, per, past * ATT_HEADS, V_DIM), F32),
            pltpu.VMEM((slots, per, past * ATT_HEADS, V_DIM), F32),
            pltpu.SemaphoreType.DMA((2, slots)),
        ],
        compiler_params=_cparams(("arbitrary",)),
        name="attn_sample",
    )(q, k_new, v_new, cache_k, cache_v, lamv, g, bias_c, bias_n)


def _split3(x):
    hi = x.astype(BF16)
    r = x - hi.astype(F32)
    mid = r.astype(BF16)
    lo = (r - mid.astype(F32)).astype(BF16)
    return hi, mid, lo


def _ssd_chunk(xs, bm, cm, dt, z, s_ref, a_heads, dskip, gn, e3, es3, t3, ones_k, lc):
    seg_w = SSM_HEADS * lc
    half = D_SSM // SSM_GROUPS
    d3 = jnp.concatenate(_split3(dt), axis=1)
    dt_x = jnp.dot(d3, e3, preferred_element_type=F32)
    la3 = jnp.concatenate(_split3(dt * a_heads), axis=0)
    ac3 = jnp.concatenate(_split3(jnp.dot(t3, la3, preferred_element_type=F32)), axis=1)
    acol = jnp.dot(ac3, e3, preferred_element_type=F32)
    acol_s = acol if seg_w == D_SSM else jnp.dot(ac3, es3, preferred_element_type=F32)
    t_idx = lax.broadcasted_iota(jnp.int32, (lc, seg_w), 0)
    s_idx = lax.broadcasted_iota(jnp.int32, (lc, seg_w), 1) % lc
    arow = jnp.sum(jnp.where(t_idx == s_idx, acol_s, 0.0), axis=0, keepdims=True)
    decay = jnp.where(t_idx >= s_idx, jnp.exp(acol_s - arow), 0.0)

    cb16 = cm.astype(BF16)
    bb16 = bm.astype(BF16)
    hpg = SSM_HEADS // SSM_GROUPS
    cbs = []
    for g in range(SSM_GROUPS):
        gs = slice(g * SSM_STATE, (g + 1) * SSM_STATE)
        b_rep = jnp.concatenate([bb16[:, gs]] * hpg, axis=0)
        cbs.append(lax.dot_general(cb16[:, gs], b_rep, _NT, preferred_element_type=F32))
    mmat = (jnp.concatenate(cbs, axis=1) * decay).astype(BF16)

    xdt = xs * dt_x
    xdt16 = xdt.astype(BF16)
    hk = (2 * LANES) // lc
    wd = hk * SSM_HEAD_DIM
    blk = (lax.broadcasted_iota(jnp.int32, (hk * lc, wd), 0) // lc
           == lax.broadcasted_iota(jnp.int32, (hk * lc, wd), 1) // SSM_HEAD_DIM)
    parts = []
    for i in range(SSM_HEADS // hk):
        xd = xdt16[:, i * wd:(i + 1) * wd]
        bd = jnp.where(blk, jnp.concatenate([xd] * hk, axis=0), jnp.zeros((), BF16))
        parts.append(jnp.dot(mmat[:, i * hk * lc:(i + 1) * hk * lc], bd, preferred_element_type=F32))
    y_intra = jnp.concatenate(parts, axis=1) if len(parts) > 1 else parts[0]

    yi = []
    for g in range(SSM_GROUPS):
        sg = s_ref[g * half:(g + 1) * half, :].astype(BF16)
        yi.append(lax.dot_general(cb16[:, g * SSM_STATE:(g + 1) * SSM_STATE], sg, _NT,
                                  preferred_element_type=F32))
    y_inter = jnp.exp(acol) * jnp.concatenate(yi, axis=1)

    alast = acol[lc - 1:lc, :]
    dec_end = jnp.exp(alast - acol)
    xd_end = (xdt * dec_end).astype(BF16)
    krow = lax.broadcasted_iota(jnp.int32, (ones_k.shape[0], D_SSM), 0)
    a_hi, a_mid, a_lo = (piece.astype(F32) for piece in _split3(alast))
    pieces = jnp.where(krow == 0, a_hi, jnp.where(krow == 1, a_mid, jnp.where(krow == 2, a_lo, 0.0)))
    acl = lax.dot_general(pieces.astype(BF16), ones_k, _TN, preferred_element_type=F32)
    for g in range(SSM_GROUPS):
        rows = slice(g * half, (g + 1) * half)
        upd = lax.dot_general(xd_end[:, rows], bb16[:, g * SSM_STATE:(g + 1) * SSM_STATE], _TN,
                              preferred_element_type=F32)
        s_ref[rows, :] = jnp.exp(acl[rows, :]) * s_ref[rows, :] + upd

    y = (y_intra + y_inter + dskip * xs) * _silu(z)
    outs = []
    for g in range(SSM_GROUPS):
        cs = slice(g * half, (g + 1) * half)
        outs.append(_rms(y[:, cs], gn[:, cs]))
    return jnp.concatenate(outs, axis=1)


def _ssd_kernel(*refs, lc, nchunk, has_state):
    if has_state:
        (xs_ref, bc_ref, dt_ref, z_ref, prev_ref, h0_ref, cw_ref, cbias_ref, dtb_ref, alog_ref,
         dskip_ref, gn_ref, e3_ref, es3_ref, t3_ref, ones_ref,
         y_ref, s_ref, scr_x, scr_bc) = refs
    else:
        (xs_ref, bc_ref, dt_ref, z_ref, cw_ref, cbias_ref, dtb_ref, alog_ref,
         dskip_ref, gn_ref, e3_ref, es3_ref, t3_ref, ones_ref,
         y_ref, s_ref, scr_x, scr_bc) = refs
    rows = lc * nchunk
    pad = SUBLANES

    def conv_silu(scr_x, scr_bc, nrows):
        xc = cbias_ref[:, 0:D_SSM]
        bcc = cbias_ref[:, D_SSM:CONV_DIM]
        for tap in range(SSM_CONV):
            off = pad - (SSM_CONV - 1) + tap
            xc = xc + scr_x[off:off + nrows, :] * cw_ref[tap:tap + 1, 0:D_SSM]
            bcc = bcc + scr_bc[off:off + nrows, :] * cw_ref[tap:tap + 1, D_SSM:CONV_DIM]
        return _silu(xc), _silu(bcc)

    dt_in = dt_ref[...] + dtb_ref[...]
    dt = jnp.maximum(dt_in, 0.0) + jnp.log1p(jnp.exp(-jnp.abs(dt_in)))
    a_heads = -jnp.exp(alog_ref[...])
    nbm = SSM_GROUPS * SSM_STATE

    def chunk(c, xs_c, bc_c, s2d):
        rs = slice(c * lc, (c + 1) * lc)
        y = _ssd_chunk(xs_c, bc_c[:, 0:nbm], bc_c[:, nbm:2 * nbm], dt[rs],
                       z_ref[rs, :], s2d, a_heads, dskip_ref[...], gn_ref[...],
                       e3_ref[...], es3_ref[...], t3_ref[...], ones_ref[...], lc)
        y_ref[rs, :] = y.astype(y_ref.dtype)

    if has_state:
        for c in range(nchunk):
            rs = slice(c * lc, (c + 1) * lc)
            scr_x[c, 0:pad, :] = prev_ref[c, :, 0:D_SSM]
            scr_bc[c, 0:pad, :] = prev_ref[c, :, D_SSM:CONV_DIM]
            scr_x[c, pad:pad + lc, :] = xs_ref[rs, :]
            scr_bc[c, pad:pad + lc, :] = bc_ref[rs, :]
            s_ref[c] = h0_ref[c]
            xs_act, bc_act = conv_silu(scr_x.at[c], scr_bc.at[c], lc)
            chunk(c, xs_act, bc_act, s_ref.at[c])
    else:
        @pl.when(pl.program_id(0) == 0)
        def _():
            scr_x[0:pad, :] = jnp.zeros((pad, D_SSM), F32)
            scr_bc[0:pad, :] = jnp.zeros((pad, BC_DIM), F32)
            s_ref[...] = jnp.zeros(s_ref.shape, F32)

        scr_x[pad:pad + rows, :] = xs_ref[...]
        scr_bc[pad:pad + rows, :] = bc_ref[...]
        xs_act, bc_act = conv_silu(scr_x, scr_bc, rows)
        scr_x[0:pad, :] = scr_x[rows:rows + pad, :]
        scr_bc[0:pad, :] = scr_bc[rows:rows + pad, :]
        for c in range(nchunk):
            rs = slice(c * lc, (c + 1) * lc)
            chunk(c, xs_act[rs], bc_act[rs], s_ref)


def _ssd_constants(lc):
    seg_w = SSM_HEADS * lc
    head_of_lane = np.arange(D_SSM) // SSM_HEAD_DIM
    e = (np.arange(LANES)[:, None] == head_of_lane[None, :]).astype(np.float32)
    es = (np.arange(LANES)[:, None] == (np.arange(seg_w) // lc)[None, :]).astype(np.float32)
    tri = np.tril(np.ones((lc, lc), np.float32))
    return (jnp.asarray(np.concatenate([e] * 3, axis=0), BF16),
            jnp.asarray(np.concatenate([es] * 3, axis=0), BF16),
            jnp.asarray(np.concatenate([tri] * 3, axis=1), BF16),
            jnp.ones((2 * SUBLANES, SSM_STATE), BF16))


def _ssd(xs, bc, dt, z, conv_w, conv_b, dt_bias, a_log, d_skip, gn, lc, nchunk,
         conv_prev=None, h0=None):
    m = xs.shape[0]
    rows = lc * nchunk
    has_state = h0 is not None
    e3, es3, t3, ones_k = _ssd_constants(lc)
    dtb = jnp.zeros((1, LANES), F32).at[0, :SSM_HEADS].set(dt_bias)
    alog = jnp.zeros((1, LANES), F32).at[0, :SSM_HEADS].set(a_log)
    dskip_x = jnp.repeat(d_skip, SSM_HEAD_DIM)[None, :]

    def const(shape):
        return pl.BlockSpec(shape, lambda i: (0,) * len(shape))

    def rowblk(width):
        return pl.BlockSpec((rows, width), lambda i: (i, 0))

    in_specs = [rowblk(D_SSM), rowblk(BC_DIM), rowblk(LANES), rowblk(D_SSM)]
    args = [xs, bc, dt, z]
    if has_state:
        nb = h0.shape[0]
        in_specs += [pl.BlockSpec((nchunk, SUBLANES, CONV_DIM), lambda i: (i, 0, 0)),
                     pl.BlockSpec((nchunk, D_SSM, SSM_STATE), lambda i: (i, 0, 0))]
        args += [conv_prev, h0]
        s_shape = jax.ShapeDtypeStruct((nb, D_SSM, SSM_STATE), F32)
        s_spec = pl.BlockSpec((nchunk, D_SSM, SSM_STATE), lambda i: (i, 0, 0))
        scratch = [pltpu.VMEM((nchunk, lc + SUBLANES, D_SSM), F32),
                   pltpu.VMEM((nchunk, lc + SUBLANES, BC_DIM), F32)]
    else:
        s_shape = jax.ShapeDtypeStruct((D_SSM, SSM_STATE), F32)
        s_spec = const((D_SSM, SSM_STATE))
        scratch = [pltpu.VMEM((rows + SUBLANES, D_SSM), F32),
                   pltpu.VMEM((rows + SUBLANES, BC_DIM), F32)]
    in_specs += [const((SSM_CONV, CONV_DIM)), const((1, CONV_DIM)), const((1, LANES)),
                 const((1, LANES)), const((1, D_SSM)), const((1, D_SSM)),
                 const(e3.shape), const(es3.shape), const(t3.shape), const(ones_k.shape)]
    args += [conv_w, conv_b[None, :], dtb, alog, dskip_x, gn[None, :], e3, es3, t3, ones_k]
    return pl.pallas_call(
        functools.partial(_ssd_kernel, lc=lc, nchunk=nchunk, has_state=has_state),
        grid=(m // rows,),
        in_specs=in_specs,
        out_specs=(rowblk(D_SSM), s_spec),
        out_shape=(jax.ShapeDtypeStruct((m, D_SSM), BF16), s_shape),
        scratch_shapes=scratch,
        compiler_params=_cparams(("arbitrary",)),
        name="ssd_sample" if has_state else "ssd_prompt",
    )(*args)


def _out_proj_kernel(o_ref, y_ref, x_ref, w_ref, g_ref, x2_ref, hf_ref, w16_ref=None):
    if w16_ref is not None:
        w16_ref[...] = w_ref[...].astype(BF16)
        w_ref = w16_ref
    x2 = (x_ref[...]
          + jnp.dot(o_ref[...], w_ref[0:D_ATT, :], preferred_element_type=F32)
          + jnp.dot(y_ref[...], w_ref[D_ATT:D_ATT + D_SSM, :], preferred_element_type=F32))
    x2_ref[...] = x2
    hf_ref[...] = _rms(x2, g_ref[...]).astype(hf_ref.dtype)


def _out_proj(o, y, x, w, g, tm):
    m = x.shape[0]
    w_spec = pl.BlockSpec((D_ATT + D_SSM, D_MODEL), lambda i: (0, 0), pipeline_mode=pl.Buffered(1))
    out_specs = (pl.BlockSpec((tm, D_MODEL), lambda i: (i, 0)),
                 pl.BlockSpec((tm, D_MODEL), lambda i: (i, 0)))
    out_shape = (jax.ShapeDtypeStruct((m, D_MODEL), F32),
                 jax.ShapeDtypeStruct((m, D_MODEL), BF16))
    if w.dtype != BF16:
        assert m == tm, "the bf16 weight copy is written once: needs a single row tile"
        out_specs += (pl.BlockSpec((D_ATT + D_SSM, D_MODEL), lambda i: (0, 0)),)
        out_shape += (jax.ShapeDtypeStruct(w.shape, BF16),)
    return pl.pallas_call(
        _out_proj_kernel,
        grid=(m // tm,),
        in_specs=[
            pl.BlockSpec((tm, D_ATT), lambda i: (i, 0)),
            pl.BlockSpec((tm, D_SSM), lambda i: (i, 0)),
            pl.BlockSpec((tm, D_MODEL), lambda i: (i, 0)),
            w_spec,
            pl.BlockSpec((1, D_MODEL), lambda i: (0, 0)),
        ],
        out_specs=out_specs,
        out_shape=out_shape,
        compiler_params=_cparams(("arbitrary",)),
        name="out_proj",
    )(o, y, x, w, g)


_RESIDUAL_PIECES = 8


def _ffn_kernel(*refs, tm, seq, final_norm):
    if seq is None:
        (hf_ref, x2_ref, wg_ref, wu_ref, wd_ref, cw_ref, cb_ref, gfin_ref,
         out_ref, gl_ref, g_scr, act_scr, carry_scr) = refs
    else:
        (hf_ref, x2_ref, wg_ref, wu_ref, wd_ref, cw_ref, cb_ref, gfin_ref, prev_ref, sel1_ref, sel2_ref,
         out_ref, gl_ref, wg16_ref, wu16_ref, wd16_ref, g_scr, act_scr) = refs
    i = pl.program_id(0)
    f = pl.program_id(1)
    nf = pl.num_programs(1) - 1
    pad = SUBLANES
    res_rows = tm // _RESIDUAL_PIECES

    def weight(w_ref, w16_ref):
        if seq is None:
            return w_ref[...]
        w16 = w_ref[...].astype(BF16)
        w16_ref[...] = w16
        return w16

    def gate_up():
        hf = hf_ref[...]
        gate = jnp.dot(hf, weight(wg_ref, None if seq is None else wg16_ref), preferred_element_type=F32)
        up = jnp.dot(hf, weight(wu_ref, None if seq is None else wu16_ref), preferred_element_type=F32)
        g_scr[pad:pad + tm, :] = gate
        if seq is None:
            g_scr[0:pad, :] = carry_scr[f]
            carry_scr[f] = g_scr[tm:tm + pad, :]
            gl_ref[...] = g_scr[tm:tm + pad, :]
            g1 = g_scr[pad - 1:pad - 1 + tm, :]
            g2 = g_scr[pad - 2:pad - 2 + tm, :]
        else:
            g_scr[0:pad, :] = jnp.zeros((pad, gate.shape[1]), F32)
            gl_ref[...] = gate
            prev3 = jnp.concatenate(_split3(prev_ref[...]), axis=0)
            ov1 = jnp.dot(sel1_ref[...], prev3, preferred_element_type=F32)
            ov2 = jnp.dot(sel2_ref[...], prev3, preferred_element_type=F32)
            pos = lax.broadcasted_iota(jnp.int32, gate.shape, 0) % seq
            g1 = jnp.where(pos == 0, ov1, g_scr[pad - 1:pad - 1 + tm, :])
            g2 = jnp.where(pos < 2, ov2, g_scr[pad - 2:pad - 2 + tm, :])
        conv = cb_ref[...] + g2 * cw_ref[0:1, :] + g1 * cw_ref[1:2, :] + gate * cw_ref[2:3, :]
        act_scr[...] = (_silu(conv) * up).astype(BF16)

    def down():
        return jnp.dot(act_scr[...], weight(wd_ref, None if seq is None else wd16_ref),
                       preferred_element_type=F32)

    @pl.when(f == 0)
    def _():
        if seq is None:
            @pl.when(i == 0)
            def _():
                carry_scr[...] = jnp.zeros(carry_scr.shape, F32)
        out_ref[...] = jnp.zeros(out_ref.shape, F32)
        out_ref[0:res_rows, :] = x2_ref[...]
        gate_up()

    @pl.when((f > 0) & (f < nf))
    def _():
        contrib = down()
        gate_up()
        out_ref[...] += contrib
        r0 = pl.multiple_of(jnp.minimum(f, _RESIDUAL_PIECES - 1) * res_rows, res_rows)
        out_ref[pl.ds(r0, res_rows), :] += jnp.where(f < _RESIDUAL_PIECES, x2_ref[...], 0.0)

    @pl.when(f == nf)
    def _():
        res = out_ref[...] + down()
        out_ref[...] = _rms(res, gfin_ref[...]) if final_norm else res


def _ffn_prev_selectors(m, seq, nprev):
    sel1 = np.zeros((m, 3 * nprev), np.float32)
    sel2 = np.zeros((m, 3 * nprev), np.float32)
    nstate = FFN_CONV - 1
    for b in range(m // seq):
        for piece in range(3):
            base = piece * nprev + b * nstate
            sel1[b * seq, base + 1] = 1.0
            sel2[b * seq, base + 0] = 1.0
            sel2[b * seq + 1, base + 1] = 1.0
    return jnp.asarray(sel1, BF16), jnp.asarray(sel2, BF16)


def _ffn(hf, x2, wg, wu, wd, cw, cb, gfin, tm, tf, final_norm, seq=None, prev=None):
    m = hf.shape[0]
    nf = D_FF // tf
    assert nf >= _RESIDUAL_PIECES and tm % (_RESIDUAL_PIECES * SUBLANES) == 0
    cur = lambda f: jnp.minimum(f, nf - 1)
    last = lambda f: jnp.maximum(f - 1, 0)
    in_specs = [
        pl.BlockSpec((tm, D_MODEL), lambda i, f: (i, 0)),
        pl.BlockSpec((tm // _RESIDUAL_PIECES, D_MODEL),
                     lambda i, f: (i * _RESIDUAL_PIECES + jnp.minimum(f, _RESIDUAL_PIECES - 1), 0)),
        pl.BlockSpec((D_MODEL, tf), lambda i, f: (0, cur(f))),
        pl.BlockSpec((D_MODEL, tf), lambda i, f: (0, cur(f))),
        pl.BlockSpec((tf, D_MODEL), lambda i, f: (last(f), 0)),
        pl.BlockSpec((FFN_CONV, tf), lambda i, f: (0, cur(f))),
        pl.BlockSpec((1, tf), lambda i, f: (0, cur(f))),
        pl.BlockSpec((1, D_MODEL), lambda i, f: (0, 0)),
    ]
    args = [hf, x2, wg, wu, wd, cw, cb[None, :], gfin[None, :]]
    scratch = [pltpu.VMEM((tm + SUBLANES, tf), F32), pltpu.VMEM((tm, tf), BF16)]
    if seq is None:
        gl_shape = jax.ShapeDtypeStruct((m // tm * SUBLANES, D_FF), F32)
        gl_spec = pl.BlockSpec((SUBLANES, tf), lambda i, f: (i, cur(f)))
        scratch.append(pltpu.VMEM((nf, SUBLANES, tf), F32))
    else:
        assert m == tm, "sample FFN handles all sequences in one row tile"
        nprev = prev.shape[0]
        sel1, sel2 = _ffn_prev_selectors(m, seq, nprev)
        in_specs += [pl.BlockSpec((nprev, tf), lambda i, f: (0, cur(f))),
                     pl.BlockSpec(sel1.shape, lambda i, f: (0, 0)),
                     pl.BlockSpec(sel2.shape, lambda i, f: (0, 0))]
        args += [prev, sel1, sel2]
        gl_shape = jax.ShapeDtypeStruct((m, D_FF), F32)
        gl_spec = pl.BlockSpec((tm, tf), lambda i, f: (i, cur(f)))
    out_specs = [pl.BlockSpec((tm, D_MODEL), lambda i, f: (i, 0)), gl_spec]
    out_shape = [jax.ShapeDtypeStruct((m, D_MODEL), F32), gl_shape]
    if seq is not None:
        out_specs += [in_specs[2], in_specs[3], in_specs[4]]
        out_shape += [jax.ShapeDtypeStruct(wg.shape, BF16), jax.ShapeDtypeStruct(wu.shape, BF16),
                      jax.ShapeDtypeStruct(wd.shape, BF16)]
    return pl.pallas_call(
        functools.partial(_ffn_kernel, tm=tm, seq=seq, final_norm=final_norm),
        grid=(m // tm, nf + 1),
        in_specs=in_specs,
        out_specs=tuple(out_specs),
        out_shape=tuple(out_shape),
        scratch_shapes=scratch,
        compiler_params=_cparams(("arbitrary", "arbitrary")),
        name="ffn_sample" if seq is not None else "ffn_prompt",
    )(*args)


def _tiling(m, has_state):
    return dict(
        proj_tm=min(1024, m),
        attn_tile=256,
        ssd_chunks=2 * _SAMPLE_SEQS_PER_STEP if has_state else 8,
        out_tm=min(512, m),
        ffn_tm=min(512, m) if has_state else min(1024, m),
        ffn_tf=256 if has_state else 512,
    )


def _layer(x, w, lam_init, final_norm, gfin, *, batch, seq, state=None):
    m = x.shape[0]
    lamv = jnp.stack([w["lambda_q1"], w["lambda_k1"], w["lambda_q2"], w["lambda_k2"]])
    tiles = _tiling(m, state is not None)
    w16 = {}
    q, k, v, z, xs, bc, dt, *extra = _in_proj(x, w["norm_mix_g"][None, :], w["w_in_main"], w["w_in_dt"],
                                              tm=tiles["proj_tm"])
    if len(extra) == 1:
        w16["w_in_main"] = extra[0]

    if state is None:
        k16, v16 = extra if len(extra) == 2 else (k.astype(BF16), v.astype(BF16))
        slopes = jnp.broadcast_to(
            jnp.asarray(2.0 ** (-8.0 * np.arange(1, ATT_HEADS + 1) / ATT_HEADS), F32)[:, None, None],
            (ATT_HEADS, 1, LANES))
        o = _attn_prompt(q, k16, v16, slopes, lamv, w["attn_subln_g"][None, :], lam_init,
                         tile=tiles["attn_tile"])
        y, s_new = _ssd(xs, bc, dt, z, w["conv_w"], w["conv_b"], w["dt_bias"], w["a_log"],
                        w["d_skip"], w["ssm_norm_g"], lc=CHUNK, nchunk=tiles["ssd_chunks"])
        conv_new = jnp.concatenate([xs[m - (SSM_CONV - 1):], bc[m - (SSM_CONV - 1):]], axis=-1)[None]
        s_new = s_new[None]
    else:
        cache_k, cache_v, conv_prev, ssm_prev, ffn_prev = state
        past = cache_k.shape[1]
        assert past % CHUNK == 0 and seq <= CHUNK
        o = _attn_sample(q, k, v, cache_k.reshape(batch, past * ATT_HEADS, V_DIM),
                         cache_v.reshape(batch, past * ATT_HEADS, V_DIM),
                         lamv, w["attn_subln_g"][None, :], lam_init, seq)
        prev8 = jnp.pad(conv_prev, ((0, 0), (SUBLANES - (SSM_CONV - 1), 0), (0, 0)))
        y, s_new = _ssd(xs, bc, dt, z, w["conv_w"], w["conv_b"], w["dt_bias"], w["a_log"],
                        w["d_skip"], w["ssm_norm_g"], lc=seq, nchunk=tiles["ssd_chunks"],
                        conv_prev=prev8, h0=ssm_prev.reshape(batch, D_SSM, SSM_STATE))
        conv_new = jnp.concatenate([xs.reshape(batch, seq, D_SSM)[:, seq - (SSM_CONV - 1):],
                                    bc.reshape(batch, seq, BC_DIM)[:, seq - (SSM_CONV - 1):]], axis=-1)

    x2, hf, *extra = _out_proj(o, y, x, w["w_out"], w["norm_ffn_g"][None, :], tm=tiles["out_tm"])
    if extra:
        w16["w_out"] = extra[0]

    ffn_args = (hf, x2, w["w_gate"], w["w_up"], w["w_down"], w["ffn_conv_w"], w["ffn_conv_b"], gfin)
    ffn_tiles = dict(tm=tiles["ffn_tm"], tf=tiles["ffn_tf"], final_norm=final_norm)
    if state is None:
        x3, gl = _ffn(*ffn_args, **ffn_tiles)
        ffn_new = gl[None, gl.shape[0] - (FFN_CONV - 1):]
    else:
        x3, gl, w16["w_gate"], w16["w_up"], w16["w_down"] = _ffn(
            *ffn_args, **ffn_tiles, seq=seq, prev=ffn_prev.reshape(batch * (FFN_CONV - 1), D_FF))
        ffn_new = gl.reshape(batch, seq, D_FF)[:, seq - (FFN_CONV - 1):]
    k_new = k.reshape(batch, seq, ATT_HEADS, 2 * QK_DIM)
    v_new = v.reshape(batch, seq, ATT_HEADS, V_DIM)
    s_new = s_new.reshape(batch, SSM_HEADS, SSM_HEAD_DIM, SSM_STATE)
    return (x3, k_new, v_new, conv_new, s_new, ffn_new), w16


def kernel(x_prompt, x_sample, cache_k, cache_v, state_ssm_conv, state_ssm, state_ffn_conv, norm_mix_g, w_in, lambda_q1, lambda_k1, lambda_q2, lambda_k2, attn_subln_g, conv_w, conv_b, dt_bias, a_log, d_skip, ssm_norm_g, w_out, norm_ffn_g, w_gate, w_up, ffn_conv_w, ffn_conv_b, w_down, norm_final_g):
    depth = w_in.shape[0]
    pb, pl_len, _ = x_prompt.shape
    sb, sl_len, _ = x_sample.shape
    assert pb == 1 and pl_len % CHUNK == 0
    xp = x_prompt.reshape(pb * pl_len, D_MODEL)
    xs = x_sample.reshape(sb * sl_len, D_MODEL)
    n_main = 2 * D_ATT + D_ATT + D_SSM + CONV_DIM
    outs_p, outs_s = [], []
    for layer in range(depth):
        lam_init = 0.8 - 0.6 * math.exp(-0.3 * layer)
        w_in_t = jnp.swapaxes(w_in[layer], 0, 1)
        w = dict(
            norm_mix_g=norm_mix_g[layer],
            w_in_main=w_in_t,
            w_in_dt=jnp.pad(w_in_t[n_main:].astype(BF16), ((0, LANES - SSM_HEADS), (0, 0))),
            lambda_q1=lambda_q1[layer], lambda_k1=lambda_k1[layer],
            lambda_q2=lambda_q2[layer], lambda_k2=lambda_k2[layer],
            attn_subln_g=attn_subln_g[layer], conv_w=conv_w[layer], conv_b=conv_b[layer],
            dt_bias=dt_bias[layer], a_log=a_log[layer], d_skip=d_skip[layer],
            ssm_norm_g=ssm_norm_g[layer], w_out=w_out[layer],
            norm_ffn_g=norm_ffn_g[layer], w_gate=w_gate[layer],
            w_up=w_up[layer], ffn_conv_w=ffn_conv_w[layer],
            ffn_conv_b=ffn_conv_b[layer], w_down=w_down[layer],
        )
        last = layer == depth - 1
        (xs, *new_s), w16 = _layer(xs, w, lam_init, last, norm_final_g, batch=sb, seq=sl_len,
                                   state=(cache_k[layer], cache_v[layer], state_ssm_conv[layer],
                                          state_ssm[layer], state_ffn_conv[layer]))
        (xp, *new_p), _ = _layer(xp, {**w, **w16}, lam_init, last, norm_final_g, batch=pb, seq=pl_len)
        outs_p.append(new_p)
        outs_s.append(new_s)
    stack = lambda outs, idx: jnp.stack([o[idx] for o in outs])
    return (xp.reshape(pb, pl_len, D_MODEL), xs.reshape(sb, sl_len, D_MODEL),
            *[stack(outs_p, idx) for idx in range(5)],
            *[stack(outs_s, idx) for idx in range(5)])
```

```python
import functools
import math

import jax
import jax.numpy as jnp
import numpy as np
from jax import lax
from jax.experimental import pallas as pl
from jax.experimental.pallas import tpu as pltpu

F32 = jnp.float32
BF16 = jnp.bfloat16

D_MODEL = 2048
CHUNK = 64
ATT_HEADS = 8
QK_DIM = 64
V_DIM = 128
D_ATT = ATT_HEADS * V_DIM
SSM_HEADS = 16
SSM_HEAD_DIM = 64
D_SSM = SSM_HEADS * SSM_HEAD_DIM
SSM_GROUPS = 2
SSM_STATE = 128
SSM_CONV = 4
BC_DIM = 2 * SSM_GROUPS * SSM_STATE
CONV_DIM = D_SSM + BC_DIM
D_FF = 5632
FFN_CONV = 3
EPS = 1e-6
LOG2E = math.log2(math.e)
Q_SCALE = QK_DIM ** -0.5 * LOG2E
LANES = 128
SUBLANES = 8
VMEM_LIMIT = 58 * 1024 * 1024

_NT = (((1,), (1,)), ((), ()))
_TN = (((0,), (0,)), ((), ()))


def _cparams(sem):
    return pltpu.CompilerParams(dimension_semantics=sem, vmem_limit_bytes=VMEM_LIMIT)


def _silu(x):
    return x * jax.nn.sigmoid(x)


def _rms(x, g):
    return x * lax.rsqrt(jnp.mean(x * x, axis=-1, keepdims=True) + EPS) * g


_PROJ_TN = 512
_PROJ_SEGS = ((0, 2), (2, 4), (4, 6), (6, 8), (8, 10), (10, 11))


def _in_proj_kernel(x_ref, g_ref, w_ref, wdt_ref,
                    q_ref, k_ref, v_ref, z_ref, xs_ref, bc_ref, dt_ref, *rest):
    w16_ref, k16_ref, v16_ref = None, None, None
    if len(rest) == 2:
        w16_ref, h_scr = rest
    else:
        k16_ref, v16_ref, h_scr = rest
    j = pl.program_id(1)

    @pl.when(j == 0)
    def _():
        hb = _rms(x_ref[...], g_ref[...]).astype(BF16)
        h_scr[...] = hb
        dt_ref[...] = lax.dot_general(hb, wdt_ref[...], _NT, preferred_element_type=F32)

    outs = (q_ref, k_ref, v_ref, z_ref, xs_ref, bc_ref)
    for (lo, hi), ref in zip(_PROJ_SEGS, outs):
        @pl.when((j >= lo) & (j < hi))
        def _(ref=ref):
            w = w_ref[...]
            if w16_ref is not None:
                w = w.astype(BF16)
                w16_ref[...] = w
            res = lax.dot_general(h_scr[...], w, _NT, preferred_element_type=F32)
            val = res * Q_SCALE if ref is q_ref else res
            ref[...] = val.astype(ref.dtype)
            copy_ref = k16_ref if ref is k_ref else v16_ref if ref is v_ref else None
            if copy_ref is not None:
                copy_ref[...] = res.astype(BF16)


def _in_proj(x, g, w_main, w_dt, tm):
    m = x.shape[0]
    tn = _PROJ_TN
    nj = w_main.shape[0] // tn

    def seg_spec(lo, hi):
        return pl.BlockSpec((tm, tn), lambda i, j: (i, jnp.clip(j - lo, 0, hi - lo - 1)))

    out_shape = (
        jax.ShapeDtypeStruct((m, D_ATT), BF16),
        jax.ShapeDtypeStruct((m, D_ATT), F32),
        jax.ShapeDtypeStruct((m, D_ATT), F32),
        jax.ShapeDtypeStruct((m, D_SSM), F32),
        jax.ShapeDtypeStruct((m, D_SSM), F32),
        jax.ShapeDtypeStruct((m, BC_DIM), F32),
        jax.ShapeDtypeStruct((m, LANES), F32),
    )
    out_specs = tuple(seg_spec(lo, hi) for lo, hi in _PROJ_SEGS) + (
        pl.BlockSpec((tm, LANES), lambda i, j: (i, 0)),)
    if w_main.dtype != BF16:
        assert m == tm, "the bf16 weight copy is written once per tile: needs a single row tile"
        out_shape += (jax.ShapeDtypeStruct((nj * tn, D_MODEL), BF16),)
        out_specs += (pl.BlockSpec((tn, D_MODEL), lambda i, j: (j, 0)),)
    else:
        out_shape += (jax.ShapeDtypeStruct((m, D_ATT), BF16),) * 2
        out_specs += (seg_spec(*_PROJ_SEGS[1]), seg_spec(*_PROJ_SEGS[2]))
    return pl.pallas_call(
        _in_proj_kernel,
        grid=(m // tm, nj),
        in_specs=[
            pl.BlockSpec((tm, D_MODEL), lambda i, j: (i, 0)),
            pl.BlockSpec((1, D_MODEL), lambda i, j: (0, 0)),
            pl.BlockSpec((tn, D_MODEL), lambda i, j: (j, 0)),
            pl.BlockSpec((LANES, D_MODEL), lambda i, j: (0, 0)),
        ],
        out_specs=out_specs,
        out_shape=out_shape,
        scratch_shapes=[pltpu.VMEM((tm, D_MODEL), BF16)],
        compiler_params=_cparams(("arbitrary", "arbitrary")),
        name="in_proj",
    )(x, g, w_main, w_dt)


def _lambda_value(lamv_ref, lam_init):
    lv = lamv_ref[...]
    s1 = jnp.sum(lv[0:1] * lv[1:2], axis=-1, keepdims=True)
    s2 = jnp.sum(lv[2:3] * lv[3:4], axis=-1, keepdims=True)
    return jnp.exp(s1) - jnp.exp(s2) + lam_init


def _split_maps(q):
    lane = lax.broadcasted_iota(jnp.int32, q.shape, 1)
    zero = jnp.zeros_like(q)
    return jnp.where(lane < QK_DIM, q, zero), jnp.where(lane >= QK_DIM, q, zero)


_VT_CHUNK = 512
_VT_PAD = 16
_UNROLL_SHIFT = 2
_UNROLL = 1 << _UNROLL_SHIFT
_HEADS_PER_STEP = 2
_QTILES_PER_GROUP = 4


def _attn_prompt_kernel(q_ref, k_ref, v_ref, slope_ref, lamv_ref, g_ref, dmask_ref, dneg_ref, o_ref,
                        k_scr, vt_scr, acc_scr, t0_scr, t1_scr, dbias_scr, stat_scr,
                        *, tile, lam_init):
    length = k_ref.shape[0]
    width = 2 * tile
    nheads = k_scr.shape[0]
    lam = _lambda_value(lamv_ref, lam_init)

    lane = lax.broadcasted_iota(jnp.int32, (length, LANES), 1)
    koff = lax.broadcasted_iota(jnp.int32, (length, LANES), 0) % tile
    koff = jnp.where(lane < 3, koff, 0).astype(F32).astype(BF16)
    extra = (lax.broadcasted_iota(jnp.int32, (_VT_PAD, tile), 0) == 0).astype(F32).astype(BF16)
    brow = lax.broadcasted_iota(jnp.int32, (LANES, width), 0)
    per = _VT_CHUNK // tile
    ws, q_biases = [], []
    for hh in range(nheads):
        cols = slice(hh * V_DIM, (hh + 1) * V_DIM)
        k_scr[hh, :, 0:V_DIM] = k_ref[:, cols]
        k_scr[hh, :, V_DIM:V_DIM + LANES] = koff
        for c in range(length // _VT_CHUNK):
            vt = v_ref[c * _VT_CHUNK:(c + 1) * _VT_CHUNK, cols].astype(F32).T.astype(BF16)
            for s in range(per):
                vt_scr[hh, c * per + s, 0:V_DIM, :] = vt[:, s * tile:(s + 1) * tile]
                vt_scr[hh, c * per + s, V_DIM:V_DIM + _VT_PAD, :] = extra
        w = slope_ref[hh][:, :1] * LOG2E
        w_hi, w_mid, w_lo = (piece.astype(F32) for piece in _split3(w))
        q_bias = jnp.where(brow == 0, w_hi, jnp.where(brow == 1, w_mid, jnp.where(brow == 2, w_lo, 0.0)))
        ws.append(w)
        q_biases.append(q_bias.astype(BF16))
        dbias_scr[hh] = w * dmask_ref[...] + dneg_ref[...]

    def q_pair(qp, _):
        _attn_prompt_pair(qp, ws, lam, q_biases, q_ref, g_ref, o_ref,
                          k_scr, vt_scr, acc_scr, (t0_scr, t1_scr),
                          dbias_scr, stat_scr, tile=tile, lam_init=lam_init)
        return 0

    lax.fori_loop(0, length // (_QTILES_PER_GROUP * tile), q_pair, 0)


def _attn_prompt_pair(qp, ws, lam, q_biases, q_ref, g_ref, o_ref,
                      k_scr, vt_scr, acc_scr, t_slots, dbias_scr, stat_scr, *, tile, lam_init):
    width = 2 * tile
    nheads = len(ws)
    group = _QTILES_PER_GROUP
    streams = [(hh, u) for hh in range(nheads) for u in range(group)]
    nfull = group * qp
    qis = [nfull + u for _, u in streams]
    rows = [pl.ds(pl.multiple_of(qi * tile, tile), tile) for qi in qis]
    q_augs = []
    for s, (hh, _) in enumerate(streams):
        q1, q2 = _split_maps(q_ref[rows[s], hh * V_DIM:(hh + 1) * V_DIM])
        q_t = jnp.concatenate([q1, q2], axis=0).astype(F32).T.astype(BF16)
        q_augs.append(jnp.concatenate([q_t, q_biases[hh]], axis=0))

    def scores(s, j):
        k0 = pl.multiple_of(jnp.minimum(j, qis[s]) * tile, tile)
        return jnp.dot(k_scr[streams[s][0], pl.ds(k0, tile), :], q_augs[s], preferred_element_type=F32)

    TMAX = 0

    def step(s, j, slot, m_old):
        hh = streams[s][0]
        off = -ws[hh] * jnp.asarray((qis[s] - j) * tile, F32)
        m_new = jnp.maximum(m_old, stat_scr[s, TMAX:TMAX + 1, :] + off)
        t_next = scores(s, j + 1)
        t_slots[1 - slot][s] = t_next
        stat_scr[s, TMAX:TMAX + 1, :] = jnp.max(t_next, axis=0, keepdims=True)
        p = jnp.exp2(t_slots[slot][s] + (off - m_new)).astype(BF16)
        pv = jnp.dot(vt_scr[hh, j], p, preferred_element_type=F32)
        acc_scr[s] = jnp.exp2(m_old - m_new) * acc_scr[s] + pv
        return m_new

    def run(first, ntiles_per_iter, niter, carry):
        def body(i, carry):
            for k in range(ntiles_per_iter):
                carry = tuple(step(s, first + ntiles_per_iter * i + k, k % 2, carry[s])
                              for s in range(len(streams)))
            return carry
        return lax.fori_loop(0, niter, body, carry)

    carry = []
    for s in range(len(streams)):
        acc_scr[s] = jnp.zeros(acc_scr.shape[1:], F32)
        t_first = scores(s, 0)
        t_slots[0][s] = t_first
        stat_scr[s, TMAX:TMAX + 1, :] = jnp.max(t_first, axis=0, keepdims=True)
        carry.append(jnp.full((1, width), -0.5 * float(jnp.finfo(F32).max), F32))
    n_main = lax.shift_right_logical(nfull, _UNROLL_SHIFT)
    done = n_main * _UNROLL
    carry = run(0, _UNROLL, n_main, tuple(carry))
    carry = run(done, 2, lax.shift_right_logical(nfull - done, 1), carry)

    m_run = list(carry)
    diag_slot = [0] * len(streams)
    for e in range(group - 1):
        for s, (hh, u) in enumerate(streams):
            if u > e:
                m_run[s] = step(s, nfull + e, diag_slot[s], m_run[s])
                diag_slot[s] = 1 - diag_slot[s]
    for s, (hh, u) in enumerate(streams):
        t = t_slots[diag_slot[s]][s] + dbias_scr[hh]
        m_new = jnp.maximum(m_run[s], jnp.max(t, axis=0, keepdims=True))
        p = jnp.exp2(t - m_new).astype(BF16)
        acc_scr[s] = (jnp.exp2(m_run[s] - m_new) * acc_scr[s]
                      + jnp.dot(vt_scr[hh, qis[s]], p, preferred_element_type=F32))
    for s, (hh, u) in enumerate(streams):
        acc = acc_scr[s]
        on = acc[0:V_DIM] / acc[V_DIM:V_DIM + 1]
        o_t = on[:, :tile] - lam * on[:, tile:]
        o_t = o_t * lax.rsqrt(jnp.mean(o_t * o_t, axis=0, keepdims=True) + EPS)
        o_ref[rows[s], hh * V_DIM:(hh + 1) * V_DIM] = (
            o_t.T * g_ref[...] * (1.0 - lam_init)).astype(o_ref.dtype)


def _attn_prompt(q, k, v, slopes, lamv, g, lam_init, tile):
    length = q.shape[0]
    key = np.arange(tile)[:, None]
    qry = np.tile(np.arange(tile), 2)[None, :]
    dmask = jnp.asarray((qry - np.abs(qry - key)) - key, F32)
    dneg = jnp.asarray(np.where(key // CHUNK <= qry // CHUNK, 0.0, -np.inf), F32)
    nh = _HEADS_PER_STEP
    nstream = nh * _QTILES_PER_GROUP
    diag_spec = pl.BlockSpec((tile, 2 * tile), lambda h: (0, 0), pipeline_mode=pl.Buffered(1))
    head_spec = pl.BlockSpec((length, nh * V_DIM), lambda h: (0, h), pipeline_mode=pl.Buffered(1))
    kv_spec = pl.BlockSpec((length, nh * V_DIM), lambda h: (0, h))
    return pl.pallas_call(
        functools.partial(_attn_prompt_kernel, tile=tile, lam_init=lam_init),
        grid=(ATT_HEADS // nh,),
        in_specs=[
            head_spec, kv_spec, kv_spec,
            pl.BlockSpec((nh, 1, LANES), lambda h: (h, 0, 0)),
            pl.BlockSpec((4, QK_DIM), lambda h: (0, 0)),
            pl.BlockSpec((1, V_DIM), lambda h: (0, 0)),
            diag_spec, diag_spec,
        ],
        out_specs=head_spec,
        out_shape=jax.ShapeDtypeStruct((length, D_ATT), BF16),
        scratch_shapes=[
            pltpu.VMEM((nh, length, V_DIM + LANES), BF16),
            pltpu.VMEM((nh, length // tile, V_DIM + _VT_PAD, tile), BF16),
            pltpu.VMEM((nstream, V_DIM + _VT_PAD, 2 * tile), F32),
            pltpu.VMEM((nstream, tile, 2 * tile), F32),
            pltpu.VMEM((nstream, tile, 2 * tile), F32),
            pltpu.VMEM((nh, tile, 2 * tile), F32),
            pltpu.VMEM((nstream, SUBLANES, 2 * tile), F32),
        ],
        compiler_params=_cparams(("arbitrary",)),
        name="attn_prompt",
    )(q, k, v, slopes, lamv, g, dmask, dneg)


_SAMPLE_SEQS_PER_STEP = 2


_SAMPLE_CACHE_SLOTS = 3


def _attn_sample_kernel(q_ref, kn_ref, vn_ref, ck_hbm, cv_hbm, lamv_ref, g_ref, bias_c_ref, bias_n_ref,
                        o_ref, ck_ring, cv_ring, sem, *, seq, past, lam_init):
    b = pl.program_id(0)
    nsteps = pl.num_programs(0)
    slots, per = ck_ring.shape[0], ck_ring.shape[1]

    def fetches(step, slot):
        src = pl.ds(step * per, per)
        return (pltpu.make_async_copy(ck_hbm.at[src], ck_ring.at[slot], sem.at[0, slot]),
                pltpu.make_async_copy(cv_hbm.at[src], cv_ring.at[slot], sem.at[1, slot]))

    @pl.when(b == 0)
    def _():
        for s in range(slots - 1):
            for thread, copy in enumerate(fetches(s, s)):
                copy.start(priority=thread)

    ahead = b + slots - 1

    @pl.when(ahead < nsteps)
    def _():
        for thread, copy in enumerate(fetches(ahead, ahead % slots)):
            copy.start(priority=thread)

    slot = b % slots
    for copy in fetches(b, slot):
        copy.wait()
    ck_ref, cv_ref = ck_ring.at[slot], cv_ring.at[slot]

    lam = _lambda_value(lamv_ref, lam_init)
    rows = 2 * seq
    for bi in range(ck_ref.shape[0]):
        br = slice(bi * seq, (bi + 1) * seq)
        sc, sn = [], []
        for h in range(ATT_HEADS):
            sl = slice(h * V_DIM, (h + 1) * V_DIM)
            q1, q2 = _split_maps(q_ref[br, sl])
            qq = jnp.concatenate([q1, q2], axis=0)
            kc = ck_ref[bi, pl.ds(h, past, stride=ATT_HEADS), :].astype(BF16)
            sc.append(lax.dot_general(qq, kc, _NT, preferred_element_type=F32))
            sn.append(lax.dot_general(qq, kn_ref[br, sl].astype(BF16), _NT, preferred_element_type=F32))
        sc = jnp.concatenate(sc, axis=0) - bias_c_ref[...]
        sn = jnp.concatenate(sn, axis=0) - bias_n_ref[...]
        m = jnp.maximum(jnp.max(sc, axis=-1, keepdims=True), jnp.max(sn, axis=-1, keepdims=True))
        pc = jnp.exp2(sc - m)
        pn = jnp.exp2(sn - m)
        inv_l = 1.0 / (jnp.sum(pc, axis=-1, keepdims=True) + jnp.sum(pn, axis=-1, keepdims=True))
        pc = pc.astype(BF16)
        pn = pn.astype(BF16)
        for h in range(ATT_HEADS):
            sl = slice(h * V_DIM, (h + 1) * V_DIM)
            hr = slice(h * rows, (h + 1) * rows)
            vc = cv_ref[bi, pl.ds(h, past, stride=ATT_HEADS), :].astype(BF16)
            acc = (jnp.dot(pc[hr], vc, preferred_element_type=F32)
                   + jnp.dot(pn[hr], vn_ref[br, sl].astype(BF16), preferred_element_type=F32))
            on = acc * inv_l[hr]
            o = on[:seq] - lam * on[seq:]
            o_ref[br, sl] = (_rms(o, g_ref[...]) * (1.0 - lam_init)).astype(o_ref.dtype)


def _attn_sample(q, k_new, v_new, cache_k, cache_v, lamv, g, lam_init, seq):
    nb, past = cache_k.shape[0], cache_k.shape[1] // ATT_HEADS
    slope = np.repeat(2.0 ** (-8.0 * np.arange(1, ATT_HEADS + 1) / ATT_HEADS), 2 * seq)[:, None] * LOG2E
    qpos = np.tile(np.arange(seq), 2 * ATT_HEADS)[:, None]
    bias_c = jnp.asarray(slope * (qpos + past - np.arange(past)[None, :]), F32)
    bias_n = jnp.asarray(slope * np.abs(qpos - np.arange(seq)[None, :]), F32)
    per = _SAMPLE_SEQS_PER_STEP
    row_spec = pl.BlockSpec((per * seq, D_ATT), lambda b: (b, 0))
    cache_spec = pl.BlockSpec(memory_space=pl.ANY)
    slots = _SAMPLE_CACHE_SLOTS
    assert nb % per == 0 and nb // per >= slots - 1

    def const(arr):
        return pl.BlockSpec(arr.shape, lambda b: (0, 0), pipeline_mode=pl.Buffered(1))

    return pl.pallas_call(
        functools.partial(_attn_sample_kernel, seq=seq, past=past, lam_init=lam_init),
        grid=(nb // per,),
        in_specs=[row_spec, row_spec, row_spec, cache_spec, cache_spec,
                  pl.BlockSpec((4, QK_DIM), lambda b: (0, 0)),
                  pl.BlockSpec((1, V_DIM), lambda b: (0, 0)),
                  const(bias_c), const(bias_n)],
        out_specs=row_spec,
        out_shape=jax.ShapeDtypeStruct((nb * seq, D_ATT), BF16),
        scratch_shapes=[
            pltpu.VMEM((slots, per, past * ATT_HEADS, V_DIM), F32),
            pltpu.VMEM((slots, per, past * ATT_HEADS, V_DIM), F32),
            pltpu.SemaphoreType.DMA((2, slots)),
        ],
        compiler_params=_cparams(("arbitrary",)),
        name="attn_sample",
    )(q, k_new, v_new, cache_k, cache_v, lamv, g, bias_c, bias_n)


def _split3(x):
    hi = x.astype(BF16)
    r = x - hi.astype(F32)
    mid = r.astype(BF16)
    lo = (r - mid.astype(F32)).astype(BF16)
    return hi, mid, lo


def _ssd_chunk(xs, bm, cm, dt, z, s_ref, a_heads, dskip, gn, e3, es3, t3, ones_k, lc):
    seg_w = SSM_HEADS * lc
    half = D_SSM // SSM_GROUPS
    d3 = jnp.concatenate(_split3(dt), axis=1)
    dt_x = jnp.dot(d3, e3, preferred_element_type=F32)
    la3 = jnp.concatenate(_split3(dt * a_heads), axis=0)
    ac3 = jnp.concatenate(_split3(jnp.dot(t3, la3, preferred_element_type=F32)), axis=1)
    acol = jnp.dot(ac3, e3, preferred_element_type=F32)
    acol_s = acol if seg_w == D_SSM else jnp.dot(ac3, es3, preferred_element_type=F32)
    t_idx = lax.broadcasted_iota(jnp.int32, (lc, seg_w), 0)
    s_idx = lax.broadcasted_iota(jnp.int32, (lc, seg_w), 1) % lc
    arow = jnp.sum(jnp.where(t_idx == s_idx, acol_s, 0.0), axis=0, keepdims=True)
    decay = jnp.where(t_idx >= s_idx, jnp.exp(acol_s - arow), 0.0)

    cb16 = cm.astype(BF16)
    bb16 = bm.astype(BF16)
    hpg = SSM_HEADS // SSM_GROUPS
    cbs = []
    for g in range(SSM_GROUPS):
        gs = slice(g * SSM_STATE, (g + 1) * SSM_STATE)
        b_rep = jnp.concatenate([bb16[:, gs]] * hpg, axis=0)
        cbs.append(lax.dot_general(cb16[:, gs], b_rep, _NT, preferred_element_type=F32))
    mmat = (jnp.concatenate(cbs, axis=1) * decay).astype(BF16)

    xdt = xs * dt_x
    xdt16 = xdt.astype(BF16)
    hk = (2 * LANES) // lc
    wd = hk * SSM_HEAD_DIM
    blk = (lax.broadcasted_iota(jnp.int32, (hk * lc, wd), 0) // lc
           == lax.broadcasted_iota(jnp.int32, (hk * lc, wd), 1) // SSM_HEAD_DIM)
    parts = []
    for i in range(SSM_HEADS // hk):
        xd = xdt16[:, i * wd:(i + 1) * wd]
        bd = jnp.where(blk, jnp.concatenate([xd] * hk, axis=0), jnp.zeros((), BF16))
        parts.append(jnp.dot(mmat[:, i * hk * lc:(i + 1) * hk * lc], bd, preferred_element_type=F32))
    y_intra = jnp.concatenate(parts, axis=1) if len(parts) > 1 else parts[0]

    yi = []
    for g in range(SSM_GROUPS):
        sg = s_ref[g * half:(g + 1) * half, :].astype(BF16)
        yi.append(lax.dot_general(cb16[:, g * SSM_STATE:(g + 1) * SSM_STATE], sg, _NT,
                                  preferred_element_type=F32))
    y_inter = jnp.exp(acol) * jnp.concatenate(yi, axis=1)

    alast = acol[lc - 1:lc, :]
    dec_end = jnp.exp(alast - acol)
    xd_end = (xdt * dec_end).astype(BF16)
    krow = lax.broadcasted_iota(jnp.int32, (ones_k.shape[0], D_SSM), 0)
    a_hi, a_mid, a_lo = (piece.astype(F32) for piece in _split3(alast))
    pieces = jnp.where(krow == 0, a_hi, jnp.where(krow == 1, a_mid, jnp.where(krow == 2, a_lo, 0.0)))
    acl = lax.dot_general(pieces.astype(BF16), ones_k, _TN, preferred_element_type=F32)
    for g in range(SSM_GROUPS):
        rows = slice(g * half, (g + 1) * half)
        upd = lax.dot_general(xd_end[:, rows], bb16[:, g * SSM_STATE:(g + 1) * SSM_STATE], _TN,
                              preferred_element_type=F32)
        s_ref[rows, :] = jnp.exp(acl[rows, :]) * s_ref[rows, :] + upd

    y = (y_intra + y_inter + dskip * xs) * _silu(z)
    outs = []
    for g in range(SSM_GROUPS):
        cs = slice(g * half, (g + 1) * half)
        outs.append(_rms(y[:, cs], gn[:, cs]))
    return jnp.concatenate(outs, axis=1)


def _ssd_kernel(*refs, lc, nchunk, has_state):
    if has_state:
        (xs_ref, bc_ref, dt_ref, z_ref, prev_ref, h0_ref, cw_ref, cbias_ref, dtb_ref, alog_ref,
         dskip_ref, gn_ref, e3_ref, es3_ref, t3_ref, ones_ref,
         y_ref, s_ref, scr_x, scr_bc) = refs
    else:
        (xs_ref, bc_ref, dt_ref, z_ref, cw_ref, cbias_ref, dtb_ref, alog_ref,
         dskip_ref, gn_ref, e3_ref, es3_ref, t3_ref, ones_ref,
         y_ref, s_ref, scr_x, scr_bc) = refs
    rows = lc * nchunk
    pad = SUBLANES

    def conv_silu(scr_x, scr_bc, nrows):
        xc = cbias_ref[:, 0:D_SSM]
        bcc = cbias_ref[:, D_SSM:CONV_DIM]
        for tap in range(SSM_CONV):
            off = pad - (SSM_CONV - 1) + tap
            xc = xc + scr_x[off:off + nrows, :] * cw_ref[tap:tap + 1, 0:D_SSM]
            bcc = bcc + scr_bc[off:off + nrows, :] * cw_ref[tap:tap + 1, D_SSM:CONV_DIM]
        return _silu(xc), _silu(bcc)

    dt_in = dt_ref[...] + dtb_ref[...]
    dt = jnp.maximum(dt_in, 0.0) + jnp.log1p(jnp.exp(-jnp.abs(dt_in)))
    a_heads = -jnp.exp(alog_ref[...])
    nbm = SSM_GROUPS * SSM_STATE

    def chunk(c, xs_c, bc_c, s2d):
        rs = slice(c * lc, (c + 1) * lc)
        y = _ssd_chunk(xs_c, bc_c[:, 0:nbm], bc_c[:, nbm:2 * nbm], dt[rs],
                       z_ref[rs, :], s2d, a_heads, dskip_ref[...], gn_ref[...],
                       e3_ref[...], es3_ref[...], t3_ref[...], ones_ref[...], lc)
        y_ref[rs, :] = y.astype(y_ref.dtype)

    if has_state:
        for c in range(nchunk):
            rs = slice(c * lc, (c + 1) * lc)
            scr_x[c, 0:pad, :] = prev_ref[c, :, 0:D_SSM]
            scr_bc[c, 0:pad, :] = prev_ref[c, :, D_SSM:CONV_DIM]
            scr_x[c, pad:pad + lc, :] = xs_ref[rs, :]
            scr_bc[c, pad:pad + lc, :] = bc_ref[rs, :]
            s_ref[c] = h0_ref[c]
            xs_act, bc_act = conv_silu(scr_x.at[c], scr_bc.at[c], lc)
            chunk(c, xs_act, bc_act, s_ref.at[c])
    else:
        @pl.when(pl.program_id(0) == 0)
        def _():
            scr_x[0:pad, :] = jnp.zeros((pad, D_SSM), F32)
            scr_bc[0:pad, :] = jnp.zeros((pad, BC_DIM), F32)
            s_ref[...] = jnp.zeros(s_ref.shape, F32)

        scr_x[pad:pad + rows, :] = xs_ref[...]
        scr_bc[pad:pad + rows, :] = bc_ref[...]
        xs_act, bc_act = conv_silu(scr_x, scr_bc, rows)
        scr_x[0:pad, :] = scr_x[rows:rows + pad, :]
        scr_bc[0:pad, :] = scr_bc[rows:rows + pad, :]
        for c in range(nchunk):
            rs = slice(c * lc, (c + 1) * lc)
            chunk(c, xs_act[rs], bc_act[rs], s_ref)


def _ssd_constants(lc):
    seg_w = SSM_HEADS * lc
    head_of_lane = np.arange(D_SSM) // SSM_HEAD_DIM
    e = (np.arange(LANES)[:, None] == head_of_lane[None, :]).astype(np.float32)
    es = (np.arange(LANES)[:, None] == (np.arange(seg_w) // lc)[None, :]).astype(np.float32)
    tri = np.tril(np.ones((lc, lc), np.float32))
    return (jnp.asarray(np.concatenate([e] * 3, axis=0), BF16),
            jnp.asarray(np.concatenate([es] * 3, axis=0), BF16),
            jnp.asarray(np.concatenate([tri] * 3, axis=1), BF16),
            jnp.ones((2 * SUBLANES, SSM_STATE), BF16))


def _ssd(xs, bc, dt, z, conv_w, conv_b, dt_bias, a_log, d_skip, gn, lc, nchunk,
         conv_prev=None, h0=None):
    m = xs.shape[0]
    rows = lc * nchunk
    has_state = h0 is not None
    e3, es3, t3, ones_k = _ssd_constants(lc)
    dtb = jnp.zeros((1, LANES), F32).at[0, :SSM_HEADS].set(dt_bias)
    alog = jnp.zeros((1, LANES), F32).at[0, :SSM_HEADS].set(a_log)
    dskip_x = jnp.repeat(d_skip, SSM_HEAD_DIM)[None, :]

    def const(shape):
        return pl.BlockSpec(shape, lambda i: (0,) * len(shape))

    def rowblk(width):
        return pl.BlockSpec((rows, width), lambda i: (i, 0))

    in_specs = [rowblk(D_SSM), rowblk(BC_DIM), rowblk(LANES), rowblk(D_SSM)]
    args = [xs, bc, dt, z]
    if has_state:
        nb = h0.shape[0]
        in_specs += [pl.BlockSpec((nchunk, SUBLANES, CONV_DIM), lambda i: (i, 0, 0)),
                     pl.BlockSpec((nchunk, D_SSM, SSM_STATE), lambda i: (i, 0, 0))]
        args += [conv_prev, h0]
        s_shape = jax.ShapeDtypeStruct((nb, D_SSM, SSM_STATE), F32)
        s_spec = pl.BlockSpec((nchunk, D_SSM, SSM_STATE), lambda i: (i, 0, 0))
        scratch = [pltpu.VMEM((nchunk, lc + SUBLANES, D_SSM), F32),
                   pltpu.VMEM((nchunk, lc + SUBLANES, BC_DIM), F32)]
    else:
        s_shape = jax.ShapeDtypeStruct((D_SSM, SSM_STATE), F32)
        s_spec = const((D_SSM, SSM_STATE))
        scratch = [pltpu.VMEM((rows + SUBLANES, D_SSM), F32),
                   pltpu.VMEM((rows + SUBLANES, BC_DIM), F32)]
    in_specs += [const((SSM_CONV, CONV_DIM)), const((1, CONV_DIM)), const((1, LANES)),
                 const((1, LANES)), const((1, D_SSM)), const((1, D_SSM)),
                 const(e3.shape), const(es3.shape), const(t3.shape), const(ones_k.shape)]
    args += [conv_w, conv_b[None, :], dtb, alog, dskip_x, gn[None, :], e3, es3, t3, ones_k]
    return pl.pallas_call(
        functools.partial(_ssd_kernel, lc=lc, nchunk=nchunk, has_state=has_state),
        grid=(m // rows,),
        in_specs=in_specs,
        out_specs=(rowblk(D_SSM), s_spec),
        out_shape=(jax.ShapeDtypeStruct((m, D_SSM), BF16), s_shape),
        scratch_shapes=scratch,
        compiler_params=_cparams(("arbitrary",)),
        name="ssd_sample" if has_state else "ssd_prompt",
    )(*args)


def _out_proj_kernel(o_ref, y_ref, x_ref, w_ref, g_ref, x2_ref, hf_ref, w16_ref=None):
    if w16_ref is not None:
        w16_ref[...] = w_ref[...].astype(BF16)
        w_ref = w16_ref
    x2 = (x_ref[...]
          + jnp.dot(o_ref[...], w_ref[0:D_ATT, :], preferred_element_type=F32)
          + jnp.dot(y_ref[...], w_ref[D_ATT:D_ATT + D_SSM, :], preferred_element_type=F32))
    x2_ref[...] = x2
    hf_ref[...] = _rms(x2, g_ref[...]).astype(hf_ref.dtype)


def _out_proj(o, y, x, w, g, tm):
    m = x.shape[0]
    w_spec = pl.BlockSpec((D_ATT + D_SSM, D_MODEL), lambda i: (0, 0), pipeline_mode=pl.Buffered(1))
    out_specs = (pl.BlockSpec((tm, D_MODEL), lambda i: (i, 0)),
                 pl.BlockSpec((tm, D_MODEL), lambda i: (i, 0)))
    out_shape = (jax.ShapeDtypeStruct((m, D_MODEL), F32),
                 jax.ShapeDtypeStruct((m, D_MODEL), BF16))
    if w.dtype != BF16:
        assert m == tm, "the bf16 weight copy is written once: needs a single row tile"
        out_specs += (pl.BlockSpec((D_ATT + D_SSM, D_MODEL), lambda i: (0, 0)),)
        out_shape += (jax.ShapeDtypeStruct(w.shape, BF16),)
    return pl.pallas_call(
        _out_proj_kernel,
        grid=(m // tm,),
        in_specs=[
            pl.BlockSpec((tm, D_ATT), lambda i: (i, 0)),
            pl.BlockSpec((tm, D_SSM), lambda i: (i, 0)),
            pl.BlockSpec((tm, D_MODEL), lambda i: (i, 0)),
            w_spec,
            pl.BlockSpec((1, D_MODEL), lambda i: (0, 0)),
        ],
        out_specs=out_specs,
        out_shape=out_shape,
        compiler_params=_cparams(("arbitrary",)),
        name="out_proj",
    )(o, y, x, w, g)


_RESIDUAL_PIECES = 8


def _ffn_kernel(*refs, tm, seq, final_norm):
    if seq is None:
        (hf_ref, x2_ref, wg_ref, wu_ref, wd_ref, cw_ref, cb_ref, gfin_ref,
         out_ref, gl_ref, g_scr, act_scr, carry_scr) = refs
    else:
        (hf_ref, x2_ref, wg_ref, wu_ref, wd_ref, cw_ref, cb_ref, gfin_ref, prev_ref, sel1_ref, sel2_ref,
         out_ref, gl_ref, wg16_ref, wu16_ref, wd16_ref, g_scr, act_scr) = refs
    i = pl.program_id(0)
    f = pl.program_id(1)
    nf = pl.num_programs(1) - 1
    pad = SUBLANES
    res_rows = tm // _RESIDUAL_PIECES

    def weight(w_ref, w16_ref):
        if seq is None:
            return w_ref[...]
        w16 = w_ref[...].astype(BF16)
        w16_ref[...] = w16
        return w16

    def gate_up():
        hf = hf_ref[...]
        gate = jnp.dot(hf, weight(wg_ref, None if seq is None else wg16_ref), preferred_element_type=F32)
        up = jnp.dot(hf, weight(wu_ref, None if seq is None else wu16_ref), preferred_element_type=F32)
        g_scr[pad:pad + tm, :] = gate
        if seq is None:
            g_scr[0:pad, :] = carry_scr[f]
            carry_scr[f] = g_scr[tm:tm + pad, :]
            gl_ref[...] = g_scr[tm:tm + pad, :]
            g1 = g_scr[pad - 1:pad - 1 + tm, :]
            g2 = g_scr[pad - 2:pad - 2 + tm, :]
        else:
            g_scr[0:pad, :] = jnp.zeros((pad, gate.shape[1]), F32)
            gl_ref[...] = gate
            prev3 = jnp.concatenate(_split3(prev_ref[...]), axis=0)
            ov1 = jnp.dot(sel1_ref[...], prev3, preferred_element_type=F32)
            ov2 = jnp.dot(sel2_ref[...], prev3, preferred_element_type=F32)
            pos = lax.broadcasted_iota(jnp.int32, gate.shape, 0) % seq
            g1 = jnp.where(pos == 0, ov1, g_scr[pad - 1:pad - 1 + tm, :])
            g2 = jnp.where(pos < 2, ov2, g_scr[pad - 2:pad - 2 + tm, :])
        conv = cb_ref[...] + g2 * cw_ref[0:1, :] + g1 * cw_ref[1:2, :] + gate * cw_ref[2:3, :]
        act_scr[...] = (_silu(conv) * up).astype(BF16)

    def down():
        return jnp.dot(act_scr[...], weight(wd_ref, None if seq is None else wd16_ref),
                       preferred_element_type=F32)

    @pl.when(f == 0)
    def _():
        if seq is None:
            @pl.when(i == 0)
            def _():
                carry_scr[...] = jnp.zeros(carry_scr.shape, F32)
        out_ref[...] = jnp.zeros(out_ref.shape, F32)
        out_ref[0:res_rows, :] = x2_ref[...]
        gate_up()

    @pl.when((f > 0) & (f < nf))
    def _():
        contrib = down()
        gate_up()
        out_ref[...] += contrib
        r0 = pl.multiple_of(jnp.minimum(f, _RESIDUAL_PIECES - 1) * res_rows, res_rows)
        out_ref[pl.ds(r0, res_rows), :] += jnp.where(f < _RESIDUAL_PIECES, x2_ref[...], 0.0)

    @pl.when(f == nf)
    def _():
        res = out_ref[...] + down()
        out_ref[...] = _rms(res, gfin_ref[...]) if final_norm else res


def _ffn_prev_selectors(m, seq, nprev):
    sel1 = np.zeros((m, 3 * nprev), np.float32)
    sel2 = np.zeros((m, 3 * nprev), np.float32)
    nstate = FFN_CONV - 1
    for b in range(m // seq):
        for piece in range(3):
            base = piece * nprev + b * nstate
            sel1[b * seq, base + 1] = 1.0
            sel2[b * seq, base + 0] = 1.0
            sel2[b * seq + 1, base + 1] = 1.0
    return jnp.asarray(sel1, BF16), jnp.asarray(sel2, BF16)


def _ffn(hf, x2, wg, wu, wd, cw, cb, gfin, tm, tf, final_norm, seq=None, prev=None):
    m = hf.shape[0]
    nf = D_FF // tf
    assert nf >= _RESIDUAL_PIECES and tm % (_RESIDUAL_PIECES * SUBLANES) == 0
    cur = lambda f: jnp.minimum(f, nf - 1)
    last = lambda f: jnp.maximum(f - 1, 0)
    in_specs = [
        pl.BlockSpec((tm, D_MODEL), lambda i, f: (i, 0)),
        pl.BlockSpec((tm // _RESIDUAL_PIECES, D_MODEL),
                     lambda i, f: (i * _RESIDUAL_PIECES + jnp.minimum(f, _RESIDUAL_PIECES - 1), 0)),
        pl.BlockSpec((D_MODEL, tf), lambda i, f: (0, cur(f))),
        pl.BlockSpec((D_MODEL, tf), lambda i, f: (0, cur(f))),
        pl.BlockSpec((tf, D_MODEL), lambda i, f: (last(f), 0)),
        pl.BlockSpec((FFN_CONV, tf), lambda i, f: (0, cur(f))),
        pl.BlockSpec((1, tf), lambda i, f: (0, cur(f))),
        pl.BlockSpec((1, D_MODEL), lambda i, f: (0, 0)),
    ]
    args = [hf, x2, wg, wu, wd, cw, cb[None, :], gfin[None, :]]
    scratch = [pltpu.VMEM((tm + SUBLANES, tf), F32), pltpu.VMEM((tm, tf), BF16)]
    if seq is None:
        gl_shape = jax.ShapeDtypeStruct((m // tm * SUBLANES, D_FF), F32)
        gl_spec = pl.BlockSpec((SUBLANES, tf), lambda i, f: (i, cur(f)))
        scratch.append(pltpu.VMEM((nf, SUBLANES, tf), F32))
    else:
        assert m == tm, "sample FFN handles all sequences in one row tile"
        nprev = prev.shape[0]
        sel1, sel2 = _ffn_prev_selectors(m, seq, nprev)
        in_specs += [pl.BlockSpec((nprev, tf), lambda i, f: (0, cur(f))),
                     pl.BlockSpec(sel1.shape, lambda i, f: (0, 0)),
                     pl.BlockSpec(sel2.shape, lambda i, f: (0, 0))]
        args += [prev, sel1, sel2]
        gl_shape = jax.ShapeDtypeStruct((m, D_FF), F32)
        gl_spec = pl.BlockSpec((tm, tf), lambda i, f: (i, cur(f)))
    out_specs = [pl.BlockSpec((tm, D_MODEL), lambda i, f: (i, 0)), gl_spec]
    out_shape = [jax.ShapeDtypeStruct((m, D_MODEL), F32), gl_shape]
    if seq is not None:
        out_specs += [in_specs[2], in_specs[3], in_specs[4]]
        out_shape += [jax.ShapeDtypeStruct(wg.shape, BF16), jax.ShapeDtypeStruct(wu.shape, BF16),
                      jax.ShapeDtypeStruct(wd.shape, BF16)]
    return pl.pallas_call(
        functools.partial(_ffn_kernel, tm=tm, seq=seq, final_norm=final_norm),
        grid=(m // tm, nf + 1),
        in_specs=in_specs,
        out_specs=tuple(out_specs),
        out_shape=tuple(out_shape),
        scratch_shapes=scratch,
        compiler_params=_cparams(("arbitrary", "arbitrary")),
        name="ffn_sample" if seq is not None else "ffn_prompt",
    )(*args)


def _tiling(m, has_state):
    return dict(
        proj_tm=min(1024, m),
        attn_tile=256,
        ssd_chunks=2 * _SAMPLE_SEQS_PER_STEP if has_state else 8,
        out_tm=min(512, m),
        ffn_tm=min(512, m) if has_state else min(1024, m),
        ffn_tf=256 if has_state else 512,
    )


def _layer(x, w, lam_init, final_norm, gfin, *, batch, seq, state=None):
    m = x.shape[0]
    lamv = jnp.stack([w["lambda_q1"], w["lambda_k1"], w["lambda_q2"], w["lambda_k2"]])
    tiles = _tiling(m, state is not None)
    w16 = {}
    q, k, v, z, xs, bc, dt, *extra = _in_proj(x, w["norm_mix_g"][None, :], w["w_in_main"], w["w_in_dt"],
                                              tm=tiles["proj_tm"])
    if len(extra) == 1:
        w16["w_in_main"] = extra[0]

    if state is None:
        k16, v16 = extra if len(extra) == 2 else (k.astype(BF16), v.astype(BF16))
        slopes = jnp.broadcast_to(
            jnp.asarray(2.0 ** (-8.0 * np.arange(1, ATT_HEADS + 1) / ATT_HEADS), F32)[:, None, None],
            (ATT_HEADS, 1, LANES))
        o = _attn_prompt(q, k16, v16, slopes, lamv, w["attn_subln_g"][None, :], lam_init,
                         tile=tiles["attn_tile"])
        y, s_new = _ssd(xs, bc, dt, z, w["conv_w"], w["conv_b"], w["dt_bias"], w["a_log"],
                        w["d_skip"], w["ssm_norm_g"], lc=CHUNK, nchunk=tiles["ssd_chunks"])
        conv_new = jnp.concatenate([xs[m - (SSM_CONV - 1):], bc[m - (SSM_CONV - 1):]], axis=-1)[None]
        s_new = s_new[None]
    else:
        cache_k, cache_v, conv_prev, ssm_prev, ffn_prev = state
        past = cache_k.shape[1]
        assert past % CHUNK == 0 and seq <= CHUNK
        o = _attn_sample(q, k, v, cache_k.reshape(batch, past * ATT_HEADS, V_DIM),
                         cache_v.reshape(batch, past * ATT_HEADS, V_DIM),
                         lamv, w["attn_subln_g"][None, :], lam_init, seq)
        prev8 = jnp.pad(conv_prev, ((0, 0), (SUBLANES - (SSM_CONV - 1), 0), (0, 0)))
        y, s_new = _ssd(xs, bc, dt, z, w["conv_w"], w["conv_b"], w["dt_bias"], w["a_log"],
                        w["d_skip"], w["ssm_norm_g"], lc=seq, nchunk=tiles["ssd_chunks"],
                        conv_prev=prev8, h0=ssm_prev.reshape(batch, D_SSM, SSM_STATE))
        conv_new = jnp.concatenate([xs.reshape(batch, seq, D_SSM)[:, seq - (SSM_CONV - 1):],
                                    bc.reshape(batch, seq, BC_DIM)[:, seq - (SSM_CONV - 1):]], axis=-1)

    x2, hf, *extra = _out_proj(o, y, x, w["w_out"], w["norm_ffn_g"][None, :], tm=tiles["out_tm"])
    if extra:
        w16["w_out"] = extra[0]

    ffn_args = (hf, x2, w["w_gate"], w["w_up"], w["w_down"], w["ffn_conv_w"], w["ffn_conv_b"], gfin)
    ffn_tiles = dict(tm=tiles["ffn_tm"], tf=tiles["ffn_tf"], final_norm=final_norm)
    if state is None:
        x3, gl = _ffn(*ffn_args, **ffn_tiles)
        ffn_new = gl[None, gl.shape[0] - (FFN_CONV - 1):]
    else:
        x3, gl, w16["w_gate"], w16["w_up"], w16["w_down"] = _ffn(
            *ffn_args, **ffn_tiles, seq=seq, prev=ffn_prev.reshape(batch * (FFN_CONV - 1), D_FF))
        ffn_new = gl.reshape(batch, seq, D_FF)[:, seq - (FFN_CONV - 1):]
    k_new = k.reshape(batch, seq, ATT_HEADS, 2 * QK_DIM)
    v_new = v.reshape(batch, seq, ATT_HEADS, V_DIM)
    s_new = s_new.reshape(batch, SSM_HEADS, SSM_HEAD_DIM, SSM_STATE)
    return (x3, k_new, v_new, conv_new, s_new, ffn_new), w16


def kernel(x_prompt, x_sample, cache_k, cache_v, state_ssm_conv, state_ssm, state_ffn_conv, norm_mix_g, w_in, lambda_q1, lambda_k1, lambda_q2, lambda_k2, attn_subln_g, conv_w, conv_b, dt_bias, a_log, d_skip, ssm_norm_g, w_out, norm_ffn_g, w_gate, w_up, ffn_conv_w, ffn_conv_b, w_down, norm_final_g):
    depth = w_in.shape[0]
    pb, pl_len, _ = x_prompt.shape
    sb, sl_len, _ = x_sample.shape
    assert pb == 1 and pl_len % CHUNK == 0
    xp = x_prompt.reshape(pb * pl_len, D_MODEL)
    xs = x_sample.reshape(sb * sl_len, D_MODEL)
    n_main = 2 * D_ATT + D_ATT + D_SSM + CONV_DIM
    outs_p, outs_s = [], []
    for layer in range(depth):
        lam_init = 0.8 - 0.6 * math.exp(-0.3 * layer)
        w_in_t = jnp.swapaxes(w_in[layer], 0, 1)
        w = dict(
            norm_mix_g=norm_mix_g[layer],
            w_in_main=w_in_t,
            w_in_dt=jnp.pad(w_in_t[n_main:].astype(BF16), ((0, LANES - SSM_HEADS), (0, 0))),
            lambda_q1=lambda_q1[layer], lambda_k1=lambda_k1[layer],
            lambda_q2=lambda_q2[layer], lambda_k2=lambda_k2[layer],
            attn_subln_g=attn_subln_g[layer], conv_w=conv_w[layer], conv_b=conv_b[layer],
            dt_bias=dt_bias[layer], a_log=a_log[layer], d_skip=d_skip[layer],
            ssm_norm_g=ssm_norm_g[layer], w_out=w_out[layer],
            norm_ffn_g=norm_ffn_g[layer], w_gate=w_gate[layer],
            w_up=w_up[layer], ffn_conv_w=ffn_conv_w[layer],
            ffn_conv_b=ffn_conv_b[layer], w_down=w_down[layer],
        )
        last = layer == depth - 1
        (xs, *new_s), w16 = _layer(xs, w, lam_init, last, norm_final_g, batch=sb, seq=sl_len,
                                   state=(cache_k[layer], cache_v[layer], state_ssm_conv[layer],
                                          state_ssm[layer], state_ffn_conv[layer]))
        (xp, *new_p), _ = _layer(xp, {**w, **w16}, lam_init, last, norm_final_g, batch=pb, seq=pl_len)
        outs_p.append(new_p)
        outs_s.append(new_s)
    stack = lambda outs, idx: jnp.stack([o[idx] for o in outs])
    return (xp.reshape(pb, pl_len, D_MODEL), xs.reshape(sb, sl_len, D_MODEL),
            *[stack(outs_p, idx) for idx in range(5)],
            *[stack(outs_s, idx) for idx in range(5)])
```
